```python
import jax, jax.numpy as jnp
from jax import lax
import numpy as np

D_MODEL = 2048
BATCH = 4
SEQ = 2048
DEPTH = 1
DEC_BATCH = 128
DEC_SEQ = 1
PAST_LEN = 16384
PAGE_SIZE = 128

A_HEADS = 4
A_WIDTH = D_MODEL // 2
A_DK = A_WIDTH // A_HEADS
A_DV = A_WIDTH // A_HEADS
CONV_W = 4
B_HEADS = 4
B_WIDTH = D_MODEL // 2
B_KWIDTH = B_WIDTH // 2
B_DK = B_KWIDTH // B_HEADS
B_DV = B_WIDTH // B_HEADS
GATE_RANK = 16
GATE_TAU = 16.0
CHUNK = 64
ALPHA = (2 * DEPTH) ** 0.25
BETA = (8 * DEPTH) ** -0.25
LN_EPS = 1e-5

SEG_SIZES = (2 * A_WIDTH, A_WIDTH, A_HEADS, A_HEADS, A_WIDTH, A_WIDTH,
             B_KWIDTH, B_KWIDTH, B_WIDTH, GATE_RANK, B_WIDTH, D_MODEL, D_MODEL)
N_IN = (2 * A_WIDTH + A_WIDTH + 2 * A_HEADS + 2 * A_WIDTH
        + 2 * B_KWIDTH + B_WIDTH + GATE_RANK + B_WIDTH + 2 * D_MODEL)

kernel_name = "mlstm_gla_gated_hybrid_step"


def _split_cols(p):
    idx = []
    acc = 0
    for s in SEG_SIZES[:-1]:
        acc += s
        idx.append(acc)
    return jnp.split(p, idx, axis=-1)


def _heads(a, h):
    bt, l, w = a.shape
    return a.reshape(bt, l, h, w // h).transpose(0, 2, 1, 3)


def _to_chunks(a, ch):
    bt, h, l = a.shape[:3]
    a = a.reshape((bt, h, l // ch, ch) + a.shape[3:])
    return jnp.moveaxis(a, 2, 0)


def _from_chunks(a):
    a = jnp.moveaxis(a, 0, 2)
    return a.reshape(a.shape[:2] + (a.shape[2] * a.shape[3],) + a.shape[4:])


def _layernorm(x, g, b):
    xf = x.astype(jnp.float32)
    mu = jnp.mean(xf, axis=-1, keepdims=True)
    var = jnp.mean(jnp.square(xf - mu), axis=-1, keepdims=True)
    return (xf - mu) * lax.rsqrt(var + LN_EPS) * g.astype(jnp.float32) + b.astype(jnp.float32)


def _mlstm_chunk(q, k, v, itil, logf, C, n, m):
    L = q.shape[2]
    causal = jnp.tril(jnp.ones((L, L), dtype=bool))
    b = jnp.cumsum(logf, axis=-1)
    dmat = b[..., :, None] - b[..., None, :] + itil[..., None, :]
    dmat = jnp.where(causal, dmat, -jnp.inf)
    inter = b + m[..., None]
    m_t = jnp.maximum(inter, jnp.max(dmat, axis=-1))
    w = jnp.exp(dmat - m_t[..., None])
    decay = jnp.exp(inter - m_t)
    s = jnp.einsum('bhtd,bhsd->bhts', q, k) * w
    num = decay[..., None] * jnp.einsum('bhtd,bhde->bhte', q, C) + jnp.einsum('bhts,bhse->bhte', s, v)
    den = decay * jnp.einsum('bhtd,bhd->bht', q, n) + jnp.sum(s, axis=-1)
    h = num / jnp.maximum(jnp.abs(den), jnp.exp(-m_t))[..., None]
    m_new = m_t[..., -1]
    wk = jnp.exp(b[..., -1:] - b + itil - m_new[..., None])
    dec = jnp.exp(b[..., -1] + m - m_new)
    C_new = dec[..., None, None] * C + jnp.einsum('bhs,bhsd,bhse->bhde', wk, k, v)
    n_new = dec[..., None] * n + jnp.einsum('bhs,bhsd->bhd', wk, k)
    return h, C_new, n_new, m_new


def _gla_chunk(q, k, v, loga, S):
    L = q.shape[2]
    causal = jnp.tril(jnp.ones((L, L), dtype=bool))[:, :, None]
    bc = jnp.cumsum(loga, axis=2)
    diff = bc[:, :, :, None, :] - bc[:, :, None, :, :]
    diff = jnp.where(causal, diff, -jnp.inf)
    a = jnp.einsum('bhtd,bhsd,bhtsd->bhts', q, k, jnp.exp(diff))
    o = jnp.einsum('bhtd,bhde->bhte', q * jnp.exp(bc), S) + jnp.einsum('bhts,bhse->bhte', a, v)
    last = bc[:, :, -1:, :]
    S_new = jnp.exp(last[:, :, 0, :])[..., None] * S + jnp.einsum('bhsd,bhse->bhde', k * jnp.exp(last - bc), v)
    return o, S_new


def _mlstm(q, k, v, itil, logf, C, n, m):
    L = q.shape[2]
    ch = CHUNK if L % CHUNK == 0 else L
    xs = tuple(_to_chunks(a, ch) for a in (q, k, v, itil, logf))

    def step(carry, inp):
        h, C2, n2, m2 = _mlstm_chunk(*inp, *carry)
        return (C2, n2, m2), h

    (C, n, m), h = lax.scan(step, (C, n, m), xs)
    return _from_chunks(h), C, n, m


def _gla(q, k, v, loga, S):
    L = q.shape[2]
    ch = CHUNK if L % CHUNK == 0 else L
    xs = tuple(_to_chunks(a, ch) for a in (q, k, v, loga))

    def step(S_c, inp):
        o, S2 = _gla_chunk(*inp, S_c)
        return S2, o

    S, o = lax.scan(step, S, xs)
    return _from_chunks(o), S


def _layer(x, conv_buf, C, n, m, S, w_in, conv_w, conv_b, b_i, b_f, a_norm_g,
           w_gate_up, b_gate, b_norm_g, w_pa, w_pb, w_out, ln_g, ln_b):
    f32 = jnp.float32
    bt, L, _ = x.shape
    p = x @ w_in
    (qk_pre, a_v, a_i, a_f, a_o, a_z, b_q, b_k, b_v, b_g, b_z, g_a, g_b) = _split_cols(p)
    ext = jnp.concatenate([conv_buf.astype(qk_pre.dtype), qk_pre], axis=1)
    conv = conv_b
    for j in range(CONV_W):
        conv = conv + ext[:, j:j + L] * conv_w[j]
    new_buf = ext[:, L:]
    qk = jax.nn.silu(conv)
    a_q, a_k = jnp.split(qk, 2, axis=-1)
    q = _heads(a_q, A_HEADS).astype(f32)
    k = _heads(a_k, A_HEADS).astype(f32) * (A_DK ** -0.5)
    v = _heads(a_v, A_HEADS).astype(f32)
    itil = (a_i + b_i).astype(f32).transpose(0, 2, 1)
    logf = jax.nn.log_sigmoid((a_f + b_f).astype(f32)).transpose(0, 2, 1)
    h, C_new, n_new, m_new = _mlstm(q, k, v, itil, logf, C.astype(f32), n.astype(f32), m.astype(f32))
    h = h.transpose(0, 2, 1, 3)
    mu = jnp.mean(h, axis=-1, keepdims=True)
    var = jnp.mean(jnp.square(h - mu), axis=-1, keepdims=True)
    hn = (h - mu) * lax.rsqrt(var + LN_EPS) * a_norm_g.astype(f32).reshape(A_HEADS, A_DV)
    ya = hn.reshape(bt, L, A_WIDTH) * jax.nn.sigmoid(a_o.astype(f32)) * jax.nn.silu(a_z.astype(f32))
    gq = _heads(b_q, B_HEADS).astype(f32) * (B_DK ** -0.5)
    gk = _heads(b_k, B_HEADS).astype(f32)
    gv = _heads(b_v, B_HEADS).astype(f32)
    loga = jax.nn.log_sigmoid((b_g @ w_gate_up + b_gate).astype(f32)) / GATE_TAU
    loga = _heads(loga, B_HEADS)
    o, S_new = _gla(gq, gk, gv, loga, S.astype(f32))
    o = o.transpose(0, 2, 1, 3)
    on = o * lax.rsqrt(jnp.mean(jnp.square(o), axis=-1, keepdims=True) + LN_EPS)
    on = on * b_norm_g.astype(f32).reshape(B_HEADS, B_DV)
    yb = on.reshape(bt, L, B_WIDTH) * jax.nn.silu(b_z.astype(f32))
    ya = ya.astype(x.dtype) @ w_pa
    yb = yb.astype(x.dtype) @ w_pb
    merged = jax.nn.sigmoid(g_a) * ya + jax.nn.sigmoid(g_b) * yb
    y = merged @ w_out
    out = _layernorm(ALPHA * x + y, ln_g, ln_b).astype(x.dtype)
    dt = x.dtype
    return out, C_new.astype(dt), n_new.astype(dt), m_new.astype(dt), new_buf.astype(dt), S_new.astype(dt)


def setup_inputs(seed: int = 0) -> dict:
    key = jax.random.key(seed)
    ks = jax.random.split(key, 24)
    nrm = jax.random.normal
    f32 = jnp.float32
    return {
        "x_prompt": nrm(ks[0], (BATCH, SEQ, D_MODEL), f32),
        "x_sample": nrm(ks[1], (DEC_BATCH, DEC_SEQ, D_MODEL), f32),
        "state_mlstm_C": 0.1 * nrm(ks[2], (DEPTH, DEC_BATCH, A_HEADS, A_DK, A_DV), f32),
        "state_mlstm_n": 0.1 * nrm(ks[3], (DEPTH, DEC_BATCH, A_HEADS, A_DK), f32),
        "state_mlstm_m": nrm(ks[4], (DEPTH, DEC_BATCH, A_HEADS), f32),
        "state_conv": nrm(ks[5], (DEPTH, DEC_BATCH, CONV_W - 1, 2 * A_WIDTH), f32),
        "state_gla_S": 0.1 * nrm(ks[6], (DEPTH, DEC_BATCH, B_HEADS, B_DK, B_DV), f32),
        "w_in": nrm(ks[7], (DEPTH, D_MODEL, N_IN), f32) * D_MODEL ** -0.5,
        "conv_w": nrm(ks[8], (DEPTH, CONV_W, 2 * A_WIDTH), f32) * CONV_W ** -0.5,
        "conv_b": 0.01 * nrm(ks[9], (DEPTH, 2 * A_WIDTH), f32),
        "b_i": 0.1 * nrm(ks[10], (DEPTH, A_HEADS), f32),
        "b_f": 3.0 + 0.1 * nrm(ks[11], (DEPTH, A_HEADS), f32),
        "a_norm_g": 1.0 + 0.02 * nrm(ks[12], (DEPTH, A_WIDTH), f32),
        "w_gate_up": nrm(ks[13], (DEPTH, GATE_RANK, B_KWIDTH), f32) * GATE_RANK ** -0.5,
        "b_gate": 0.01 * nrm(ks[14], (DEPTH, B_KWIDTH), f32),
        "b_norm_g": 1.0 + 0.02 * nrm(ks[15], (DEPTH, B_WIDTH), f32),
        "w_pa": nrm(ks[16], (DEPTH, A_WIDTH, D_MODEL), f32) * (A_WIDTH ** -0.5 * BETA),
        "w_pb": nrm(ks[17], (DEPTH, B_WIDTH, D_MODEL), f32) * (B_WIDTH ** -0.5 * BETA),
        "w_out": nrm(ks[18], (DEPTH, D_MODEL, D_MODEL), f32) * (D_MODEL ** -0.5 * BETA),
        "ln_g": 1.0 + 0.02 * nrm(ks[19], (DEPTH, D_MODEL), f32),
        "ln_b": 0.01 * nrm(ks[20], (DEPTH, D_MODEL), f32),
    }


def reference(x_prompt, x_sample, state_mlstm_C, state_mlstm_n, state_mlstm_m, state_conv,
              state_gla_S, w_in, conv_w, conv_b, b_i, b_f, a_norm_g, w_gate_up, b_gate,
              b_norm_g, w_pa, w_pb, w_out, ln_g, ln_b):
    f32 = jnp.float32
    xp = x_prompt
    xs = x_sample
    pC, pn, pm, pconv, pS = [], [], [], [], []
    sC, sn, sm, sconv, sS = [], [], [], [], []
    for d in range(DEPTH):
        params = (w_in[d], conv_w[d], conv_b[d], b_i[d], b_f[d], a_norm_g[d], w_gate_up[d],
                  b_gate[d], b_norm_g[d], w_pa[d], w_pb[d], w_out[d], ln_g[d], ln_b[d])
        C0 = jnp.zeros((BATCH, A_HEADS, A_DK, A_DV), f32)
        n0 = jnp.zeros((BATCH, A_HEADS, A_DK), f32)
        m0 = jnp.zeros((BATCH, A_HEADS), f32)
        buf0 = jnp.zeros((BATCH, CONV_W - 1, 2 * A_WIDTH), xp.dtype)
        S0 = jnp.zeros((BATCH, B_HEADS, B_DK, B_DV), f32)
        xp, c1, c2, c3, c4, c5 = _layer(xp, buf0, C0, n0, m0, S0, *params)
        pC.append(c1); pn.append(c2); pm.append(c3); pconv.append(c4); pS.append(c5)
        xs, e1, e2, e3, e4, e5 = _layer(xs, state_conv[d], state_mlstm_C[d], state_mlstm_n[d],
                                        state_mlstm_m[d], state_gla_S[d], *params)
        sC.append(e1); sn.append(e2); sm.append(e3); sconv.append(e4); sS.append(e5)
    return (xp, xs,
            jnp.stack(pC), jnp.stack(pn), jnp.stack(pm), jnp.stack(pconv), jnp.stack(pS),
            jnp.stack(sC), jnp.stack(sn), jnp.stack(sm), jnp.stack(sconv), jnp.stack(sS))
```

```python
import functools

import jax
import jax.numpy as jnp
from jax import lax
from jax.experimental import pallas as pl
from jax.experimental.pallas import tpu as pltpu

F32 = jnp.float32
BF16 = jnp.bfloat16

D_MODEL = 2048
A_HEADS = 4
A_WIDTH = 1024
A_DK = 256
A_DV = 256
CONV_W = 4
B_HEADS = 4
B_WIDTH = 1024
B_KWIDTH = 512
B_DK = 128
B_DV = 256
GATE_RANK = 16
GATE_TAU = 16.0
ALPHA = 2.0 ** 0.25
LN_EPS = 1e-5

LANES = 128
SUBLANES = 8
VMEM_LIMIT = 48 * 1024 * 1024

COL_Q = 0
COL_K = A_WIDTH
COL_AV = 2 * A_WIDTH
COL_AO = 3 * A_WIDTH
COL_AZ = 4 * A_WIDTH
COL_BQ = 5 * A_WIDTH
COL_BK = COL_BQ + B_KWIDTH
COL_BV = COL_BK + B_KWIDTH
COL_BZ = COL_BV + B_WIDTH
COL_GA = COL_BZ + B_WIDTH
COL_GB = COL_GA + D_MODEL
COL_GATE = COL_GB + D_MODEL
N_PROJ = COL_GATE + A_HEADS * LANES

MLSTM_CHUNK = 256
GLA_STEP = 256
GLA_BLOCK = 16


def _dot(a, b):
    return jnp.dot(a, b, preferred_element_type=F32)


def _dot_nt(a, b):
    return lax.dot_general(a, b, (((1,), (1,)), ((), ())), preferred_element_type=F32)


def _dot_tn(a, b):
    return lax.dot_general(a, b, (((0,), (0,)), ((), ())), preferred_element_type=F32)


def _mask_dot(mask_bf16, x):
    hi = x.astype(BF16)
    r1 = x - hi.astype(F32)
    mid = r1.astype(BF16)
    lo = (r1 - mid.astype(F32)).astype(BF16)
    return _dot(mask_bf16, hi) + _dot(mask_bf16, mid) + _dot(mask_bf16, lo)


def _log_sigmoid(z):
    return -(jnp.maximum(-z, 0.0) + jnp.log1p(jnp.exp(-jnp.abs(z))))


def _sigmoid(z):
    return 1.0 / (1.0 + jnp.exp(-z))


def _silu(z):
    return z * _sigmoid(z)


def _proj_kernel(x_ref, w_ref, o_ref, xb_ref):
    @pl.when(pl.program_id(1) == 0)
    def _():
        xb_ref[...] = x_ref[...].astype(BF16)

    o_ref[...] = _dot(xb_ref[...], w_ref[...])


def _proj(x, w, tm, tn):
    t, k = x.shape
    n = w.shape[1]
    return pl.pallas_call(
        _proj_kernel,
        grid=(t // tm, n // tn),
        in_specs=[pl.BlockSpec((tm, k), lambda i, j: (i, 0)),
                  pl.BlockSpec((k, tn), lambda i, j: (0, j))],
        out_specs=pl.BlockSpec((tm, tn), lambda i, j: (i, j)),
        out_shape=jax.ShapeDtypeStruct((t, n), F32),
        scratch_shapes=[pltpu.VMEM((tm, k), BF16)],
        compiler_params=pltpu.CompilerParams(
            dimension_semantics=("parallel", "arbitrary"), vmem_limit_bytes=VMEM_LIMIT),
        name="proj",
    )(x, w)


def _mlstm_kernel(qp_ref, kp_ref, v_ref, ao_ref, az_ref, g_ref, cwq_ref, cwk_ref, cbq_ref, cbk_ref,
                  gb_ref, ng_ref, ya_ref, c_ref, n_ref, m_ref, extq_ref, extk_ref):
    L = MLSTM_CHUNK
    c = pl.program_id(2)

    @pl.when(c == 0)
    def _():
        c_ref[...] = jnp.zeros_like(c_ref)
        n_ref[...] = jnp.zeros_like(n_ref)
        m_ref[...] = jnp.zeros_like(m_ref)
        extq_ref[L:L + SUBLANES, :] = jnp.zeros((SUBLANES, A_DK), F32)
        extk_ref[L:L + SUBLANES, :] = jnp.zeros((SUBLANES, A_DK), F32)

    def conv_silu(x_ref, ext_ref, w_ref, b_ref):
        ext_ref[0:SUBLANES, :] = ext_ref[L:L + SUBLANES, :]
        ext_ref[SUBLANES:SUBLANES + L, :] = x_ref[...]
        acc = b_ref[...] + ext_ref[SUBLANES - 3:SUBLANES - 3 + L, :] * w_ref[0:1, :]
        for j in range(1, CONV_W):
            acc = acc + ext_ref[SUBLANES - 3 + j:SUBLANES - 3 + j + L, :] * w_ref[j:j + 1, :]
        return _silu(acc)

    q = conv_silu(qp_ref, extq_ref, cwq_ref, cbq_ref)
    k = conv_silu(kp_ref, extk_ref, cwk_ref, cbk_ref) * (A_DK ** -0.5)
    v = v_ref[...]

    g = g_ref[...] + gb_ref[...]
    lane = lax.broadcasted_iota(jnp.int32, (L, LANES), 1)
    g2 = jnp.where(lane == 1, _log_sigmoid(g), g)
    row = lax.broadcasted_iota(jnp.int32, (L, L), 0)
    col = lax.broadcasted_iota(jnp.int32, (L, L), 1)
    causal = row >= col
    cum = _mask_dot(causal.astype(BF16), g2)
    x2 = jnp.where(lane == 1, cum, g2)
    x2t = x2.T
    itil_col, b_col = x2[:, 0:1], x2[:, 1:2]
    itil_row, b_row = x2t[0:1, :], x2t[1:2, :]

    m_prev = m_ref[0, 0, 0:1, 0:1]
    dmat = jnp.where(causal, b_col - b_row + itil_row, -jnp.inf)
    inter = b_col + m_prev
    m_t = jnp.maximum(inter, jnp.max(dmat, axis=1, keepdims=True))
    w = jnp.exp(dmat - m_t)
    decay = jnp.exp(inter - m_t)

    qb, kb, vb = q.astype(BF16), k.astype(BF16), v.astype(BF16)
    c_old = c_ref[0, 0]
    n_old = n_ref[0, 0]
    s = _dot_nt(qb, kb) * w
    num = decay * _dot(qb, c_old.astype(BF16)) + _dot(s.astype(BF16), vb)
    den = decay * jnp.sum(q * n_old, axis=1, keepdims=True) + jnp.sum(s, axis=1, keepdims=True)
    h = num / jnp.maximum(jnp.abs(den), jnp.exp(-m_t))

    m_new = m_t[L - 1:L, :]
    b_last = b_col[L - 1:L, :]
    wk = jnp.exp(b_last - b_col + itil_col - m_new)
    dec = jnp.exp(b_last + m_prev - m_new)
    kw = k * wk
    c_ref[0, 0] = dec * c_old + _dot(kw.T.astype(BF16), vb)
    n_ref[0, 0] = dec * n_old + jnp.sum(kw, axis=0, keepdims=True)
    m_ref[0, 0] = jnp.broadcast_to(m_new, (1, LANES))

    mu = jnp.mean(h, axis=1, keepdims=True)
    hc = h - mu
    var = jnp.mean(hc * hc, axis=1, keepdims=True)
    hn = hc * lax.rsqrt(var + LN_EPS) * ng_ref[...]
    ya_ref[...] = (hn * _sigmoid(ao_ref[...]) * _silu(az_ref[...])).astype(ya_ref.dtype)


def _mlstm(p_all, batch, seq, conv_w, conv_b, gate_bias, a_norm_g):
    L = MLSTM_CHUNK
    nc = seq // L
    t = batch * seq
    qblk = A_DK // A_DK

    def rows(b, h, c):
        return b * nc + c

    def pcol(base):
        return pl.BlockSpec((L, A_DK), lambda b, h, c: (rows(b, h, c), base // A_DK + h))

    in_specs = [
        pcol(COL_Q), pcol(COL_K), pcol(COL_AV), pcol(COL_AO), pcol(COL_AZ),
        pl.BlockSpec((L, LANES), lambda b, h, c: (rows(b, h, c), COL_GATE // LANES + h)),
        pl.BlockSpec((CONV_W, A_DK), lambda b, h, c: (0, h)),
        pl.BlockSpec((CONV_W, A_DK), lambda b, h, c: (0, A_HEADS + h)),
        pl.BlockSpec((1, A_DK), lambda b, h, c: (0, h)),
        pl.BlockSpec((1, A_DK), lambda b, h, c: (0, A_HEADS + h)),
        pl.BlockSpec((1, LANES), lambda b, h, c: (0, h)),
        pl.BlockSpec((1, A_DV), lambda b, h, c: (0, h)),
    ]
    out_specs = [
        pl.BlockSpec((L, A_DV), lambda b, h, c: (rows(b, h, c), h)),
        pl.BlockSpec((1, 1, A_DK, A_DV), lambda b, h, c: (b, h, 0, 0)),
        pl.BlockSpec((1, 1, 1, A_DK), lambda b, h, c: (b, h, 0, 0)),
        pl.BlockSpec((1, 1, 1, LANES), lambda b, h, c: (b, h, 0, 0)),
    ]
    out_shape = [
        jax.ShapeDtypeStruct((t, A_WIDTH), BF16),
        jax.ShapeDtypeStruct((batch, A_HEADS, A_DK, A_DV), F32),
        jax.ShapeDtypeStruct((batch, A_HEADS, 1, A_DK), F32),
        jax.ShapeDtypeStruct((batch, A_HEADS, 1, LANES), F32),
    ]
    del qblk
    return pl.pallas_call(
        _mlstm_kernel,
        grid=(batch, A_HEADS, nc),
        in_specs=in_specs,
        out_specs=out_specs,
        out_shape=out_shape,
        scratch_shapes=[pltpu.VMEM((L + 2 * SUBLANES, A_DK), F32),
                        pltpu.VMEM((L + 2 * SUBLANES, A_DK), F32)],
        compiler_params=pltpu.CompilerParams(
            dimension_semantics=("parallel", "parallel", "arbitrary"), vmem_limit_bytes=VMEM_LIMIT),
        name="mlstm",
    )(p_all, p_all, p_all, p_all, p_all, p_all, conv_w, conv_w, conv_b, conv_b, gate_bias, a_norm_g)


def _gla_gate_log(bg_tile, wg_ref, bgate_ref):
    lr = bg_tile[:, 2:2 + GATE_RANK]
    z = _dot(lr.astype(BF16), wg_ref[...].astype(BF16)) + bgate_ref[...]
    return _log_sigmoid(z) / GATE_TAU


def _gla_kernel(q_ref, k_ref, v_ref, bz_ref, g_ref, wg_ref, bgate_ref, ng_ref,
                yb_ref, s_out_ref, st_ref, qs_ref, qt_ref, kt_ref, bc_ref, eb_ref, o_ref):
    L = GLA_STEP
    nb = L // GLA_BLOCK
    c = pl.program_id(2)

    @pl.when(c == 0)
    def _():
        st_ref[...] = jnp.zeros_like(st_ref)

    loga = _gla_gate_log(g_ref[...], wg_ref, bgate_ref)
    row = lax.broadcasted_iota(jnp.int32, (L, L), 0)
    col = lax.broadcasted_iota(jnp.int32, (L, L), 1)
    same = (row // GLA_BLOCK) == (col // GLA_BLOCK)
    bc = _mask_dot((same & (row >= col)).astype(BF16), loga)
    bl = _mask_dot(same.astype(BF16), loga)
    qs = q_ref[...] * (B_DK ** -0.5)
    qs_ref[...] = qs
    qt_ref[...] = qs * jnp.exp(bc)
    kt_ref[...] = k_ref[...] * jnp.exp(bl - bc)
    bc_ref[...] = bc
    eb_ref[...] = jnp.exp(bl)

    srow = lax.broadcasted_iota(jnp.int32, (GLA_BLOCK, 1), 0)
    lane_b = lax.broadcasted_iota(jnp.int32, (GLA_BLOCK, GLA_BLOCK), 1)

    def block(j, carry):
        r = pl.multiple_of(j * GLA_BLOCK, GLA_BLOCK)
        sl = pl.ds(r, GLA_BLOCK)
        st = st_ref[...]
        vb = v_ref[sl, :].astype(BF16)
        o_inter = _dot_nt(qt_ref[sl, :].astype(BF16), st.astype(BF16))
        qj, kj, bcj = qs_ref[sl, :], k_ref[sl, :], bc_ref[sl, :]
        a = jnp.zeros((GLA_BLOCK, GLA_BLOCK), F32)
        for s in range(GLA_BLOCK):
            e = jnp.exp(jnp.minimum(bcj - bcj[s:s + 1, :], 0.0))
            a_col = jnp.sum(qj * kj[s:s + 1, :] * e, axis=1, keepdims=True)
            a_col = jnp.where(srow >= s, a_col, 0.0)
            a = jnp.where(lane_b == s, a_col, a)
        o_ref[sl, :] = o_inter + _dot(a.astype(BF16), vb)
        st_ref[...] = st * eb_ref[pl.ds(r, 1), :] + _dot_tn(vb, kt_ref[sl, :].astype(BF16))
        return carry

    lax.fori_loop(0, nb, block, 0)

    o = o_ref[...]
    on = o * lax.rsqrt(jnp.mean(o * o, axis=1, keepdims=True) + LN_EPS) * ng_ref[...]
    yb_ref[...] = (on * _silu(bz_ref[...])).astype(yb_ref.dtype)

    @pl.when(c == pl.num_programs(2) - 1)
    def _():
        s_out_ref[0, 0] = st_ref[...].T


def _gla(p_all, batch, seq, w_gate_up, b_gate, b_norm_g):
    L = GLA_STEP
    nc = seq // L
    t = batch * seq

    def rows(b, h, c):
        return b * nc + c

    in_specs = [
        pl.BlockSpec((L, B_DK), lambda b, h, c: (rows(b, h, c), COL_BQ // B_DK + h)),
        pl.BlockSpec((L, B_DK), lambda b, h, c: (rows(b, h, c), COL_BK // B_DK + h)),
        pl.BlockSpec((L, B_DV), lambda b, h, c: (rows(b, h, c), COL_BV // B_DV + h)),
        pl.BlockSpec((L, B_DV), lambda b, h, c: (rows(b, h, c), COL_BZ // B_DV + h)),
        pl.BlockSpec((L, LANES), lambda b, h, c: (rows(b, h, c), COL_GATE // LANES + h)),
        pl.BlockSpec((GATE_RANK, B_DK), lambda b, h, c: (0, h)),
        pl.BlockSpec((1, B_DK), lambda b, h, c: (0, h)),
        pl.BlockSpec((1, B_DV), lambda b, h, c: (0, h)),
    ]
    out_specs = [
        pl.BlockSpec((L, B_DV), lambda b, h, c: (rows(b, h, c), h)),
        pl.BlockSpec((1, 1, B_DK, B_DV), lambda b, h, c: (b, h, 0, 0)),
    ]
    out_shape = [
        jax.ShapeDtypeStruct((t, B_WIDTH), BF16),
        jax.ShapeDtypeStruct((batch, B_HEADS, B_DK, B_DV), F32),
    ]
    return pl.pallas_call(
        _gla_kernel,
        grid=(batch, B_HEADS, nc),
        in_specs=in_specs,
        out_specs=out_specs,
        out_shape=out_shape,
        scratch_shapes=[pltpu.VMEM((B_DV, B_DK), F32),
                        pltpu.VMEM((L, B_DK), F32),
                        pltpu.VMEM((L, B_DK), F32),
                        pltpu.VMEM((L, B_DK), F32),
                        pltpu.VMEM((L, B_DK), F32),
                        pltpu.VMEM((L, B_DK), F32),
                        pltpu.VMEM((L, B_DV), F32)],
        compiler_params=pltpu.CompilerParams(
            dimension_semantics=("parallel", "parallel", "arbitrary"), vmem_limit_bytes=VMEM_LIMIT),
        name="gla",
    )(p_all, p_all, p_all, p_all, p_all, w_gate_up, b_gate, b_norm_g)


DEC_TOKENS = 8


def _decode_kernel(qp_ref, kp_ref, av_ref, ao_ref, az_ref, bq_ref, bk_ref, bv_ref, bz_ref, g_ref,
                   sq0_ref, sq1_ref, sq2_ref, sk0_ref, sk1_ref, sk2_ref,
                   cwq_ref, cwk_ref, cbq_ref, cbk_ref, gb_ref, ang_ref, wg_ref, bgate_ref, bng_ref,
                   c_ref, n_ref, m_ref, s_ref,
                   ya_ref, yb_ref, c_out_ref, n_out_ref, m_out_ref, s_out_ref):
    TB = DEC_TOKENS
    h_idx = pl.program_id(1)

    def conv_silu(s0, s1, s2, x, w_ref, b_ref):
        acc = b_ref[...] + s0[...] * w_ref[0:1, :]
        acc = acc + s1[...] * w_ref[1:2, :]
        acc = acc + s2[...] * w_ref[2:3, :]
        acc = acc + x[...] * w_ref[3:4, :]
        return _silu(acc)

    q = conv_silu(sq0_ref, sq1_ref, sq2_ref, qp_ref, cwq_ref, cbq_ref)
    k = conv_silu(sk0_ref, sk1_ref, sk2_ref, kp_ref, cwk_ref, cbk_ref) * (A_DK ** -0.5)
    v = av_ref[...]
    g = g_ref[...]
    gbias = gb_ref[...]
    itil = g[:, 0:1] + gbias[:, 0:1]
    logf = _log_sigmoid(g[:, 1:2] + gbias[:, 1:2])
    lane_h = lax.broadcasted_iota(jnp.int32, (TB, A_HEADS), 1)
    m_prev = jnp.sum(jnp.where(lane_h == h_idx, m_ref[...], 0.0), axis=1, keepdims=True)
    inter = logf + m_prev
    m_t = jnp.maximum(inter, itil)
    w = jnp.exp(itil - m_t)
    decay = jnp.exp(inter - m_t)
    n_old = n_ref[:, 0, 0, :]
    s = jnp.sum(q * k, axis=1, keepdims=True) * w
    den = decay * jnp.sum(q * n_old, axis=1, keepdims=True) + s
    scale = 1.0 / jnp.maximum(jnp.abs(den), jnp.exp(-m_t))
    kw = k * w
    n_out_ref[:, 0, 0, :] = decay * n_old + kw
    m_out_ref[...] = jnp.broadcast_to(m_t, (TB, LANES))

    rows8 = lax.broadcasted_iota(jnp.int32, (SUBLANES, 1), 0)
    qb, kwb, vb = q.astype(BF16), kw.astype(BF16), v.astype(BF16)
    h_rows = []
    for t in range(TB):
        c_old = c_ref[t, 0]
        sel = rows8 == t
        qc = _dot(qb, c_old.astype(BF16))[t:t + 1, :]
        h_rows.append((decay[t:t + 1, :] * qc + s[t:t + 1, :] * v[t:t + 1, :]) * scale[t:t + 1, :])
        outer = _dot_tn(jnp.where(sel, kwb, jnp.zeros_like(kwb)), vb)
        c_out_ref[t, 0] = decay[t:t + 1, :] * c_old + outer
    h = jnp.concatenate(h_rows, axis=0)
    mu = jnp.mean(h, axis=1, keepdims=True)
    hc = h - mu
    var = jnp.mean(hc * hc, axis=1, keepdims=True)
    hn = hc * lax.rsqrt(var + LN_EPS) * ang_ref[...]
    ya_ref[...] = (hn * _sigmoid(ao_ref[...]) * _silu(az_ref[...])).astype(ya_ref.dtype)

    gq = bq_ref[...] * (B_DK ** -0.5)
    gk = bk_ref[...]
    gv = bv_ref[...]
    loga = _gla_gate_log(g, wg_ref, bgate_ref)
    eb = jnp.exp(loga)
    a = jnp.sum(gq * gk, axis=1, keepdims=True)
    ebt = jnp.concatenate([eb, jnp.zeros((LANES - TB, B_DK), F32)], axis=0).T
    qeb, gkb, gvb = (gq * eb).astype(BF16), gk.astype(BF16), gv.astype(BF16)
    o_rows = []
    for t in range(TB):
        s_old = s_ref[t, 0]
        sel = rows8 == t
        o_rows.append(_dot(qeb, s_old.astype(BF16))[t:t + 1, :] + a[t:t + 1, :] * gv[t:t + 1, :])
        outer = _dot_tn(jnp.where(sel, gkb, jnp.zeros_like(gkb)), gvb)
        s_out_ref[t, 0] = ebt[:, t:t + 1] * s_old + outer
    o = jnp.concatenate(o_rows, axis=0)
    on = o * lax.rsqrt(jnp.mean(o * o, axis=1, keepdims=True) + LN_EPS) * bng_ref[...]
    yb_ref[...] = (on * _silu(bz_ref[...])).astype(yb_ref.dtype)


def _decode(p_s, conv_state, conv_w, conv_b, gate_bias, a_norm_g, w_gate_up, b_gate, b_norm_g,
            c_state, n_state, m_state, s_state):
    nseq = p_s.shape[0]
    TB = DEC_TOKENS

    def pcol(base, width):
        return pl.BlockSpec((TB, width), lambda i, h: (i, base // width + h))

    def cstate(j, base):
        return pl.BlockSpec((TB, A_DK), lambda i, h: (i, (j * 2 * A_WIDTH + base) // A_DK + h))

    in_specs = [
        pcol(COL_Q, A_DK), pcol(COL_K, A_DK), pcol(COL_AV, A_DV), pcol(COL_AO, A_DV), pcol(COL_AZ, A_DV),
        pcol(COL_BQ, B_DK), pcol(COL_BK, B_DK), pcol(COL_BV, B_DV), pcol(COL_BZ, B_DV),
        pcol(COL_GATE, LANES),
        cstate(0, COL_Q), cstate(1, COL_Q), cstate(2, COL_Q),
        cstate(0, COL_K), cstate(1, COL_K), cstate(2, COL_K),
        pl.BlockSpec((CONV_W, A_DK), lambda i, h: (0, h)),
        pl.BlockSpec((CONV_W, A_DK), lambda i, h: (0, A_HEADS + h)),
        pl.BlockSpec((1, A_DK), lambda i, h: (0, h)),
        pl.BlockSpec((1, A_DK), lambda i, h: (0, A_HEADS + h)),
        pl.BlockSpec((1, LANES), lambda i, h: (0, h)),
        pl.BlockSpec((1, A_DV), lambda i, h: (0, h)),
        pl.BlockSpec((GATE_RANK, B_DK), lambda i, h: (0, h)),
        pl.BlockSpec((1, B_DK), lambda i, h: (0, h)),
        pl.BlockSpec((1, B_DV), lambda i, h: (0, h)),
        pl.BlockSpec((TB, 1, A_DK, A_DV), lambda i, h: (i, h, 0, 0)),
        pl.BlockSpec((TB, 1, 1, A_DK), lambda i, h: (i, h, 0, 0)),
        pl.BlockSpec((TB, A_HEADS), lambda i, h: (i, 0)),
        pl.BlockSpec((TB, 1, B_DK, B_DV), lambda i, h: (i, h, 0, 0)),
    ]
    out_specs = [
        pl.BlockSpec((TB, A_DV), lambda i, h: (i, h)),
        pl.BlockSpec((TB, B_DV), lambda i, h: (i, h)),
        pl.BlockSpec((TB, 1, A_DK, A_DV), lambda i, h: (i, h, 0, 0)),
        pl.BlockSpec((TB, 1, 1, A_DK), lambda i, h: (i, h, 0, 0)),
        pl.BlockSpec((TB, LANES), lambda i, h: (i, h)),
        pl.BlockSpec((TB, 1, B_DK, B_DV), lambda i, h: (i, h, 0, 0)),
    ]
    out_shape = [
        jax.ShapeDtypeStruct((nseq, A_WIDTH), BF16),
        jax.ShapeDtypeStruct((nseq, B_WIDTH), BF16),
        jax.ShapeDtypeStruct((nseq, A_HEADS, A_DK, A_DV), F32),
        jax.ShapeDtypeStruct((nseq, A_HEADS, 1, A_DK), F32),
        jax.ShapeDtypeStruct((nseq, A_HEADS * LANES), F32),
        jax.ShapeDtypeStruct((nseq, B_HEADS, B_DK, B_DV), F32),
    ]
    return pl.pallas_call(
        _decode_kernel,
        grid=(nseq // TB, A_HEADS),
        in_specs=in_specs,
        out_specs=out_specs,
        out_shape=out_shape,
        compiler_params=pltpu.CompilerParams(
            dimension_semantics=("parallel", "parallel"), vmem_limit_bytes=VMEM_LIMIT),
        name="decode",
    )(p_s, p_s, p_s, p_s, p_s, p_s, p_s, p_s, p_s, p_s,
      conv_state, conv_state, conv_state, conv_state, conv_state, conv_state,
      conv_w, conv_w, conv_b, conv_b, gate_bias, a_norm_g, w_gate_up, b_gate, b_norm_g,
      c_state, n_state, m_state, s_state)


def _out_kernel(ya_ref, yb_ref, ga_ref, gb_ref, x_ref, wpa_ref, wpb_ref, wo_ref, lng_ref, lnb_ref, o_ref):
    pa = _dot(ya_ref[...], wpa_ref[...])
    pb = _dot(yb_ref[...], wpb_ref[...])
    merged = _sigmoid(ga_ref[...]) * pa + _sigmoid(gb_ref[...]) * pb
    y = _dot(merged.astype(BF16), wo_ref[...])
    r = ALPHA * x_ref[...] + y
    mu = jnp.mean(r, axis=1, keepdims=True)
    rc = r - mu
    var = jnp.mean(rc * rc, axis=1, keepdims=True)
    o_ref[...] = rc * lax.rsqrt(var + LN_EPS) * lng_ref[...] + lnb_ref[...]


def _out(ya, yb, p_all, x, w_pa, w_pb, w_out, ln_g, ln_b, tm):
    t = x.shape[0]
    const = lambda i: (0, 0)
    single = pl.Buffered(1)
    in_specs = [
        pl.BlockSpec((tm, A_WIDTH), lambda i: (i, 0)),
        pl.BlockSpec((tm, B_WIDTH), lambda i: (i, 0)),
        pl.BlockSpec((tm, D_MODEL), lambda i: (i, COL_GA // D_MODEL)),
        pl.BlockSpec((tm, D_MODEL), lambda i: (i, COL_GB // D_MODEL)),
        pl.BlockSpec((tm, D_MODEL), lambda i: (i, 0)),
        pl.BlockSpec((A_WIDTH, D_MODEL), const, pipeline_mode=single),
        pl.BlockSpec((B_WIDTH, D_MODEL), const, pipeline_mode=single),
        pl.BlockSpec((D_MODEL, D_MODEL), const, pipeline_mode=single),
        pl.BlockSpec((1, D_MODEL), const),
        pl.BlockSpec((1, D_MODEL), const),
    ]
    return pl.pallas_call(
        _out_kernel,
        grid=(t // tm,),
        in_specs=in_specs,
        out_specs=pl.BlockSpec((tm, D_MODEL), lambda i: (i, 0)),
        out_shape=jax.ShapeDtypeStruct((t, D_MODEL), F32),
        compiler_params=pltpu.CompilerParams(
            dimension_semantics=("parallel",), vmem_limit_bytes=VMEM_LIMIT),
        name="outproj",
    )(ya, yb, p_all, p_all, x, w_pa, w_pb, w_out, ln_g, ln_b)


def _relayout_w_in(w_in):
    s_qk = 0
    s_av = 2 * A_WIDTH
    s_i = s_av + A_WIDTH
    s_f = s_i + A_HEADS
    s_ao = s_f + A_HEADS
    s_bg = s_ao + 2 * A_WIDTH + 2 * B_KWIDTH + B_WIDTH
    s_bz = s_bg + GATE_RANK
    main = [w_in[:, s_qk:s_i], w_in[:, s_ao:s_bg], w_in[:, s_bz:]]
    gate_cols = []
    pad = jnp.zeros((D_MODEL, LANES - 2 - GATE_RANK), w_in.dtype)
    for h in range(A_HEADS):
        gate_cols += [w_in[:, s_i + h:s_i + h + 1], w_in[:, s_f + h:s_f + h + 1],
                      w_in[:, s_bg:s_bg + GATE_RANK], pad]
    return jnp.concatenate(main + gate_cols, axis=1).astype(BF16)


def kernel(x_prompt, x_sample, state_mlstm_C, state_mlstm_n, state_mlstm_m, state_conv, state_gla_S,
           w_in, conv_w, conv_b, b_i, b_f, a_norm_g, w_gate_up, b_gate, b_norm_g, w_pa, w_pb, w_out,
           ln_g, ln_b):
    batch, seq, _ = x_prompt.shape
    nseq = x_sample.shape[0]
    d = 0

    wp = _relayout_w_in(w_in[d])
    wpa, wpb, wo = w_pa[d].astype(BF16), w_pb[d].astype(BF16), w_out[d].astype(BF16)
    cw = conv_w[d]
    cb = conv_b[d][None, :]
    gate_bias = jnp.zeros((A_HEADS, LANES), F32).at[:, 0].set(b_i[d]).at[:, 1].set(b_f[d]).reshape(1, -1)
    ang = a_norm_g[d][None, :]
    bng = b_norm_g[d][None, :]
    wg = w_gate_up[d]
    bgate = b_gate[d][None, :]
    lng, lnb = ln_g[d][None, :], ln_b[d][None, :]

    xp = x_prompt.reshape(batch * seq, D_MODEL)
    p_all = _proj(xp, wp, tm=1024, tn=512)
    ya, p_c, p_n, p_m = _mlstm(p_all, batch, seq, cw, cb, gate_bias, ang)
    yb, p_s = _gla(p_all, batch, seq, wg, bgate, bng)
    y_prompt = _out(ya, yb, p_all, xp, wpa, wpb, wo, lng, lnb, tm=256).reshape(batch, seq, D_MODEL)
    p_conv = p_all.reshape(batch, seq, N_PROJ)[:, seq - (CONV_W - 1):, :2 * A_WIDTH]

    xs = x_sample.reshape(nseq, D_MODEL)
    ps_all = _proj(xs, wp, tm=nseq, tn=512)
    conv_state = state_conv[d].reshape(nseq, (CONV_W - 1) * 2 * A_WIDTH)
    ya_s, yb_s, s_c, s_n, s_m, s_s = _decode(
        ps_all, conv_state, cw, cb, gate_bias, ang, wg, bgate, bng,
        state_mlstm_C[d], state_mlstm_n[d][:, :, None, :], state_mlstm_m[d], state_gla_S[d])
    y_sample = _out(ya_s, yb_s, ps_all, xs, wpa, wpb, wo, lng, lnb, tm=nseq).reshape(nseq, 1, D_MODEL)
    s_conv = jnp.concatenate([state_conv[d][:, 1:, :], ps_all[:, None, :2 * A_WIDTH]], axis=1)

    return (y_prompt, y_sample,
            p_c[None], p_n[:, :, 0, :][None], p_m[:, :, 0, 0][None], p_conv[None], p_s[None],
            s_c[None], s_n[:, :, 0, :][None], s_m.reshape(nseq, A_HEADS, LANES)[:, :, 0][None],
            s_conv[None], s_s[None])
```

```python
import functools

import jax
import jax.numpy as jnp
from jax import lax
from jax.experimental import pallas as pl
from jax.experimental.pallas import tpu as pltpu

F32 = jnp.float32
BF16 = jnp.bfloat16

D_MODEL = 2048
A_HEADS = 4
A_WIDTH = 1024
A_DK = 256
A_DV = 256
CONV_W = 4
B_HEADS = 4
B_WIDTH = 1024
B_KWIDTH = 512
B_DK = 128
B_DV = 256
GATE_RANK = 16
GATE_TAU = 16.0
ALPHA = 2.0 ** 0.25
LN_EPS = 1e-5

LANES = 128
SUBLANES = 8
VMEM_LIMIT = 48 * 1024 * 1024

COL_Q = 0
COL_K = A_WIDTH
COL_AV = 2 * A_WIDTH
COL_AO = 3 * A_WIDTH
COL_AZ = 4 * A_WIDTH
COL_BQ = 5 * A_WIDTH
COL_BK = COL_BQ + B_KWIDTH
COL_BV = COL_BK + B_KWIDTH
COL_BZ = COL_BV + B_WIDTH
COL_GA = COL_BZ + B_WIDTH
COL_GB = COL_GA + D_MODEL
COL_GATE = COL_GB + D_MODEL
N_PROJ = COL_GATE + A_HEADS * LANES

MLSTM_CHUNK = 256
GLA_STEP = 256
GLA_BLOCK = 16


def _dot(a, b):
    return jnp.dot(a, b, preferred_element_type=F32)


def _dot_nt(a, b):
    return lax.dot_general(a, b, (((1,), (1,)), ((), ())), preferred_element_type=F32)


def _dot_tn(a, b):
    return lax.dot_general(a, b, (((0,), (0,)), ((), ())), preferred_element_type=F32)


def _mask_dot(mask_bf16, x):
    hi = x.astype(BF16)
    r1 = x - hi.astype(F32)
    mid = r1.astype(BF16)
    lo = (r1 - mid.astype(F32)).astype(BF16)
    return _dot(mask_bf16, hi) + _dot(mask_bf16, mid) + _dot(mask_bf16, lo)


def _log_sigmoid(z):
    return -(jnp.maximum(-z, 0.0) + jnp.log1p(jnp.exp(-jnp.abs(z))))


def _sigmoid(z):
    return 1.0 / (1.0 + jnp.exp(-z))


def _silu(z):
    return z * _sigmoid(z)


def _proj_kernel(x_ref, w_ref, o_ref, xb_ref):
    @pl.when(pl.program_id(1) == 0)
    def _():
        xb_ref[...] = x_ref[...].astype(BF16)

    o_ref[...] = _dot(xb_ref[...], w_ref[...])


def _proj(x, w, tm, tn):
    t, k = x.shape
    n = w.shape[1]
    return pl.pallas_call(
        _proj_kernel,
        grid=(t // tm, n // tn),
        in_specs=[pl.BlockSpec((tm, k), lambda i, j: (i, 0)),
                  pl.BlockSpec((k, tn), lambda i, j: (0, j))],
        out_specs=pl.BlockSpec((tm, tn), lambda i, j: (i, j)),
        out_shape=jax.ShapeDtypeStruct((t, n), F32),
        scratch_shapes=[pltpu.VMEM((tm, k), BF16)],
        compiler_params=pltpu.CompilerParams(
            dimension_semantics=("parallel", "arbitrary"), vmem_limit_bytes=VMEM_LIMIT),
        name="proj",
    )(x, w)


def _mlstm_kernel(qp_ref, kp_ref, v_ref, ao_ref, az_ref, g_ref, cwq_ref, cwk_ref, cbq_ref, cbk_ref,
                  gb_ref, ng_ref, ya_ref, c_ref, n_ref, m_ref, extq_ref, extk_ref):
    L = MLSTM_CHUNK
    c = pl.program_id(2)

    @pl.when(c == 0)
    def _():
        c_ref[...] = jnp.zeros_like(c_ref)
        n_ref[...] = jnp.zeros_like(n_ref)
        m_ref[...] = jnp.zeros_like(m_ref)
        extq_ref[L:L + SUBLANES, :] = jnp.zeros((SUBLANES, A_DK), F32)
        extk_ref[L:L + SUBLANES, :] = jnp.zeros((SUBLANES, A_DK), F32)

    def conv_silu(x_ref, ext_ref, w_ref, b_ref):
        ext_ref[0:SUBLANES, :] = ext_ref[L:L + SUBLANES, :]
        ext_ref[SUBLANES:SUBLANES + L, :] = x_ref[...]
        acc = b_ref[...] + ext_ref[SUBLANES - 3:SUBLANES - 3 + L, :] * w_ref[0:1, :]
        for j in range(1, CONV_W):
            acc = acc + ext_ref[SUBLANES - 3 + j:SUBLANES - 3 + j + L, :] * w_ref[j:j + 1, :]
        return _silu(acc)

    q = conv_silu(qp_ref, extq_ref, cwq_ref, cbq_ref)
    k = conv_silu(kp_ref, extk_ref, cwk_ref, cbk_ref) * (A_DK ** -0.5)
    v = v_ref[...]

    g = g_ref[...] + gb_ref[...]
    lane = lax.broadcasted_iota(jnp.int32, (L, LANES), 1)
    g2 = jnp.where(lane == 1, _log_sigmoid(g), g)
    row = lax.broadcasted_iota(jnp.int32, (L, L), 0)
    col = lax.broadcasted_iota(jnp.int32, (L, L), 1)
    causal = row >= col
    cum = _mask_dot(causal.astype(BF16), g2)
    x2 = jnp.where(lane == 1, cum, g2)
    x2t = x2.T
    itil_col, b_col = x2[:, 0:1], x2[:, 1:2]
    itil_row, b_row = x2t[0:1, :], x2t[1:2, :]

    m_prev = m_ref[0, 0, 0:1, 0:1]
    dmat = jnp.where(causal, b_col - b_row + itil_row, -jnp.inf)
    inter = b_col + m_prev
    m_t = jnp.maximum(inter, jnp.max(dmat, axis=1, keepdims=True))
    w = jnp.exp(dmat - m_t)
    decay = jnp.exp(inter - m_t)

    qb, kb, vb = q.astype(BF16), k.astype(BF16), v.astype(BF16)
    c_old = c_ref[0, 0]
    n_old = n_ref[0, 0]
    s = _dot_nt(qb, kb) * w
    num = decay * _dot(qb, c_old.astype(BF16)) + _dot(s.astype(BF16), vb)
    den = decay * jnp.sum(q * n_old, axis=1, keepdims=True) + jnp.sum(s, axis=1, keepdims=True)
    h = num / jnp.maximum(jnp.abs(den), jnp.exp(-m_t))

    m_new = m_t[L - 1:L, :]
    b_last = b_col[L - 1:L, :]
    wk = jnp.exp(b_last - b_col + itil_col - m_new)
    dec = jnp.exp(b_last + m_prev - m_new)
    kw = k * wk
    c_ref[0, 0] = dec * c_old + _dot(kw.T.astype(BF16), vb)
    n_ref[0, 0] = dec * n_old + jnp.sum(kw, axis=0, keepdims=True)
    m_ref[0, 0] = jnp.broadcast_to(m_new, (1, LANES))

    mu = jnp.mean(h, axis=1, keepdims=True)
    hc = h - mu
    var = jnp.mean(hc * hc, axis=1, keepdims=True)
    hn = hc * lax.rsqrt(var + LN_EPS) * ng_ref[...]
    ya_ref[...] = (hn * _sigmoid(ao_ref[...]) * _silu(az_ref[...])).astype(ya_ref.dtype)


def _mlstm(p_all, batch, seq, conv_w, conv_b, gate_bias, a_norm_g):
    L = MLSTM_CHUNK
    nc = seq // L
    t = batch * seq
    qblk = A_DK // A_DK

    def rows(b, h, c):
        return b * nc + c

    def pcol(base):
        return pl.BlockSpec((L, A_DK), lambda b, h, c: (rows(b, h, c), base // A_DK + h))

    in_specs = [
        pcol(COL_Q), pcol(COL_K), pcol(COL_AV), pcol(COL_AO), pcol(COL_AZ),
        pl.BlockSpec((L, LANES), lambda b, h, c: (rows(b, h, c), COL_GATE // LANES + h)),
        pl.BlockSpec((CONV_W, A_DK), lambda b, h, c: (0, h)),
        pl.BlockSpec((CONV_W, A_DK), lambda b, h, c: (0, A_HEADS + h)),
        pl.BlockSpec((1, A_DK), lambda b, h, c: (0, h)),
        pl.BlockSpec((1, A_DK), lambda b, h, c: (0, A_HEADS + h)),
        pl.BlockSpec((1, LANES), lambda b, h, c: (0, h)),
        pl.BlockSpec((1, A_DV), lambda b, h, c: (0, h)),
    ]
    out_specs = [
        pl.BlockSpec((L, A_DV), lambda b, h, c: (rows(b, h, c), h)),
        pl.BlockSpec((1, 1, A_DK, A_DV), lambda b, h, c: (b, h, 0, 0)),
        pl.BlockSpec((1, 1, 1, A_DK), lambda b, h, c: (b, h, 0, 0)),
        pl.BlockSpec((1, 1, 1, LANES), lambda b, h, c: (b, h, 0, 0)),
    ]
    out_shape = [
        jax.ShapeDtypeStruct((t, A_WIDTH), BF16),
        jax.ShapeDtypeStruct((batch, A_HEADS, A_DK, A_DV), F32),
        jax.ShapeDtypeStruct((batch, A_HEADS, 1, A_DK), F32),
        jax.ShapeDtypeStruct((batch, A_HEADS, 1, LANES), F32),
    ]
    del qblk
    return pl.pallas_call(
        _mlstm_kernel,
        grid=(batch, A_HEADS, nc),
        in_specs=in_specs,
        out_specs=out_specs,
        out_shape=out_shape,
        scratch_shapes=[pltpu.VMEM((L + 2 * SUBLANES, A_DK), F32),
                        pltpu.VMEM((L + 2 * SUBLANES, A_DK), F32)],
        compiler_params=pltpu.CompilerParams(
            dimension_semantics=("parallel", "parallel", "arbitrary"), vmem_limit_bytes=VMEM_LIMIT),
        name="mlstm",
    )(p_all, p_all, p_all, p_all, p_all, p_all, conv_w, conv_w, conv_b, conv_b, gate_bias, a_norm_g)


def _gla_gate_log(bg_tile, wg_ref, bgate_ref):
    lr = bg_tile[:, 2:2 + GATE_RANK]
    z = _dot(lr.astype(BF16), wg_ref[...].astype(BF16)) + bgate_ref[...]
    return _log_sigmoid(z) / GATE_TAU


def _gla_kernel(q_ref, k_ref, v_ref, bz_ref, g_ref, wg_ref, bgate_ref, ng_ref,
                yb_ref, s_out_ref, st_ref):
    L = GLA_STEP
    B = GLA_BLOCK
    nb = L // B
    c = pl.program_id(2)

    @pl.when(c == 0)
    def _():
        st_ref[...] = jnp.zeros_like(st_ref)

    loga = _gla_gate_log(g_ref[...], wg_ref, bgate_ref)
    row = lax.broadcasted_iota(jnp.int32, (L, L), 0)
    col = lax.broadcasted_iota(jnp.int32, (L, L), 1)
    same = (row // B) == (col // B)
    bc = _mask_dot((same & (row >= col)).astype(BF16), loga)
    bl = _mask_dot(same.astype(BF16), loga)
    k = k_ref[...]
    qs = q_ref[...] * (B_DK ** -0.5)
    qt = (qs * jnp.exp(bc)).astype(BF16)
    kt = (k * jnp.exp(bl - bc)).astype(BF16)
    eb = jnp.exp(bl)
    vb = v_ref[...].astype(BF16)

    bc3, q3, k3 = (x.reshape(nb, B, B_DK) for x in (bc, qs, k))
    t_in = lax.broadcasted_iota(jnp.int32, (L, 1), 0) % B
    s_in = col % B
    a = jnp.zeros((L, L), F32)
    for s in range(B):
        e = jnp.exp(jnp.minimum(bc3 - bc3[:, s:s + 1, :], 0.0))
        a_col = jnp.sum(q3 * k3[:, s:s + 1, :] * e, axis=2, keepdims=True).reshape(L, 1)
        a = jnp.where(s_in == s, jnp.where(t_in >= s, a_col, 0.0), a)
    a = jnp.where(same, a, 0.0)
    o_diag = _dot(a.astype(BF16), vb)

    st = st_ref[...]
    o_parts = []
    for j in range(nb):
        lo, hi = j * B, (j + 1) * B
        o_parts.append(o_diag[lo:hi, :] + _dot_nt(qt[lo:hi, :], st.astype(BF16)))
        st = st * eb[lo:lo + 1, :] + _dot_tn(vb[lo:hi, :], kt[lo:hi, :])
    st_ref[...] = st
    o = jnp.concatenate(o_parts, axis=0)

    on = o * lax.rsqrt(jnp.mean(o * o, axis=1, keepdims=True) + LN_EPS) * ng_ref[...]
    yb_ref[...] = (on * _silu(bz_ref[...])).astype(yb_ref.dtype)

    @pl.when(c == pl.num_programs(2) - 1)
    def _():
        s_out_ref[0, 0] = st.T


def _gla(p_all, batch, seq, w_gate_up, b_gate, b_norm_g):
    L = GLA_STEP
    nc = seq // L
    t = batch * seq

    def rows(b, h, c):
        return b * nc + c

    in_specs = [
        pl.BlockSpec((L, B_DK), lambda b, h, c: (rows(b, h, c), COL_BQ // B_DK + h)),
        pl.BlockSpec((L, B_DK), lambda b, h, c: (rows(b, h, c), COL_BK // B_DK + h)),
        pl.BlockSpec((L, B_DV), lambda b, h, c: (rows(b, h, c), COL_BV // B_DV + h)),
        pl.BlockSpec((L, B_DV), lambda b, h, c: (rows(b, h, c), COL_BZ // B_DV + h)),
        pl.BlockSpec((L, LANES), lambda b, h, c: (rows(b, h, c), COL_GATE // LANES + h)),
        pl.BlockSpec((GATE_RANK, B_DK), lambda b, h, c: (0, h)),
        pl.BlockSpec((1, B_DK), lambda b, h, c: (0, h)),
        pl.BlockSpec((1, B_DV), lambda b, h, c: (0, h)),
    ]
    out_specs = [
        pl.BlockSpec((L, B_DV), lambda b, h, c: (rows(b, h, c), h)),
        pl.BlockSpec((1, 1, B_DK, B_DV), lambda b, h, c: (b, h, 0, 0)),
    ]
    out_shape = [
        jax.ShapeDtypeStruct((t, B_WIDTH), BF16),
        jax.ShapeDtypeStruct((batch, B_HEADS, B_DK, B_DV), F32),
    ]
    return pl.pallas_call(
        _gla_kernel,
        grid=(batch, B_HEADS, nc),
        in_specs=in_specs,
        out_specs=out_specs,
        out_shape=out_shape,
        scratch_shapes=[pltpu.VMEM((B_DV, B_DK), F32)],
        compiler_params=pltpu.CompilerParams(
            dimension_semantics=("parallel", "parallel", "arbitrary"), vmem_limit_bytes=VMEM_LIMIT),
        name="gla",
    )(p_all, p_all, p_all, p_all, p_all, w_gate_up, b_gate, b_norm_g)


DEC_TOKENS = 8


def _decode_kernel(qp_ref, kp_ref, av_ref, ao_ref, az_ref, bq_ref, bk_ref, bv_ref, bz_ref, g_ref,
                   sq0_ref, sq1_ref, sq2_ref, sk0_ref, sk1_ref, sk2_ref,
                   cwq_ref, cwk_ref, cbq_ref, cbk_ref, gb_ref, ang_ref, wg_ref, bgate_ref, bng_ref,
                   c_ref, n_ref, m_ref, s_ref,
                   ya_ref, yb_ref, c_out_ref, n_out_ref, m_out_ref, s_out_ref):
    TB = DEC_TOKENS
    h_idx = pl.program_id(1)

    def conv_silu(s0, s1, s2, x, w_ref, b_ref):
        acc = b_ref[...] + s0[...] * w_ref[0:1, :]
        acc = acc + s1[...] * w_ref[1:2, :]
        acc = acc + s2[...] * w_ref[2:3, :]
        acc = acc + x[...] * w_ref[3:4, :]
        return _silu(acc)

    q = conv_silu(sq0_ref, sq1_ref, sq2_ref, qp_ref, cwq_ref, cbq_ref)
    k = conv_silu(sk0_ref, sk1_ref, sk2_ref, kp_ref, cwk_ref, cbk_ref) * (A_DK ** -0.5)
    v = av_ref[...]
    g = g_ref[...]
    gbias = gb_ref[...]
    itil = g[:, 0:1] + gbias[:, 0:1]
    logf = _log_sigmoid(g[:, 1:2] + gbias[:, 1:2])
    lane_h = lax.broadcasted_iota(jnp.int32, (TB, A_HEADS), 1)
    m_prev = jnp.sum(jnp.where(lane_h == h_idx, m_ref[...], 0.0), axis=1, keepdims=True)
    inter = logf + m_prev
    m_t = jnp.maximum(inter, itil)
    w = jnp.exp(itil - m_t)
    decay = jnp.exp(inter - m_t)
    n_old = n_ref[:, 0, 0, :]
    s = jnp.sum(q * k, axis=1, keepdims=True) * w
    den = decay * jnp.sum(q * n_old, axis=1, keepdims=True) + s
    scale = 1.0 / jnp.maximum(jnp.abs(den), jnp.exp(-m_t))
    kw = k * w
    n_out_ref[:, 0, 0, :] = decay * n_old + kw
    m_out_ref[...] = jnp.broadcast_to(m_t, (TB, LANES))

    rows8 = lax.broadcasted_iota(jnp.int32, (SUBLANES, 1), 0)
    qb, kwb, vb = q.astype(BF16), kw.astype(BF16), v.astype(BF16)
    h_rows = []
    for t in range(TB):
        c_old = c_ref[t, 0]
        sel = rows8 == t
        qc = _dot(qb, c_old.astype(BF16))[t:t + 1, :]
        h_rows.append((decay[t:t + 1, :] * qc + s[t:t + 1, :] * v[t:t + 1, :]) * scale[t:t + 1, :])
        outer = _dot_tn(jnp.where(sel, kwb, jnp.zeros_like(kwb)), vb)
        c_out_ref[t, 0] = decay[t:t + 1, :] * c_old + outer
    h = jnp.concatenate(h_rows, axis=0)
    mu = jnp.mean(h, axis=1, keepdims=True)
    hc = h - mu
    var = jnp.mean(hc * hc, axis=1, keepdims=True)
    hn = hc * lax.rsqrt(var + LN_EPS) * ang_ref[...]
    ya_ref[...] = (hn * _sigmoid(ao_ref[...]) * _silu(az_ref[...])).astype(ya_ref.dtype)

    gq = bq_ref[...] * (B_DK ** -0.5)
    gk = bk_ref[...]
    gv = bv_ref[...]
    loga = _gla_gate_log(g, wg_ref, bgate_ref)
    eb = jnp.exp(loga)
    a = jnp.sum(gq * gk, axis=1, keepdims=True)
    ebt = jnp.concatenate([eb, jnp.zeros((LANES - TB, B_DK), F32)], axis=0).T
    qeb, gkb, gvb = (gq * eb).astype(BF16), gk.astype(BF16), gv.astype(BF16)
    o_rows = []
    for t in range(TB):
        s_old = s_ref[t, 0]
        sel = rows8 == t
        o_rows.append(_dot(qeb, s_old.astype(BF16))[t:t + 1, :] + a[t:t + 1, :] * gv[t:t + 1, :])
        outer = _dot_tn(jnp.where(sel, gkb, jnp.zeros_like(gkb)), gvb)
        s_out_ref[t, 0] = ebt[:, t:t + 1] * s_old + outer
    o = jnp.concatenate(o_rows, axis=0)
    on = o * lax.rsqrt(jnp.mean(o * o, axis=1, keepdims=True) + LN_EPS) * bng_ref[...]
    yb_ref[...] = (on * _silu(bz_ref[...])).astype(yb_ref.dtype)


def _decode(p_s, conv_state, conv_w, conv_b, gate_bias, a_norm_g, w_gate_up, b_gate, b_norm_g,
            c_state, n_state, m_state, s_state):
    nseq = p_s.shape[0]
    TB = DEC_TOKENS

    def pcol(base, width):
        return pl.BlockSpec((TB, width), lambda i, h: (i, base // width + h))

    def cstate(j, base):
        return pl.BlockSpec((TB, A_DK), lambda i, h: (i, (j * 2 * A_WIDTH + base) // A_DK + h))

    in_specs = [
        pcol(COL_Q, A_DK), pcol(COL_K, A_DK), pcol(COL_AV, A_DV), pcol(COL_AO, A_DV), pcol(COL_AZ, A_DV),
        pcol(COL_BQ, B_DK), pcol(COL_BK, B_DK), pcol(COL_BV, B_DV), pcol(COL_BZ, B_DV),
        pcol(COL_GATE, LANES),
        cstate(0, COL_Q), cstate(1, COL_Q), cstate(2, COL_Q),
        cstate(0, COL_K), cstate(1, COL_K), cstate(2, COL_K),
        pl.BlockSpec((CONV_W, A_DK), lambda i, h: (0, h)),
        pl.BlockSpec((CONV_W, A_DK), lambda i, h: (0, A_HEADS + h)),
        pl.BlockSpec((1, A_DK), lambda i, h: (0, h)),
        pl.BlockSpec((1, A_DK), lambda i, h: (0, A_HEADS + h)),
        pl.BlockSpec((1, LANES), lambda i, h: (0, h)),
        pl.BlockSpec((1, A_DV), lambda i, h: (0, h)),
        pl.BlockSpec((GATE_RANK, B_DK), lambda i, h: (0, h)),
        pl.BlockSpec((1, B_DK), lambda i, h: (0, h)),
        pl.BlockSpec((1, B_DV), lambda i, h: (0, h)),
        pl.BlockSpec((TB, 1, A_DK, A_DV), lambda i, h: (i, h, 0, 0)),
        pl.BlockSpec((TB, 1, 1, A_DK), lambda i, h: (i, h, 0, 0)),
        pl.BlockSpec((TB, A_HEADS), lambda i, h: (i, 0)),
        pl.BlockSpec((TB, 1, B_DK, B_DV), lambda i, h: (i, h, 0, 0)),
    ]
    out_specs = [
        pl.BlockSpec((TB, A_DV), lambda i, h: (i, h)),
        pl.BlockSpec((TB, B_DV), lambda i, h: (i, h)),
        pl.BlockSpec((TB, 1, A_DK, A_DV), lambda i, h: (i, h, 0, 0)),
        pl.BlockSpec((TB, 1, 1, A_DK), lambda i, h: (i, h, 0, 0)),
        pl.BlockSpec((TB, LANES), lambda i, h: (i, h)),
        pl.BlockSpec((TB, 1, B_DK, B_DV), lambda i, h: (i, h, 0, 0)),
    ]
    out_shape = [
        jax.ShapeDtypeStruct((nseq, A_WIDTH), BF16),
        jax.ShapeDtypeStruct((nseq, B_WIDTH), BF16),
        jax.ShapeDtypeStruct((nseq, A_HEADS, A_DK, A_DV), F32),
        jax.ShapeDtypeStruct((nseq, A_HEADS, 1, A_DK), F32),
        jax.ShapeDtypeStruct((nseq, A_HEADS * LANES), F32),
        jax.ShapeDtypeStruct((nseq, B_HEADS, B_DK, B_DV), F32),
    ]
    return pl.pallas_call(
        _decode_kernel,
        grid=(nseq // TB, A_HEADS),
        in_specs=in_specs,
        out_specs=out_specs,
        out_shape=out_shape,
        compiler_params=pltpu.CompilerParams(
            dimension_semantics=("parallel", "parallel"), vmem_limit_bytes=VMEM_LIMIT),
        name="decode",
    )(p_s, p_s, p_s, p_s, p_s, p_s, p_s, p_s, p_s, p_s,
      conv_state, conv_state, conv_state, conv_state, conv_state, conv_state,
      conv_w, conv_w, conv_b, conv_b, gate_bias, a_norm_g, w_gate_up, b_gate, b_norm_g,
      c_state, n_state, m_state, s_state)


def _out_kernel(ya_ref, yb_ref, ga_ref, gb_ref, x_ref, wpa_ref, wpb_ref, wo_ref, lng_ref, lnb_ref, o_ref):
    pa = _dot(ya_ref[...], wpa_ref[...])
    pb = _dot(yb_ref[...], wpb_ref[...])
    merged = _sigmoid(ga_ref[...]) * pa + _sigmoid(gb_ref[...]) * pb
    y = _dot(merged.astype(BF16), wo_ref[...])
    r = ALPHA * x_ref[...] + y
    mu = jnp.mean(r, axis=1, keepdims=True)
    rc = r - mu
    var = jnp.mean(rc * rc, axis=1, keepdims=True)
    o_ref[...] = rc * lax.rsqrt(var + LN_EPS) * lng_ref[...] + lnb_ref[...]


def _out(ya, yb, p_all, x, w_pa, w_pb, w_out, ln_g, ln_b, tm):
    t = x.shape[0]
    const = lambda i: (0, 0)
    single = pl.Buffered(1)
    in_specs = [
        pl.BlockSpec((tm, A_WIDTH), lambda i: (i, 0)),
        pl.BlockSpec((tm, B_WIDTH), lambda i: (i, 0)),
        pl.BlockSpec((tm, D_MODEL), lambda i: (i, COL_GA // D_MODEL)),
        pl.BlockSpec((tm, D_MODEL), lambda i: (i, COL_GB // D_MODEL)),
        pl.BlockSpec((tm, D_MODEL), lambda i: (i, 0)),
        pl.BlockSpec((A_WIDTH, D_MODEL), const, pipeline_mode=single),
        pl.BlockSpec((B_WIDTH, D_MODEL), const, pipeline_mode=single),
        pl.BlockSpec((D_MODEL, D_MODEL), const, pipeline_mode=single),
        pl.BlockSpec((1, D_MODEL), const),
        pl.BlockSpec((1, D_MODEL), const),
    ]
    return pl.pallas_call(
        _out_kernel,
        grid=(t // tm,),
        in_specs=in_specs,
        out_specs=pl.BlockSpec((tm, D_MODEL), lambda i: (i, 0)),
        out_shape=jax.ShapeDtypeStruct((t, D_MODEL), F32),
        compiler_params=pltpu.CompilerParams(
            dimension_semantics=("parallel",), vmem_limit_bytes=VMEM_LIMIT),
        name="outproj",
    )(ya, yb, p_all, p_all, x, w_pa, w_pb, w_out, ln_g, ln_b)


def _relayout_w_in(w_in):
    s_qk = 0
    s_av = 2 * A_WIDTH
    s_i = s_av + A_WIDTH
    s_f = s_i + A_HEADS
    s_ao = s_f + A_HEADS
    s_bg = s_ao + 2 * A_WIDTH + 2 * B_KWIDTH + B_WIDTH
    s_bz = s_bg + GATE_RANK
    main = [w_in[:, s_qk:s_i], w_in[:, s_ao:s_bg], w_in[:, s_bz:]]
    gate_cols = []
    pad = jnp.zeros((D_MODEL, LANES - 2 - GATE_RANK), w_in.dtype)
    for h in range(A_HEADS):
        gate_cols += [w_in[:, s_i + h:s_i + h + 1], w_in[:, s_f + h:s_f + h + 1],
                      w_in[:, s_bg:s_bg + GATE_RANK], pad]
    return jnp.concatenate(main + gate_cols, axis=1).astype(BF16)


def kernel(x_prompt, x_sample, state_mlstm_C, state_mlstm_n, state_mlstm_m, state_conv, state_gla_S,
           w_in, conv_w, conv_b, b_i, b_f, a_norm_g, w_gate_up, b_gate, b_norm_g, w_pa, w_pb, w_out,
           ln_g, ln_b):
    batch, seq, _ = x_prompt.shape
    nseq = x_sample.shape[0]
    assert w_in.shape[0] == 1, "single-layer step"
    d = 0

    def layer(a):
        return a.reshape(a.shape[1:])

    wp = _relayout_w_in(layer(w_in))
    wpa, wpb, wo = w_pa[d].astype(BF16), w_pb[d].astype(BF16), w_out[d].astype(BF16)
    cw = conv_w[d]
    cb = conv_b[d][None, :]
    gate_bias = jnp.zeros((A_HEADS, LANES), F32).at[:, 0].set(b_i[d]).at[:, 1].set(b_f[d]).reshape(1, -1)
    ang = a_norm_g[d][None, :]
    bng = b_norm_g[d][None, :]
    wg = w_gate_up[d]
    bgate = b_gate[d][None, :]
    lng, lnb = ln_g[d][None, :], ln_b[d][None, :]

    xp = x_prompt.reshape(batch * seq, D_MODEL)
    p_all = _proj(xp, wp, tm=1024, tn=512)
    ya, p_c, p_n, p_m = _mlstm(p_all, batch, seq, cw, cb, gate_bias, ang)
    yb, p_s = _gla(p_all, batch, seq, wg, bgate, bng)
    y_prompt = _out(ya, yb, p_all, xp, wpa, wpb, wo, lng, lnb, tm=256).reshape(batch, seq, D_MODEL)
    p_conv = p_all.reshape(batch, seq, N_PROJ)[:, seq - (CONV_W - 1):, :2 * A_WIDTH]

    xs = x_sample.reshape(nseq, D_MODEL)
    ps_all = _proj(xs, wp, tm=nseq, tn=512)
    conv_state = state_conv.reshape(nseq, (CONV_W - 1) * 2 * A_WIDTH)
    ya_s, yb_s, s_c, s_n, s_m, s_s = _decode(
        ps_all, conv_state, cw, cb, gate_bias, ang, wg, bgate, bng,
        layer(state_mlstm_C), state_mlstm_n.reshape(nseq, A_HEADS, 1, A_DK), layer(state_mlstm_m),
        layer(state_gla_S))
    y_sample = _out(ya_s, yb_s, ps_all, xs, wpa, wpb, wo, lng, lnb, tm=nseq).reshape(nseq, 1, D_MODEL)
    s_conv = jnp.concatenate([layer(state_conv)[:, 1:, :], ps_all[:, None, :2 * A_WIDTH]], axis=1)

    return (y_prompt, y_sample,
            p_c[None], p_n[:, :, 0, :][None], p_m[:, :, 0, 0][None], p_conv[None], p_s[None],
            s_c[None], s_n[:, :, 0, :][None], s_m.reshape(nseq, A_HEADS, LANES)[:, :, 0][None],
            s_conv[None], s_s[None])
```

```python
import functools

import jax
import jax.numpy as jnp
from jax import lax
from jax.experimental import pallas as pl
from jax.experimental.pallas import tpu as pltpu

F32 = jnp.float32
BF16 = jnp.bfloat16

D_MODEL = 2048
A_HEADS = 4
A_WIDTH = 1024
A_DK = 256
A_DV = 256
CONV_W = 4
B_HEADS = 4
B_WIDTH = 1024
B_KWIDTH = 512
B_DK = 128
B_DV = 256
GATE_RANK = 16
GATE_TAU = 16.0
ALPHA = 2.0 ** 0.25
LN_EPS = 1e-5

LANES = 128
SUBLANES = 8
VMEM_LIMIT = 48 * 1024 * 1024

COL_Q = 0
COL_K = A_WIDTH
COL_AV = 2 * A_WIDTH
COL_AO = 3 * A_WIDTH
COL_AZ = 4 * A_WIDTH
COL_BQ = 5 * A_WIDTH
COL_BK = COL_BQ + B_KWIDTH
COL_BV = COL_BK + B_KWIDTH
COL_BZ = COL_BV + B_WIDTH
COL_GA = COL_BZ + B_WIDTH
COL_GB = COL_GA + D_MODEL
COL_GATE = COL_GB + D_MODEL
N_PROJ = COL_GATE + A_HEADS * LANES

MLSTM_CHUNK = 256
GLA_STEP = 256
GLA_BLOCK = 16


def _dot(a, b):
    return jnp.dot(a, b, preferred_element_type=F32)


def _dot_nt(a, b):
    return lax.dot_general(a, b, (((1,), (1,)), ((), ())), preferred_element_type=F32)


def _dot_tn(a, b):
    return lax.dot_general(a, b, (((0,), (0,)), ((), ())), preferred_element_type=F32)


def _mask_dot(mask_bf16, x):
    hi = x.astype(BF16)
    r1 = x - hi.astype(F32)
    mid = r1.astype(BF16)
    lo = (r1 - mid.astype(F32)).astype(BF16)
    return _dot(mask_bf16, hi) + _dot(mask_bf16, mid) + _dot(mask_bf16, lo)


def _log_sigmoid(z):
    return -(jnp.maximum(-z, 0.0) + jnp.log1p(jnp.exp(-jnp.abs(z))))


def _sigmoid(z):
    return 1.0 / (1.0 + jnp.exp(-z))


def _silu(z):
    return z * _sigmoid(z)


def _proj_kernel(x_ref, w_ref, o_ref, xb_ref):
    @pl.when(pl.program_id(1) == 0)
    def _():
        xb_ref[...] = x_ref[...].astype(BF16)

    o_ref[...] = _dot(xb_ref[...], w_ref[...])


def _proj(x, w, tm, tn):
    t, k = x.shape
    n = w.shape[1]
    return pl.pallas_call(
        _proj_kernel,
        grid=(t // tm, n // tn),
        in_specs=[pl.BlockSpec((tm, k), lambda i, j: (i, 0)),
                  pl.BlockSpec((k, tn), lambda i, j: (0, j))],
        out_specs=pl.BlockSpec((tm, tn), lambda i, j: (i, j)),
        out_shape=jax.ShapeDtypeStruct((t, n), F32),
        scratch_shapes=[pltpu.VMEM((tm, k), BF16)],
        compiler_params=pltpu.CompilerParams(
            dimension_semantics=("parallel", "arbitrary"), vmem_limit_bytes=VMEM_LIMIT),
        name="proj",
    )(x, w)


def _mlstm_kernel(qp_ref, kp_ref, v_ref, ao_ref, az_ref, g_ref, cwq_ref, cwk_ref, cbq_ref, cbk_ref,
                  gb_ref, ng_ref, ya_ref, c_ref, n_ref, m_ref, extq_ref, extk_ref):
    L = MLSTM_CHUNK
    c = pl.program_id(2)

    @pl.when(c == 0)
    def _():
        c_ref[...] = jnp.zeros_like(c_ref)
        n_ref[...] = jnp.zeros_like(n_ref)
        m_ref[...] = jnp.zeros_like(m_ref)
        extq_ref[L:L + SUBLANES, :] = jnp.zeros((SUBLANES, A_DK), F32)
        extk_ref[L:L + SUBLANES, :] = jnp.zeros((SUBLANES, A_DK), F32)

    def conv_silu(x_ref, ext_ref, w_ref, b_ref):
        ext_ref[0:SUBLANES, :] = ext_ref[L:L + SUBLANES, :]
        ext_ref[SUBLANES:SUBLANES + L, :] = x_ref[...]
        acc = b_ref[...] + ext_ref[SUBLANES - 3:SUBLANES - 3 + L, :] * w_ref[0:1, :]
        for j in range(1, CONV_W):
            acc = acc + ext_ref[SUBLANES - 3 + j:SUBLANES - 3 + j + L, :] * w_ref[j:j + 1, :]
        return _silu(acc)

    q = conv_silu(qp_ref, extq_ref, cwq_ref, cbq_ref)
    k = conv_silu(kp_ref, extk_ref, cwk_ref, cbk_ref) * (A_DK ** -0.5)
    v = v_ref[...]

    g = g_ref[...] + gb_ref[...]
    lane = lax.broadcasted_iota(jnp.int32, (L, LANES), 1)
    g2 = jnp.where(lane == 1, _log_sigmoid(g), g)
    row = lax.broadcasted_iota(jnp.int32, (L, L), 0)
    col = lax.broadcasted_iota(jnp.int32, (L, L), 1)
    causal = row >= col
    cum = _mask_dot(causal.astype(BF16), g2)
    x2 = jnp.where(lane == 1, cum, g2)
    x2t = x2.T
    itil_col, b_col = x2[:, 0:1], x2[:, 1:2]
    itil_row, b_row = x2t[0:1, :], x2t[1:2, :]

    m_prev = m_ref[0, 0, 0:1, 0:1]
    dmat = jnp.where(causal, b_col - b_row + itil_row, -jnp.inf)
    inter = b_col + m_prev
    m_t = jnp.maximum(inter, jnp.max(dmat, axis=1, keepdims=True))
    w = jnp.exp(dmat - m_t)
    decay = jnp.exp(inter - m_t)

    qb, kb, vb = q.astype(BF16), k.astype(BF16), v.astype(BF16)
    c_old = c_ref[0, 0]
    n_old = n_ref[0, 0]
    s = _dot_nt(qb, kb) * w
    num = decay * _dot(qb, c_old.astype(BF16)) + _dot(s.astype(BF16), vb)
    den = decay * jnp.sum(q * n_old, axis=1, keepdims=True) + jnp.sum(s, axis=1, keepdims=True)
    h = num / jnp.maximum(jnp.abs(den), jnp.exp(-m_t))

    m_new = m_t[L - 1:L, :]
    b_last = b_col[L - 1:L, :]
    wk = jnp.exp(b_last - b_col + itil_col - m_new)
    dec = jnp.exp(b_last + m_prev - m_new)
    kw = k * wk
    c_ref[0, 0] = dec * c_old + _dot(kw.T.astype(BF16), vb)
    n_ref[0, 0] = dec * n_old + jnp.sum(kw, axis=0, keepdims=True)
    m_ref[0, 0] = jnp.broadcast_to(m_new, (1, LANES))

    mu = jnp.mean(h, axis=1, keepdims=True)
    hc = h - mu
    var = jnp.mean(hc * hc, axis=1, keepdims=True)
    hn = hc * lax.rsqrt(var + LN_EPS) * ng_ref[...]
    ya_ref[...] = (hn * _sigmoid(ao_ref[...]) * _silu(az_ref[...])).astype(ya_ref.dtype)


def _mlstm(p_all, batch, seq, conv_w, conv_b, gate_bias, a_norm_g):
    L = MLSTM_CHUNK
    nc = seq // L
    t = batch * seq
    qblk = A_DK // A_DK

    def rows(b, h, c):
        return b * nc + c

    def pcol(base):
        return pl.BlockSpec((L, A_DK), lambda b, h, c: (rows(b, h, c), base // A_DK + h))

    in_specs = [
        pcol(COL_Q), pcol(COL_K), pcol(COL_AV), pcol(COL_AO), pcol(COL_AZ),
        pl.BlockSpec((L, LANES), lambda b, h, c: (rows(b, h, c), COL_GATE // LANES + h)),
        pl.BlockSpec((CONV_W, A_DK), lambda b, h, c: (0, h)),
        pl.BlockSpec((CONV_W, A_DK), lambda b, h, c: (0, A_HEADS + h)),
        pl.BlockSpec((1, A_DK), lambda b, h, c: (0, h)),
        pl.BlockSpec((1, A_DK), lambda b, h, c: (0, A_HEADS + h)),
        pl.BlockSpec((1, LANES), lambda b, h, c: (0, h)),
        pl.BlockSpec((1, A_DV), lambda b, h, c: (0, h)),
    ]
    out_specs = [
        pl.BlockSpec((L, A_DV), lambda b, h, c: (rows(b, h, c), h)),
        pl.BlockSpec((1, 1, A_DK, A_DV), lambda b, h, c: (b, h, 0, 0)),
        pl.BlockSpec((1, 1, 1, A_DK), lambda b, h, c: (b, h, 0, 0)),
        pl.BlockSpec((1, 1, 1, LANES), lambda b, h, c: (b, h, 0, 0)),
    ]
    out_shape = [
        jax.ShapeDtypeStruct((t, A_WIDTH), BF16),
        jax.ShapeDtypeStruct((batch, A_HEADS, A_DK, A_DV), F32),
        jax.ShapeDtypeStruct((batch, A_HEADS, 1, A_DK), F32),
        jax.ShapeDtypeStruct((batch, A_HEADS, 1, LANES), F32),
    ]
    del qblk
    return pl.pallas_call(
        _mlstm_kernel,
        grid=(batch, A_HEADS, nc),
        in_specs=in_specs,
        out_specs=out_specs,
        out_shape=out_shape,
        scratch_shapes=[pltpu.VMEM((L + 2 * SUBLANES, A_DK), F32),
                        pltpu.VMEM((L + 2 * SUBLANES, A_DK), F32)],
        compiler_params=pltpu.CompilerParams(
            dimension_semantics=("parallel", "parallel", "arbitrary"), vmem_limit_bytes=VMEM_LIMIT),
        name="mlstm",
    )(p_all, p_all, p_all, p_all, p_all, p_all, conv_w, conv_w, conv_b, conv_b, gate_bias, a_norm_g)


def _gla_gate_log(bg_tile, wg_ref, bgate_ref):
    lr = bg_tile[:, 2:2 + GATE_RANK]
    z = _dot(lr.astype(BF16), wg_ref[...].astype(BF16)) + bgate_ref[...]
    return _log_sigmoid(z) / GATE_TAU


def _gla_kernel(q_ref, k_ref, v_ref, bz_ref, g_ref, wg_ref, bgate_ref, ng_ref,
                yb_ref, s_out_ref, st_ref):
    L = GLA_STEP
    B = GLA_BLOCK
    nb = L // B
    c = pl.program_id(2)

    @pl.when(c == 0)
    def _():
        st_ref[...] = jnp.zeros_like(st_ref)

    loga = _gla_gate_log(g_ref[...], wg_ref, bgate_ref)
    row = lax.broadcasted_iota(jnp.int32, (L, L), 0)
    col = lax.broadcasted_iota(jnp.int32, (L, L), 1)
    same = (row // B) == (col // B)
    bc = _mask_dot((same & (row >= col)).astype(BF16), loga)
    bl = _mask_dot(same.astype(BF16), loga)
    k = k_ref[...]
    qs = q_ref[...] * (B_DK ** -0.5)
    qt = (qs * jnp.exp(bc)).astype(BF16)
    kt = (k * jnp.exp(bl - bc)).astype(BF16)
    eb = jnp.exp(bl)
    vb = v_ref[...].astype(BF16)

    bc3, q3, k3 = (x.reshape(nb, B, B_DK) for x in (bc, qs, k))
    t_in = lax.broadcasted_iota(jnp.int32, (L, 1), 0) % B
    s_in = col % B
    a = jnp.zeros((L, L), F32)
    for s in range(B):
        e = jnp.exp(jnp.minimum(bc3 - bc3[:, s:s + 1, :], 0.0))
        a_col = jnp.sum(q3 * k3[:, s:s + 1, :] * e, axis=2, keepdims=True).reshape(L, 1)
        a = jnp.where(s_in == s, jnp.where(t_in >= s, a_col, 0.0), a)
    a = jnp.where(same, a, 0.0)
    o_diag = _dot(a.astype(BF16), vb)

    st = st_ref[...]
    o_parts = []
    for j in range(nb):
        lo, hi = j * B, (j + 1) * B
        o_parts.append(o_diag[lo:hi, :] + _dot_nt(qt[lo:hi, :], st.astype(BF16)))
        st = st * eb[lo:lo + 1, :] + _dot_tn(vb[lo:hi, :], kt[lo:hi, :])
    st_ref[...] = st
    o = jnp.concatenate(o_parts, axis=0)

    on = o * lax.rsqrt(jnp.mean(o * o, axis=1, keepdims=True) + LN_EPS) * ng_ref[...]
    yb_ref[...] = (on * _silu(bz_ref[...])).astype(yb_ref.dtype)

    @pl.when(c == pl.num_programs(2) - 1)
    def _():
        s_out_ref[0, 0] = st.T


def _gla(p_all, batch, seq, w_gate_up, b_gate, b_norm_g):
    L = GLA_STEP
    nc = seq // L
    t = batch * seq

    def rows(b, h, c):
        return b * nc + c

    in_specs = [
        pl.BlockSpec((L, B_DK), lambda b, h, c: (rows(b, h, c), COL_BQ // B_DK + h)),
        pl.BlockSpec((L, B_DK), lambda b, h, c: (rows(b, h, c), COL_BK // B_DK + h)),
        pl.BlockSpec((L, B_DV), lambda b, h, c: (rows(b, h, c), COL_BV // B_DV + h)),
        pl.BlockSpec((L, B_DV), lambda b, h, c: (rows(b, h, c), COL_BZ // B_DV + h)),
        pl.BlockSpec((L, LANES), lambda b, h, c: (rows(b, h, c), COL_GATE // LANES + h)),
        pl.BlockSpec((GATE_RANK, B_DK), lambda b, h, c: (0, h)),
        pl.BlockSpec((1, B_DK), lambda b, h, c: (0, h)),
        pl.BlockSpec((1, B_DV), lambda b, h, c: (0, h)),
    ]
    out_specs = [
        pl.BlockSpec((L, B_DV), lambda b, h, c: (rows(b, h, c), h)),
        pl.BlockSpec((1, 1, B_DK, B_DV), lambda b, h, c: (b, h, 0, 0)),
    ]
    out_shape = [
        jax.ShapeDtypeStruct((t, B_WIDTH), BF16),
        jax.ShapeDtypeStruct((batch, B_HEADS, B_DK, B_DV), F32),
    ]
    return pl.pallas_call(
        _gla_kernel,
        grid=(batch, B_HEADS, nc),
        in_specs=in_specs,
        out_specs=out_specs,
        out_shape=out_shape,
        scratch_shapes=[pltpu.VMEM((B_DV, B_DK), F32)],
        compiler_params=pltpu.CompilerParams(
            dimension_semantics=("parallel", "parallel", "arbitrary"), vmem_limit_bytes=VMEM_LIMIT),
        name="gla",
    )(p_all, p_all, p_all, p_all, p_all, w_gate_up, b_gate, b_norm_g)


DEC_TOKENS = 8


def _decode_kernel(qp_ref, kp_ref, av_ref, ao_ref, az_ref, bq_ref, bk_ref, bv_ref, bz_ref, g_ref,
                   sq0_ref, sq1_ref, sq2_ref, sk0_ref, sk1_ref, sk2_ref,
                   cwq_ref, cwk_ref, cbq_ref, cbk_ref, gb_ref, ang_ref, wg_ref, bgate_ref, bng_ref,
                   c_ref, n_ref, m_ref, s_ref,
                   ya_ref, yb_ref, c_out_ref, n_out_ref, m_out_ref, s_out_ref):
    TB = DEC_TOKENS
    h_idx = pl.program_id(1)

    def conv_silu(s0, s1, s2, x, w_ref, b_ref):
        acc = b_ref[...] + s0[...] * w_ref[0:1, :]
        acc = acc + s1[...] * w_ref[1:2, :]
        acc = acc + s2[...] * w_ref[2:3, :]
        acc = acc + x[...] * w_ref[3:4, :]
        return _silu(acc)

    q = conv_silu(sq0_ref, sq1_ref, sq2_ref, qp_ref, cwq_ref, cbq_ref)
    k = conv_silu(sk0_ref, sk1_ref, sk2_ref, kp_ref, cwk_ref, cbk_ref) * (A_DK ** -0.5)
    v = av_ref[...]
    g = g_ref[...]
    gbias = gb_ref[...]
    itil = g[:, 0:1] + gbias[:, 0:1]
    logf = _log_sigmoid(g[:, 1:2] + gbias[:, 1:2])
    lane_h = lax.broadcasted_iota(jnp.int32, (TB, A_HEADS), 1)
    m_prev = jnp.sum(jnp.where(lane_h == h_idx, m_ref[...], 0.0), axis=1, keepdims=True)
    inter = logf + m_prev
    m_t = jnp.maximum(inter, itil)
    w = jnp.exp(itil - m_t)
    decay = jnp.exp(inter - m_t)
    n_old = n_ref[:, 0, 0, :]
    s = jnp.sum(q * k, axis=1, keepdims=True) * w
    den = decay * jnp.sum(q * n_old, axis=1, keepdims=True) + s
    scale = 1.0 / jnp.maximum(jnp.abs(den), jnp.exp(-m_t))
    kw = k * w
    n_out_ref[:, 0, 0, :] = decay * n_old + kw
    m_out_ref[...] = jnp.broadcast_to(m_t, (TB, LANES))

    rows8 = lax.broadcasted_iota(jnp.int32, (SUBLANES, 1), 0)
    qb, kwb, vb = q.astype(BF16), kw.astype(BF16), v.astype(BF16)
    h_rows = []
    for t in range(TB):
        c_old = c_ref[t, 0]
        sel = rows8 == t
        qc = _dot(qb, c_old.astype(BF16))[t:t + 1, :]
        h_rows.append((decay[t:t + 1, :] * qc + s[t:t + 1, :] * v[t:t + 1, :]) * scale[t:t + 1, :])
        outer = _dot_tn(jnp.where(sel, kwb, jnp.zeros_like(kwb)), vb)
        c_out_ref[t, 0] = decay[t:t + 1, :] * c_old + outer
    h = jnp.concatenate(h_rows, axis=0)
    mu = jnp.mean(h, axis=1, keepdims=True)
    hc = h - mu
    var = jnp.mean(hc * hc, axis=1, keepdims=True)
    hn = hc * lax.rsqrt(var + LN_EPS) * ang_ref[...]
    ya_ref[...] = (hn * _sigmoid(ao_ref[...]) * _silu(az_ref[...])).astype(ya_ref.dtype)

    gq = bq_ref[...] * (B_DK ** -0.5)
    gk = bk_ref[...]
    gv = bv_ref[...]
    loga = _gla_gate_log(g, wg_ref, bgate_ref)
    eb = jnp.exp(loga)
    a = jnp.sum(gq * gk, axis=1, keepdims=True)
    ebt = jnp.concatenate([eb, jnp.zeros((LANES - TB, B_DK), F32)], axis=0).T
    qeb, gkb, gvb = (gq * eb).astype(BF16), gk.astype(BF16), gv.astype(BF16)
    o_rows = []
    for t in range(TB):
        s_old = s_ref[t, 0]
        sel = rows8 == t
        o_rows.append(_dot(qeb, s_old.astype(BF16))[t:t + 1, :] + a[t:t + 1, :] * gv[t:t + 1, :])
        outer = _dot_tn(jnp.where(sel, gkb, jnp.zeros_like(gkb)), gvb)
        s_out_ref[t, 0] = ebt[:, t:t + 1] * s_old + outer
    o = jnp.concatenate(o_rows, axis=0)
    on = o * lax.rsqrt(jnp.mean(o * o, axis=1, keepdims=True) + LN_EPS) * bng_ref[...]
    yb_ref[...] = (on * _silu(bz_ref[...])).astype(yb_ref.dtype)


def _decode(p_s, conv_state, conv_w, conv_b, gate_bias, a_norm_g, w_gate_up, b_gate, b_norm_g,
            c_state, n_state, m_state, s_state):
    nseq = p_s.shape[0]
    TB = DEC_TOKENS

    def pcol(base, width):
        return pl.BlockSpec((TB, width), lambda i, h: (i, base // width + h))

    def cstate(j, base):
        return pl.BlockSpec((TB, A_DK), lambda i, h: (i, (j * 2 * A_WIDTH + base) // A_DK + h))

    in_specs = [
        pcol(COL_Q, A_DK), pcol(COL_K, A_DK), pcol(COL_AV, A_DV), pcol(COL_AO, A_DV), pcol(COL_AZ, A_DV),
        pcol(COL_BQ, B_DK), pcol(COL_BK, B_DK), pcol(COL_BV, B_DV), pcol(COL_BZ, B_DV),
        pcol(COL_GATE, LANES),
        cstate(0, COL_Q), cstate(1, COL_Q), cstate(2, COL_Q),
        cstate(0, COL_K), cstate(1, COL_K), cstate(2, COL_K),
        pl.BlockSpec((CONV_W, A_DK), lambda i, h: (0, h)),
        pl.BlockSpec((CONV_W, A_DK), lambda i, h: (0, A_HEADS + h)),
        pl.BlockSpec((1, A_DK), lambda i, h: (0, h)),
        pl.BlockSpec((1, A_DK), lambda i, h: (0, A_HEADS + h)),
        pl.BlockSpec((1, LANES), lambda i, h: (0, h)),
        pl.BlockSpec((1, A_DV), lambda i, h: (0, h)),
        pl.BlockSpec((GATE_RANK, B_DK), lambda i, h: (0, h)),
        pl.BlockSpec((1, B_DK), lambda i, h: (0, h)),
        pl.BlockSpec((1, B_DV), lambda i, h: (0, h)),
        pl.BlockSpec((TB, 1, A_DK, A_DV), lambda i, h: (i, h, 0, 0)),
        pl.BlockSpec((TB, 1, 1, A_DK), lambda i, h: (i, h, 0, 0)),
        pl.BlockSpec((TB, A_HEADS), lambda i, h: (i, 0)),
        pl.BlockSpec((TB, 1, B_DK, B_DV), lambda i, h: (i, h, 0, 0)),
    ]
    out_specs = [
        pl.BlockSpec((TB, A_DV), lambda i, h: (i, h)),
        pl.BlockSpec((TB, B_DV), lambda i, h: (i, h)),
        pl.BlockSpec((TB, 1, A_DK, A_DV), lambda i, h: (i, h, 0, 0)),
        pl.BlockSpec((TB, 1, 1, A_DK), lambda i, h: (i, h, 0, 0)),
        pl.BlockSpec((TB, LANES), lambda i, h: (i, h)),
        pl.BlockSpec((TB, 1, B_DK, B_DV), lambda i, h: (i, h, 0, 0)),
    ]
    out_shape = [
        jax.ShapeDtypeStruct((nseq, A_WIDTH), BF16),
        jax.ShapeDtypeStruct((nseq, B_WIDTH), BF16),
        jax.ShapeDtypeStruct((nseq, A_HEADS, A_DK, A_DV), F32),
        jax.ShapeDtypeStruct((nseq, A_HEADS, 1, A_DK), F32),
        jax.ShapeDtypeStruct((nseq, A_HEADS * LANES), F32),
        jax.ShapeDtypeStruct((nseq, B_HEADS, B_DK, B_DV), F32),
    ]
    return pl.pallas_call(
        _decode_kernel,
        grid=(nseq // TB, A_HEADS),
        in_specs=in_specs,
        out_specs=out_specs,
        out_shape=out_shape,
        compiler_params=pltpu.CompilerParams(
            dimension_semantics=("parallel", "parallel"), vmem_limit_bytes=VMEM_LIMIT),
        name="decode",
    )(p_s, p_s, p_s, p_s, p_s, p_s, p_s, p_s, p_s, p_s,
      conv_state, conv_state, conv_state, conv_state, conv_state, conv_state,
      conv_w, conv_w, conv_b, conv_b, gate_bias, a_norm_g, w_gate_up, b_gate, b_norm_g,
      c_state, n_state, m_state, s_state)


def _out_kernel(ya_ref, yb_ref, ga_ref, gb_ref, x_ref, wpa_ref, wpb_ref, wo_ref, lng_ref, lnb_ref, o_ref):
    pa = _dot(ya_ref[...], wpa_ref[...])
    pb = _dot(yb_ref[...], wpb_ref[...])
    merged = _sigmoid(ga_ref[...]) * pa + _sigmoid(gb_ref[...]) * pb
    y = _dot(merged.astype(BF16), wo_ref[...])
    r = ALPHA * x_ref[...] + y
    mu = jnp.mean(r, axis=1, keepdims=True)
    rc = r - mu
    var = jnp.mean(rc * rc, axis=1, keepdims=True)
    o_ref[...] = rc * lax.rsqrt(var + LN_EPS) * lng_ref[...] + lnb_ref[...]


def _out(ya, yb, p_all, x, w_pa, w_pb, w_out, ln_g, ln_b, tm):
    t = x.shape[0]
    const = lambda i: (0, 0)
    single = pl.Buffered(1)
    in_specs = [
        pl.BlockSpec((tm, A_WIDTH), lambda i: (i, 0)),
        pl.BlockSpec((tm, B_WIDTH), lambda i: (i, 0)),
        pl.BlockSpec((tm, D_MODEL), lambda i: (i, COL_GA // D_MODEL)),
        pl.BlockSpec((tm, D_MODEL), lambda i: (i, COL_GB // D_MODEL)),
        pl.BlockSpec((tm, D_MODEL), lambda i: (i, 0)),
        pl.BlockSpec((A_WIDTH, D_MODEL), const, pipeline_mode=single),
        pl.BlockSpec((B_WIDTH, D_MODEL), const, pipeline_mode=single),
        pl.BlockSpec((D_MODEL, D_MODEL), const, pipeline_mode=single),
        pl.BlockSpec((1, D_MODEL), const),
        pl.BlockSpec((1, D_MODEL), const),
    ]
    return pl.pallas_call(
        _out_kernel,
        grid=(t // tm,),
        in_specs=in_specs,
        out_specs=pl.BlockSpec((tm, D_MODEL), lambda i: (i, 0)),
        out_shape=jax.ShapeDtypeStruct((t, D_MODEL), F32),
        compiler_params=pltpu.CompilerParams(
            dimension_semantics=("parallel",), vmem_limit_bytes=VMEM_LIMIT),
        name="outproj",
    )(ya, yb, p_all, p_all, x, w_pa, w_pb, w_out, ln_g, ln_b)


def _relayout_w_in(w_in):
    s_qk = 0
    s_av = 2 * A_WIDTH
    s_i = s_av + A_WIDTH
    s_f = s_i + A_HEADS
    s_ao = s_f + A_HEADS
    s_bg = s_ao + 2 * A_WIDTH + 2 * B_KWIDTH + B_WIDTH
    s_bz = s_bg + GATE_RANK
    main = [w_in[:, s_qk:s_i], w_in[:, s_ao:s_bg], w_in[:, s_bz:]]
    gate_cols = []
    pad = jnp.zeros((D_MODEL, LANES - 2 - GATE_RANK), w_in.dtype)
    for h in range(A_HEADS):
        gate_cols += [w_in[:, s_i + h:s_i + h + 1], w_in[:, s_f + h:s_f + h + 1],
                      w_in[:, s_bg:s_bg + GATE_RANK], pad]
    return jnp.concatenate(main + gate_cols, axis=1).astype(BF16)


def kernel(x_prompt, x_sample, state_mlstm_C, state_mlstm_n, state_mlstm_m, state_conv, state_gla_S,
           w_in, conv_w, conv_b, b_i, b_f, a_norm_g, w_gate_up, b_gate, b_norm_g, w_pa, w_pb, w_out,
           ln_g, ln_b):
    batch, seq, _ = x_prompt.shape
    nseq = x_sample.shape[0]
    assert w_in.shape[0] == 1, "single-layer step"
    d = 0

    def layer(a):
        return a.reshape(a.shape[1:])

    wp = _relayout_w_in(layer(w_in))
    wpa, wpb, wo = w_pa[d].astype(BF16), w_pb[d].astype(BF16), w_out[d].astype(BF16)
    cw = conv_w[d]
    cb = conv_b[d][None, :]
    gate_bias = jnp.zeros((A_HEADS, LANES), F32).at[:, 0].set(b_i[d]).at[:, 1].set(b_f[d]).reshape(1, -1)
    ang = a_norm_g[d][None, :]
    bng = b_norm_g[d][None, :]
    wg = w_gate_up[d]
    bgate = b_gate[d][None, :]
    lng, lnb = ln_g[d][None, :], ln_b[d][None, :]

    xp = x_prompt.reshape(batch * seq, D_MODEL)
    p_all = _proj(xp, wp, tm=1024, tn=512)
    ya, p_c, p_n, p_m = _mlstm(p_all, batch, seq, cw, cb, gate_bias, ang)
    yb, p_s = _gla(p_all, batch, seq, wg, bgate, bng)
    y_prompt = _out(ya, yb, p_all, xp, wpa, wpb, wo, lng, lnb, tm=256).reshape(batch, seq, D_MODEL)
    p_conv = p_all.reshape(batch, seq, N_PROJ)[:, seq - (CONV_W - 1):, :2 * A_WIDTH]

    xs = x_sample.reshape(nseq, D_MODEL)
    ps_all = _proj(xs, wp, tm=nseq, tn=512)
    conv_state = state_conv.reshape(nseq, (CONV_W - 1) * 2 * A_WIDTH)
    ya_s, yb_s, s_c, s_n, s_m, s_s = _decode(
        ps_all, conv_state, cw, cb, gate_bias, ang, wg, bgate, bng,
        layer(state_mlstm_C), state_mlstm_n.reshape(nseq, A_HEADS, 1, A_DK), layer(state_mlstm_m),
        layer(state_gla_S))
    y_sample = _out(ya_s, yb_s, ps_all, xs, wpa, wpb, wo, lng, lnb, tm=nseq).reshape(nseq, 1, D_MODEL)
    s_conv = jnp.concatenate([layer(state_conv)[:, 1:, :], ps_all[:, None, :2 * A_WIDTH]], axis=1)

    def stacked(a):
        return a.reshape((1,) + a.shape)

    return (y_prompt, y_sample,
            stacked(p_c), p_n.reshape(1, batch, A_HEADS, A_DK), stacked(p_m[:, :, 0, 0]),
            stacked(p_conv), stacked(p_s),
            stacked(s_c), s_n.reshape(1, nseq, A_HEADS, A_DK),
            stacked(s_m.reshape(nseq, A_HEADS, LANES)[:, :, 0]), stacked(s_conv), stacked(s_s))
```

```python
import functools

import jax
import jax.numpy as jnp
from jax import lax
from jax.experimental import pallas as pl
from jax.experimental.pallas import tpu as pltpu

F32 = jnp.float32
BF16 = jnp.bfloat16

D_MODEL = 2048
A_HEADS = 4
A_WIDTH = 1024
A_DK = 256
A_DV = 256
CONV_W = 4
B_HEADS = 4
B_WIDTH = 1024
B_KWIDTH = 512
B_DK = 128
B_DV = 256
GATE_RANK = 16
GATE_TAU = 16.0
ALPHA = 2.0 ** 0.25
LN_EPS = 1e-5

LANES = 128
SUBLANES = 8
VMEM_LIMIT = 48 * 1024 * 1024

COL_Q = 0
COL_K = A_WIDTH
COL_AV = 2 * A_WIDTH
COL_AO = 3 * A_WIDTH
COL_AZ = 4 * A_WIDTH
COL_BQ = 5 * A_WIDTH
COL_BK = COL_BQ + B_KWIDTH
COL_BV = COL_BK + B_KWIDTH
COL_BZ = COL_BV + B_WIDTH
COL_GA = COL_BZ + B_WIDTH
COL_GB = COL_GA + D_MODEL
COL_GATE = COL_GB + D_MODEL
N_PROJ = COL_GATE + A_HEADS * LANES

MLSTM_CHUNK = 256
GLA_STEP = 256
GLA_BLOCK = 16
GLA_HEADS_PER_STEP = 2


def _dot(a, b):
    return jnp.dot(a, b, preferred_element_type=F32)


def _dot_nt(a, b):
    return lax.dot_general(a, b, (((1,), (1,)), ((), ())), preferred_element_type=F32)


def _dot_tn(a, b):
    return lax.dot_general(a, b, (((0,), (0,)), ((), ())), preferred_element_type=F32)


def _mask_dot(mask_bf16, x):
    hi = x.astype(BF16)
    r1 = x - hi.astype(F32)
    mid = r1.astype(BF16)
    lo = (r1 - mid.astype(F32)).astype(BF16)
    return _dot(mask_bf16, hi) + _dot(mask_bf16, mid) + _dot(mask_bf16, lo)


def _log_sigmoid(z):
    return jnp.minimum(z, 0.0) - jnp.log(1.0 + jnp.exp(-jnp.abs(z)))


def _sigmoid(z):
    return 1.0 / (1.0 + jnp.exp(-z))


def _silu(z):
    return z * _sigmoid(z)


def _proj_kernel(x_ref, w_ref, o_ref, xb_ref):
    @pl.when(pl.program_id(1) == 0)
    def _():
        xb_ref[...] = x_ref[...].astype(BF16)

    o_ref[...] = _dot(xb_ref[...], w_ref[...])


def _proj(x, w, tm, tn):
    t, k = x.shape
    n = w.shape[1]
    return pl.pallas_call(
        _proj_kernel,
        grid=(t // tm, n // tn),
        in_specs=[pl.BlockSpec((tm, k), lambda i, j: (i, 0)),
                  pl.BlockSpec((k, tn), lambda i, j: (0, j))],
        out_specs=pl.BlockSpec((tm, tn), lambda i, j: (i, j)),
        out_shape=jax.ShapeDtypeStruct((t, n), F32),
        scratch_shapes=[pltpu.VMEM((tm, k), BF16)],
        compiler_params=pltpu.CompilerParams(
            dimension_semantics=("parallel", "arbitrary"), vmem_limit_bytes=VMEM_LIMIT),
        name="proj",
    )(x, w)


def _mlstm_kernel(qp_ref, kp_ref, v_ref, ao_ref, az_ref, g_ref, cwq_ref, cwk_ref, cbq_ref, cbk_ref,
                  gb_ref, ng_ref, ya_ref, c_ref, n_ref, m_ref, extq_ref, extk_ref):
    L = MLSTM_CHUNK
    c = pl.program_id(2)

    @pl.when(c == 0)
    def _():
        c_ref[...] = jnp.zeros_like(c_ref)
        n_ref[...] = jnp.zeros_like(n_ref)
        m_ref[...] = jnp.zeros_like(m_ref)
        extq_ref[L:L + SUBLANES, :] = jnp.zeros((SUBLANES, A_DK), F32)
        extk_ref[L:L + SUBLANES, :] = jnp.zeros((SUBLANES, A_DK), F32)

    def conv_silu(x_ref, ext_ref, w_ref, b_ref):
        ext_ref[0:SUBLANES, :] = ext_ref[L:L + SUBLANES, :]
        ext_ref[SUBLANES:SUBLANES + L, :] = x_ref[...]
        acc = b_ref[...] + ext_ref[SUBLANES - 3:SUBLANES - 3 + L, :] * w_ref[0:1, :]
        for j in range(1, CONV_W):
            acc = acc + ext_ref[SUBLANES - 3 + j:SUBLANES - 3 + j + L, :] * w_ref[j:j + 1, :]
        return _silu(acc)

    q = conv_silu(qp_ref, extq_ref, cwq_ref, cbq_ref)
    k = conv_silu(kp_ref, extk_ref, cwk_ref, cbk_ref) * (A_DK ** -0.5)
    v = v_ref[...]

    g = g_ref[...] + gb_ref[...]
    lane = lax.broadcasted_iota(jnp.int32, (L, LANES), 1)
    g2 = jnp.where(lane == 1, _log_sigmoid(g), g)
    row = lax.broadcasted_iota(jnp.int32, (L, L), 0)
    col = lax.broadcasted_iota(jnp.int32, (L, L), 1)
    causal = row >= col
    cum = _mask_dot(causal.astype(BF16), g2)
    x2 = jnp.where(lane == 1, cum, g2)
    x2t = x2.T
    itil_col, b_col = x2[:, 0:1], x2[:, 1:2]
    itil_row, b_row = x2t[0:1, :], x2t[1:2, :]

    m_prev = m_ref[0, 0, 0:1, 0:1]
    dmat = jnp.where(causal, b_col - b_row + itil_row, -jnp.inf)
    inter = b_col + m_prev
    m_t = jnp.maximum(inter, jnp.max(dmat, axis=1, keepdims=True))
    w = jnp.exp(dmat - m_t)
    decay = jnp.exp(inter - m_t)

    qb, kb, vb = q.astype(BF16), k.astype(BF16), v.astype(BF16)
    c_old = c_ref[0, 0]
    n_old = n_ref[0, 0]
    s = _dot_nt(qb, kb) * w
    num = decay * _dot(qb, c_old.astype(BF16)) + _dot(s.astype(BF16), vb)
    den = decay * jnp.sum(q * n_old, axis=1, keepdims=True) + jnp.sum(s, axis=1, keepdims=True)
    h = num / jnp.maximum(jnp.abs(den), jnp.exp(-m_t))

    m_new = m_t[L - 1:L, :]
    b_last = b_col[L - 1:L, :]
    wk = jnp.exp(b_last - b_col + itil_col - m_new)
    dec = jnp.exp(b_last + m_prev - m_new)
    kw = k * wk
    c_ref[0, 0] = dec * c_old + _dot(kw.T.astype(BF16), vb)
    n_ref[0, 0] = dec * n_old + jnp.sum(kw, axis=0, keepdims=True)
    m_ref[0, 0] = jnp.broadcast_to(m_new, (1, LANES))

    mu = jnp.mean(h, axis=1, keepdims=True)
    hc = h - mu
    var = jnp.mean(hc * hc, axis=1, keepdims=True)
    hn = hc * lax.rsqrt(var + LN_EPS) * ng_ref[...]
    ya_ref[...] = (hn * _sigmoid(ao_ref[...]) * _silu(az_ref[...])).astype(ya_ref.dtype)


def _mlstm(p_all, batch, seq, conv_w, conv_b, gate_bias, a_norm_g):
    L = MLSTM_CHUNK
    nc = seq // L
    t = batch * seq
    qblk = A_DK // A_DK

    def rows(b, h, c):
        return b * nc + c

    def pcol(base):
        return pl.BlockSpec((L, A_DK), lambda b, h, c: (rows(b, h, c), base // A_DK + h))

    in_specs = [
        pcol(COL_Q), pcol(COL_K), pcol(COL_AV), pcol(COL_AO), pcol(COL_AZ),
        pl.BlockSpec((L, LANES), lambda b, h, c: (rows(b, h, c), COL_GATE // LANES + h)),
        pl.BlockSpec((CONV_W, A_DK), lambda b, h, c: (0, h)),
        pl.BlockSpec((CONV_W, A_DK), lambda b, h, c: (0, A_HEADS + h)),
        pl.BlockSpec((1, A_DK), lambda b, h, c: (0, h)),
        pl.BlockSpec((1, A_DK), lambda b, h, c: (0, A_HEADS + h)),
        pl.BlockSpec((1, LANES), lambda b, h, c: (0, h)),
        pl.BlockSpec((1, A_DV), lambda b, h, c: (0, h)),
    ]
    out_specs = [
        pl.BlockSpec((L, A_DV), lambda b, h, c: (rows(b, h, c), h)),
        pl.BlockSpec((1, 1, A_DK, A_DV), lambda b, h, c: (b, h, 0, 0)),
        pl.BlockSpec((1, 1, 1, A_DK), lambda b, h, c: (b, h, 0, 0)),
        pl.BlockSpec((1, 1, 1, LANES), lambda b, h, c: (b, h, 0, 0)),
    ]
    out_shape = [
        jax.ShapeDtypeStruct((t, A_WIDTH), BF16),
        jax.ShapeDtypeStruct((batch, A_HEADS, A_DK, A_DV), F32),
        jax.ShapeDtypeStruct((batch, A_HEADS, 1, A_DK), F32),
        jax.ShapeDtypeStruct((batch, A_HEADS, 1, LANES), F32),
    ]
    del qblk
    return pl.pallas_call(
        _mlstm_kernel,
        grid=(batch, A_HEADS, nc),
        in_specs=in_specs,
        out_specs=out_specs,
        out_shape=out_shape,
        scratch_shapes=[pltpu.VMEM((L + 2 * SUBLANES, A_DK), F32),
                        pltpu.VMEM((L + 2 * SUBLANES, A_DK), F32)],
        compiler_params=pltpu.CompilerParams(
            dimension_semantics=("parallel", "parallel", "arbitrary"), vmem_limit_bytes=VMEM_LIMIT),
        name="mlstm",
    )(p_all, p_all, p_all, p_all, p_all, p_all, conv_w, conv_w, conv_b, conv_b, gate_bias, a_norm_g)


def _gla_gate_log(bg_tile, wg, bgate):
    lr = bg_tile[:, 2:2 + GATE_RANK]
    z = _dot(lr.astype(BF16), wg.astype(BF16)) + bgate
    return _log_sigmoid(z) / GATE_TAU


def _gla_kernel(q_ref, k_ref, v_ref, bz_ref, g_ref, wg_ref, bgate_ref, ng_ref,
                yb_ref, s_out_ref, st_ref):
    L = GLA_STEP
    B = GLA_BLOCK
    nb = L // B
    c = pl.program_id(2)

    @pl.when(c == 0)
    def _():
        st_ref[...] = jnp.zeros_like(st_ref)

    row = lax.broadcasted_iota(jnp.int32, (L, L), 0)
    col = lax.broadcasted_iota(jnp.int32, (L, L), 1)
    same = (row // B) == (col // B)
    tri_b = (same & (row >= col)).astype(BF16)
    same_b = same.astype(BF16)
    t_in = lax.broadcasted_iota(jnp.int32, (L, 1), 0) % B
    s_in = col % B

    for hh in range(GLA_HEADS_PER_STEP):
        kcols = slice(hh * B_DK, (hh + 1) * B_DK)
        vcols = slice(hh * B_DV, (hh + 1) * B_DV)
        loga = _gla_gate_log(g_ref[:, hh * LANES:(hh + 1) * LANES], wg_ref[:, kcols], bgate_ref[:, kcols])
        bc = _mask_dot(tri_b, loga)
        bl = _mask_dot(same_b, loga)
        k = k_ref[:, kcols]
        qs = q_ref[:, kcols] * (B_DK ** -0.5)
        qt = (qs * jnp.exp(bc)).astype(BF16)
        kt = (k * jnp.exp(bl - bc)).astype(BF16)
        eb = jnp.exp(bl)
        vb = v_ref[:, vcols].astype(BF16)

        bc3, q3, k3 = (x.reshape(nb, B, B_DK) for x in (bc, qs, k))
        a = jnp.zeros((L, L), F32)
        for s in range(B):
            e = jnp.exp(jnp.minimum(bc3 - bc3[:, s:s + 1, :], 0.0))
            a_col = jnp.sum(q3 * k3[:, s:s + 1, :] * e, axis=2, keepdims=True).reshape(L, 1)
            a = jnp.where(s_in == s, jnp.where(t_in >= s, a_col, 0.0), a)
        a = jnp.where(same, a, 0.0)
        o_diag = _dot(a.astype(BF16), vb)

        st = st_ref[hh]
        o_parts = []
        for j in range(nb):
            lo, hi = j * B, (j + 1) * B
            o_parts.append(o_diag[lo:hi, :] + _dot_nt(qt[lo:hi, :], st.astype(BF16)))
            st = st * eb[lo:lo + 1, :] + _dot_tn(vb[lo:hi, :], kt[lo:hi, :])
        st_ref[hh] = st
        o = jnp.concatenate(o_parts, axis=0)

        on = o * lax.rsqrt(jnp.mean(o * o, axis=1, keepdims=True) + LN_EPS) * ng_ref[:, vcols]
        yb_ref[:, vcols] = (on * _silu(bz_ref[:, vcols])).astype(yb_ref.dtype)

    @pl.when(c == pl.num_programs(2) - 1)
    def _():
        for hh in range(GLA_HEADS_PER_STEP):
            s_out_ref[0, hh] = st_ref[hh].T


def _gla(p_all, batch, seq, w_gate_up, b_gate, b_norm_g):
    L = GLA_STEP
    hps = GLA_HEADS_PER_STEP
    nc = seq // L
    t = batch * seq
    kw, vw = hps * B_DK, hps * B_DV

    def rows(b, h, c):
        return b * nc + c

    in_specs = [
        pl.BlockSpec((L, kw), lambda b, h, c: (rows(b, h, c), COL_BQ // kw + h)),
        pl.BlockSpec((L, kw), lambda b, h, c: (rows(b, h, c), COL_BK // kw + h)),
        pl.BlockSpec((L, vw), lambda b, h, c: (rows(b, h, c), COL_BV // vw + h)),
        pl.BlockSpec((L, vw), lambda b, h, c: (rows(b, h, c), COL_BZ // vw + h)),
        pl.BlockSpec((L, hps * LANES), lambda b, h, c: (rows(b, h, c), COL_GATE // (hps * LANES) + h)),
        pl.BlockSpec((GATE_RANK, kw), lambda b, h, c: (0, h)),
        pl.BlockSpec((1, kw), lambda b, h, c: (0, h)),
        pl.BlockSpec((1, vw), lambda b, h, c: (0, h)),
    ]
    out_specs = [
        pl.BlockSpec((L, vw), lambda b, h, c: (rows(b, h, c), h)),
        pl.BlockSpec((1, hps, B_DK, B_DV), lambda b, h, c: (b, h, 0, 0)),
    ]
    out_shape = [
        jax.ShapeDtypeStruct((t, B_WIDTH), BF16),
        jax.ShapeDtypeStruct((batch, B_HEADS, B_DK, B_DV), F32),
    ]
    return pl.pallas_call(
        _gla_kernel,
        grid=(batch, B_HEADS // hps, nc),
        in_specs=in_specs,
        out_specs=out_specs,
        out_shape=out_shape,
        scratch_shapes=[pltpu.VMEM((hps, B_DV, B_DK), F32)],
        compiler_params=pltpu.CompilerParams(
            dimension_semantics=("parallel", "parallel", "arbitrary"), vmem_limit_bytes=VMEM_LIMIT),
        name="gla",
    )(p_all, p_all, p_all, p_all, p_all, w_gate_up, b_gate, b_norm_g)


DEC_TOKENS = 8


def _decode_kernel(qp_ref, kp_ref, av_ref, ao_ref, az_ref, bq_ref, bk_ref, bv_ref, bz_ref, g_ref,
                   sq0_ref, sq1_ref, sq2_ref, sk0_ref, sk1_ref, sk2_ref,
                   cwq_ref, cwk_ref, cbq_ref, cbk_ref, gb_ref, ang_ref, wg_ref, bgate_ref, bng_ref,
                   c_ref, n_ref, m_ref, s_ref,
                   ya_ref, yb_ref, c_out_ref, n_out_ref, m_out_ref, s_out_ref):
    TB = DEC_TOKENS
    h_idx = pl.program_id(1)

    def conv_silu(s0, s1, s2, x, w_ref, b_ref):
        acc = b_ref[...] + s0[...] * w_ref[0:1, :]
        acc = acc + s1[...] * w_ref[1:2, :]
        acc = acc + s2[...] * w_ref[2:3, :]
        acc = acc + x[...] * w_ref[3:4, :]
        return _silu(acc)

    q = conv_silu(sq0_ref, sq1_ref, sq2_ref, qp_ref, cwq_ref, cbq_ref)
    k = conv_silu(sk0_ref, sk1_ref, sk2_ref, kp_ref, cwk_ref, cbk_ref) * (A_DK ** -0.5)
    v = av_ref[...]
    g = g_ref[...]
    gbias = gb_ref[...]
    itil = g[:, 0:1] + gbias[:, 0:1]
    logf = _log_sigmoid(g[:, 1:2] + gbias[:, 1:2])
    lane_h = lax.broadcasted_iota(jnp.int32, (TB, A_HEADS), 1)
    m_prev = jnp.sum(jnp.where(lane_h == h_idx, m_ref[...], 0.0), axis=1, keepdims=True)
    inter = logf + m_prev
    m_t = jnp.maximum(inter, itil)
    w = jnp.exp(itil - m_t)
    decay = jnp.exp(inter - m_t)
    n_old = n_ref[:, 0, 0, :]
    s = jnp.sum(q * k, axis=1, keepdims=True) * w
    den = decay * jnp.sum(q * n_old, axis=1, keepdims=True) + s
    scale = 1.0 / jnp.maximum(jnp.abs(den), jnp.exp(-m_t))
    kw = k * w
    n_out_ref[:, 0, 0, :] = decay * n_old + kw
    m_out_ref[...] = jnp.broadcast_to(m_t, (TB, LANES))

    rows8 = lax.broadcasted_iota(jnp.int32, (SUBLANES, 1), 0)
    qb, kwb, vb = q.astype(BF16), kw.astype(BF16), v.astype(BF16)
    h_rows = []
    for t in range(TB):
        c_old = c_ref[t, 0]
        sel = rows8 == t
        qc = _dot(qb, c_old.astype(BF16))[t:t + 1, :]
        h_rows.append((decay[t:t + 1, :] * qc + s[t:t + 1, :] * v[t:t + 1, :]) * scale[t:t + 1, :])
        outer = _dot_tn(jnp.where(sel, kwb, jnp.zeros_like(kwb)), vb)
        c_out_ref[t, 0] = decay[t:t + 1, :] * c_old + outer
    h = jnp.concatenate(h_rows, axis=0)
    mu = jnp.mean(h, axis=1, keepdims=True)
    hc = h - mu
    var = jnp.mean(hc * hc, axis=1, keepdims=True)
    hn = hc * lax.rsqrt(var + LN_EPS) * ang_ref[...]
    ya_ref[...] = (hn * _sigmoid(ao_ref[...]) * _silu(az_ref[...])).astype(ya_ref.dtype)

    gq = bq_ref[...] * (B_DK ** -0.5)
    gk = bk_ref[...]
    gv = bv_ref[...]
    loga = _gla_gate_log(g, wg_ref[...], bgate_ref[...])
    eb = jnp.exp(loga)
    a = jnp.sum(gq * gk, axis=1, keepdims=True)
    ebt = jnp.concatenate([eb, jnp.zeros((LANES - TB, B_DK), F32)], axis=0).T
    qeb, gkb, gvb = (gq * eb).astype(BF16), gk.astype(BF16), gv.astype(BF16)
    o_rows = []
    for t in range(TB):
        s_old = s_ref[t, 0]
        sel = rows8 == t
        o_rows.append(_dot(qeb, s_old.astype(BF16))[t:t + 1, :] + a[t:t + 1, :] * gv[t:t + 1, :])
        outer = _dot_tn(jnp.where(sel, gkb, jnp.zeros_like(gkb)), gvb)
        s_out_ref[t, 0] = ebt[:, t:t + 1] * s_old + outer
    o = jnp.concatenate(o_rows, axis=0)
    on = o * lax.rsqrt(jnp.mean(o * o, axis=1, keepdims=True) + LN_EPS) * bng_ref[...]
    yb_ref[...] = (on * _silu(bz_ref[...])).astype(yb_ref.dtype)


def _decode(p_s, conv_state, conv_w, conv_b, gate_bias, a_norm_g, w_gate_up, b_gate, b_norm_g,
            c_state, n_state, m_state, s_state):
    nseq = p_s.shape[0]
    TB = DEC_TOKENS

    def pcol(base, width):
        return pl.BlockSpec((TB, width), lambda i, h: (i, base // width + h))

    def cstate(j, base):
        return pl.BlockSpec((TB, A_DK), lambda i, h: (i, (j * 2 * A_WIDTH + base) // A_DK + h))

    in_specs = [
        pcol(COL_Q, A_DK), pcol(COL_K, A_DK), pcol(COL_AV, A_DV), pcol(COL_AO, A_DV), pcol(COL_AZ, A_DV),
        pcol(COL_BQ, B_DK), pcol(COL_BK, B_DK), pcol(COL_BV, B_DV), pcol(COL_BZ, B_DV),
        pcol(COL_GATE, LANES),
        cstate(0, COL_Q), cstate(1, COL_Q), cstate(2, COL_Q),
        cstate(0, COL_K), cstate(1, COL_K), cstate(2, COL_K),
        pl.BlockSpec((CONV_W, A_DK), lambda i, h: (0, h)),
        pl.BlockSpec((CONV_W, A_DK), lambda i, h: (0, A_HEADS + h)),
        pl.BlockSpec((1, A_DK), lambda i, h: (0, h)),
        pl.BlockSpec((1, A_DK), lambda i, h: (0, A_HEADS + h)),
        pl.BlockSpec((1, LANES), lambda i, h: (0, h)),
        pl.BlockSpec((1, A_DV), lambda i, h: (0, h)),
        pl.BlockSpec((GATE_RANK, B_DK), lambda i, h: (0, h)),
        pl.BlockSpec((1, B_DK), lambda i, h: (0, h)),
        pl.BlockSpec((1, B_DV), lambda i, h: (0, h)),
        pl.BlockSpec((TB, 1, A_DK, A_DV), lambda i, h: (i, h, 0, 0)),
        pl.BlockSpec((TB, 1, 1, A_DK), lambda i, h: (i, h, 0, 0)),
        pl.BlockSpec((TB, A_HEADS), lambda i, h: (i, 0)),
        pl.BlockSpec((TB, 1, B_DK, B_DV), lambda i, h: (i, h, 0, 0)),
    ]
    out_specs = [
        pl.BlockSpec((TB, A_DV), lambda i, h: (i, h)),
        pl.BlockSpec((TB, B_DV), lambda i, h: (i, h)),
        pl.BlockSpec((TB, 1, A_DK, A_DV), lambda i, h: (i, h, 0, 0)),
        pl.BlockSpec((TB, 1, 1, A_DK), lambda i, h: (i, h, 0, 0)),
        pl.BlockSpec((TB, LANES), lambda i, h: (i, h)),
        pl.BlockSpec((TB, 1, B_DK, B_DV), lambda i, h: (i, h, 0, 0)),
    ]
    out_shape = [
        jax.ShapeDtypeStruct((nseq, A_WIDTH), BF16),
        jax.ShapeDtypeStruct((nseq, B_WIDTH), BF16),
        jax.ShapeDtypeStruct((nseq, A_HEADS, A_DK, A_DV), F32),
        jax.ShapeDtypeStruct((nseq, A_HEADS, 1, A_DK), F32),
        jax.ShapeDtypeStruct((nseq, A_HEADS * LANES), F32),
        jax.ShapeDtypeStruct((nseq, B_HEADS, B_DK, B_DV), F32),
    ]
    return pl.pallas_call(
        _decode_kernel,
        grid=(nseq // TB, A_HEADS),
        in_specs=in_specs,
        out_specs=out_specs,
        out_shape=out_shape,
        compiler_params=pltpu.CompilerParams(
            dimension_semantics=("parallel", "parallel"), vmem_limit_bytes=VMEM_LIMIT),
        name="decode",
    )(p_s, p_s, p_s, p_s, p_s, p_s, p_s, p_s, p_s, p_s,
      conv_state, conv_state, conv_state, conv_state, conv_state, conv_state,
      conv_w, conv_w, conv_b, conv_b, gate_bias, a_norm_g, w_gate_up, b_gate, b_norm_g,
      c_state, n_state, m_state, s_state)


def _out_kernel(ya_ref, yb_ref, ga_ref, gb_ref, x_ref, wpa_ref, wpb_ref, wo_ref, lng_ref, lnb_ref, o_ref):
    pa = _dot(ya_ref[...], wpa_ref[...])
    pb = _dot(yb_ref[...], wpb_ref[...])
    merged = _sigmoid(ga_ref[...]) * pa + _sigmoid(gb_ref[...]) * pb
    y = _dot(merged.astype(BF16), wo_ref[...])
    r = ALPHA * x_ref[...] + y
    mu = jnp.mean(r, axis=1, keepdims=True)
    rc = r - mu
    var = jnp.mean(rc * rc, axis=1, keepdims=True)
    o_ref[...] = rc * lax.rsqrt(var + LN_EPS) * lng_ref[...] + lnb_ref[...]


def _out(ya, yb, p_all, x, w_pa, w_pb, w_out, ln_g, ln_b, tm):
    t = x.shape[0]
    const = lambda i: (0, 0)
    single = pl.Buffered(1)
    in_specs = [
        pl.BlockSpec((tm, A_WIDTH), lambda i: (i, 0)),
        pl.BlockSpec((tm, B_WIDTH), lambda i: (i, 0)),
        pl.BlockSpec((tm, D_MODEL), lambda i: (i, COL_GA // D_MODEL)),
        pl.BlockSpec((tm, D_MODEL), lambda i: (i, COL_GB // D_MODEL)),
        pl.BlockSpec((tm, D_MODEL), lambda i: (i, 0)),
        pl.BlockSpec((A_WIDTH, D_MODEL), const, pipeline_mode=single),
        pl.BlockSpec((B_WIDTH, D_MODEL), const, pipeline_mode=single),
        pl.BlockSpec((D_MODEL, D_MODEL), const, pipeline_mode=single),
        pl.BlockSpec((1, D_MODEL), const),
        pl.BlockSpec((1, D_MODEL), const),
    ]
    return pl.pallas_call(
        _out_kernel,
        grid=(t // tm,),
        in_specs=in_specs,
        out_specs=pl.BlockSpec((tm, D_MODEL), lambda i: (i, 0)),
        out_shape=jax.ShapeDtypeStruct((t, D_MODEL), F32),
        compiler_params=pltpu.CompilerParams(
            dimension_semantics=("parallel",), vmem_limit_bytes=VMEM_LIMIT),
        name="outproj",
    )(ya, yb, p_all, p_all, x, w_pa, w_pb, w_out, ln_g, ln_b)


RELAYOUT_TN = 512
RELAYOUT_ROWS = 256
SRC_I = 3 * A_WIDTH
SRC_SHIFT_A = 2 * A_HEADS
SRC_BG = COL_BZ + SRC_SHIFT_A
SRC_SHIFT_B = SRC_SHIFT_A + GATE_RANK


def _relayout_kernel(a_ref, b_ref, g1_ref, g2_ref, o_ref):
    j = pl.program_id(0)
    tn = RELAYOUT_TN
    n_main = COL_GATE // tn
    n_rows = a_ref.shape[0] // RELAYOUT_ROWS

    def rows(r):
        return pl.ds(pl.multiple_of(r * RELAYOUT_ROWS, RELAYOUT_ROWS), RELAYOUT_ROWS)

    @pl.when(j < COL_AO // tn)
    def _():
        o_ref[...] = a_ref[...].astype(BF16)

    def shifted(shift):
        def body(r, carry):
            x = jnp.concatenate([a_ref[rows(r), :], b_ref[rows(r), :]], axis=1)
            o_ref[rows(r), :] = x[:, shift:shift + tn].astype(BF16)
            return carry
        lax.fori_loop(0, n_rows, body, 0)

    @pl.when((j >= COL_AO // tn) & (j < COL_BZ // tn))
    def _():
        shifted(SRC_SHIFT_A)

    @pl.when((j >= COL_BZ // tn) & (j < n_main))
    def _():
        shifted(SRC_SHIFT_B)

    @pl.when(j == n_main)
    def _():
        lane = lax.broadcasted_iota(jnp.int32, (RELAYOUT_ROWS, LANES), 1)

        def body(r, carry):
            g1 = g1_ref[rows(r), :]
            g2 = g2_ref[rows(r), :]
            lr = pltpu.roll(g2, (2 - (SRC_BG - COL_BZ)) % LANES, axis=1)
            for h in range(A_HEADS):
                gi = pltpu.roll(g1, (0 - h) % LANES, axis=1)
                gf = pltpu.roll(g1, (1 - A_HEADS - h) % LANES, axis=1)
                blk = jnp.where(lane == 0, gi, jnp.where(lane == 1, gf,
                                jnp.where(lane < 2 + GATE_RANK, lr, 0.0)))
                o_ref[rows(r), h * LANES:(h + 1) * LANES] = blk.astype(BF16)
            return carry
        lax.fori_loop(0, n_rows, body, 0)


def _relayout_w_in(w_in):
    k, n_src = w_in.shape
    tn = RELAYOUT_TN
    last_lane_blk = n_src // LANES
    return pl.pallas_call(
        _relayout_kernel,
        grid=(N_PROJ // tn,),
        in_specs=[pl.BlockSpec((k, tn), lambda j: (0, j)),
                  pl.BlockSpec((k, LANES), lambda j: (0, jnp.minimum((j + 1) * (tn // LANES), last_lane_blk))),
                  pl.BlockSpec((k, LANES), lambda j: (0, SRC_I // LANES)),
                  pl.BlockSpec((k, LANES), lambda j: (0, COL_BZ // LANES))],
        out_specs=pl.BlockSpec((k, tn), lambda j: (0, j)),
        out_shape=jax.ShapeDtypeStruct((k, N_PROJ), BF16),
        compiler_params=pltpu.CompilerParams(
            dimension_semantics=("parallel",), vmem_limit_bytes=VMEM_LIMIT),
        name="relayout",
    )(w_in, w_in, w_in, w_in)


def kernel(x_prompt, x_sample, state_mlstm_C, state_mlstm_n, state_mlstm_m, state_conv, state_gla_S,
           w_in, conv_w, conv_b, b_i, b_f, a_norm_g, w_gate_up, b_gate, b_norm_g, w_pa, w_pb, w_out,
           ln_g, ln_b):
    batch, seq, _ = x_prompt.shape
    nseq = x_sample.shape[0]
    assert w_in.shape[0] == 1, "single-layer step"
    d = 0

    def layer(a):
        return a.reshape(a.shape[1:])

    wp = _relayout_w_in(layer(w_in))
    wpa, wpb, wo = w_pa[d].astype(BF16), w_pb[d].astype(BF16), w_out[d].astype(BF16)
    cw = conv_w[d]
    cb = conv_b[d][None, :]
    gate_bias = jnp.zeros((A_HEADS, LANES), F32).at[:, 0].set(b_i[d]).at[:, 1].set(b_f[d]).reshape(1, -1)
    ang = a_norm_g[d][None, :]
    bng = b_norm_g[d][None, :]
    wg = w_gate_up[d]
    bgate = b_gate[d][None, :]
    lng, lnb = ln_g[d][None, :], ln_b[d][None, :]

    xp = x_prompt.reshape(batch * seq, D_MODEL)
    p_all = _proj(xp, wp, tm=1024, tn=512)
    ya, p_c, p_n, p_m = _mlstm(p_all, batch, seq, cw, cb, gate_bias, ang)
    yb, p_s = _gla(p_all, batch, seq, wg, bgate, bng)
    y_prompt = _out(ya, yb, p_all, xp, wpa, wpb, wo, lng, lnb, tm=256).reshape(batch, seq, D_MODEL)
    p_conv = p_all.reshape(batch, seq, N_PROJ)[:, seq - (CONV_W - 1):, :2 * A_WIDTH]

    xs = x_sample.reshape(nseq, D_MODEL)
    ps_all = _proj(xs, wp, tm=nseq, tn=512)
    conv_state = state_conv.reshape(nseq, (CONV_W - 1) * 2 * A_WIDTH)
    ya_s, yb_s, s_c, s_n, s_m, s_s = _decode(
        ps_all, conv_state, cw, cb, gate_bias, ang, wg, bgate, bng,
        layer(state_mlstm_C), state_mlstm_n.reshape(nseq, A_HEADS, 1, A_DK), layer(state_mlstm_m),
        layer(state_gla_S))
    y_sample = _out(ya_s, yb_s, ps_all, xs, wpa, wpb, wo, lng, lnb, tm=nseq).reshape(nseq, 1, D_MODEL)
    s_conv = jnp.concatenate([layer(state_conv)[:, 1:, :], ps_all[:, None, :2 * A_WIDTH]], axis=1)

    def stacked(a):
        return a.reshape((1,) + a.shape)

    return (y_prompt, y_sample,
            stacked(p_c), p_n.reshape(1, batch, A_HEADS, A_DK), stacked(p_m[:, :, 0, 0]),
            stacked(p_conv), stacked(p_s),
            stacked(s_c), s_n.reshape(1, nseq, A_HEADS, A_DK),
            stacked(s_m.reshape(nseq, A_HEADS, LANES)[:, :, 0]), stacked(s_conv), stacked(s_s))
```

```python
import functools

import jax
import jax.numpy as jnp
from jax import lax
from jax.experimental import pallas as pl
from jax.experimental.pallas import tpu as pltpu

F32 = jnp.float32
BF16 = jnp.bfloat16

D_MODEL = 2048
A_HEADS = 4
A_WIDTH = 1024
A_DK = 256
A_DV = 256
CONV_W = 4
B_HEADS = 4
B_WIDTH = 1024
B_KWIDTH = 512
B_DK = 128
B_DV = 256
GATE_RANK = 16
GATE_TAU = 16.0
ALPHA = 2.0 ** 0.25
LN_EPS = 1e-5

LANES = 128
SUBLANES = 8
VMEM_LIMIT = 48 * 1024 * 1024

COL_Q = 0
COL_K = A_WIDTH
COL_AV = 2 * A_WIDTH
COL_AO = 3 * A_WIDTH
COL_AZ = 4 * A_WIDTH
COL_BQ = 5 * A_WIDTH
COL_BK = COL_BQ + B_KWIDTH
COL_BV = COL_BK + B_KWIDTH
COL_BZ = COL_BV + B_WIDTH
COL_GA = COL_BZ + B_WIDTH
COL_GB = COL_GA + D_MODEL
COL_GATE = COL_GB + D_MODEL
N_PROJ = COL_GATE + A_HEADS * LANES

MLSTM_CHUNK = 256
GLA_STEP = 256
GLA_BLOCK = 16
GLA_HEADS_PER_STEP = 2


def _dot(a, b):
    return jnp.dot(a, b, preferred_element_type=F32)


def _dot_nt(a, b):
    return lax.dot_general(a, b, (((1,), (1,)), ((), ())), preferred_element_type=F32)


def _dot_tn(a, b):
    return lax.dot_general(a, b, (((0,), (0,)), ((), ())), preferred_element_type=F32)


def _mask_dot(mask_bf16, x):
    hi = x.astype(BF16)
    r1 = x - hi.astype(F32)
    mid = r1.astype(BF16)
    lo = (r1 - mid.astype(F32)).astype(BF16)
    return _dot(mask_bf16, hi) + _dot(mask_bf16, mid) + _dot(mask_bf16, lo)


def _log_sigmoid(z):
    return jnp.minimum(z, 0.0) - jnp.log(1.0 + jnp.exp(-jnp.abs(z)))


def _sigmoid(z):
    return 1.0 / (1.0 + jnp.exp(-z))


def _silu(z):
    return z * _sigmoid(z)


def _proj_kernel(x_ref, w_ref, o_ref, xb_ref):
    @pl.when(pl.program_id(1) == 0)
    def _():
        xb_ref[...] = x_ref[...].astype(BF16)

    o_ref[...] = _dot_nt(xb_ref[...], w_ref[...])


def _proj(x, w_t, tm, tn):
    t, k = x.shape
    n = w_t.shape[0]
    return pl.pallas_call(
        _proj_kernel,
        grid=(t // tm, n // tn),
        in_specs=[pl.BlockSpec((tm, k), lambda i, j: (i, 0)),
                  pl.BlockSpec((tn, k), lambda i, j: (j, 0))],
        out_specs=pl.BlockSpec((tm, tn), lambda i, j: (i, j)),
        out_shape=jax.ShapeDtypeStruct((t, n), F32),
        scratch_shapes=[pltpu.VMEM((tm, k), BF16)],
        compiler_params=pltpu.CompilerParams(
            dimension_semantics=("parallel", "arbitrary"), vmem_limit_bytes=VMEM_LIMIT),
        name="proj",
    )(x, w_t)


def _mlstm_kernel(qp_ref, kp_ref, v_ref, ao_ref, az_ref, g_ref, cwq_ref, cwk_ref, cbq_ref, cbk_ref,
                  gb_ref, ng_ref, ya_ref, c_ref, n_ref, m_ref, extq_ref, extk_ref):
    L = MLSTM_CHUNK
    c = pl.program_id(2)

    @pl.when(c == 0)
    def _():
        c_ref[...] = jnp.zeros_like(c_ref)
        n_ref[...] = jnp.zeros_like(n_ref)
        m_ref[...] = jnp.zeros_like(m_ref)
        extq_ref[L:L + SUBLANES, :] = jnp.zeros((SUBLANES, A_DK), F32)
        extk_ref[L:L + SUBLANES, :] = jnp.zeros((SUBLANES, A_DK), F32)

    def conv_silu(x_ref, ext_ref, w_ref, b_ref):
        ext_ref[0:SUBLANES, :] = ext_ref[L:L + SUBLANES, :]
        ext_ref[SUBLANES:SUBLANES + L, :] = x_ref[...]
        acc = b_ref[...] + ext_ref[SUBLANES - 3:SUBLANES - 3 + L, :] * w_ref[0:1, :]
        for j in range(1, CONV_W):
            acc = acc + ext_ref[SUBLANES - 3 + j:SUBLANES - 3 + j + L, :] * w_ref[j:j + 1, :]
        return _silu(acc)

    q = conv_silu(qp_ref, extq_ref, cwq_ref, cbq_ref)
    k = conv_silu(kp_ref, extk_ref, cwk_ref, cbk_ref) * (A_DK ** -0.5)
    v = v_ref[...]

    g = g_ref[...] + gb_ref[...]
    lane = lax.broadcasted_iota(jnp.int32, (L, LANES), 1)
    g2 = jnp.where(lane == 1, _log_sigmoid(g), g)
    row = lax.broadcasted_iota(jnp.int32, (L, L), 0)
    col = lax.broadcasted_iota(jnp.int32, (L, L), 1)
    causal = row >= col
    cum = _mask_dot(causal.astype(BF16), g2)
    x2 = jnp.where(lane == 1, cum, g2)
    x2t = x2.T
    itil_col, b_col = x2[:, 0:1], x2[:, 1:2]
    itil_row, b_row = x2t[0:1, :], x2t[1:2, :]

    m_prev = m_ref[0, 0, 0:1, 0:1]
    dmat = jnp.where(causal, b_col - b_row + itil_row, -jnp.inf)
    inter = b_col + m_prev
    m_t = jnp.maximum(inter, jnp.max(dmat, axis=1, keepdims=True))
    w = jnp.exp(dmat - m_t)
    decay = jnp.exp(inter - m_t)

    qb, kb, vb = q.astype(BF16), k.astype(BF16), v.astype(BF16)
    c_old = c_ref[0, 0]
    n_old = n_ref[0, 0]
    s = _dot_nt(qb, kb) * w
    num = decay * _dot(qb, c_old.astype(BF16)) + _dot(s.astype(BF16), vb)
    den = decay * jnp.sum(q * n_old, axis=1, keepdims=True) + jnp.sum(s, axis=1, keepdims=True)
    h = num / jnp.maximum(jnp.abs(den), jnp.exp(-m_t))

    m_new = m_t[L - 1:L, :]
    b_last = b_col[L - 1:L, :]
    wk = jnp.exp(b_last - b_col + itil_col - m_new)
    dec = jnp.exp(b_last + m_prev - m_new)
    kw = k * wk
    c_ref[0, 0] = dec * c_old + _dot(kw.T.astype(BF16), vb)
    n_ref[0, 0] = dec * n_old + jnp.sum(kw, axis=0, keepdims=True)
    m_ref[0, 0] = jnp.broadcast_to(m_new, (1, LANES))

    mu = jnp.mean(h, axis=1, keepdims=True)
    hc = h - mu
    var = jnp.mean(hc * hc, axis=1, keepdims=True)
    hn = hc * lax.rsqrt(var + LN_EPS) * ng_ref[...]
    ya_ref[...] = (hn * _sigmoid(ao_ref[...]) * _silu(az_ref[...])).astype(ya_ref.dtype)


def _mlstm(p_all, batch, seq, conv_w, conv_b, gate_bias, a_norm_g):
    L = MLSTM_CHUNK
    nc = seq // L
    t = batch * seq
    qblk = A_DK // A_DK

    def rows(b, h, c):
        return b * nc + c

    def pcol(base):
        return pl.BlockSpec((L, A_DK), lambda b, h, c: (rows(b, h, c), base // A_DK + h))

    in_specs = [
        pcol(COL_Q), pcol(COL_K), pcol(COL_AV), pcol(COL_AO), pcol(COL_AZ),
        pl.BlockSpec((L, LANES), lambda b, h, c: (rows(b, h, c), COL_GATE // LANES + h)),
        pl.BlockSpec((CONV_W, A_DK), lambda b, h, c: (0, h)),
        pl.BlockSpec((CONV_W, A_DK), lambda b, h, c: (0, A_HEADS + h)),
        pl.BlockSpec((1, A_DK), lambda b, h, c: (0, h)),
        pl.BlockSpec((1, A_DK), lambda b, h, c: (0, A_HEADS + h)),
        pl.BlockSpec((1, LANES), lambda b, h, c: (0, h)),
        pl.BlockSpec((1, A_DV), lambda b, h, c: (0, h)),
    ]
    out_specs = [
        pl.BlockSpec((L, A_DV), lambda b, h, c: (rows(b, h, c), h)),
        pl.BlockSpec((1, 1, A_DK, A_DV), lambda b, h, c: (b, h, 0, 0)),
        pl.BlockSpec((1, 1, 1, A_DK), lambda b, h, c: (b, h, 0, 0)),
        pl.BlockSpec((1, 1, 1, LANES), lambda b, h, c: (b, h, 0, 0)),
    ]
    out_shape = [
        jax.ShapeDtypeStruct((t, A_WIDTH), BF16),
        jax.ShapeDtypeStruct((batch, A_HEADS, A_DK, A_DV), F32),
        jax.ShapeDtypeStruct((batch, A_HEADS, 1, A_DK), F32),
        jax.ShapeDtypeStruct((batch, A_HEADS, 1, LANES), F32),
    ]
    del qblk
    return pl.pallas_call(
        _mlstm_kernel,
        grid=(batch, A_HEADS, nc),
        in_specs=in_specs,
        out_specs=out_specs,
        out_shape=out_shape,
        scratch_shapes=[pltpu.VMEM((L + 2 * SUBLANES, A_DK), F32),
                        pltpu.VMEM((L + 2 * SUBLANES, A_DK), F32)],
        compiler_params=pltpu.CompilerParams(
            dimension_semantics=("parallel", "parallel", "arbitrary"), vmem_limit_bytes=VMEM_LIMIT),
        name="mlstm",
    )(p_all, p_all, p_all, p_all, p_all, p_all, conv_w, conv_w, conv_b, conv_b, gate_bias, a_norm_g)


def _gla_gate_log(bg_tile, wg, bgate):
    lr = bg_tile[:, 2:2 + GATE_RANK]
    z = _dot(lr.astype(BF16), wg.astype(BF16)) + bgate
    return _log_sigmoid(z) / GATE_TAU


def _gla_kernel(q_ref, k_ref, v_ref, bz_ref, g_ref, wg_ref, bgate_ref, ng_ref,
                yb_ref, s_out_ref, st_ref):
    L = GLA_STEP
    B = GLA_BLOCK
    nb = L // B
    c = pl.program_id(2)

    @pl.when(c == 0)
    def _():
        st_ref[...] = jnp.zeros_like(st_ref)

    row = lax.broadcasted_iota(jnp.int32, (L, L), 0)
    col = lax.broadcasted_iota(jnp.int32, (L, L), 1)
    same = (row // B) == (col // B)
    tri_b = (same & (row >= col)).astype(BF16)
    same_b = same.astype(BF16)
    t_in = lax.broadcasted_iota(jnp.int32, (L, 1), 0) % B
    s_in = col % B

    for hh in range(GLA_HEADS_PER_STEP):
        kcols = slice(hh * B_DK, (hh + 1) * B_DK)
        vcols = slice(hh * B_DV, (hh + 1) * B_DV)
        loga = _gla_gate_log(g_ref[:, hh * LANES:(hh + 1) * LANES], wg_ref[:, kcols], bgate_ref[:, kcols])
        bc = _mask_dot(tri_b, loga)
        bl = _mask_dot(same_b, loga)
        k = k_ref[:, kcols]
        qs = q_ref[:, kcols] * (B_DK ** -0.5)
        qt = (qs * jnp.exp(bc)).astype(BF16)
        kt = (k * jnp.exp(bl - bc)).astype(BF16)
        eb = jnp.exp(bl)
        vb = v_ref[:, vcols].astype(BF16)

        bc3, q3, k3 = (x.reshape(nb, B, B_DK) for x in (bc, qs, k))
        a = jnp.zeros((L, L), F32)
        for s in range(B):
            e = jnp.exp(jnp.minimum(bc3 - bc3[:, s:s + 1, :], 0.0))
            a_col = jnp.sum(q3 * k3[:, s:s + 1, :] * e, axis=2, keepdims=True).reshape(L, 1)
            a = jnp.where(s_in == s, jnp.where(t_in >= s, a_col, 0.0), a)
        a = jnp.where(same, a, 0.0)
        o_diag = _dot(a.astype(BF16), vb)

        st = st_ref[hh]
        o_parts = []
        for j in range(nb):
            lo, hi = j * B, (j + 1) * B
            o_parts.append(o_diag[lo:hi, :] + _dot_nt(qt[lo:hi, :], st.astype(BF16)))
            st = st * eb[lo:lo + 1, :] + _dot_tn(vb[lo:hi, :], kt[lo:hi, :])
        st_ref[hh] = st
        o = jnp.concatenate(o_parts, axis=0)

        on = o * lax.rsqrt(jnp.mean(o * o, axis=1, keepdims=True) + LN_EPS) * ng_ref[:, vcols]
        yb_ref[:, vcols] = (on * _silu(bz_ref[:, vcols])).astype(yb_ref.dtype)

    @pl.when(c == pl.num_programs(2) - 1)
    def _():
        for hh in range(GLA_HEADS_PER_STEP):
            s_out_ref[0, hh] = st_ref[hh].T


def _gla(p_all, batch, seq, w_gate_up, b_gate, b_norm_g):
    L = GLA_STEP
    hps = GLA_HEADS_PER_STEP
    nc = seq // L
    t = batch * seq
    kw, vw = hps * B_DK, hps * B_DV

    def rows(b, h, c):
        return b * nc + c

    in_specs = [
        pl.BlockSpec((L, kw), lambda b, h, c: (rows(b, h, c), COL_BQ // kw + h)),
        pl.BlockSpec((L, kw), lambda b, h, c: (rows(b, h, c), COL_BK // kw + h)),
        pl.BlockSpec((L, vw), lambda b, h, c: (rows(b, h, c), COL_BV // vw + h)),
        pl.BlockSpec((L, vw), lambda b, h, c: (rows(b, h, c), COL_BZ // vw + h)),
        pl.BlockSpec((L, hps * LANES), lambda b, h, c: (rows(b, h, c), COL_GATE // (hps * LANES) + h)),
        pl.BlockSpec((GATE_RANK, kw), lambda b, h, c: (0, h)),
        pl.BlockSpec((1, kw), lambda b, h, c: (0, h)),
        pl.BlockSpec((1, vw), lambda b, h, c: (0, h)),
    ]
    out_specs = [
        pl.BlockSpec((L, vw), lambda b, h, c: (rows(b, h, c), h)),
        pl.BlockSpec((1, hps, B_DK, B_DV), lambda b, h, c: (b, h, 0, 0)),
    ]
    out_shape = [
        jax.ShapeDtypeStruct((t, B_WIDTH), BF16),
        jax.ShapeDtypeStruct((batch, B_HEADS, B_DK, B_DV), F32),
    ]
    return pl.pallas_call(
        _gla_kernel,
        grid=(batch, B_HEADS // hps, nc),
        in_specs=in_specs,
        out_specs=out_specs,
        out_shape=out_shape,
        scratch_shapes=[pltpu.VMEM((hps, B_DV, B_DK), F32)],
        compiler_params=pltpu.CompilerParams(
            dimension_semantics=("parallel", "parallel", "arbitrary"), vmem_limit_bytes=VMEM_LIMIT),
        name="gla",
    )(p_all, p_all, p_all, p_all, p_all, w_gate_up, b_gate, b_norm_g)


DEC_TOKENS = 8


def _decode_kernel(qp_ref, kp_ref, av_ref, ao_ref, az_ref, bq_ref, bk_ref, bv_ref, bz_ref, g_ref,
                   sq0_ref, sq1_ref, sq2_ref, sk0_ref, sk1_ref, sk2_ref,
                   cwq_ref, cwk_ref, cbq_ref, cbk_ref, gb_ref, ang_ref, wg_ref, bgate_ref, bng_ref,
                   c_ref, n_ref, m_ref, s_ref,
                   ya_ref, yb_ref, c_out_ref, n_out_ref, m_out_ref, s_out_ref):
    TB = DEC_TOKENS
    h_idx = pl.program_id(1)

    def conv_silu(s0, s1, s2, x, w_ref, b_ref):
        acc = b_ref[...] + s0[...] * w_ref[0:1, :]
        acc = acc + s1[...] * w_ref[1:2, :]
        acc = acc + s2[...] * w_ref[2:3, :]
        acc = acc + x[...] * w_ref[3:4, :]
        return _silu(acc)

    q = conv_silu(sq0_ref, sq1_ref, sq2_ref, qp_ref, cwq_ref, cbq_ref)
    k = conv_silu(sk0_ref, sk1_ref, sk2_ref, kp_ref, cwk_ref, cbk_ref) * (A_DK ** -0.5)
    v = av_ref[...]
    g = g_ref[...]
    gbias = gb_ref[...]
    itil = g[:, 0:1] + gbias[:, 0:1]
    logf = _log_sigmoid(g[:, 1:2] + gbias[:, 1:2])
    lane_h = lax.broadcasted_iota(jnp.int32, (TB, A_HEADS), 1)
    m_prev = jnp.sum(jnp.where(lane_h == h_idx, m_ref[...], 0.0), axis=1, keepdims=True)
    inter = logf + m_prev
    m_t = jnp.maximum(inter, itil)
    w = jnp.exp(itil - m_t)
    decay = jnp.exp(inter - m_t)
    n_old = n_ref[:, 0, 0, :]
    s = jnp.sum(q * k, axis=1, keepdims=True) * w
    den = decay * jnp.sum(q * n_old, axis=1, keepdims=True) + s
    scale = 1.0 / jnp.maximum(jnp.abs(den), jnp.exp(-m_t))
    kw = k * w
    n_out_ref[:, 0, 0, :] = decay * n_old + kw
    m_out_ref[...] = jnp.broadcast_to(m_t, (TB, LANES))

    rows8 = lax.broadcasted_iota(jnp.int32, (SUBLANES, 1), 0)
    qb, kwb, vb = q.astype(BF16), kw.astype(BF16), v.astype(BF16)
    h_rows = []
    for t in range(TB):
        c_old = c_ref[t, 0]
        sel = rows8 == t
        qc = _dot(qb, c_old.astype(BF16))[t:t + 1, :]
        h_rows.append((decay[t:t + 1, :] * qc + s[t:t + 1, :] * v[t:t + 1, :]) * scale[t:t + 1, :])
        outer = _dot_tn(jnp.where(sel, kwb, jnp.zeros_like(kwb)), vb)
        c_out_ref[t, 0] = decay[t:t + 1, :] * c_old + outer
    h = jnp.concatenate(h_rows, axis=0)
    mu = jnp.mean(h, axis=1, keepdims=True)
    hc = h - mu
    var = jnp.mean(hc * hc, axis=1, keepdims=True)
    hn = hc * lax.rsqrt(var + LN_EPS) * ang_ref[...]
    ya_ref[...] = (hn * _sigmoid(ao_ref[...]) * _silu(az_ref[...])).astype(ya_ref.dtype)

    gq = bq_ref[...] * (B_DK ** -0.5)
    gk = bk_ref[...]
    gv = bv_ref[...]
    loga = _gla_gate_log(g, wg_ref[...], bgate_ref[...])
    eb = jnp.exp(loga)
    a = jnp.sum(gq * gk, axis=1, keepdims=True)
    ebt = jnp.concatenate([eb, jnp.zeros((LANES - TB, B_DK), F32)], axis=0).T
    qeb, gkb, gvb = (gq * eb).astype(BF16), gk.astype(BF16), gv.astype(BF16)
    o_rows = []
    for t in range(TB):
        s_old = s_ref[t, 0]
        sel = rows8 == t
        o_rows.append(_dot(qeb, s_old.astype(BF16))[t:t + 1, :] + a[t:t + 1, :] * gv[t:t + 1, :])
        outer = _dot_tn(jnp.where(sel, gkb, jnp.zeros_like(gkb)), gvb)
        s_out_ref[t, 0] = ebt[:, t:t + 1] * s_old + outer
    o = jnp.concatenate(o_rows, axis=0)
    on = o * lax.rsqrt(jnp.mean(o * o, axis=1, keepdims=True) + LN_EPS) * bng_ref[...]
    yb_ref[...] = (on * _silu(bz_ref[...])).astype(yb_ref.dtype)


def _decode(p_s, conv_state, conv_w, conv_b, gate_bias, a_norm_g, w_gate_up, b_gate, b_norm_g,
            c_state, n_state, m_state, s_state):
    nseq = p_s.shape[0]
    TB = DEC_TOKENS

    def pcol(base, width):
        return pl.BlockSpec((TB, width), lambda i, h: (i, base // width + h))

    def cstate(j, base):
        return pl.BlockSpec((TB, A_DK), lambda i, h: (i, (j * 2 * A_WIDTH + base) // A_DK + h))

    in_specs = [
        pcol(COL_Q, A_DK), pcol(COL_K, A_DK), pcol(COL_AV, A_DV), pcol(COL_AO, A_DV), pcol(COL_AZ, A_DV),
        pcol(COL_BQ, B_DK), pcol(COL_BK, B_DK), pcol(COL_BV, B_DV), pcol(COL_BZ, B_DV),
        pcol(COL_GATE, LANES),
        cstate(0, COL_Q), cstate(1, COL_Q), cstate(2, COL_Q),
        cstate(0, COL_K), cstate(1, COL_K), cstate(2, COL_K),
        pl.BlockSpec((CONV_W, A_DK), lambda i, h: (0, h)),
        pl.BlockSpec((CONV_W, A_DK), lambda i, h: (0, A_HEADS + h)),
        pl.BlockSpec((1, A_DK), lambda i, h: (0, h)),
        pl.BlockSpec((1, A_DK), lambda i, h: (0, A_HEADS + h)),
        pl.BlockSpec((1, LANES), lambda i, h: (0, h)),
        pl.BlockSpec((1, A_DV), lambda i, h: (0, h)),
        pl.BlockSpec((GATE_RANK, B_DK), lambda i, h: (0, h)),
        pl.BlockSpec((1, B_DK), lambda i, h: (0, h)),
        pl.BlockSpec((1, B_DV), lambda i, h: (0, h)),
        pl.BlockSpec((TB, 1, A_DK, A_DV), lambda i, h: (i, h, 0, 0)),
        pl.BlockSpec((TB, 1, 1, A_DK), lambda i, h: (i, h, 0, 0)),
        pl.BlockSpec((TB, A_HEADS), lambda i, h: (i, 0)),
        pl.BlockSpec((TB, 1, B_DK, B_DV), lambda i, h: (i, h, 0, 0)),
    ]
    out_specs = [
        pl.BlockSpec((TB, A_DV), lambda i, h: (i, h)),
        pl.BlockSpec((TB, B_DV), lambda i, h: (i, h)),
        pl.BlockSpec((TB, 1, A_DK, A_DV), lambda i, h: (i, h, 0, 0)),
        pl.BlockSpec((TB, 1, 1, A_DK), lambda i, h: (i, h, 0, 0)),
        pl.BlockSpec((TB, LANES), lambda i, h: (i, h)),
        pl.BlockSpec((TB, 1, B_DK, B_DV), lambda i, h: (i, h, 0, 0)),
    ]
    out_shape = [
        jax.ShapeDtypeStruct((nseq, A_WIDTH), BF16),
        jax.ShapeDtypeStruct((nseq, B_WIDTH), BF16),
        jax.ShapeDtypeStruct((nseq, A_HEADS, A_DK, A_DV), F32),
        jax.ShapeDtypeStruct((nseq, A_HEADS, 1, A_DK), F32),
        jax.ShapeDtypeStruct((nseq, A_HEADS * LANES), F32),
        jax.ShapeDtypeStruct((nseq, B_HEADS, B_DK, B_DV), F32),
    ]
    return pl.pallas_call(
        _decode_kernel,
        grid=(nseq // TB, A_HEADS),
        in_specs=in_specs,
        out_specs=out_specs,
        out_shape=out_shape,
        compiler_params=pltpu.CompilerParams(
            dimension_semantics=("parallel", "parallel"), vmem_limit_bytes=VMEM_LIMIT),
        name="decode",
    )(p_s, p_s, p_s, p_s, p_s, p_s, p_s, p_s, p_s, p_s,
      conv_state, conv_state, conv_state, conv_state, conv_state, conv_state,
      conv_w, conv_w, conv_b, conv_b, gate_bias, a_norm_g, w_gate_up, b_gate, b_norm_g,
      c_state, n_state, m_state, s_state)


def _out_kernel(ya_ref, yb_ref, ga_ref, gb_ref, x_ref, wpa_ref, wpb_ref, wo_ref, lng_ref, lnb_ref, o_ref):
    pa = _dot(ya_ref[...], wpa_ref[...])
    pb = _dot(yb_ref[...], wpb_ref[...])
    merged = _sigmoid(ga_ref[...]) * pa + _sigmoid(gb_ref[...]) * pb
    y = _dot(merged.astype(BF16), wo_ref[...])
    r = ALPHA * x_ref[...] + y
    mu = jnp.mean(r, axis=1, keepdims=True)
    rc = r - mu
    var = jnp.mean(rc * rc, axis=1, keepdims=True)
    o_ref[...] = rc * lax.rsqrt(var + LN_EPS) * lng_ref[...] + lnb_ref[...]


def _out(ya, yb, p_all, x, w_pa, w_pb, w_out, ln_g, ln_b, tm):
    t = x.shape[0]
    const = lambda i: (0, 0)
    single = pl.Buffered(1)
    in_specs = [
        pl.BlockSpec((tm, A_WIDTH), lambda i: (i, 0)),
        pl.BlockSpec((tm, B_WIDTH), lambda i: (i, 0)),
        pl.BlockSpec((tm, D_MODEL), lambda i: (i, COL_GA // D_MODEL)),
        pl.BlockSpec((tm, D_MODEL), lambda i: (i, COL_GB // D_MODEL)),
        pl.BlockSpec((tm, D_MODEL), lambda i: (i, 0)),
        pl.BlockSpec((A_WIDTH, D_MODEL), const, pipeline_mode=single),
        pl.BlockSpec((B_WIDTH, D_MODEL), const, pipeline_mode=single),
        pl.BlockSpec((D_MODEL, D_MODEL), const, pipeline_mode=single),
        pl.BlockSpec((1, D_MODEL), const),
        pl.BlockSpec((1, D_MODEL), const),
    ]
    return pl.pallas_call(
        _out_kernel,
        grid=(t // tm,),
        in_specs=in_specs,
        out_specs=pl.BlockSpec((tm, D_MODEL), lambda i: (i, 0)),
        out_shape=jax.ShapeDtypeStruct((t, D_MODEL), F32),
        compiler_params=pltpu.CompilerParams(
            dimension_semantics=("parallel",), vmem_limit_bytes=VMEM_LIMIT),
        name="outproj",
    )(ya, yb, p_all, p_all, x, w_pa, w_pb, w_out, ln_g, ln_b)


RELAYOUT_TN = 512
SRC_I = 3 * A_WIDTH
SRC_SHIFT_A = 2 * A_HEADS
SRC_BG = COL_BZ + SRC_SHIFT_A
SRC_SHIFT_B = SRC_SHIFT_A + GATE_RANK
GATE_SRC_ROWS = 4 * SUBLANES


def _relayout_kernel(src_ref, gif_ref, gb0_ref, gb1_ref, o_ref):
    j = pl.program_id(0)
    tn = RELAYOUT_TN
    n_main = COL_GATE // tn

    @pl.when(j < n_main)
    def _():
        o_ref[...] = src_ref[...].astype(BF16)

    @pl.when(j == n_main)
    def _():
        g = jnp.concatenate([gif_ref[...], gb0_ref[...], gb1_ref[...],
                             jnp.zeros((SUBLANES, src_ref.shape[1]), F32)], axis=0).astype(BF16)
        r = lax.broadcasted_iota(jnp.int32, (tn, GATE_SRC_ROWS), 0)
        c = lax.broadcasted_iota(jnp.int32, (tn, GATE_SRC_ROWS), 1)
        h, l = r // LANES, r % LANES
        sel = (((l == 0) & (c == h)) | ((l == 1) & (c == A_HEADS + h))
               | ((l >= 2) & (l < 2 + GATE_RANK) & (c == l + (2 * A_HEADS - 2))))
        o_ref[...] = _dot(sel.astype(BF16), g).astype(BF16)


def _relayout_w_in(w_in_t):
    n_src, k = w_in_t.shape
    tn = RELAYOUT_TN

    def src_row(j):
        shift = jnp.where(j < COL_AO // tn, 0, jnp.where(j < COL_BZ // tn, SRC_SHIFT_A, SRC_SHIFT_B))
        group = jnp.minimum(j * (tn // SUBLANES) + shift // SUBLANES, (n_src - tn) // SUBLANES)
        return group * SUBLANES

    return pl.pallas_call(
        _relayout_kernel,
        grid=(N_PROJ // tn,),
        in_specs=[pl.BlockSpec((pl.Element(tn), pl.Element(k)), lambda j: (src_row(j), 0)),
                  pl.BlockSpec((SUBLANES, k), lambda j: (SRC_I // SUBLANES, 0)),
                  pl.BlockSpec((SUBLANES, k), lambda j: (SRC_BG // SUBLANES, 0)),
                  pl.BlockSpec((SUBLANES, k), lambda j: (SRC_BG // SUBLANES + 1, 0))],
        out_specs=pl.BlockSpec((tn, k), lambda j: (j, 0)),
        out_shape=jax.ShapeDtypeStruct((N_PROJ, k), BF16),
        compiler_params=pltpu.CompilerParams(
            dimension_semantics=("parallel",), vmem_limit_bytes=VMEM_LIMIT),
        name="relayout",
    )(w_in_t, w_in_t, w_in_t, w_in_t)


def kernel(x_prompt, x_sample, state_mlstm_C, state_mlstm_n, state_mlstm_m, state_conv, state_gla_S,
           w_in, conv_w, conv_b, b_i, b_f, a_norm_g, w_gate_up, b_gate, b_norm_g, w_pa, w_pb, w_out,
           ln_g, ln_b):
    batch, seq, _ = x_prompt.shape
    nseq = x_sample.shape[0]
    assert w_in.shape[0] == 1, "single-layer step"
    d = 0

    def layer(a):
        return a.reshape(a.shape[1:])

    wp = _relayout_w_in(layer(w_in).T)
    wpa, wpb, wo = w_pa[d].astype(BF16), w_pb[d].astype(BF16), w_out[d].astype(BF16)
    cw = conv_w[d]
    cb = conv_b[d][None, :]
    gate_bias = jnp.zeros((A_HEADS, LANES), F32).at[:, 0].set(b_i[d]).at[:, 1].set(b_f[d]).reshape(1, -1)
    ang = a_norm_g[d][None, :]
    bng = b_norm_g[d][None, :]
    wg = w_gate_up[d]
    bgate = b_gate[d][None, :]
    lng, lnb = ln_g[d][None, :], ln_b[d][None, :]

    xp = x_prompt.reshape(batch * seq, D_MODEL)
    p_all = _proj(xp, wp, tm=1024, tn=512)
    ya, p_c, p_n, p_m = _mlstm(p_all, batch, seq, cw, cb, gate_bias, ang)
    yb, p_s = _gla(p_all, batch, seq, wg, bgate, bng)
    y_prompt = _out(ya, yb, p_all, xp, wpa, wpb, wo, lng, lnb, tm=256).reshape(batch, seq, D_MODEL)
    p_conv = p_all.reshape(batch, seq, N_PROJ)[:, seq - (CONV_W - 1):, :2 * A_WIDTH]

    xs = x_sample.reshape(nseq, D_MODEL)
    ps_all = _proj(xs, wp, tm=nseq, tn=512)
    conv_state = state_conv.reshape(nseq, (CONV_W - 1) * 2 * A_WIDTH)
    ya_s, yb_s, s_c, s_n, s_m, s_s = _decode(
        ps_all, conv_state, cw, cb, gate_bias, ang, wg, bgate, bng,
        layer(state_mlstm_C), state_mlstm_n.reshape(nseq, A_HEADS, 1, A_DK), layer(state_mlstm_m),
        layer(state_gla_S))
    y_sample = _out(ya_s, yb_s, ps_all, xs, wpa, wpb, wo, lng, lnb, tm=nseq).reshape(nseq, 1, D_MODEL)
    s_conv = jnp.concatenate([layer(state_conv)[:, 1:, :], ps_all[:, None, :2 * A_WIDTH]], axis=1)

    def stacked(a):
        return a.reshape((1,) + a.shape)

    return (y_prompt, y_sample,
            stacked(p_c), p_n.reshape(1, batch, A_HEADS, A_DK), stacked(p_m[:, :, 0, 0]),
            stacked(p_conv), stacked(p_s),
            stacked(s_c), s_n.reshape(1, nseq, A_HEADS, A_DK),
            stacked(s_m.reshape(nseq, A_HEADS, LANES)[:, :, 0]), stacked(s_conv), stacked(s_s))
```

```python
import functools

import jax
import jax.numpy as jnp
from jax import lax
from jax.experimental import pallas as pl
from jax.experimental.pallas import tpu as pltpu

F32 = jnp.float32
BF16 = jnp.bfloat16

D_MODEL = 2048
A_HEADS = 4
A_WIDTH = 1024
A_DK = 256
A_DV = 256
CONV_W = 4
B_HEADS = 4
B_WIDTH = 1024
B_KWIDTH = 512
B_DK = 128
B_DV = 256
GATE_RANK = 16
GATE_TAU = 16.0
ALPHA = 2.0 ** 0.25
LN_EPS = 1e-5

LANES = 128
SUBLANES = 8
VMEM_LIMIT = 48 * 1024 * 1024

COL_Q = 0
COL_K = A_WIDTH
COL_AV = 2 * A_WIDTH
COL_AO = 3 * A_WIDTH
COL_AZ = 4 * A_WIDTH
COL_BQ = 5 * A_WIDTH
COL_BK = COL_BQ + B_KWIDTH
COL_BV = COL_BK + B_KWIDTH
COL_BZ = COL_BV + B_WIDTH
COL_GA = COL_BZ + B_WIDTH
COL_GB = COL_GA + D_MODEL
COL_GATE = COL_GB + D_MODEL
N_PROJ = COL_GATE + A_HEADS * LANES

MLSTM_CHUNK = 256
MLSTM_HEADS_PER_STEP = 2
GLA_STEP = 256
GLA_BLOCK = 16
GLA_HEADS_PER_STEP = 2


def _dot(a, b):
    return jnp.dot(a, b, preferred_element_type=F32)


def _dot_nt(a, b):
    return lax.dot_general(a, b, (((1,), (1,)), ((), ())), preferred_element_type=F32)


def _dot_tn(a, b):
    return lax.dot_general(a, b, (((0,), (0,)), ((), ())), preferred_element_type=F32)


def _mask_dot(mask_bf16, x):
    hi = x.astype(BF16)
    r1 = x - hi.astype(F32)
    mid = r1.astype(BF16)
    lo = (r1 - mid.astype(F32)).astype(BF16)
    return _dot(mask_bf16, hi) + _dot(mask_bf16, mid) + _dot(mask_bf16, lo)


def _log_sigmoid(z):
    return jnp.minimum(z, 0.0) - jnp.log(1.0 + jnp.exp(-jnp.abs(z)))


def _sigmoid(z):
    return 1.0 / (1.0 + jnp.exp(-z))


def _silu(z):
    return z * _sigmoid(z)


def _proj_kernel(x_ref, w_ref, o_ref, xb_ref):
    @pl.when(pl.program_id(1) == 0)
    def _():
        xb_ref[...] = x_ref[...].astype(BF16)

    o_ref[...] = _dot_nt(xb_ref[...], w_ref[...])


def _proj(x, w_t, tm, tn):
    t, k = x.shape
    n = w_t.shape[0]
    return pl.pallas_call(
        _proj_kernel,
        grid=(t // tm, n // tn),
        in_specs=[pl.BlockSpec((tm, k), lambda i, j: (i, 0)),
                  pl.BlockSpec((tn, k), lambda i, j: (j, 0))],
        out_specs=pl.BlockSpec((tm, tn), lambda i, j: (i, j)),
        out_shape=jax.ShapeDtypeStruct((t, n), F32),
        scratch_shapes=[pltpu.VMEM((tm, k), BF16)],
        compiler_params=pltpu.CompilerParams(
            dimension_semantics=("parallel", "arbitrary"), vmem_limit_bytes=VMEM_LIMIT),
        name="proj",
    )(x, w_t)


def _mlstm_kernel(qp_ref, kp_ref, v_ref, ao_ref, az_ref, g_ref, cwq_ref, cwk_ref, cbq_ref, cbk_ref,
                  gb_ref, ng_ref, ya_ref, c_ref, n_ref, m_ref, tailq_ref, tailk_ref):
    L = MLSTM_CHUNK
    c = pl.program_id(2)

    @pl.when(c == 0)
    def _():
        c_ref[...] = jnp.zeros_like(c_ref)
        n_ref[...] = jnp.zeros_like(n_ref)
        m_ref[...] = jnp.zeros_like(m_ref)
        tailq_ref[...] = jnp.zeros_like(tailq_ref)
        tailk_ref[...] = jnp.zeros_like(tailk_ref)

    row8 = lax.broadcasted_iota(jnp.int32, (SUBLANES, 1), 0)

    def conv_silu(x_ref, tail_ref, w_ref, b_ref):
        x = x_ref[...]
        tail = tail_ref[...]
        acc = b_ref[...] + x * w_ref[CONV_W - 1:CONV_W, :]
        for j in range(1, CONV_W):
            xs = pltpu.roll(x, j, axis=0)
            head = jnp.where(row8 < j, pltpu.roll(tail, j, axis=0), xs[0:SUBLANES, :])
            xs = jnp.concatenate([head, xs[SUBLANES:, :]], axis=0)
            acc = acc + xs * w_ref[CONV_W - 1 - j:CONV_W - j, :]
        tail_ref[...] = x[L - SUBLANES:, :]
        return _silu(acc)

    q_all = conv_silu(qp_ref, tailq_ref, cwq_ref, cbq_ref)
    k_all = conv_silu(kp_ref, tailk_ref, cwk_ref, cbk_ref) * (A_DK ** -0.5)

    lane = lax.broadcasted_iota(jnp.int32, (L, LANES), 1)
    row = lax.broadcasted_iota(jnp.int32, (L, L), 0)
    col = lax.broadcasted_iota(jnp.int32, (L, L), 1)
    causal = row >= col
    causal_b = causal.astype(BF16)

    for hh in range(MLSTM_HEADS_PER_STEP):
        cols = slice(hh * A_DK, (hh + 1) * A_DK)
        q, k, v = q_all[:, cols], k_all[:, cols], v_ref[:, cols]

        g = g_ref[:, hh * LANES:(hh + 1) * LANES] + gb_ref[:, hh * LANES:(hh + 1) * LANES]
        g2 = jnp.where(lane == 1, _log_sigmoid(g), g)
        cum = _mask_dot(causal_b, g2)
        x2 = jnp.where(lane == 1, cum, g2)
        x2t = x2.T
        itil_col, b_col = x2[:, 0:1], x2[:, 1:2]
        itil_row, b_row = x2t[0:1, :], x2t[1:2, :]

        m_prev = m_ref[0, hh, 0:1, 0:1]
        dmat = jnp.where(causal, b_col - b_row + itil_row, -jnp.inf)
        inter = b_col + m_prev
        m_t = jnp.maximum(inter, jnp.max(dmat, axis=1, keepdims=True))
        w = jnp.exp(dmat - m_t)
        decay = jnp.exp(inter - m_t)

        qb, kb, vb = q.astype(BF16), k.astype(BF16), v.astype(BF16)
        c_old = c_ref[0, hh]
        n_old = n_ref[0, hh]
        s = _dot_nt(qb, kb) * w
        num = decay * _dot(qb, c_old.astype(BF16)) + _dot(s.astype(BF16), vb)
        den = decay * jnp.sum(q * n_old, axis=1, keepdims=True) + jnp.sum(s, axis=1, keepdims=True)
        h = num / jnp.maximum(jnp.abs(den), jnp.exp(-m_t))

        m_new = m_t[L - 1:L, :]
        b_last = b_col[L - 1:L, :]
        wk = jnp.exp(b_last - b_col + itil_col - m_new)
        dec = jnp.exp(b_last + m_prev - m_new)
        kw = k * wk
        c_ref[0, hh] = dec * c_old + _dot(kw.T.astype(BF16), vb)
        n_ref[0, hh] = dec * n_old + jnp.sum(kw, axis=0, keepdims=True)
        m_ref[0, hh] = jnp.broadcast_to(m_new, (1, LANES))

        mu = jnp.mean(h, axis=1, keepdims=True)
        hc = h - mu
        var = jnp.mean(hc * hc, axis=1, keepdims=True)
        hn = hc * lax.rsqrt(var + LN_EPS) * ng_ref[:, cols]
        ya_ref[:, cols] = (hn * _sigmoid(ao_ref[:, cols]) * _silu(az_ref[:, cols])).astype(ya_ref.dtype)


def _mlstm(p_all, batch, seq, conv_w, conv_b, gate_bias, a_norm_g):
    L = MLSTM_CHUNK
    hps = MLSTM_HEADS_PER_STEP
    nc = seq // L
    t = batch * seq
    wd = hps * A_DK
    k_off = A_WIDTH // wd

    def rows(b, h, c):
        return b * nc + c

    def pcol(base):
        return pl.BlockSpec((L, wd), lambda b, h, c: (rows(b, h, c), base // wd + h))

    in_specs = [
        pcol(COL_Q), pcol(COL_K), pcol(COL_AV), pcol(COL_AO), pcol(COL_AZ),
        pl.BlockSpec((L, hps * LANES), lambda b, h, c: (rows(b, h, c), COL_GATE // (hps * LANES) + h)),
        pl.BlockSpec((CONV_W, wd), lambda b, h, c: (0, h)),
        pl.BlockSpec((CONV_W, wd), lambda b, h, c: (0, k_off + h)),
        pl.BlockSpec((1, wd), lambda b, h, c: (0, h)),
        pl.BlockSpec((1, wd), lambda b, h, c: (0, k_off + h)),
        pl.BlockSpec((1, hps * LANES), lambda b, h, c: (0, h)),
        pl.BlockSpec((1, wd), lambda b, h, c: (0, h)),
    ]
    out_specs = [
        pl.BlockSpec((L, wd), lambda b, h, c: (rows(b, h, c), h)),
        pl.BlockSpec((1, hps, A_DK, A_DV), lambda b, h, c: (b, h, 0, 0)),
        pl.BlockSpec((1, hps, 1, A_DK), lambda b, h, c: (b, h, 0, 0)),
        pl.BlockSpec((1, hps, 1, LANES), lambda b, h, c: (b, h, 0, 0)),
    ]
    out_shape = [
        jax.ShapeDtypeStruct((t, A_WIDTH), BF16),
        jax.ShapeDtypeStruct((batch, A_HEADS, A_DK, A_DV), F32),
        jax.ShapeDtypeStruct((batch, A_HEADS, 1, A_DK), F32),
        jax.ShapeDtypeStruct((batch, A_HEADS, 1, LANES), F32),
    ]
    return pl.pallas_call(
        _mlstm_kernel,
        grid=(batch, A_HEADS // hps, nc),
        in_specs=in_specs,
        out_specs=out_specs,
        out_shape=out_shape,
        scratch_shapes=[pltpu.VMEM((SUBLANES, wd), F32),
                        pltpu.VMEM((SUBLANES, wd), F32)],
        compiler_params=pltpu.CompilerParams(
            dimension_semantics=("parallel", "parallel", "arbitrary"), vmem_limit_bytes=VMEM_LIMIT),
        name="mlstm",
    )(p_all, p_all, p_all, p_all, p_all, p_all, conv_w, conv_w, conv_b, conv_b, gate_bias, a_norm_g)


def _gla_gate_log(bg_tile, wg, bgate):
    lr = bg_tile[:, 2:2 + GATE_RANK]
    z = _dot(lr.astype(BF16), wg.astype(BF16)) + bgate
    return _log_sigmoid(z) / GATE_TAU


def _gla_kernel(q_ref, k_ref, v_ref, bz_ref, g_ref, wg_ref, bgate_ref, ng_ref,
                yb_ref, s_out_ref, st_ref):
    L = GLA_STEP
    B = GLA_BLOCK
    nb = L // B
    c = pl.program_id(2)

    @pl.when(c == 0)
    def _():
        st_ref[...] = jnp.zeros_like(st_ref)

    row = lax.broadcasted_iota(jnp.int32, (L, L), 0)
    col = lax.broadcasted_iota(jnp.int32, (L, L), 1)
    same = (row // B) == (col // B)
    tri_b = (same & (row >= col)).astype(BF16)
    same_b = same.astype(BF16)
    t_in = lax.broadcasted_iota(jnp.int32, (L, 1), 0) % B
    s_in = col % B

    for hh in range(GLA_HEADS_PER_STEP):
        kcols = slice(hh * B_DK, (hh + 1) * B_DK)
        vcols = slice(hh * B_DV, (hh + 1) * B_DV)
        loga = _gla_gate_log(g_ref[:, hh * LANES:(hh + 1) * LANES], wg_ref[:, kcols], bgate_ref[:, kcols])
        bc = _mask_dot(tri_b, loga)
        bl = _mask_dot(same_b, loga)
        k = k_ref[:, kcols]
        qs = q_ref[:, kcols] * (B_DK ** -0.5)
        qt = (qs * jnp.exp(bc)).astype(BF16)
        kt = (k * jnp.exp(bl - bc)).astype(BF16)
        eb = jnp.exp(bl)
        vb = v_ref[:, vcols].astype(BF16)

        bc3, q3, k3 = (x.reshape(nb, B, B_DK) for x in (bc, qs, k))
        a = jnp.zeros((L, L), F32)
        for s in range(B):
            e = jnp.exp(jnp.minimum(bc3 - bc3[:, s:s + 1, :], 0.0))
            a_col = jnp.sum(q3 * k3[:, s:s + 1, :] * e, axis=2, keepdims=True).reshape(L, 1)
            a = jnp.where(s_in == s, jnp.where(t_in >= s, a_col, 0.0), a)
        a = jnp.where(same, a, 0.0)
        o_diag = _dot(a.astype(BF16), vb)

        st = st_ref[hh]
        o_parts = []
        for j in range(nb):
            lo, hi = j * B, (j + 1) * B
            o_parts.append(o_diag[lo:hi, :] + _dot_nt(qt[lo:hi, :], st.astype(BF16)))
            st = st * eb[lo:lo + 1, :] + _dot_tn(vb[lo:hi, :], kt[lo:hi, :])
        st_ref[hh] = st
        o = jnp.concatenate(o_parts, axis=0)

        on = o * lax.rsqrt(jnp.mean(o * o, axis=1, keepdims=True) + LN_EPS) * ng_ref[:, vcols]
        yb_ref[:, vcols] = (on * _silu(bz_ref[:, vcols])).astype(yb_ref.dtype)

    @pl.when(c == pl.num_programs(2) - 1)
    def _():
        for hh in range(GLA_HEADS_PER_STEP):
            s_out_ref[0, hh] = st_ref[hh].T


def _gla(p_all, batch, seq, w_gate_up, b_gate, b_norm_g):
    L = GLA_STEP
    hps = GLA_HEADS_PER_STEP
    nc = seq // L
    t = batch * seq
    kw, vw = hps * B_DK, hps * B_DV

    def rows(b, h, c):
        return b * nc + c

    in_specs = [
        pl.BlockSpec((L, kw), lambda b, h, c: (rows(b, h, c), COL_BQ // kw + h)),
        pl.BlockSpec((L, kw), lambda b, h, c: (rows(b, h, c), COL_BK // kw + h)),
        pl.BlockSpec((L, vw), lambda b, h, c: (rows(b, h, c), COL_BV // vw + h)),
        pl.BlockSpec((L, vw), lambda b, h, c: (rows(b, h, c), COL_BZ // vw + h)),
        pl.BlockSpec((L, hps * LANES), lambda b, h, c: (rows(b, h, c), COL_GATE // (hps * LANES) + h)),
        pl.BlockSpec((GATE_RANK, kw), lambda b, h, c: (0, h)),
        pl.BlockSpec((1, kw), lambda b, h, c: (0, h)),
        pl.BlockSpec((1, vw), lambda b, h, c: (0, h)),
    ]
    out_specs = [
        pl.BlockSpec((L, vw), lambda b, h, c: (rows(b, h, c), h)),
        pl.BlockSpec((1, hps, B_DK, B_DV), lambda b, h, c: (b, h, 0, 0)),
    ]
    out_shape = [
        jax.ShapeDtypeStruct((t, B_WIDTH), BF16),
        jax.ShapeDtypeStruct((batch, B_HEADS, B_DK, B_DV), F32),
    ]
    return pl.pallas_call(
        _gla_kernel,
        grid=(batch, B_HEADS // hps, nc),
        in_specs=in_specs,
        out_specs=out_specs,
        out_shape=out_shape,
        scratch_shapes=[pltpu.VMEM((hps, B_DV, B_DK), F32)],
        compiler_params=pltpu.CompilerParams(
            dimension_semantics=("parallel", "parallel", "arbitrary"), vmem_limit_bytes=VMEM_LIMIT),
        name="gla",
    )(p_all, p_all, p_all, p_all, p_all, w_gate_up, b_gate, b_norm_g)


DEC_TOKENS = 16


def _decode_kernel(qp_ref, kp_ref, av_ref, ao_ref, az_ref, bq_ref, bk_ref, bv_ref, bz_ref, g_ref,
                   sq0_ref, sq1_ref, sq2_ref, sk0_ref, sk1_ref, sk2_ref,
                   cwq_ref, cwk_ref, cbq_ref, cbk_ref, gb_ref, ang_ref, wg_ref, bgate_ref, bng_ref,
                   c_ref, n_ref, m_ref, s_ref,
                   ya_ref, yb_ref, c_out_ref, n_out_ref, m_out_ref, s_out_ref):
    TB = DEC_TOKENS
    h_idx = pl.program_id(1)

    def conv_silu(s0, s1, s2, x, w_ref, b_ref):
        acc = b_ref[...] + s0[...] * w_ref[0:1, :]
        acc = acc + s1[...] * w_ref[1:2, :]
        acc = acc + s2[...] * w_ref[2:3, :]
        acc = acc + x[...] * w_ref[3:4, :]
        return _silu(acc)

    q = conv_silu(sq0_ref, sq1_ref, sq2_ref, qp_ref, cwq_ref, cbq_ref)
    k = conv_silu(sk0_ref, sk1_ref, sk2_ref, kp_ref, cwk_ref, cbk_ref) * (A_DK ** -0.5)
    v = av_ref[...]
    g = g_ref[...]
    gbias = gb_ref[...]
    itil = g[:, 0:1] + gbias[:, 0:1]
    logf = _log_sigmoid(g[:, 1:2] + gbias[:, 1:2])
    lane_h = lax.broadcasted_iota(jnp.int32, (TB, A_HEADS), 1)
    m_prev = jnp.sum(jnp.where(lane_h == h_idx, m_ref[...], 0.0), axis=1, keepdims=True)
    inter = logf + m_prev
    m_t = jnp.maximum(inter, itil)
    w = jnp.exp(itil - m_t)
    decay = jnp.exp(inter - m_t)
    n_old = n_ref[:, 0, 0, :]
    s = jnp.sum(q * k, axis=1, keepdims=True) * w
    den = decay * jnp.sum(q * n_old, axis=1, keepdims=True) + s
    scale = 1.0 / jnp.maximum(jnp.abs(den), jnp.exp(-m_t))
    kw = k * w
    n_out_ref[:, 0, 0, :] = decay * n_old + kw
    m_out_ref[...] = jnp.broadcast_to(m_t, (TB, LANES))

    rows8 = lax.broadcasted_iota(jnp.int32, (TB, 1), 0)
    qb, kwb, vb = q.astype(BF16), kw.astype(BF16), v.astype(BF16)
    h_rows = []
    for t in range(TB):
        c_old = c_ref[t, 0]
        sel = rows8 == t
        qc = _dot(qb, c_old.astype(BF16))[t:t + 1, :]
        h_rows.append((decay[t:t + 1, :] * qc + s[t:t + 1, :] * v[t:t + 1, :]) * scale[t:t + 1, :])
        outer = _dot_tn(jnp.where(sel, kwb, jnp.zeros_like(kwb)), vb)
        c_out_ref[t, 0] = decay[t:t + 1, :] * c_old + outer
    h = jnp.concatenate(h_rows, axis=0)
    mu = jnp.mean(h, axis=1, keepdims=True)
    hc = h - mu
    var = jnp.mean(hc * hc, axis=1, keepdims=True)
    hn = hc * lax.rsqrt(var + LN_EPS) * ang_ref[...]
    ya_ref[...] = (hn * _sigmoid(ao_ref[...]) * _silu(az_ref[...])).astype(ya_ref.dtype)

    gq = bq_ref[...] * (B_DK ** -0.5)
    gk = bk_ref[...]
    gv = bv_ref[...]
    loga = _gla_gate_log(g, wg_ref[...], bgate_ref[...])
    eb = jnp.exp(loga)
    a = jnp.sum(gq * gk, axis=1, keepdims=True)
    ebt = jnp.concatenate([eb, jnp.zeros((LANES - TB, B_DK), F32)], axis=0).T
    qeb, gkb, gvb = (gq * eb).astype(BF16), gk.astype(BF16), gv.astype(BF16)
    o_rows = []
    for t in range(TB):
        s_old = s_ref[t, 0]
        sel = rows8 == t
        o_rows.append(_dot(qeb, s_old.astype(BF16))[t:t + 1, :] + a[t:t + 1, :] * gv[t:t + 1, :])
        outer = _dot_tn(jnp.where(sel, gkb, jnp.zeros_like(gkb)), gvb)
        s_out_ref[t, 0] = ebt[:, t:t + 1] * s_old + outer
    o = jnp.concatenate(o_rows, axis=0)
    on = o * lax.rsqrt(jnp.mean(o * o, axis=1, keepdims=True) + LN_EPS) * bng_ref[...]
    yb_ref[...] = (on * _silu(bz_ref[...])).astype(yb_ref.dtype)


def _decode(p_s, conv_state, conv_w, conv_b, gate_bias, a_norm_g, w_gate_up, b_gate, b_norm_g,
            c_state, n_state, m_state, s_state):
    nseq = p_s.shape[0]
    TB = DEC_TOKENS

    def pcol(base, width):
        return pl.BlockSpec((TB, width), lambda i, h: (i, base // width + h))

    def cstate(j, base):
        return pl.BlockSpec((TB, A_DK), lambda i, h: (i, (j * 2 * A_WIDTH + base) // A_DK + h))

    in_specs = [
        pcol(COL_Q, A_DK), pcol(COL_K, A_DK), pcol(COL_AV, A_DV), pcol(COL_AO, A_DV), pcol(COL_AZ, A_DV),
        pcol(COL_BQ, B_DK), pcol(COL_BK, B_DK), pcol(COL_BV, B_DV), pcol(COL_BZ, B_DV),
        pcol(COL_GATE, LANES),
        cstate(0, COL_Q), cstate(1, COL_Q), cstate(2, COL_Q),
        cstate(0, COL_K), cstate(1, COL_K), cstate(2, COL_K),
        pl.BlockSpec((CONV_W, A_DK), lambda i, h: (0, h)),
        pl.BlockSpec((CONV_W, A_DK), lambda i, h: (0, A_HEADS + h)),
        pl.BlockSpec((1, A_DK), lambda i, h: (0, h)),
        pl.BlockSpec((1, A_DK), lambda i, h: (0, A_HEADS + h)),
        pl.BlockSpec((1, LANES), lambda i, h: (0, h)),
        pl.BlockSpec((1, A_DV), lambda i, h: (0, h)),
        pl.BlockSpec((GATE_RANK, B_DK), lambda i, h: (0, h)),
        pl.BlockSpec((1, B_DK), lambda i, h: (0, h)),
        pl.BlockSpec((1, B_DV), lambda i, h: (0, h)),
        pl.BlockSpec((TB, 1, A_DK, A_DV), lambda i, h: (i, h, 0, 0)),
        pl.BlockSpec((TB, 1, 1, A_DK), lambda i, h: (i, h, 0, 0)),
        pl.BlockSpec((TB, A_HEADS), lambda i, h: (i, 0)),
        pl.BlockSpec((TB, 1, B_DK, B_DV), lambda i, h: (i, h, 0, 0)),
    ]
    out_specs = [
        pl.BlockSpec((TB, A_DV), lambda i, h: (i, h)),
        pl.BlockSpec((TB, B_DV), lambda i, h: (i, h)),
        pl.BlockSpec((TB, 1, A_DK, A_DV), lambda i, h: (i, h, 0, 0)),
        pl.BlockSpec((TB, 1, 1, A_DK), lambda i, h: (i, h, 0, 0)),
        pl.BlockSpec((TB, LANES), lambda i, h: (i, h)),
        pl.BlockSpec((TB, 1, B_DK, B_DV), lambda i, h: (i, h, 0, 0)),
    ]
    out_shape = [
        jax.ShapeDtypeStruct((nseq, A_WIDTH), BF16),
        jax.ShapeDtypeStruct((nseq, B_WIDTH), BF16),
        jax.ShapeDtypeStruct((nseq, A_HEADS, A_DK, A_DV), F32),
        jax.ShapeDtypeStruct((nseq, A_HEADS, 1, A_DK), F32),
        jax.ShapeDtypeStruct((nseq, A_HEADS * LANES), F32),
        jax.ShapeDtypeStruct((nseq, B_HEADS, B_DK, B_DV), F32),
    ]
    return pl.pallas_call(
        _decode_kernel,
        grid=(nseq // TB, A_HEADS),
        in_specs=in_specs,
        out_specs=out_specs,
        out_shape=out_shape,
        compiler_params=pltpu.CompilerParams(
            dimension_semantics=("parallel", "parallel"), vmem_limit_bytes=VMEM_LIMIT),
        name="decode",
    )(p_s, p_s, p_s, p_s, p_s, p_s, p_s, p_s, p_s, p_s,
      conv_state, conv_state, conv_state, conv_state, conv_state, conv_state,
      conv_w, conv_w, conv_b, conv_b, gate_bias, a_norm_g, w_gate_up, b_gate, b_norm_g,
      c_state, n_state, m_state, s_state)


def _out_kernel(ya_ref, yb_ref, ga_ref, gb_ref, x_ref, wpa_ref, wpb_ref, wo_ref, lng_ref, lnb_ref, o_ref):
    pa = _dot(ya_ref[...], wpa_ref[...])
    pb = _dot(yb_ref[...], wpb_ref[...])
    merged = _sigmoid(ga_ref[...]) * pa + _sigmoid(gb_ref[...]) * pb
    y = _dot(merged.astype(BF16), wo_ref[...])
    r = ALPHA * x_ref[...] + y
    mu = jnp.mean(r, axis=1, keepdims=True)
    rc = r - mu
    var = jnp.mean(rc * rc, axis=1, keepdims=True)
    o_ref[...] = rc * lax.rsqrt(var + LN_EPS) * lng_ref[...] + lnb_ref[...]


def _out(ya, yb, p_all, x, w_pa, w_pb, w_out, ln_g, ln_b, tm):
    t = x.shape[0]
    const = lambda i: (0, 0)
    single = pl.Buffered(1)
    in_specs = [
        pl.BlockSpec((tm, A_WIDTH), lambda i: (i, 0)),
        pl.BlockSpec((tm, B_WIDTH), lambda i: (i, 0)),
        pl.BlockSpec((tm, D_MODEL), lambda i: (i, COL_GA // D_MODEL)),
        pl.BlockSpec((tm, D_MODEL), lambda i: (i, COL_GB // D_MODEL)),
        pl.BlockSpec((tm, D_MODEL), lambda i: (i, 0)),
        pl.BlockSpec((A_WIDTH, D_MODEL), const, pipeline_mode=single),
        pl.BlockSpec((B_WIDTH, D_MODEL), const, pipeline_mode=single),
        pl.BlockSpec((D_MODEL, D_MODEL), const, pipeline_mode=single),
        pl.BlockSpec((1, D_MODEL), const),
        pl.BlockSpec((1, D_MODEL), const),
    ]
    return pl.pallas_call(
        _out_kernel,
        grid=(t // tm,),
        in_specs=in_specs,
        out_specs=pl.BlockSpec((tm, D_MODEL), lambda i: (i, 0)),
        out_shape=jax.ShapeDtypeStruct((t, D_MODEL), F32),
        compiler_params=pltpu.CompilerParams(
            dimension_semantics=("parallel",), vmem_limit_bytes=VMEM_LIMIT),
        name="outproj",
    )(ya, yb, p_all, p_all, x, w_pa, w_pb, w_out, ln_g, ln_b)


RELAYOUT_TN = 512
SRC_I = 3 * A_WIDTH
SRC_SHIFT_A = 2 * A_HEADS
SRC_BG = COL_BZ + SRC_SHIFT_A
SRC_SHIFT_B = SRC_SHIFT_A + GATE_RANK
GATE_SRC_ROWS = 4 * SUBLANES


def _relayout_kernel(src_ref, gif_ref, gb0_ref, gb1_ref, o_ref):
    j = pl.program_id(0)
    tn = RELAYOUT_TN
    n_main = COL_GATE // tn

    @pl.when(j < n_main)
    def _():
        o_ref[...] = src_ref[...].astype(BF16)

    @pl.when(j == n_main)
    def _():
        g = jnp.concatenate([gif_ref[...], gb0_ref[...], gb1_ref[...],
                             jnp.zeros((SUBLANES, src_ref.shape[1]), F32)], axis=0).astype(BF16)
        r = lax.broadcasted_iota(jnp.int32, (tn, GATE_SRC_ROWS), 0)
        c = lax.broadcasted_iota(jnp.int32, (tn, GATE_SRC_ROWS), 1)
        h, l = r // LANES, r % LANES
        sel = (((l == 0) & (c == h)) | ((l == 1) & (c == A_HEADS + h))
               | ((l >= 2) & (l < 2 + GATE_RANK) & (c == l + (2 * A_HEADS - 2))))
        o_ref[...] = _dot(sel.astype(BF16), g).astype(BF16)


def _relayout_w_in(w_in_t):
    n_src, k = w_in_t.shape
    tn = RELAYOUT_TN

    def src_row(j):
        shift = jnp.where(j < COL_AO // tn, 0, jnp.where(j < COL_BZ // tn, SRC_SHIFT_A, SRC_SHIFT_B))
        group = jnp.minimum(j * (tn // SUBLANES) + shift // SUBLANES, (n_src - tn) // SUBLANES)
        return group * SUBLANES

    return pl.pallas_call(
        _relayout_kernel,
        grid=(N_PROJ // tn,),
        in_specs=[pl.BlockSpec((pl.Element(tn), pl.Element(k)), lambda j: (src_row(j), 0)),
                  pl.BlockSpec((SUBLANES, k), lambda j: (SRC_I // SUBLANES, 0)),
                  pl.BlockSpec((SUBLANES, k), lambda j: (SRC_BG // SUBLANES, 0)),
                  pl.BlockSpec((SUBLANES, k), lambda j: (SRC_BG // SUBLANES + 1, 0))],
        out_specs=pl.BlockSpec((tn, k), lambda j: (j, 0)),
        out_shape=jax.ShapeDtypeStruct((N_PROJ, k), BF16),
        compiler_params=pltpu.CompilerParams(
            dimension_semantics=("parallel",), vmem_limit_bytes=VMEM_LIMIT),
        name="relayout",
    )(w_in_t, w_in_t, w_in_t, w_in_t)


def kernel(x_prompt, x_sample, state_mlstm_C, state_mlstm_n, state_mlstm_m, state_conv, state_gla_S,
           w_in, conv_w, conv_b, b_i, b_f, a_norm_g, w_gate_up, b_gate, b_norm_g, w_pa, w_pb, w_out,
           ln_g, ln_b):
    batch, seq, _ = x_prompt.shape
    nseq = x_sample.shape[0]
    assert w_in.shape[0] == 1, "single-layer step"
    d = 0

    def layer(a):
        return a.reshape(a.shape[1:])

    wp = _relayout_w_in(layer(w_in).T)
    wpa, wpb, wo = w_pa[d].astype(BF16), w_pb[d].astype(BF16), w_out[d].astype(BF16)
    cw = conv_w[d]
    cb = conv_b[d][None, :]
    gate_bias = jnp.zeros((A_HEADS, LANES), F32).at[:, 0].set(b_i[d]).at[:, 1].set(b_f[d]).reshape(1, -1)
    ang = a_norm_g[d][None, :]
    bng = b_norm_g[d][None, :]
    wg = w_gate_up[d]
    bgate = b_gate[d][None, :]
    lng, lnb = ln_g[d][None, :], ln_b[d][None, :]

    xp = x_prompt.reshape(batch * seq, D_MODEL)
    p_all = _proj(xp, wp, tm=1024, tn=1280)
    ya, p_c, p_n, p_m = _mlstm(p_all, batch, seq, cw, cb, gate_bias, ang)
    yb, p_s = _gla(p_all, batch, seq, wg, bgate, bng)
    y_prompt = _out(ya, yb, p_all, xp, wpa, wpb, wo, lng, lnb, tm=256).reshape(batch, seq, D_MODEL)
    p_conv = p_all.reshape(batch, seq, N_PROJ)[:, seq - (CONV_W - 1):, :2 * A_WIDTH]

    xs = x_sample.reshape(nseq, D_MODEL)
    ps_all = _proj(xs, wp, tm=nseq, tn=512)
    conv_state = state_conv.reshape(nseq, (CONV_W - 1) * 2 * A_WIDTH)
    ya_s, yb_s, s_c, s_n, s_m, s_s = _decode(
        ps_all, conv_state, cw, cb, gate_bias, ang, wg, bgate, bng,
        layer(state_mlstm_C), state_mlstm_n.reshape(nseq, A_HEADS, 1, A_DK), layer(state_mlstm_m),
        layer(state_gla_S))
    y_sample = _out(ya_s, yb_s, ps_all, xs, wpa, wpb, wo, lng, lnb, tm=nseq).reshape(nseq, 1, D_MODEL)
    s_conv = jnp.concatenate([layer(state_conv)[:, 1:, :], ps_all[:, None, :2 * A_WIDTH]], axis=1)

    def stacked(a):
        return a.reshape((1,) + a.shape)

    return (y_prompt, y_sample,
            stacked(p_c), p_n.reshape(1, batch, A_HEADS, A_DK), stacked(p_m[:, :, 0, 0]),
            stacked(p_conv), stacked(p_s),
            stacked(s_c), s_n.reshape(1, nseq, A_HEADS, A_DK),
            stacked(s_m.reshape(nseq, A_HEADS, LANES)[:, :, 0]), stacked(s_conv), stacked(s_s))
```

```python
import functools

import jax
import jax.numpy as jnp
from jax import lax
from jax.experimental import pallas as pl
from jax.experimental.pallas import tpu as pltpu

F32 = jnp.float32
BF16 = jnp.bfloat16

D_MODEL = 2048
A_HEADS = 4
A_WIDTH = 1024
A_DK = 256
A_DV = 256
CONV_W = 4
B_HEADS = 4
B_WIDTH = 1024
B_KWIDTH = 512
B_DK = 128
B_DV = 256
GATE_RANK = 16
GATE_TAU = 16.0
ALPHA = 2.0 ** 0.25
LN_EPS = 1e-5
LOG2_E = 1.4426950408889634

LANES = 128
SUBLANES = 8
VMEM_LIMIT = 48 * 1024 * 1024

COL_Q = 0
COL_K = A_WIDTH
COL_AV = 2 * A_WIDTH
COL_AO = 3 * A_WIDTH
COL_AZ = 4 * A_WIDTH
COL_BQ = 5 * A_WIDTH
COL_BK = COL_BQ + B_KWIDTH
COL_BV = COL_BK + B_KWIDTH
COL_BZ = COL_BV + B_WIDTH
COL_GA = COL_BZ + B_WIDTH
COL_GB = COL_GA + D_MODEL
COL_GATE = COL_GB + D_MODEL
N_PROJ = COL_GATE + A_HEADS * LANES

MLSTM_CHUNK = 256
MLSTM_HEADS_PER_STEP = 2
GLA_STEP = 256
GLA_BLOCK = 16
GLA_HEADS_PER_STEP = 2


def _dot(a, b):
    return jnp.dot(a, b, preferred_element_type=F32)


def _dot_nt(a, b):
    return lax.dot_general(a, b, (((1,), (1,)), ((), ())), preferred_element_type=F32)


def _dot_tn(a, b):
    return lax.dot_general(a, b, (((0,), (0,)), ((), ())), preferred_element_type=F32)


def _mask_dot(mask_bf16, x):
    hi = x.astype(BF16)
    r1 = x - hi.astype(F32)
    mid = r1.astype(BF16)
    lo = (r1 - mid.astype(F32)).astype(BF16)
    return _dot(mask_bf16, hi) + _dot(mask_bf16, mid) + _dot(mask_bf16, lo)


def _log_sigmoid(z):
    return jnp.minimum(z, 0.0) - jnp.log(1.0 + jnp.exp(-jnp.abs(z)))


def _sigmoid(z):
    return 1.0 / (1.0 + jnp.exp(-z))


def _silu(z):
    return z * _sigmoid(z)


def _proj_kernel(x_ref, w_ref, o_ref, xb_ref):
    @pl.when(pl.program_id(1) == 0)
    def _():
        xb_ref[...] = x_ref[...].astype(BF16)

    o_ref[...] = _dot_nt(xb_ref[...], w_ref[...])


def _proj(x, w_t, tm, tn):
    t, k = x.shape
    n = w_t.shape[0]
    return pl.pallas_call(
        _proj_kernel,
        grid=(t // tm, n // tn),
        in_specs=[pl.BlockSpec((tm, k), lambda i, j: (i, 0)),
                  pl.BlockSpec((tn, k), lambda i, j: (j, 0))],
        out_specs=pl.BlockSpec((tm, tn), lambda i, j: (i, j)),
        out_shape=jax.ShapeDtypeStruct((t, n), F32),
        scratch_shapes=[pltpu.VMEM((tm, k), BF16)],
        compiler_params=pltpu.CompilerParams(
            dimension_semantics=("parallel", "arbitrary"), vmem_limit_bytes=VMEM_LIMIT),
        name="proj",
    )(x, w_t)


def _mlstm_kernel(qp_ref, kp_ref, v_ref, ao_ref, az_ref, g_ref, cwq_ref, cwk_ref, cbq_ref, cbk_ref,
                  gb_ref, ng_ref, ya_ref, c_ref, n_ref, m_ref, tailq_ref, tailk_ref):
    L = MLSTM_CHUNK
    c = pl.program_id(2)

    @pl.when(c == 0)
    def _():
        c_ref[...] = jnp.zeros_like(c_ref)
        n_ref[...] = jnp.zeros_like(n_ref)
        m_ref[...] = jnp.zeros_like(m_ref)
        tailq_ref[...] = jnp.zeros_like(tailq_ref)
        tailk_ref[...] = jnp.zeros_like(tailk_ref)

    row8 = lax.broadcasted_iota(jnp.int32, (SUBLANES, 1), 0)

    def conv_silu(x_ref, tail_ref, w_ref, b_ref):
        x = x_ref[...]
        tail = tail_ref[...]
        acc = b_ref[...] + x * w_ref[CONV_W - 1:CONV_W, :]
        for j in range(1, CONV_W):
            xs = pltpu.roll(x, j, axis=0)
            head = jnp.where(row8 < j, pltpu.roll(tail, j, axis=0), xs[0:SUBLANES, :])
            xs = jnp.concatenate([head, xs[SUBLANES:, :]], axis=0)
            acc = acc + xs * w_ref[CONV_W - 1 - j:CONV_W - j, :]
        tail_ref[...] = x[L - SUBLANES:, :]
        return _silu(acc)

    q_all = conv_silu(qp_ref, tailq_ref, cwq_ref, cbq_ref)
    k_all = conv_silu(kp_ref, tailk_ref, cwk_ref, cbk_ref) * (A_DK ** -0.5)

    lane = lax.broadcasted_iota(jnp.int32, (L, LANES), 1)
    row = lax.broadcasted_iota(jnp.int32, (L, L), 0)
    col = lax.broadcasted_iota(jnp.int32, (L, L), 1)
    causal = row >= col
    causal_b = causal.astype(BF16)

    for hh in range(MLSTM_HEADS_PER_STEP):
        cols = slice(hh * A_DK, (hh + 1) * A_DK)
        q, k, v = q_all[:, cols], k_all[:, cols], v_ref[:, cols]

        g = g_ref[:, hh * LANES:(hh + 1) * LANES] + gb_ref[:, hh * LANES:(hh + 1) * LANES]
        g2 = jnp.where(lane == 1, _log_sigmoid(g), g)
        cum = _mask_dot(causal_b, g2)
        x2 = jnp.where(lane == 1, cum, g2)
        x2t = x2.T
        itil_col, b_col = x2[:, 0:1], x2[:, 1:2]
        itil_row, b_row = x2t[0:1, :], x2t[1:2, :]

        m_prev = m_ref[0, hh, 0:1, 0:1]
        dmat = jnp.where(causal, b_col - b_row + itil_row, -jnp.inf)
        inter = b_col + m_prev
        m_t = jnp.maximum(inter, jnp.max(dmat, axis=1, keepdims=True))
        w = jnp.exp(dmat - m_t)
        decay = jnp.exp(inter - m_t)

        qb, kb, vb = q.astype(BF16), k.astype(BF16), v.astype(BF16)
        c_old = c_ref[0, hh]
        n_old = n_ref[0, hh]
        s = _dot_nt(qb, kb) * w
        num = decay * _dot(qb, c_old.astype(BF16)) + _dot(s.astype(BF16), vb)
        den = decay * jnp.sum(q * n_old, axis=1, keepdims=True) + jnp.sum(s, axis=1, keepdims=True)
        h = num / jnp.maximum(jnp.abs(den), jnp.exp(-m_t))

        m_new = m_t[L - 1:L, :]
        b_last = b_col[L - 1:L, :]
        wk = jnp.exp(b_last - b_col + itil_col - m_new)
        dec = jnp.exp(b_last + m_prev - m_new)
        kw = k * wk
        c_ref[0, hh] = dec * c_old + _dot(kw.T.astype(BF16), vb)
        n_ref[0, hh] = dec * n_old + jnp.sum(kw, axis=0, keepdims=True)
        m_ref[0, hh] = jnp.broadcast_to(m_new, (1, LANES))

        mu = jnp.mean(h, axis=1, keepdims=True)
        hc = h - mu
        var = jnp.mean(hc * hc, axis=1, keepdims=True)
        hn = hc * lax.rsqrt(var + LN_EPS) * ng_ref[:, cols]
        ya_ref[:, cols] = (hn * _sigmoid(ao_ref[:, cols]) * _silu(az_ref[:, cols])).astype(ya_ref.dtype)


def _mlstm(p_all, batch, seq, conv_w, conv_b, gate_bias, a_norm_g):
    L = MLSTM_CHUNK
    hps = MLSTM_HEADS_PER_STEP
    nc = seq // L
    t = batch * seq
    wd = hps * A_DK
    k_off = A_WIDTH // wd

    def rows(b, h, c):
        return b * nc + c

    def pcol(base):
        return pl.BlockSpec((L, wd), lambda b, h, c: (rows(b, h, c), base // wd + h))

    in_specs = [
        pcol(COL_Q), pcol(COL_K), pcol(COL_AV), pcol(COL_AO), pcol(COL_AZ),
        pl.BlockSpec((L, hps * LANES), lambda b, h, c: (rows(b, h, c), COL_GATE // (hps * LANES) + h)),
        pl.BlockSpec((CONV_W, wd), lambda b, h, c: (0, h)),
        pl.BlockSpec((CONV_W, wd), lambda b, h, c: (0, k_off + h)),
        pl.BlockSpec((1, wd), lambda b, h, c: (0, h)),
        pl.BlockSpec((1, wd), lambda b, h, c: (0, k_off + h)),
        pl.BlockSpec((1, hps * LANES), lambda b, h, c: (0, h)),
        pl.BlockSpec((1, wd), lambda b, h, c: (0, h)),
    ]
    out_specs = [
        pl.BlockSpec((L, wd), lambda b, h, c: (rows(b, h, c), h)),
        pl.BlockSpec((1, hps, A_DK, A_DV), lambda b, h, c: (b, h, 0, 0)),
        pl.BlockSpec((1, hps, 1, A_DK), lambda b, h, c: (b, h, 0, 0)),
        pl.BlockSpec((1, hps, 1, LANES), lambda b, h, c: (b, h, 0, 0)),
    ]
    out_shape = [
        jax.ShapeDtypeStruct((t, A_WIDTH), BF16),
        jax.ShapeDtypeStruct((batch, A_HEADS, A_DK, A_DV), F32),
        jax.ShapeDtypeStruct((batch, A_HEADS, 1, A_DK), F32),
        jax.ShapeDtypeStruct((batch, A_HEADS, 1, LANES), F32),
    ]
    return pl.pallas_call(
        _mlstm_kernel,
        grid=(batch, A_HEADS // hps, nc),
        in_specs=in_specs,
        out_specs=out_specs,
        out_shape=out_shape,
        scratch_shapes=[pltpu.VMEM((SUBLANES, wd), F32),
                        pltpu.VMEM((SUBLANES, wd), F32)],
        compiler_params=pltpu.CompilerParams(
            dimension_semantics=("parallel", "parallel", "arbitrary"), vmem_limit_bytes=VMEM_LIMIT),
        name="mlstm",
    )(p_all, p_all, p_all, p_all, p_all, p_all, conv_w, conv_w, conv_b, conv_b, gate_bias, a_norm_g)


def _gla_gate_log(bg_tile, wg, bgate):
    lr = bg_tile[:, 2:2 + GATE_RANK]
    z = _dot(lr.astype(BF16), wg.astype(BF16)) + bgate
    return _log_sigmoid(z) / GATE_TAU


def _gla_kernel(q_ref, k_ref, v_ref, bz_ref, g_ref, wg_ref, bgate_ref, ng_ref,
                yb_ref, s_out_ref, st_ref, ds_ref, sb_ref):
    L = GLA_STEP
    B = GLA_BLOCK
    nb = L // B
    c = pl.program_id(2)

    @pl.when(c == 0)
    def _():
        st_ref[...] = jnp.zeros_like(st_ref)

    row = lax.broadcasted_iota(jnp.int32, (L, L), 0)
    col = lax.broadcasted_iota(jnp.int32, (L, L), 1)
    same = (row // B) == (col // B)
    tri_b = (same & (row >= col)).astype(BF16)
    same_b = same.astype(BF16)
    t_in = lax.broadcasted_iota(jnp.int32, (L, 1), 0) % B
    s_lane = lax.broadcasted_iota(jnp.int32, (1, LANES), 1)

    for hh in range(GLA_HEADS_PER_STEP):
        kcols = slice(hh * B_DK, (hh + 1) * B_DK)
        vcols = slice(hh * B_DV, (hh + 1) * B_DV)
        loga = _gla_gate_log(g_ref[:, hh * LANES:(hh + 1) * LANES], wg_ref[:, kcols], bgate_ref[:, kcols])
        bc = _mask_dot(tri_b, loga)
        bl = _mask_dot(same_b, loga)
        k = k_ref[:, kcols]
        qs = q_ref[:, kcols] * (B_DK ** -0.5)
        qt = (qs * jnp.exp(bc)).astype(BF16)
        kt = (k * jnp.exp(bl - bc)).astype(BF16)
        eb = jnp.exp(bl)
        vb = v_ref[:, vcols].astype(BF16)

        bc3, q3, k3 = (x.reshape(nb, B, B_DK) for x in (bc * LOG2_E, qs, k))
        a = jnp.zeros((L, LANES), F32)
        for s in range(B):
            e = jnp.exp2(jnp.minimum(bc3 - bc3[:, s:s + 1, :], 0.0))
            a_col = jnp.sum(q3 * k3[:, s:s + 1, :] * e, axis=2, keepdims=True).reshape(L, 1)
            a = jnp.where(s_lane == s, a_col, a)
        ab = jnp.where(t_in >= s_lane, a, 0.0)[:, 0:B].astype(BF16)

        for j in range(nb):
            ds_ref[hh, j] = _dot_tn(vb[j * B:(j + 1) * B, :], kt[j * B:(j + 1) * B, :])
        st = st_ref[hh]
        for j in range(nb):
            sb_ref[hh, j] = st.astype(BF16)
            st = st * eb[j * B:j * B + 1, :] + ds_ref[hh, j]
        st_ref[hh] = st
        o = jnp.concatenate(
            [_dot(ab[j * B:(j + 1) * B, :], vb[j * B:(j + 1) * B, :])
             + _dot_nt(qt[j * B:(j + 1) * B, :], sb_ref[hh, j]) for j in range(nb)], axis=0)

        on = o * lax.rsqrt(jnp.mean(o * o, axis=1, keepdims=True) + LN_EPS) * ng_ref[:, vcols]
        yb_ref[:, vcols] = (on * _silu(bz_ref[:, vcols])).astype(yb_ref.dtype)

    @pl.when(c == pl.num_programs(2) - 1)
    def _():
        for hh in range(GLA_HEADS_PER_STEP):
            s_out_ref[0, hh] = st_ref[hh].T


def _gla(p_all, batch, seq, w_gate_up, b_gate, b_norm_g):
    L = GLA_STEP
    hps = GLA_HEADS_PER_STEP
    nc = seq // L
    t = batch * seq
    kw, vw = hps * B_DK, hps * B_DV

    def rows(b, h, c):
        return b * nc + c

    in_specs = [
        pl.BlockSpec((L, kw), lambda b, h, c: (rows(b, h, c), COL_BQ // kw + h)),
        pl.BlockSpec((L, kw), lambda b, h, c: (rows(b, h, c), COL_BK // kw + h)),
        pl.BlockSpec((L, vw), lambda b, h, c: (rows(b, h, c), COL_BV // vw + h)),
        pl.BlockSpec((L, vw), lambda b, h, c: (rows(b, h, c), COL_BZ // vw + h)),
        pl.BlockSpec((L, hps * LANES), lambda b, h, c: (rows(b, h, c), COL_GATE // (hps * LANES) + h)),
        pl.BlockSpec((GATE_RANK, kw), lambda b, h, c: (0, h)),
        pl.BlockSpec((1, kw), lambda b, h, c: (0, h)),
        pl.BlockSpec((1, vw), lambda b, h, c: (0, h)),
    ]
    out_specs = [
        pl.BlockSpec((L, vw), lambda b, h, c: (rows(b, h, c), h)),
        pl.BlockSpec((1, hps, B_DK, B_DV), lambda b, h, c: (b, h, 0, 0)),
    ]
    out_shape = [
        jax.ShapeDtypeStruct((t, B_WIDTH), BF16),
        jax.ShapeDtypeStruct((batch, B_HEADS, B_DK, B_DV), F32),
    ]
    return pl.pallas_call(
        _gla_kernel,
        grid=(batch, B_HEADS // hps, nc),
        in_specs=in_specs,
        out_specs=out_specs,
        out_shape=out_shape,
        scratch_shapes=[pltpu.VMEM((hps, B_DV, B_DK), F32),
                        pltpu.VMEM((hps, L // GLA_BLOCK, B_DV, B_DK), F32),
                        pltpu.VMEM((hps, L // GLA_BLOCK, B_DV, B_DK), BF16)],
        compiler_params=pltpu.CompilerParams(
            dimension_semantics=("parallel", "parallel", "arbitrary"), vmem_limit_bytes=VMEM_LIMIT),
        name="gla",
    )(p_all, p_all, p_all, p_all, p_all, w_gate_up, b_gate, b_norm_g)


DEC_TOKENS = 16


def _decode_kernel(qp_ref, kp_ref, av_ref, ao_ref, az_ref, bq_ref, bk_ref, bv_ref, bz_ref, g_ref,
                   sq0_ref, sq1_ref, sq2_ref, sk0_ref, sk1_ref, sk2_ref,
                   cwq_ref, cwk_ref, cbq_ref, cbk_ref, gb_ref, ang_ref, wg_ref, bgate_ref, bng_ref,
                   c_ref, n_ref, m_ref, s_ref,
                   ya_ref, yb_ref, c_out_ref, n_out_ref, m_out_ref, s_out_ref):
    TB = DEC_TOKENS
    h_idx = pl.program_id(1)

    def conv_silu(s0, s1, s2, x, w_ref, b_ref):
        acc = b_ref[...] + s0[...] * w_ref[0:1, :]
        acc = acc + s1[...] * w_ref[1:2, :]
        acc = acc + s2[...] * w_ref[2:3, :]
        acc = acc + x[...] * w_ref[3:4, :]
        return _silu(acc)

    q = conv_silu(sq0_ref, sq1_ref, sq2_ref, qp_ref, cwq_ref, cbq_ref)
    k = conv_silu(sk0_ref, sk1_ref, sk2_ref, kp_ref, cwk_ref, cbk_ref) * (A_DK ** -0.5)
    v = av_ref[...]
    g = g_ref[...]
    gbias = gb_ref[...]
    itil = g[:, 0:1] + gbias[:, 0:1]
    logf = _log_sigmoid(g[:, 1:2] + gbias[:, 1:2])
    lane_h = lax.broadcasted_iota(jnp.int32, (TB, A_HEADS), 1)
    m_prev = jnp.sum(jnp.where(lane_h == h_idx, m_ref[...], 0.0), axis=1, keepdims=True)
    inter = logf + m_prev
    m_t = jnp.maximum(inter, itil)
    w = jnp.exp(itil - m_t)
    decay = jnp.exp(inter - m_t)
    n_old = n_ref[:, 0, 0, :]
    s = jnp.sum(q * k, axis=1, keepdims=True) * w
    den = decay * jnp.sum(q * n_old, axis=1, keepdims=True) + s
    scale = 1.0 / jnp.maximum(jnp.abs(den), jnp.exp(-m_t))
    kw = k * w
    n_out_ref[:, 0, 0, :] = decay * n_old + kw
    m_out_ref[...] = jnp.broadcast_to(m_t, (TB, LANES))

    rows8 = lax.broadcasted_iota(jnp.int32, (TB, 1), 0)
    qb, kwb, vb = q.astype(BF16), kw.astype(BF16), v.astype(BF16)
    h_rows = []
    for t in range(TB):
        c_old = c_ref[t, 0]
        sel = rows8 == t
        qc = _dot(qb, c_old.astype(BF16))[t:t + 1, :]
        h_rows.append((decay[t:t + 1, :] * qc + s[t:t + 1, :] * v[t:t + 1, :]) * scale[t:t + 1, :])
        outer = _dot_tn(jnp.where(sel, kwb, jnp.zeros_like(kwb)), vb)
        c_out_ref[t, 0] = decay[t:t + 1, :] * c_old + outer
    h = jnp.concatenate(h_rows, axis=0)
    mu = jnp.mean(h, axis=1, keepdims=True)
    hc = h - mu
    var = jnp.mean(hc * hc, axis=1, keepdims=True)
    hn = hc * lax.rsqrt(var + LN_EPS) * ang_ref[...]
    ya_ref[...] = (hn * _sigmoid(ao_ref[...]) * _silu(az_ref[...])).astype(ya_ref.dtype)

    gq = bq_ref[...] * (B_DK ** -0.5)
    gk = bk_ref[...]
    gv = bv_ref[...]
    loga = _gla_gate_log(g, wg_ref[...], bgate_ref[...])
    eb = jnp.exp(loga)
    a = jnp.sum(gq * gk, axis=1, keepdims=True)
    ebt = jnp.concatenate([eb, jnp.zeros((LANES - TB, B_DK), F32)], axis=0).T
    qeb, gkb, gvb = (gq * eb).astype(BF16), gk.astype(BF16), gv.astype(BF16)
    o_rows = []
    for t in range(TB):
        s_old = s_ref[t, 0]
        sel = rows8 == t
        o_rows.append(_dot(qeb, s_old.astype(BF16))[t:t + 1, :] + a[t:t + 1, :] * gv[t:t + 1, :])
        outer = _dot_tn(jnp.where(sel, gkb, jnp.zeros_like(gkb)), gvb)
        s_out_ref[t, 0] = ebt[:, t:t + 1] * s_old + outer
    o = jnp.concatenate(o_rows, axis=0)
    on = o * lax.rsqrt(jnp.mean(o * o, axis=1, keepdims=True) + LN_EPS) * bng_ref[...]
    yb_ref[...] = (on * _silu(bz_ref[...])).astype(yb_ref.dtype)


def _decode(p_s, conv_state, conv_w, conv_b, gate_bias, a_norm_g, w_gate_up, b_gate, b_norm_g,
            c_state, n_state, m_state, s_state):
    nseq = p_s.shape[0]
    TB = DEC_TOKENS

    def pcol(base, width):
        return pl.BlockSpec((TB, width), lambda i, h: (i, base // width + h))

    def cstate(j, base):
        return pl.BlockSpec((TB, A_DK), lambda i, h: (i, (j * 2 * A_WIDTH + base) // A_DK + h))

    in_specs = [
        pcol(COL_Q, A_DK), pcol(COL_K, A_DK), pcol(COL_AV, A_DV), pcol(COL_AO, A_DV), pcol(COL_AZ, A_DV),
        pcol(COL_BQ, B_DK), pcol(COL_BK, B_DK), pcol(COL_BV, B_DV), pcol(COL_BZ, B_DV),
        pcol(COL_GATE, LANES),
        cstate(0, COL_Q), cstate(1, COL_Q), cstate(2, COL_Q),
        cstate(0, COL_K), cstate(1, COL_K), cstate(2, COL_K),
        pl.BlockSpec((CONV_W, A_DK), lambda i, h: (0, h)),
        pl.BlockSpec((CONV_W, A_DK), lambda i, h: (0, A_HEADS + h)),
        pl.BlockSpec((1, A_DK), lambda i, h: (0, h)),
        pl.BlockSpec((1, A_DK), lambda i, h: (0, A_HEADS + h)),
        pl.BlockSpec((1, LANES), lambda i, h: (0, h)),
        pl.BlockSpec((1, A_DV), lambda i, h: (0, h)),
        pl.BlockSpec((GATE_RANK, B_DK), lambda i, h: (0, h)),
        pl.BlockSpec((1, B_DK), lambda i, h: (0, h)),
        pl.BlockSpec((1, B_DV), lambda i, h: (0, h)),
        pl.BlockSpec((TB, 1, A_DK, A_DV), lambda i, h: (i, h, 0, 0)),
        pl.BlockSpec((TB, 1, 1, A_DK), lambda i, h: (i, h, 0, 0)),
        pl.BlockSpec((TB, A_HEADS), lambda i, h: (i, 0)),
        pl.BlockSpec((TB, 1, B_DK, B_DV), lambda i, h: (i, h, 0, 0)),
    ]
    out_specs = [
        pl.BlockSpec((TB, A_DV), lambda i, h: (i, h)),
        pl.BlockSpec((TB, B_DV), lambda i, h: (i, h)),
        pl.BlockSpec((TB, 1, A_DK, A_DV), lambda i, h: (i, h, 0, 0)),
        pl.BlockSpec((TB, 1, 1, A_DK), lambda i, h: (i, h, 0, 0)),
        pl.BlockSpec((TB, LANES), lambda i, h: (i, h)),
        pl.BlockSpec((TB, 1, B_DK, B_DV), lambda i, h: (i, h, 0, 0)),
    ]
    out_shape = [
        jax.ShapeDtypeStruct((nseq, A_WIDTH), BF16),
        jax.ShapeDtypeStruct((nseq, B_WIDTH), BF16),
        jax.ShapeDtypeStruct((nseq, A_HEADS, A_DK, A_DV), F32),
        jax.ShapeDtypeStruct((nseq, A_HEADS, 1, A_DK), F32),
        jax.ShapeDtypeStruct((nseq, A_HEADS * LANES), F32),
        jax.ShapeDtypeStruct((nseq, B_HEADS, B_DK, B_DV), F32),
    ]
    return pl.pallas_call(
        _decode_kernel,
        grid=(nseq // TB, A_HEADS),
        in_specs=in_specs,
        out_specs=out_specs,
        out_shape=out_shape,
        compiler_params=pltpu.CompilerParams(
            dimension_semantics=("parallel", "parallel"), vmem_limit_bytes=VMEM_LIMIT),
        name="decode",
    )(p_s, p_s, p_s, p_s, p_s, p_s, p_s, p_s, p_s, p_s,
      conv_state, conv_state, conv_state, conv_state, conv_state, conv_state,
      conv_w, conv_w, conv_b, conv_b, gate_bias, a_norm_g, w_gate_up, b_gate, b_norm_g,
      c_state, n_state, m_state, s_state)


def _out_kernel(ya_ref, yb_ref, ga_ref, gb_ref, x_ref, wpa_ref, wpb_ref, wo_ref, lng_ref, lnb_ref, o_ref):
    pa = _dot(ya_ref[...], wpa_ref[...])
    pb = _dot(yb_ref[...], wpb_ref[...])
    merged = _sigmoid(ga_ref[...]) * pa + _sigmoid(gb_ref[...]) * pb
    y = _dot(merged.astype(BF16), wo_ref[...])
    r = ALPHA * x_ref[...] + y
    mu = jnp.mean(r, axis=1, keepdims=True)
    rc = r - mu
    var = jnp.mean(rc * rc, axis=1, keepdims=True)
    o_ref[...] = rc * lax.rsqrt(var + LN_EPS) * lng_ref[...] + lnb_ref[...]


def _out(ya, yb, p_all, x, w_pa, w_pb, w_out, ln_g, ln_b, tm):
    t = x.shape[0]
    const = lambda i: (0, 0)
    single = pl.Buffered(1)
    in_specs = [
        pl.BlockSpec((tm, A_WIDTH), lambda i: (i, 0)),
        pl.BlockSpec((tm, B_WIDTH), lambda i: (i, 0)),
        pl.BlockSpec((tm, D_MODEL), lambda i: (i, COL_GA // D_MODEL)),
        pl.BlockSpec((tm, D_MODEL), lambda i: (i, COL_GB // D_MODEL)),
        pl.BlockSpec((tm, D_MODEL), lambda i: (i, 0)),
        pl.BlockSpec((A_WIDTH, D_MODEL), const, pipeline_mode=single),
        pl.BlockSpec((B_WIDTH, D_MODEL), const, pipeline_mode=single),
        pl.BlockSpec((D_MODEL, D_MODEL), const, pipeline_mode=single),
        pl.BlockSpec((1, D_MODEL), const),
        pl.BlockSpec((1, D_MODEL), const),
    ]
    return pl.pallas_call(
        _out_kernel,
        grid=(t // tm,),
        in_specs=in_specs,
        out_specs=pl.BlockSpec((tm, D_MODEL), lambda i: (i, 0)),
        out_shape=jax.ShapeDtypeStruct((t, D_MODEL), F32),
        compiler_params=pltpu.CompilerParams(
            dimension_semantics=("parallel",), vmem_limit_bytes=VMEM_LIMIT),
        name="outproj",
    )(ya, yb, p_all, p_all, x, w_pa, w_pb, w_out, ln_g, ln_b)


RELAYOUT_TN = 512
SRC_I = 3 * A_WIDTH
SRC_SHIFT_A = 2 * A_HEADS
SRC_BG = COL_BZ + SRC_SHIFT_A
SRC_SHIFT_B = SRC_SHIFT_A + GATE_RANK
GATE_SRC_ROWS = 4 * SUBLANES


def _relayout_kernel(src_ref, gif_ref, gb0_ref, gb1_ref, o_ref):
    j = pl.program_id(0)
    tn = RELAYOUT_TN
    n_main = COL_GATE // tn

    @pl.when(j < n_main)
    def _():
        o_ref[...] = src_ref[...].astype(BF16)

    @pl.when(j == n_main)
    def _():
        g = jnp.concatenate([gif_ref[...], gb0_ref[...], gb1_ref[...],
                             jnp.zeros((SUBLANES, src_ref.shape[1]), F32)], axis=0).astype(BF16)
        r = lax.broadcasted_iota(jnp.int32, (tn, GATE_SRC_ROWS), 0)
        c = lax.broadcasted_iota(jnp.int32, (tn, GATE_SRC_ROWS), 1)
        h, l = r // LANES, r % LANES
        sel = (((l == 0) & (c == h)) | ((l == 1) & (c == A_HEADS + h))
               | ((l >= 2) & (l < 2 + GATE_RANK) & (c == l + (2 * A_HEADS - 2))))
        o_ref[...] = _dot(sel.astype(BF16), g).astype(BF16)


def _relayout_w_in(w_in_t):
    n_src, k = w_in_t.shape
    tn = RELAYOUT_TN

    def src_row(j):
        shift = jnp.where(j < COL_AO // tn, 0, jnp.where(j < COL_BZ // tn, SRC_SHIFT_A, SRC_SHIFT_B))
        group = jnp.minimum(j * (tn // SUBLANES) + shift // SUBLANES, (n_src - tn) // SUBLANES)
        return group * SUBLANES

    return pl.pallas_call(
        _relayout_kernel,
        grid=(N_PROJ // tn,),
        in_specs=[pl.BlockSpec((pl.Element(tn), pl.Element(k)), lambda j: (src_row(j), 0)),
                  pl.BlockSpec((SUBLANES, k), lambda j: (SRC_I // SUBLANES, 0)),
                  pl.BlockSpec((SUBLANES, k), lambda j: (SRC_BG // SUBLANES, 0)),
                  pl.BlockSpec((SUBLANES, k), lambda j: (SRC_BG // SUBLANES + 1, 0))],
        out_specs=pl.BlockSpec((tn, k), lambda j: (j, 0)),
        out_shape=jax.ShapeDtypeStruct((N_PROJ, k), BF16),
        compiler_params=pltpu.CompilerParams(
            dimension_semantics=("parallel",), vmem_limit_bytes=VMEM_LIMIT),
        name="relayout",
    )(w_in_t, w_in_t, w_in_t, w_in_t)


def kernel(x_prompt, x_sample, state_mlstm_C, state_mlstm_n, state_mlstm_m, state_conv, state_gla_S,
           w_in, conv_w, conv_b, b_i, b_f, a_norm_g, w_gate_up, b_gate, b_norm_g, w_pa, w_pb, w_out,
           ln_g, ln_b):
    batch, seq, _ = x_prompt.shape
    nseq = x_sample.shape[0]
    assert w_in.shape[0] == 1, "single-layer step"
    d = 0

    def layer(a):
        return a.reshape(a.shape[1:])

    wp = _relayout_w_in(layer(w_in).T)
    wpa, wpb, wo = w_pa[d].astype(BF16), w_pb[d].astype(BF16), w_out[d].astype(BF16)
    cw = conv_w[d]
    cb = conv_b[d][None, :]
    gate_bias = jnp.zeros((A_HEADS, LANES), F32).at[:, 0].set(b_i[d]).at[:, 1].set(b_f[d]).reshape(1, -1)
    ang = a_norm_g[d][None, :]
    bng = b_norm_g[d][None, :]
    wg = w_gate_up[d]
    bgate = b_gate[d][None, :]
    lng, lnb = ln_g[d][None, :], ln_b[d][None, :]

    xp = x_prompt.reshape(batch * seq, D_MODEL)
    p_all = _proj(xp, wp, tm=1024, tn=1280)
    ya, p_c, p_n, p_m = _mlstm(p_all, batch, seq, cw, cb, gate_bias, ang)
    yb, p_s = _gla(p_all, batch, seq, wg, bgate, bng)
    y_prompt = _out(ya, yb, p_all, xp, wpa, wpb, wo, lng, lnb, tm=256).reshape(batch, seq, D_MODEL)
    p_conv = p_all.reshape(batch, seq, N_PROJ)[:, seq - (CONV_W - 1):, :2 * A_WIDTH]

    xs = x_sample.reshape(nseq, D_MODEL)
    ps_all = _proj(xs, wp, tm=nseq, tn=512)
    conv_state = state_conv.reshape(nseq, (CONV_W - 1) * 2 * A_WIDTH)
    ya_s, yb_s, s_c, s_n, s_m, s_s = _decode(
        ps_all, conv_state, cw, cb, gate_bias, ang, wg, bgate, bng,
        layer(state_mlstm_C), state_mlstm_n.reshape(nseq, A_HEADS, 1, A_DK), layer(state_mlstm_m),
        layer(state_gla_S))
    y_sample = _out(ya_s, yb_s, ps_all, xs, wpa, wpb, wo, lng, lnb, tm=nseq).reshape(nseq, 1, D_MODEL)
    s_conv = jnp.concatenate([layer(state_conv)[:, 1:, :], ps_all[:, None, :2 * A_WIDTH]], axis=1)

    def stacked(a):
        return a.reshape((1,) + a.shape)

    return (y_prompt, y_sample,
            stacked(p_c), p_n.reshape(1, batch, A_HEADS, A_DK), stacked(p_m[:, :, 0, 0]),
            stacked(p_conv), stacked(p_s),
            stacked(s_c), s_n.reshape(1, nseq, A_HEADS, A_DK),
            stacked(s_m.reshape(nseq, A_HEADS, LANES)[:, :, 0]), stacked(s_conv), stacked(s_s))
```

```python
import functools

import jax
import jax.numpy as jnp
from jax import lax
from jax.experimental import pallas as pl
from jax.experimental.pallas import tpu as pltpu

F32 = jnp.float32
BF16 = jnp.bfloat16

D_MODEL = 2048
A_HEADS = 4
A_WIDTH = 1024
A_DK = 256
A_DV = 256
CONV_W = 4
B_HEADS = 4
B_WIDTH = 1024
B_KWIDTH = 512
B_DK = 128
B_DV = 256
GATE_RANK = 16
GATE_TAU = 16.0
ALPHA = 2.0 ** 0.25
LN_EPS = 1e-5
LOG2_E = 1.4426950408889634

LANES = 128
SUBLANES = 8
VMEM_LIMIT = 48 * 1024 * 1024

COL_Q = 0
COL_K = A_WIDTH
COL_AV = 2 * A_WIDTH
COL_AO = 3 * A_WIDTH
COL_AZ = 4 * A_WIDTH
COL_BQ = 5 * A_WIDTH
COL_BK = COL_BQ + B_KWIDTH
COL_BV = COL_BK + B_KWIDTH
COL_BZ = COL_BV + B_WIDTH
COL_GA = COL_BZ + B_WIDTH
COL_GB = COL_GA + D_MODEL
COL_GATE = COL_GB + D_MODEL
N_PROJ = COL_GATE + A_HEADS * LANES

MLSTM_CHUNK = 256
MLSTM_HEADS_PER_STEP = 4
GLA_STEP = 256
GLA_BLOCK = 16
GLA_HEADS_PER_STEP = 4


def _dot(a, b):
    return jnp.dot(a, b, preferred_element_type=F32)


def _dot_nt(a, b):
    return lax.dot_general(a, b, (((1,), (1,)), ((), ())), preferred_element_type=F32)


def _dot_tn(a, b):
    return lax.dot_general(a, b, (((0,), (0,)), ((), ())), preferred_element_type=F32)


def _mask_dot(mask_bf16, x):
    hi = x.astype(BF16)
    r1 = x - hi.astype(F32)
    mid = r1.astype(BF16)
    lo = (r1 - mid.astype(F32)).astype(BF16)
    return _dot(mask_bf16, hi) + _dot(mask_bf16, mid) + _dot(mask_bf16, lo)


def _log_sigmoid(z):
    return jnp.minimum(z, 0.0) - jnp.log(1.0 + jnp.exp(-jnp.abs(z)))


def _sigmoid(z):
    return 1.0 / (1.0 + jnp.exp(-z))


def _silu(z):
    return z * _sigmoid(z)


def _proj_kernel(x_ref, w_ref, o_ref, xb_ref):
    @pl.when(pl.program_id(1) == 0)
    def _():
        xb_ref[...] = x_ref[...].astype(BF16)

    o_ref[...] = _dot_nt(xb_ref[...], w_ref[...])


def _proj(x, w_t, tm, tn):
    t, k = x.shape
    n = w_t.shape[0]
    return pl.pallas_call(
        _proj_kernel,
        grid=(t // tm, n // tn),
        in_specs=[pl.BlockSpec((tm, k), lambda i, j: (i, 0)),
                  pl.BlockSpec((tn, k), lambda i, j: (j, 0))],
        out_specs=pl.BlockSpec((tm, tn), lambda i, j: (i, j)),
        out_shape=jax.ShapeDtypeStruct((t, n), F32),
        scratch_shapes=[pltpu.VMEM((tm, k), BF16)],
        compiler_params=pltpu.CompilerParams(
            dimension_semantics=("parallel", "arbitrary"), vmem_limit_bytes=VMEM_LIMIT),
        name="proj",
    )(x, w_t)


def _mlstm_kernel(qp_ref, kp_ref, v_ref, ao_ref, az_ref, g_ref, cwq_ref, cwk_ref, cbq_ref, cbk_ref,
                  gb_ref, ng_ref, ya_ref, c_ref, n_ref, m_ref, tailq_ref, tailk_ref):
    L = MLSTM_CHUNK
    c = pl.program_id(2)

    @pl.when(c == 0)
    def _():
        c_ref[...] = jnp.zeros_like(c_ref)
        n_ref[...] = jnp.zeros_like(n_ref)
        m_ref[...] = jnp.zeros_like(m_ref)
        tailq_ref[...] = jnp.zeros_like(tailq_ref)
        tailk_ref[...] = jnp.zeros_like(tailk_ref)

    row8 = lax.broadcasted_iota(jnp.int32, (SUBLANES, 1), 0)

    def conv_silu(x_ref, tail_ref, w_ref, b_ref):
        x = x_ref[...]
        tail = tail_ref[...]
        acc = b_ref[...] + x * w_ref[CONV_W - 1:CONV_W, :]
        for j in range(1, CONV_W):
            xs = pltpu.roll(x, j, axis=0)
            head = jnp.where(row8 < j, pltpu.roll(tail, j, axis=0), xs[0:SUBLANES, :])
            xs = jnp.concatenate([head, xs[SUBLANES:, :]], axis=0)
            acc = acc + xs * w_ref[CONV_W - 1 - j:CONV_W - j, :]
        tail_ref[...] = x[L - SUBLANES:, :]
        return _silu(acc)

    q_all = conv_silu(qp_ref, tailq_ref, cwq_ref, cbq_ref)
    k_all = conv_silu(kp_ref, tailk_ref, cwk_ref, cbk_ref) * (A_DK ** -0.5)

    lane = lax.broadcasted_iota(jnp.int32, (L, LANES), 1)
    row = lax.broadcasted_iota(jnp.int32, (L, L), 0)
    col = lax.broadcasted_iota(jnp.int32, (L, L), 1)
    causal = row >= col
    causal_b = causal.astype(BF16)

    for hh in range(MLSTM_HEADS_PER_STEP):
        cols = slice(hh * A_DK, (hh + 1) * A_DK)
        q, k, v = q_all[:, cols], k_all[:, cols], v_ref[:, cols]

        g = g_ref[:, hh * LANES:(hh + 1) * LANES] + gb_ref[:, hh * LANES:(hh + 1) * LANES]
        g2 = jnp.where(lane == 1, _log_sigmoid(g), g)
        cum = _mask_dot(causal_b, g2)
        x2 = jnp.where(lane == 1, cum, g2)
        x2t = x2.T
        itil_col, b_col = x2[:, 0:1], x2[:, 1:2]
        itil_row, b_row = x2t[0:1, :], x2t[1:2, :]

        m_prev = m_ref[0, hh, 0:1, 0:1]
        dmat = jnp.where(causal, b_col - b_row + itil_row, -jnp.inf)
        inter = b_col + m_prev
        m_t = jnp.maximum(inter, jnp.max(dmat, axis=1, keepdims=True))
        w = jnp.exp(dmat - m_t)
        decay = jnp.exp(inter - m_t)

        qb, kb, vb = q.astype(BF16), k.astype(BF16), v.astype(BF16)
        c_old = c_ref[0, hh]
        n_old = n_ref[0, hh]
        s = _dot_nt(qb, kb) * w
        num = decay * _dot(qb, c_old.astype(BF16)) + _dot(s.astype(BF16), vb)
        den = decay * jnp.sum(q * n_old, axis=1, keepdims=True) + jnp.sum(s, axis=1, keepdims=True)
        h = num / jnp.maximum(jnp.abs(den), jnp.exp(-m_t))

        m_new = m_t[L - 1:L, :]
        b_last = b_col[L - 1:L, :]
        wk = jnp.exp(b_last - b_col + itil_col - m_new)
        dec = jnp.exp(b_last + m_prev - m_new)
        kw = k * wk
        c_ref[0, hh] = dec * c_old + _dot(kw.T.astype(BF16), vb)
        n_ref[0, hh] = dec * n_old + jnp.sum(kw, axis=0, keepdims=True)
        m_ref[0, hh] = jnp.broadcast_to(m_new, (1, LANES))

        mu = jnp.mean(h, axis=1, keepdims=True)
        hc = h - mu
        var = jnp.mean(hc * hc, axis=1, keepdims=True)
        hn = hc * lax.rsqrt(var + LN_EPS) * ng_ref[:, cols]
        ya_ref[:, cols] = (hn * _sigmoid(ao_ref[:, cols]) * _silu(az_ref[:, cols])).astype(ya_ref.dtype)


def _mlstm(p_all, batch, seq, conv_w, conv_b, gate_bias, a_norm_g):
    L = MLSTM_CHUNK
    hps = MLSTM_HEADS_PER_STEP
    nc = seq // L
    t = batch * seq
    wd = hps * A_DK
    k_off = A_WIDTH // wd

    def rows(b, h, c):
        return b * nc + c

    def pcol(base):
        return pl.BlockSpec((L, wd), lambda b, h, c: (rows(b, h, c), base // wd + h))

    in_specs = [
        pcol(COL_Q), pcol(COL_K), pcol(COL_AV), pcol(COL_AO), pcol(COL_AZ),
        pl.BlockSpec((L, hps * LANES), lambda b, h, c: (rows(b, h, c), COL_GATE // (hps * LANES) + h)),
        pl.BlockSpec((CONV_W, wd), lambda b, h, c: (0, h)),
        pl.BlockSpec((CONV_W, wd), lambda b, h, c: (0, k_off + h)),
        pl.BlockSpec((1, wd), lambda b, h, c: (0, h)),
        pl.BlockSpec((1, wd), lambda b, h, c: (0, k_off + h)),
        pl.BlockSpec((1, hps * LANES), lambda b, h, c: (0, h)),
        pl.BlockSpec((1, wd), lambda b, h, c: (0, h)),
    ]
    out_specs = [
        pl.BlockSpec((L, wd), lambda b, h, c: (rows(b, h, c), h)),
        pl.BlockSpec((1, hps, A_DK, A_DV), lambda b, h, c: (b, h, 0, 0)),
        pl.BlockSpec((1, hps, 1, A_DK), lambda b, h, c: (b, h, 0, 0)),
        pl.BlockSpec((1, hps, 1, LANES), lambda b, h, c: (b, h, 0, 0)),
    ]
    out_shape = [
        jax.ShapeDtypeStruct((t, A_WIDTH), BF16),
        jax.ShapeDtypeStruct((batch, A_HEADS, A_DK, A_DV), F32),
        jax.ShapeDtypeStruct((batch, A_HEADS, 1, A_DK), F32),
        jax.ShapeDtypeStruct((batch, A_HEADS, 1, LANES), F32),
    ]
    return pl.pallas_call(
        _mlstm_kernel,
        grid=(batch, A_HEADS // hps, nc),
        in_specs=in_specs,
        out_specs=out_specs,
        out_shape=out_shape,
        scratch_shapes=[pltpu.VMEM((SUBLANES, wd), F32),
                        pltpu.VMEM((SUBLANES, wd), F32)],
        compiler_params=pltpu.CompilerParams(
            dimension_semantics=("parallel", "parallel", "arbitrary"), vmem_limit_bytes=VMEM_LIMIT),
        name="mlstm",
    )(p_all, p_all, p_all, p_all, p_all, p_all, conv_w, conv_w, conv_b, conv_b, gate_bias, a_norm_g)


def _gla_gate_log(bg_tile, wg, bgate):
    lr = bg_tile[:, 2:2 + GATE_RANK]
    z = _dot(lr.astype(BF16), wg.astype(BF16)) + bgate
    return _log_sigmoid(z) / GATE_TAU


def _gla_kernel(q_ref, k_ref, v_ref, bz_ref, g_ref, wg_ref, bgate_ref, ng_ref,
                yb_ref, s_out_ref, st_ref, ds_ref, sb_ref):
    L = GLA_STEP
    B = GLA_BLOCK
    nb = L // B
    c = pl.program_id(2)

    @pl.when(c == 0)
    def _():
        st_ref[...] = jnp.zeros_like(st_ref)

    row = lax.broadcasted_iota(jnp.int32, (L, L), 0)
    col = lax.broadcasted_iota(jnp.int32, (L, L), 1)
    same = (row // B) == (col // B)
    tri_b = (same & (row >= col)).astype(BF16)
    same_b = same.astype(BF16)
    t_in = lax.broadcasted_iota(jnp.int32, (L, 1), 0) % B
    s_lane = lax.broadcasted_iota(jnp.int32, (1, LANES), 1)

    for hh in range(GLA_HEADS_PER_STEP):
        kcols = slice(hh * B_DK, (hh + 1) * B_DK)
        vcols = slice(hh * B_DV, (hh + 1) * B_DV)
        loga = _gla_gate_log(g_ref[:, hh * LANES:(hh + 1) * LANES], wg_ref[:, kcols], bgate_ref[:, kcols])
        bc = _mask_dot(tri_b, loga)
        bl = _mask_dot(same_b, loga)
        k = k_ref[:, kcols]
        qs = q_ref[:, kcols] * (B_DK ** -0.5)
        qt = (qs * jnp.exp(bc)).astype(BF16)
        kt = (k * jnp.exp(bl - bc)).astype(BF16)
        eb = jnp.exp(bl)
        vb = v_ref[:, vcols].astype(BF16)

        bc3, q3, k3 = (x.reshape(nb, B, B_DK) for x in (bc * LOG2_E, qs, k))
        a = jnp.zeros((L, LANES), F32)
        for s in range(B):
            e = jnp.exp2(jnp.minimum(bc3 - bc3[:, s:s + 1, :], 0.0))
            a_col = jnp.sum(q3 * k3[:, s:s + 1, :] * e, axis=2, keepdims=True).reshape(L, 1)
            a = jnp.where(s_lane == s, a_col, a)
        ab = jnp.where(t_in >= s_lane, a, 0.0)[:, 0:B].astype(BF16)

        for j in range(nb):
            ds_ref[hh, j] = _dot_tn(vb[j * B:(j + 1) * B, :], kt[j * B:(j + 1) * B, :])
        st = st_ref[hh]
        for j in range(nb):
            sb_ref[hh, j] = st.astype(BF16)
            st = st * eb[j * B:j * B + 1, :] + ds_ref[hh, j]
        st_ref[hh] = st
        o = jnp.concatenate(
            [_dot(ab[j * B:(j + 1) * B, :], vb[j * B:(j + 1) * B, :])
             + _dot_nt(qt[j * B:(j + 1) * B, :], sb_ref[hh, j]) for j in range(nb)], axis=0)

        on = o * lax.rsqrt(jnp.mean(o * o, axis=1, keepdims=True) + LN_EPS) * ng_ref[:, vcols]
        yb_ref[:, vcols] = (on * _silu(bz_ref[:, vcols])).astype(yb_ref.dtype)

    @pl.when(c == pl.num_programs(2) - 1)
    def _():
        for hh in range(GLA_HEADS_PER_STEP):
            s_out_ref[0, hh] = st_ref[hh].T


def _gla(p_all, batch, seq, w_gate_up, b_gate, b_norm_g):
    L = GLA_STEP
    hps = GLA_HEADS_PER_STEP
    nc = seq // L
    t = batch * seq
    kw, vw = hps * B_DK, hps * B_DV

    def rows(b, h, c):
        return b * nc + c

    in_specs = [
        pl.BlockSpec((L, kw), lambda b, h, c: (rows(b, h, c), COL_BQ // kw + h)),
        pl.BlockSpec((L, kw), lambda b, h, c: (rows(b, h, c), COL_BK // kw + h)),
        pl.BlockSpec((L, vw), lambda b, h, c: (rows(b, h, c), COL_BV // vw + h)),
        pl.BlockSpec((L, vw), lambda b, h, c: (rows(b, h, c), COL_BZ // vw + h)),
        pl.BlockSpec((L, hps * LANES), lambda b, h, c: (rows(b, h, c), COL_GATE // (hps * LANES) + h)),
        pl.BlockSpec((GATE_RANK, kw), lambda b, h, c: (0, h)),
        pl.BlockSpec((1, kw), lambda b, h, c: (0, h)),
        pl.BlockSpec((1, vw), lambda b, h, c: (0, h)),
    ]
    out_specs = [
        pl.BlockSpec((L, vw), lambda b, h, c: (rows(b, h, c), h)),
        pl.BlockSpec((1, hps, B_DK, B_DV), lambda b, h, c: (b, h, 0, 0)),
    ]
    out_shape = [
        jax.ShapeDtypeStruct((t, B_WIDTH), BF16),
        jax.ShapeDtypeStruct((batch, B_HEADS, B_DK, B_DV), F32),
    ]
    return pl.pallas_call(
        _gla_kernel,
        grid=(batch, B_HEADS // hps, nc),
        in_specs=in_specs,
        out_specs=out_specs,
        out_shape=out_shape,
        scratch_shapes=[pltpu.VMEM((hps, B_DV, B_DK), F32),
                        pltpu.VMEM((hps, L // GLA_BLOCK, B_DV, B_DK), F32),
                        pltpu.VMEM((hps, L // GLA_BLOCK, B_DV, B_DK), BF16)],
        compiler_params=pltpu.CompilerParams(
            dimension_semantics=("parallel", "parallel", "arbitrary"), vmem_limit_bytes=VMEM_LIMIT),
        name="gla",
    )(p_all, p_all, p_all, p_all, p_all, w_gate_up, b_gate, b_norm_g)


DEC_TOKENS = 16


def _decode_kernel(qp_ref, kp_ref, av_ref, ao_ref, az_ref, bq_ref, bk_ref, bv_ref, bz_ref, g_ref,
                   sq0_ref, sq1_ref, sq2_ref, sk0_ref, sk1_ref, sk2_ref,
                   cwq_ref, cwk_ref, cbq_ref, cbk_ref, gb_ref, ang_ref, wg_ref, bgate_ref, bng_ref,
                   c_ref, n_ref, m_ref, s_ref,
                   ya_ref, yb_ref, c_out_ref, n_out_ref, m_out_ref, s_out_ref):
    TB = DEC_TOKENS
    h_idx = pl.program_id(1)

    def conv_silu(s0, s1, s2, x, w_ref, b_ref):
        acc = b_ref[...] + s0[...] * w_ref[0:1, :]
        acc = acc + s1[...] * w_ref[1:2, :]
        acc = acc + s2[...] * w_ref[2:3, :]
        acc = acc + x[...] * w_ref[3:4, :]
        return _silu(acc)

    q = conv_silu(sq0_ref, sq1_ref, sq2_ref, qp_ref, cwq_ref, cbq_ref)
    k = conv_silu(sk0_ref, sk1_ref, sk2_ref, kp_ref, cwk_ref, cbk_ref) * (A_DK ** -0.5)
    v = av_ref[...]
    g = g_ref[...]
    gbias = gb_ref[...]
    itil = g[:, 0:1] + gbias[:, 0:1]
    logf = _log_sigmoid(g[:, 1:2] + gbias[:, 1:2])
    lane_h = lax.broadcasted_iota(jnp.int32, (TB, A_HEADS), 1)
    m_prev = jnp.sum(jnp.where(lane_h == h_idx, m_ref[...], 0.0), axis=1, keepdims=True)
    inter = logf + m_prev
    m_t = jnp.maximum(inter, itil)
    w = jnp.exp(itil - m_t)
    decay = jnp.exp(inter - m_t)
    n_old = n_ref[:, 0, 0, :]
    s = jnp.sum(q * k, axis=1, keepdims=True) * w
    den = decay * jnp.sum(q * n_old, axis=1, keepdims=True) + s
    scale = 1.0 / jnp.maximum(jnp.abs(den), jnp.exp(-m_t))
    kw = k * w
    n_out_ref[:, 0, 0, :] = decay * n_old + kw
    m_out_ref[...] = jnp.broadcast_to(m_t, (TB, LANES))

    rows8 = lax.broadcasted_iota(jnp.int32, (TB, 1), 0)
    qb, kwb, vb = q.astype(BF16), kw.astype(BF16), v.astype(BF16)
    h_rows = []
    for t in range(TB):
        c_old = c_ref[t, 0]
        sel = rows8 == t
        qc = _dot(qb, c_old.astype(BF16))[t:t + 1, :]
        h_rows.append((decay[t:t + 1, :] * qc + s[t:t + 1, :] * v[t:t + 1, :]) * scale[t:t + 1, :])
        outer = _dot_tn(jnp.where(sel, kwb, jnp.zeros_like(kwb)), vb)
        c_out_ref[t, 0] = decay[t:t + 1, :] * c_old + outer
    h = jnp.concatenate(h_rows, axis=0)
    mu = jnp.mean(h, axis=1, keepdims=True)
    hc = h - mu
    var = jnp.mean(hc * hc, axis=1, keepdims=True)
    hn = hc * lax.rsqrt(var + LN_EPS) * ang_ref[...]
    ya_ref[...] = (hn * _sigmoid(ao_ref[...]) * _silu(az_ref[...])).astype(ya_ref.dtype)

    gq = bq_ref[...] * (B_DK ** -0.5)
    gk = bk_ref[...]
    gv = bv_ref[...]
    loga = _gla_gate_log(g, wg_ref[...], bgate_ref[...])
    eb = jnp.exp(loga)
    a = jnp.sum(gq * gk, axis=1, keepdims=True)
    ebt = jnp.concatenate([eb, jnp.zeros((LANES - TB, B_DK), F32)], axis=0).T
    qeb, gkb, gvb = (gq * eb).astype(BF16), gk.astype(BF16), gv.astype(BF16)
    o_rows = []
    for t in range(TB):
        s_old = s_ref[t, 0]
        sel = rows8 == t
        o_rows.append(_dot(qeb, s_old.astype(BF16))[t:t + 1, :] + a[t:t + 1, :] * gv[t:t + 1, :])
        outer = _dot_tn(jnp.where(sel, gkb, jnp.zeros_like(gkb)), gvb)
        s_out_ref[t, 0] = ebt[:, t:t + 1] * s_old + outer
    o = jnp.concatenate(o_rows, axis=0)
    on = o * lax.rsqrt(jnp.mean(o * o, axis=1, keepdims=True) + LN_EPS) * bng_ref[...]
    yb_ref[...] = (on * _silu(bz_ref[...])).astype(yb_ref.dtype)


def _decode(p_s, conv_state, conv_w, conv_b, gate_bias, a_norm_g, w_gate_up, b_gate, b_norm_g,
            c_state, n_state, m_state, s_state):
    nseq = p_s.shape[0]
    TB = DEC_TOKENS

    def pcol(base, width):
        return pl.BlockSpec((TB, width), lambda i, h: (i, base // width + h))

    def cstate(j, base):
        return pl.BlockSpec((TB, A_DK), lambda i, h: (i, (j * 2 * A_WIDTH + base) // A_DK + h))

    in_specs = [
        pcol(COL_Q, A_DK), pcol(COL_K, A_DK), pcol(COL_AV, A_DV), pcol(COL_AO, A_DV), pcol(COL_AZ, A_DV),
        pcol(COL_BQ, B_DK), pcol(COL_BK, B_DK), pcol(COL_BV, B_DV), pcol(COL_BZ, B_DV),
        pcol(COL_GATE, LANES),
        cstate(0, COL_Q), cstate(1, COL_Q), cstate(2, COL_Q),
        cstate(0, COL_K), cstate(1, COL_K), cstate(2, COL_K),
        pl.BlockSpec((CONV_W, A_DK), lambda i, h: (0, h)),
        pl.BlockSpec((CONV_W, A_DK), lambda i, h: (0, A_HEADS + h)),
        pl.BlockSpec((1, A_DK), lambda i, h: (0, h)),
        pl.BlockSpec((1, A_DK), lambda i, h: (0, A_HEADS + h)),
        pl.BlockSpec((1, LANES), lambda i, h: (0, h)),
        pl.BlockSpec((1, A_DV), lambda i, h: (0, h)),
        pl.BlockSpec((GATE_RANK, B_DK), lambda i, h: (0, h)),
        pl.BlockSpec((1, B_DK), lambda i, h: (0, h)),
        pl.BlockSpec((1, B_DV), lambda i, h: (0, h)),
        pl.BlockSpec((TB, 1, A_DK, A_DV), lambda i, h: (i, h, 0, 0)),
        pl.BlockSpec((TB, 1, 1, A_DK), lambda i, h: (i, h, 0, 0)),
        pl.BlockSpec((TB, A_HEADS), lambda i, h: (i, 0)),
        pl.BlockSpec((TB, 1, B_DK, B_DV), lambda i, h: (i, h, 0, 0)),
    ]
    out_specs = [
        pl.BlockSpec((TB, A_DV), lambda i, h: (i, h)),
        pl.BlockSpec((TB, B_DV), lambda i, h: (i, h)),
        pl.BlockSpec((TB, 1, A_DK, A_DV), lambda i, h: (i, h, 0, 0)),
        pl.BlockSpec((TB, 1, 1, A_DK), lambda i, h: (i, h, 0, 0)),
        pl.BlockSpec((TB, LANES), lambda i, h: (i, h)),
        pl.BlockSpec((TB, 1, B_DK, B_DV), lambda i, h: (i, h, 0, 0)),
    ]
    out_shape = [
        jax.ShapeDtypeStruct((nseq, A_WIDTH), BF16),
        jax.ShapeDtypeStruct((nseq, B_WIDTH), BF16),
        jax.ShapeDtypeStruct((nseq, A_HEADS, A_DK, A_DV), F32),
        jax.ShapeDtypeStruct((nseq, A_HEADS, 1, A_DK), F32),
        jax.ShapeDtypeStruct((nseq, A_HEADS * LANES), F32),
        jax.ShapeDtypeStruct((nseq, B_HEADS, B_DK, B_DV), F32),
    ]
    return pl.pallas_call(
        _decode_kernel,
        grid=(nseq // TB, A_HEADS),
        in_specs=in_specs,
        out_specs=out_specs,
        out_shape=out_shape,
        compiler_params=pltpu.CompilerParams(
            dimension_semantics=("parallel", "parallel"), vmem_limit_bytes=VMEM_LIMIT),
        name="decode",
    )(p_s, p_s, p_s, p_s, p_s, p_s, p_s, p_s, p_s, p_s,
      conv_state, conv_state, conv_state, conv_state, conv_state, conv_state,
      conv_w, conv_w, conv_b, conv_b, gate_bias, a_norm_g, w_gate_up, b_gate, b_norm_g,
      c_state, n_state, m_state, s_state)


def _out_kernel(ya_ref, yb_ref, ga_ref, gb_ref, x_ref, wpa_ref, wpb_ref, wo_ref, lng_ref, lnb_ref, o_ref):
    pa = _dot(ya_ref[...], wpa_ref[...])
    pb = _dot(yb_ref[...], wpb_ref[...])
    merged = _sigmoid(ga_ref[...]) * pa + _sigmoid(gb_ref[...]) * pb
    y = _dot(merged.astype(BF16), wo_ref[...])
    r = ALPHA * x_ref[...] + y
    mu = jnp.mean(r, axis=1, keepdims=True)
    rc = r - mu
    var = jnp.mean(rc * rc, axis=1, keepdims=True)
    o_ref[...] = rc * lax.rsqrt(var + LN_EPS) * lng_ref[...] + lnb_ref[...]


def _out(ya, yb, p_all, x, w_pa, w_pb, w_out, ln_g, ln_b, tm):
    t = x.shape[0]
    const = lambda i: (0, 0)
    single = pl.Buffered(1)
    in_specs = [
        pl.BlockSpec((tm, A_WIDTH), lambda i: (i, 0)),
        pl.BlockSpec((tm, B_WIDTH), lambda i: (i, 0)),
        pl.BlockSpec((tm, D_MODEL), lambda i: (i, COL_GA // D_MODEL)),
        pl.BlockSpec((tm, D_MODEL), lambda i: (i, COL_GB // D_MODEL)),
        pl.BlockSpec((tm, D_MODEL), lambda i: (i, 0)),
        pl.BlockSpec((A_WIDTH, D_MODEL), const, pipeline_mode=single),
        pl.BlockSpec((B_WIDTH, D_MODEL), const, pipeline_mode=single),
        pl.BlockSpec((D_MODEL, D_MODEL), const, pipeline_mode=single),
        pl.BlockSpec((1, D_MODEL), const),
        pl.BlockSpec((1, D_MODEL), const),
    ]
    return pl.pallas_call(
        _out_kernel,
        grid=(t // tm,),
        in_specs=in_specs,
        out_specs=pl.BlockSpec((tm, D_MODEL), lambda i: (i, 0)),
        out_shape=jax.ShapeDtypeStruct((t, D_MODEL), F32),
        compiler_params=pltpu.CompilerParams(
            dimension_semantics=("parallel",), vmem_limit_bytes=VMEM_LIMIT),
        name="outproj",
    )(ya, yb, p_all, p_all, x, w_pa, w_pb, w_out, ln_g, ln_b)


RELAYOUT_TN = 512
SRC_I = 3 * A_WIDTH
SRC_SHIFT_A = 2 * A_HEADS
SRC_BG = COL_BZ + SRC_SHIFT_A
SRC_SHIFT_B = SRC_SHIFT_A + GATE_RANK
GATE_SRC_ROWS = 4 * SUBLANES


def _relayout_kernel(src_ref, gif_ref, gb0_ref, gb1_ref, o_ref):
    j = pl.program_id(0)
    tn = RELAYOUT_TN
    n_main = COL_GATE // tn

    @pl.when(j < n_main)
    def _():
        o_ref[...] = src_ref[...].astype(BF16)

    @pl.when(j == n_main)
    def _():
        g = jnp.concatenate([gif_ref[...], gb0_ref[...], gb1_ref[...],
                             jnp.zeros((SUBLANES, src_ref.shape[1]), F32)], axis=0).astype(BF16)
        r = lax.broadcasted_iota(jnp.int32, (tn, GATE_SRC_ROWS), 0)
        c = lax.broadcasted_iota(jnp.int32, (tn, GATE_SRC_ROWS), 1)
        h, l = r // LANES, r % LANES
        sel = (((l == 0) & (c == h)) | ((l == 1) & (c == A_HEADS + h))
               | ((l >= 2) & (l < 2 + GATE_RANK) & (c == l + (2 * A_HEADS - 2))))
        o_ref[...] = _dot(sel.astype(BF16), g).astype(BF16)


def _relayout_w_in(w_in_t):
    n_src, k = w_in_t.shape
    tn = RELAYOUT_TN

    def src_row(j):
        shift = jnp.where(j < COL_AO // tn, 0, jnp.where(j < COL_BZ // tn, SRC_SHIFT_A, SRC_SHIFT_B))
        group = jnp.minimum(j * (tn // SUBLANES) + shift // SUBLANES, (n_src - tn) // SUBLANES)
        return group * SUBLANES

    return pl.pallas_call(
        _relayout_kernel,
        grid=(N_PROJ // tn,),
        in_specs=[pl.BlockSpec((pl.Element(tn), pl.Element(k)), lambda j: (src_row(j), 0)),
                  pl.BlockSpec((SUBLANES, k), lambda j: (SRC_I // SUBLANES, 0)),
                  pl.BlockSpec((SUBLANES, k), lambda j: (SRC_BG // SUBLANES, 0)),
                  pl.BlockSpec((SUBLANES, k), lambda j: (SRC_BG // SUBLANES + 1, 0))],
        out_specs=pl.BlockSpec((tn, k), lambda j: (j, 0)),
        out_shape=jax.ShapeDtypeStruct((N_PROJ, k), BF16),
        compiler_params=pltpu.CompilerParams(
            dimension_semantics=("parallel",), vmem_limit_bytes=VMEM_LIMIT),
        name="relayout",
    )(w_in_t, w_in_t, w_in_t, w_in_t)


def kernel(x_prompt, x_sample, state_mlstm_C, state_mlstm_n, state_mlstm_m, state_conv, state_gla_S,
           w_in, conv_w, conv_b, b_i, b_f, a_norm_g, w_gate_up, b_gate, b_norm_g, w_pa, w_pb, w_out,
           ln_g, ln_b):
    batch, seq, _ = x_prompt.shape
    nseq = x_sample.shape[0]
    assert w_in.shape[0] == 1, "single-layer step"
    d = 0

    def layer(a):
        return a.reshape(a.shape[1:])

    wp = _relayout_w_in(layer(w_in).T)
    wpa, wpb, wo = w_pa[d].astype(BF16), w_pb[d].astype(BF16), w_out[d].astype(BF16)
    cw = conv_w[d]
    cb = conv_b[d][None, :]
    gate_bias = jnp.zeros((A_HEADS, LANES), F32).at[:, 0].set(b_i[d]).at[:, 1].set(b_f[d]).reshape(1, -1)
    ang = a_norm_g[d][None, :]
    bng = b_norm_g[d][None, :]
    wg = w_gate_up[d]
    bgate = b_gate[d][None, :]
    lng, lnb = ln_g[d][None, :], ln_b[d][None, :]

    xp = x_prompt.reshape(batch * seq, D_MODEL)
    p_all = _proj(xp, wp, tm=1024, tn=1280)
    ya, p_c, p_n, p_m = _mlstm(p_all, batch, seq, cw, cb, gate_bias, ang)
    yb, p_s = _gla(p_all, batch, seq, wg, bgate, bng)
    y_prompt = _out(ya, yb, p_all, xp, wpa, wpb, wo, lng, lnb, tm=256).reshape(batch, seq, D_MODEL)
    p_conv = p_all.reshape(batch, seq, N_PROJ)[:, seq - (CONV_W - 1):, :2 * A_WIDTH]

    xs = x_sample.reshape(nseq, D_MODEL)
    ps_all = _proj(xs, wp, tm=nseq, tn=1280)
    conv_state = state_conv.reshape(nseq, (CONV_W - 1) * 2 * A_WIDTH)
    ya_s, yb_s, s_c, s_n, s_m, s_s = _decode(
        ps_all, conv_state, cw, cb, gate_bias, ang, wg, bgate, bng,
        layer(state_mlstm_C), state_mlstm_n.reshape(nseq, A_HEADS, 1, A_DK), layer(state_mlstm_m),
        layer(state_gla_S))
    y_sample = _out(ya_s, yb_s, ps_all, xs, wpa, wpb, wo, lng, lnb, tm=nseq).reshape(nseq, 1, D_MODEL)
    s_conv = jnp.concatenate([layer(state_conv)[:, 1:, :], ps_all[:, None, :2 * A_WIDTH]], axis=1)

    def stacked(a):
        return a.reshape((1,) + a.shape)

    return (y_prompt, y_sample,
            stacked(p_c), p_n.reshape(1, batch, A_HEADS, A_DK), stacked(p_m[:, :, 0, 0]),
            stacked(p_conv), stacked(p_s),
            stacked(s_c), s_n.reshape(1, nseq, A_HEADS, A_DK),
            stacked(s_m.reshape(nseq, A_HEADS, LANES)[:, :, 0]), stacked(s_conv), stacked(s_s))
```

```python
import functools

import jax
import jax.numpy as jnp
from jax import lax
from jax.experimental import pallas as pl
from jax.experimental.pallas import tpu as pltpu

F32 = jnp.float32
BF16 = jnp.bfloat16

D_MODEL = 2048
A_HEADS = 4
A_WIDTH = 1024
A_DK = 256
A_DV = 256
CONV_W = 4
B_HEADS = 4
B_WIDTH = 1024
B_KWIDTH = 512
B_DK = 128
B_DV = 256
GATE_RANK = 16
GATE_TAU = 16.0
ALPHA = 2.0 ** 0.25
LN_EPS = 1e-5
LOG2_E = 1.4426950408889634

LANES = 128
SUBLANES = 8
VMEM_LIMIT = 48 * 1024 * 1024

COL_Q = 0
COL_K = A_WIDTH
COL_AV = 2 * A_WIDTH
COL_AO = 3 * A_WIDTH
COL_AZ = 4 * A_WIDTH
COL_BQ = 5 * A_WIDTH
COL_BK = COL_BQ + B_KWIDTH
COL_BV = COL_BK + B_KWIDTH
COL_BZ = COL_BV + B_WIDTH
COL_GA = COL_BZ + B_WIDTH
COL_GB = COL_GA + D_MODEL
COL_GATE = COL_GB + D_MODEL
N_PROJ = COL_GATE + A_HEADS * LANES

MLSTM_CHUNK = 256
MLSTM_HEADS_PER_STEP = 4
GLA_STEP = 256
GLA_BLOCK = 16
GLA_HEADS_PER_STEP = 4


def _dot(a, b):
    return jnp.dot(a, b, preferred_element_type=F32)


def _dot_nt(a, b):
    return lax.dot_general(a, b, (((1,), (1,)), ((), ())), preferred_element_type=F32)


def _dot_tn(a, b):
    return lax.dot_general(a, b, (((0,), (0,)), ((), ())), preferred_element_type=F32)


def _mask_dot(mask_bf16, x):
    hi = x.astype(BF16)
    r1 = x - hi.astype(F32)
    mid = r1.astype(BF16)
    lo = (r1 - mid.astype(F32)).astype(BF16)
    return _dot(mask_bf16, hi) + _dot(mask_bf16, mid) + _dot(mask_bf16, lo)


def _log_sigmoid(z):
    return jnp.minimum(z, 0.0) - jnp.log(1.0 + jnp.exp(-jnp.abs(z)))


def _sigmoid(z):
    return 1.0 / (1.0 + jnp.exp(-z))


def _silu(z):
    return z * _sigmoid(z)


def _proj_kernel(x_ref, w_ref, o_ref, xb_ref):
    @pl.when(pl.program_id(1) == 0)
    def _():
        xb_ref[...] = x_ref[...].astype(BF16)

    o_ref[...] = _dot_nt(xb_ref[...], w_ref[...])


def _proj(x, w_t, tm, tn):
    t, k = x.shape
    n = w_t.shape[0]
    return pl.pallas_call(
        _proj_kernel,
        grid=(t // tm, n // tn),
        in_specs=[pl.BlockSpec((tm, k), lambda i, j: (i, 0)),
                  pl.BlockSpec((tn, k), lambda i, j: (j, 0))],
        out_specs=pl.BlockSpec((tm, tn), lambda i, j: (i, j)),
        out_shape=jax.ShapeDtypeStruct((t, n), F32),
        scratch_shapes=[pltpu.VMEM((tm, k), BF16)],
        compiler_params=pltpu.CompilerParams(
            dimension_semantics=("parallel", "arbitrary"), vmem_limit_bytes=VMEM_LIMIT),
        name="proj",
    )(x, w_t)


def _mlstm_kernel(qp_ref, kp_ref, v_ref, ao_ref, az_ref, g_ref, cwq_ref, cwk_ref, cbq_ref, cbk_ref,
                  gb_ref, ng_ref, ya_ref, c_ref, n_ref, m_ref, tailq_ref, tailk_ref):
    L = MLSTM_CHUNK
    c = pl.program_id(2)

    @pl.when(c == 0)
    def _():
        c_ref[...] = jnp.zeros_like(c_ref)
        n_ref[...] = jnp.zeros_like(n_ref)
        m_ref[...] = jnp.zeros_like(m_ref)
        tailq_ref[...] = jnp.zeros_like(tailq_ref)
        tailk_ref[...] = jnp.zeros_like(tailk_ref)

    row8 = lax.broadcasted_iota(jnp.int32, (SUBLANES, 1), 0)

    def conv_silu(x_ref, tail_ref, w_ref, b_ref):
        x = x_ref[...]
        tail = tail_ref[...]
        acc = b_ref[...] + x * w_ref[CONV_W - 1:CONV_W, :]
        for j in range(1, CONV_W):
            xs = pltpu.roll(x, j, axis=0)
            head = jnp.where(row8 < j, pltpu.roll(tail, j, axis=0), xs[0:SUBLANES, :])
            xs = jnp.concatenate([head, xs[SUBLANES:, :]], axis=0)
            acc = acc + xs * w_ref[CONV_W - 1 - j:CONV_W - j, :]
        tail_ref[...] = x[L - SUBLANES:, :]
        return _silu(acc)

    q_all = conv_silu(qp_ref, tailq_ref, cwq_ref, cbq_ref)
    k_all = conv_silu(kp_ref, tailk_ref, cwk_ref, cbk_ref) * (A_DK ** -0.5)

    lane = lax.broadcasted_iota(jnp.int32, (L, LANES), 1)
    row = lax.broadcasted_iota(jnp.int32, (L, L), 0)
    col = lax.broadcasted_iota(jnp.int32, (L, L), 1)
    causal = row >= col
    causal_b = causal.astype(BF16)

    for hh in range(MLSTM_HEADS_PER_STEP):
        cols = slice(hh * A_DK, (hh + 1) * A_DK)
        q, k, v = q_all[:, cols], k_all[:, cols], v_ref[:, cols]

        g = g_ref[:, hh * LANES:(hh + 1) * LANES] + gb_ref[:, hh * LANES:(hh + 1) * LANES]
        g2 = jnp.where(lane == 1, _log_sigmoid(g), g)
        cum = _mask_dot(causal_b, g2)
        x2 = jnp.where(lane == 1, cum, g2)
        x2t = x2.T
        itil_col, b_col = x2[:, 0:1], x2[:, 1:2]
        itil_row, b_row = x2t[0:1, :], x2t[1:2, :]

        m_prev = m_ref[0, hh, 0:1, 0:1]
        dmat = jnp.where(causal, b_col - b_row + itil_row, -jnp.inf)
        inter = b_col + m_prev
        m_t = jnp.maximum(inter, jnp.max(dmat, axis=1, keepdims=True))
        w = jnp.exp(dmat - m_t)
        decay = jnp.exp(inter - m_t)

        qb, kb, vb = q.astype(BF16), k.astype(BF16), v.astype(BF16)
        c_old = c_ref[0, hh]
        n_old = n_ref[0, hh]
        s = _dot_nt(qb, kb) * w
        num = decay * _dot(qb, c_old.astype(BF16)) + _dot(s.astype(BF16), vb)
        den = decay * jnp.sum(q * n_old, axis=1, keepdims=True) + jnp.sum(s, axis=1, keepdims=True)
        h = num / jnp.maximum(jnp.abs(den), jnp.exp(-m_t))

        m_new = m_t[L - 1:L, :]
        b_last = b_col[L - 1:L, :]
        wk = jnp.exp(b_last - b_col + itil_col - m_new)
        dec = jnp.exp(b_last + m_prev - m_new)
        kw = k * wk
        c_ref[0, hh] = dec * c_old + _dot(kw.T.astype(BF16), vb)
        n_ref[0, hh] = dec * n_old + jnp.sum(kw, axis=0, keepdims=True)
        m_ref[0, hh] = jnp.broadcast_to(m_new, (1, LANES))

        mu = jnp.mean(h, axis=1, keepdims=True)
        hc = h - mu
        var = jnp.mean(hc * hc, axis=1, keepdims=True)
        hn = hc * lax.rsqrt(var + LN_EPS) * ng_ref[:, cols]
        ya_ref[:, cols] = (hn * _sigmoid(ao_ref[:, cols]) * _silu(az_ref[:, cols])).astype(ya_ref.dtype)


def _mlstm(p_all, batch, seq, conv_w, conv_b, gate_bias, a_norm_g):
    L = MLSTM_CHUNK
    hps = MLSTM_HEADS_PER_STEP
    nc = seq // L
    t = batch * seq
    wd = hps * A_DK
    k_off = A_WIDTH // wd

    def rows(b, h, c):
        return b * nc + c

    def pcol(base):
        return pl.BlockSpec((L, wd), lambda b, h, c: (rows(b, h, c), base // wd + h))

    in_specs = [
        pcol(COL_Q), pcol(COL_K), pcol(COL_AV), pcol(COL_AO), pcol(COL_AZ),
        pl.BlockSpec((L, hps * LANES), lambda b, h, c: (rows(b, h, c), COL_GATE // (hps * LANES) + h)),
        pl.BlockSpec((CONV_W, wd), lambda b, h, c: (0, h)),
        pl.BlockSpec((CONV_W, wd), lambda b, h, c: (0, k_off + h)),
        pl.BlockSpec((1, wd), lambda b, h, c: (0, h)),
        pl.BlockSpec((1, wd), lambda b, h, c: (0, k_off + h)),
        pl.BlockSpec((1, hps * LANES), lambda b, h, c: (0, h)),
        pl.BlockSpec((1, wd), lambda b, h, c: (0, h)),
    ]
    out_specs = [
        pl.BlockSpec((L, wd), lambda b, h, c: (rows(b, h, c), h)),
        pl.BlockSpec((1, hps, A_DK, A_DV), lambda b, h, c: (b, h, 0, 0)),
        pl.BlockSpec((1, hps, 1, A_DK), lambda b, h, c: (b, h, 0, 0)),
        pl.BlockSpec((1, hps, 1, LANES), lambda b, h, c: (b, h, 0, 0)),
    ]
    out_shape = [
        jax.ShapeDtypeStruct((t, A_WIDTH), BF16),
        jax.ShapeDtypeStruct((batch, A_HEADS, A_DK, A_DV), F32),
        jax.ShapeDtypeStruct((batch, A_HEADS, 1, A_DK), F32),
        jax.ShapeDtypeStruct((batch, A_HEADS, 1, LANES), F32),
    ]
    return pl.pallas_call(
        _mlstm_kernel,
        grid=(batch, A_HEADS // hps, nc),
        in_specs=in_specs,
        out_specs=out_specs,
        out_shape=out_shape,
        scratch_shapes=[pltpu.VMEM((SUBLANES, wd), F32),
                        pltpu.VMEM((SUBLANES, wd), F32)],
        compiler_params=pltpu.CompilerParams(
            dimension_semantics=("parallel", "parallel", "arbitrary"), vmem_limit_bytes=VMEM_LIMIT),
        name="mlstm",
    )(p_all, p_all, p_all, p_all, p_all, p_all, conv_w, conv_w, conv_b, conv_b, gate_bias, a_norm_g)


def _gla_gate_log(bg_tile, wg, bgate):
    lr = bg_tile[:, 2:2 + GATE_RANK]
    z = _dot(lr.astype(BF16), wg.astype(BF16)) + bgate
    return _log_sigmoid(z) / GATE_TAU


def _gla_kernel(q_ref, k_ref, v_ref, bz_ref, g_ref, wg_ref, bgate_ref, ng_ref,
                yb_ref, s_out_ref, st_ref, ds_ref, sb_ref):
    L = GLA_STEP
    B = GLA_BLOCK
    nb = L // B
    c = pl.program_id(2)

    @pl.when(c == 0)
    def _():
        st_ref[...] = jnp.zeros_like(st_ref)

    row = lax.broadcasted_iota(jnp.int32, (L, L), 0)
    col = lax.broadcasted_iota(jnp.int32, (L, L), 1)
    same = (row // B) == (col // B)
    tri_b = (same & (row >= col)).astype(BF16)
    same_b = same.astype(BF16)
    t_in = lax.broadcasted_iota(jnp.int32, (L, 1), 0) % B
    s_lane = lax.broadcasted_iota(jnp.int32, (1, LANES), 1)

    for hh in range(GLA_HEADS_PER_STEP):
        kcols = slice(hh * B_DK, (hh + 1) * B_DK)
        vcols = slice(hh * B_DV, (hh + 1) * B_DV)
        loga = _gla_gate_log(g_ref[:, hh * LANES:(hh + 1) * LANES], wg_ref[:, kcols], bgate_ref[:, kcols])
        bc = _mask_dot(tri_b, loga)
        bl = _mask_dot(same_b, loga)
        k = k_ref[:, kcols]
        qs = q_ref[:, kcols] * (B_DK ** -0.5)
        qt = (qs * jnp.exp(bc)).astype(BF16)
        kt = (k * jnp.exp(bl - bc)).astype(BF16)
        eb = jnp.exp(bl)
        vb = v_ref[:, vcols].astype(BF16)

        bc3, q3, k3 = (x.reshape(nb, B, B_DK) for x in (bc * LOG2_E, qs, k))
        groups = B // SUBLANES
        bcg = [bc3[:, g * SUBLANES:(g + 1) * SUBLANES, :] for g in range(groups)]
        qg = [q3[:, g * SUBLANES:(g + 1) * SUBLANES, :] for g in range(groups)]
        ag = [jnp.zeros((nb, SUBLANES, LANES), F32) for _ in range(groups)]
        for s in range(B):
            bc_s, k_s = bc3[:, s:s + 1, :], k3[:, s:s + 1, :]
            for g in range(s // SUBLANES, groups):
                e = jnp.exp2(jnp.minimum(bcg[g] - bc_s, 0.0))
                a_col = jnp.sum(qg[g] * k_s * e, axis=2, keepdims=True)
                ag[g] = jnp.where(s_lane == s, a_col, ag[g])
        a = jnp.concatenate(ag, axis=1).reshape(L, LANES)
        ab = jnp.where(t_in >= s_lane, a, 0.0)[:, 0:B].astype(BF16)

        for j in range(nb):
            ds_ref[hh, j] = _dot_tn(vb[j * B:(j + 1) * B, :], kt[j * B:(j + 1) * B, :])
        st = st_ref[hh]
        for j in range(nb):
            sb_ref[hh, j] = st.astype(BF16)
            st = st * eb[j * B:j * B + 1, :] + ds_ref[hh, j]
        st_ref[hh] = st
        o = jnp.concatenate(
            [_dot(ab[j * B:(j + 1) * B, :], vb[j * B:(j + 1) * B, :])
             + _dot_nt(qt[j * B:(j + 1) * B, :], sb_ref[hh, j]) for j in range(nb)], axis=0)

        on = o * lax.rsqrt(jnp.mean(o * o, axis=1, keepdims=True) + LN_EPS) * ng_ref[:, vcols]
        yb_ref[:, vcols] = (on * _silu(bz_ref[:, vcols])).astype(yb_ref.dtype)

    @pl.when(c == pl.num_programs(2) - 1)
    def _():
        for hh in range(GLA_HEADS_PER_STEP):
            s_out_ref[0, hh] = st_ref[hh].T


def _gla(p_all, batch, seq, w_gate_up, b_gate, b_norm_g):
    L = GLA_STEP
    hps = GLA_HEADS_PER_STEP
    nc = seq // L
    t = batch * seq
    kw, vw = hps * B_DK, hps * B_DV

    def rows(b, h, c):
        return b * nc + c

    in_specs = [
        pl.BlockSpec((L, kw), lambda b, h, c: (rows(b, h, c), COL_BQ // kw + h)),
        pl.BlockSpec((L, kw), lambda b, h, c: (rows(b, h, c), COL_BK // kw + h)),
        pl.BlockSpec((L, vw), lambda b, h, c: (rows(b, h, c), COL_BV // vw + h)),
        pl.BlockSpec((L, vw), lambda b, h, c: (rows(b, h, c), COL_BZ // vw + h)),
        pl.BlockSpec((L, hps * LANES), lambda b, h, c: (rows(b, h, c), COL_GATE // (hps * LANES) + h)),
        pl.BlockSpec((GATE_RANK, kw), lambda b, h, c: (0, h)),
        pl.BlockSpec((1, kw), lambda b, h, c: (0, h)),
        pl.BlockSpec((1, vw), lambda b, h, c: (0, h)),
    ]
    out_specs = [
        pl.BlockSpec((L, vw), lambda b, h, c: (rows(b, h, c), h)),
        pl.BlockSpec((1, hps, B_DK, B_DV), lambda b, h, c: (b, h, 0, 0)),
    ]
    out_shape = [
        jax.ShapeDtypeStruct((t, B_WIDTH), BF16),
        jax.ShapeDtypeStruct((batch, B_HEADS, B_DK, B_DV), F32),
    ]
    return pl.pallas_call(
        _gla_kernel,
        grid=(batch, B_HEADS // hps, nc),
        in_specs=in_specs,
        out_specs=out_specs,
        out_shape=out_shape,
        scratch_shapes=[pltpu.VMEM((hps, B_DV, B_DK), F32),
                        pltpu.VMEM((hps, L // GLA_BLOCK, B_DV, B_DK), F32),
                        pltpu.VMEM((hps, L // GLA_BLOCK, B_DV, B_DK), BF16)],
        compiler_params=pltpu.CompilerParams(
            dimension_semantics=("parallel", "parallel", "arbitrary"), vmem_limit_bytes=VMEM_LIMIT),
        name="gla",
    )(p_all, p_all, p_all, p_all, p_all, w_gate_up, b_gate, b_norm_g)


DEC_TOKENS = 16


def _decode_kernel(qp_ref, kp_ref, av_ref, ao_ref, az_ref, bq_ref, bk_ref, bv_ref, bz_ref, g_ref,
                   sq0_ref, sq1_ref, sq2_ref, sk0_ref, sk1_ref, sk2_ref,
                   cwq_ref, cwk_ref, cbq_ref, cbk_ref, gb_ref, ang_ref, wg_ref, bgate_ref, bng_ref,
                   c_ref, n_ref, m_ref, s_ref,
                   ya_ref, yb_ref, c_out_ref, n_out_ref, m_out_ref, s_out_ref):
    TB = DEC_TOKENS
    h_idx = pl.program_id(1)

    def conv_silu(s0, s1, s2, x, w_ref, b_ref):
        acc = b_ref[...] + s0[...] * w_ref[0:1, :]
        acc = acc + s1[...] * w_ref[1:2, :]
        acc = acc + s2[...] * w_ref[2:3, :]
        acc = acc + x[...] * w_ref[3:4, :]
        return _silu(acc)

    q = conv_silu(sq0_ref, sq1_ref, sq2_ref, qp_ref, cwq_ref, cbq_ref)
    k = conv_silu(sk0_ref, sk1_ref, sk2_ref, kp_ref, cwk_ref, cbk_ref) * (A_DK ** -0.5)
    v = av_ref[...]
    g = g_ref[...]
    gbias = gb_ref[...]
    itil = g[:, 0:1] + gbias[:, 0:1]
    logf = _log_sigmoid(g[:, 1:2] + gbias[:, 1:2])
    lane_h = lax.broadcasted_iota(jnp.int32, (TB, A_HEADS), 1)
    m_prev = jnp.sum(jnp.where(lane_h == h_idx, m_ref[...], 0.0), axis=1, keepdims=True)
    inter = logf + m_prev
    m_t = jnp.maximum(inter, itil)
    w = jnp.exp(itil - m_t)
    decay = jnp.exp(inter - m_t)
    n_old = n_ref[:, 0, 0, :]
    s = jnp.sum(q * k, axis=1, keepdims=True) * w
    den = decay * jnp.sum(q * n_old, axis=1, keepdims=True) + s
    scale = 1.0 / jnp.maximum(jnp.abs(den), jnp.exp(-m_t))
    kw = k * w
    n_out_ref[:, 0, 0, :] = decay * n_old + kw
    m_out_ref[...] = jnp.broadcast_to(m_t, (TB, LANES))

    rows8 = lax.broadcasted_iota(jnp.int32, (TB, 1), 0)
    qb, kwb, vb = q.astype(BF16), kw.astype(BF16), v.astype(BF16)
    h_rows = []
    for t in range(TB):
        c_old = c_ref[t, 0]
        sel = rows8 == t
        qc = _dot(qb, c_old.astype(BF16))[t:t + 1, :]
        h_rows.append((decay[t:t + 1, :] * qc + s[t:t + 1, :] * v[t:t + 1, :]) * scale[t:t + 1, :])
        outer = _dot_tn(jnp.where(sel, kwb, jnp.zeros_like(kwb)), vb)
        c_out_ref[t, 0] = decay[t:t + 1, :] * c_old + outer
    h = jnp.concatenate(h_rows, axis=0)
    mu = jnp.mean(h, axis=1, keepdims=True)
    hc = h - mu
    var = jnp.mean(hc * hc, axis=1, keepdims=True)
    hn = hc * lax.rsqrt(var + LN_EPS) * ang_ref[...]
    ya_ref[...] = (hn * _sigmoid(ao_ref[...]) * _silu(az_ref[...])).astype(ya_ref.dtype)

    gq = bq_ref[...] * (B_DK ** -0.5)
    gk = bk_ref[...]
    gv = bv_ref[...]
    loga = _gla_gate_log(g, wg_ref[...], bgate_ref[...])
    eb = jnp.exp(loga)
    a = jnp.sum(gq * gk, axis=1, keepdims=True)
    ebt = jnp.concatenate([eb, jnp.zeros((LANES - TB, B_DK), F32)], axis=0).T
    qeb, gkb, gvb = (gq * eb).astype(BF16), gk.astype(BF16), gv.astype(BF16)
    o_rows = []
    for t in range(TB):
        s_old = s_ref[t, 0]
        sel = rows8 == t
        o_rows.append(_dot(qeb, s_old.astype(BF16))[t:t + 1, :] + a[t:t + 1, :] * gv[t:t + 1, :])
        outer = _dot_tn(jnp.where(sel, gkb, jnp.zeros_like(gkb)), gvb)
        s_out_ref[t, 0] = ebt[:, t:t + 1] * s_old + outer
    o = jnp.concatenate(o_rows, axis=0)
    on = o * lax.rsqrt(jnp.mean(o * o, axis=1, keepdims=True) + LN_EPS) * bng_ref[...]
    yb_ref[...] = (on * _silu(bz_ref[...])).astype(yb_ref.dtype)


def _decode(p_s, conv_state, conv_w, conv_b, gate_bias, a_norm_g, w_gate_up, b_gate, b_norm_g,
            c_state, n_state, m_state, s_state):
    nseq = p_s.shape[0]
    TB = DEC_TOKENS

    def pcol(base, width):
        return pl.BlockSpec((TB, width), lambda i, h: (i, base // width + h))

    def cstate(j, base):
        return pl.BlockSpec((TB, A_DK), lambda i, h: (i, (j * 2 * A_WIDTH + base) // A_DK + h))

    in_specs = [
        pcol(COL_Q, A_DK), pcol(COL_K, A_DK), pcol(COL_AV, A_DV), pcol(COL_AO, A_DV), pcol(COL_AZ, A_DV),
        pcol(COL_BQ, B_DK), pcol(COL_BK, B_DK), pcol(COL_BV, B_DV), pcol(COL_BZ, B_DV),
        pcol(COL_GATE, LANES),
        cstate(0, COL_Q), cstate(1, COL_Q), cstate(2, COL_Q),
        cstate(0, COL_K), cstate(1, COL_K), cstate(2, COL_K),
        pl.BlockSpec((CONV_W, A_DK), lambda i, h: (0, h)),
        pl.BlockSpec((CONV_W, A_DK), lambda i, h: (0, A_HEADS + h)),
        pl.BlockSpec((1, A_DK), lambda i, h: (0, h)),
        pl.BlockSpec((1, A_DK), lambda i, h: (0, A_HEADS + h)),
        pl.BlockSpec((1, LANES), lambda i, h: (0, h)),
        pl.BlockSpec((1, A_DV), lambda i, h: (0, h)),
        pl.BlockSpec((GATE_RANK, B_DK), lambda i, h: (0, h)),
        pl.BlockSpec((1, B_DK), lambda i, h: (0, h)),
        pl.BlockSpec((1, B_DV), lambda i, h: (0, h)),
        pl.BlockSpec((TB, 1, A_DK, A_DV), lambda i, h: (i, h, 0, 0)),
        pl.BlockSpec((TB, 1, 1, A_DK), lambda i, h: (i, h, 0, 0)),
        pl.BlockSpec((TB, A_HEADS), lambda i, h: (i, 0)),
        pl.BlockSpec((TB, 1, B_DK, B_DV), lambda i, h: (i, h, 0, 0)),
    ]
    out_specs = [
        pl.BlockSpec((TB, A_DV), lambda i, h: (i, h)),
        pl.BlockSpec((TB, B_DV), lambda i, h: (i, h)),
        pl.BlockSpec((TB, 1, A_DK, A_DV), lambda i, h: (i, h, 0, 0)),
        pl.BlockSpec((TB, 1, 1, A_DK), lambda i, h: (i, h, 0, 0)),
        pl.BlockSpec((TB, LANES), lambda i, h: (i, h)),
        pl.BlockSpec((TB, 1, B_DK, B_DV), lambda i, h: (i, h, 0, 0)),
    ]
    out_shape = [
        jax.ShapeDtypeStruct((nseq, A_WIDTH), BF16),
        jax.ShapeDtypeStruct((nseq, B_WIDTH), BF16),
        jax.ShapeDtypeStruct((nseq, A_HEADS, A_DK, A_DV), F32),
        jax.ShapeDtypeStruct((nseq, A_HEADS, 1, A_DK), F32),
        jax.ShapeDtypeStruct((nseq, A_HEADS * LANES), F32),
        jax.ShapeDtypeStruct((nseq, B_HEADS, B_DK, B_DV), F32),
    ]
    return pl.pallas_call(
        _decode_kernel,
        grid=(nseq // TB, A_HEADS),
        in_specs=in_specs,
        out_specs=out_specs,
        out_shape=out_shape,
        compiler_params=pltpu.CompilerParams(
            dimension_semantics=("parallel", "parallel"), vmem_limit_bytes=VMEM_LIMIT),
        name="decode",
    )(p_s, p_s, p_s, p_s, p_s, p_s, p_s, p_s, p_s, p_s,
      conv_state, conv_state, conv_state, conv_state, conv_state, conv_state,
      conv_w, conv_w, conv_b, conv_b, gate_bias, a_norm_g, w_gate_up, b_gate, b_norm_g,
      c_state, n_state, m_state, s_state)


def _out_kernel(ya_ref, yb_ref, ga_ref, gb_ref, x_ref, wpa_ref, wpb_ref, wo_ref, lng_ref, lnb_ref, o_ref):
    pa = _dot(ya_ref[...], wpa_ref[...])
    pb = _dot(yb_ref[...], wpb_ref[...])
    merged = _sigmoid(ga_ref[...]) * pa + _sigmoid(gb_ref[...]) * pb
    y = _dot(merged.astype(BF16), wo_ref[...])
    r = ALPHA * x_ref[...] + y
    mu = jnp.mean(r, axis=1, keepdims=True)
    rc = r - mu
    var = jnp.mean(rc * rc, axis=1, keepdims=True)
    o_ref[...] = rc * lax.rsqrt(var + LN_EPS) * lng_ref[...] + lnb_ref[...]


def _out(ya, yb, p_all, x, w_pa, w_pb, w_out, ln_g, ln_b, tm):
    t = x.shape[0]
    const = lambda i: (0, 0)
    single = pl.Buffered(1)
    in_specs = [
        pl.BlockSpec((tm, A_WIDTH), lambda i: (i, 0)),
        pl.BlockSpec((tm, B_WIDTH), lambda i: (i, 0)),
        pl.BlockSpec((tm, D_MODEL), lambda i: (i, COL_GA // D_MODEL)),
        pl.BlockSpec((tm, D_MODEL), lambda i: (i, COL_GB // D_MODEL)),
        pl.BlockSpec((tm, D_MODEL), lambda i: (i, 0)),
        pl.BlockSpec((A_WIDTH, D_MODEL), const, pipeline_mode=single),
        pl.BlockSpec((B_WIDTH, D_MODEL), const, pipeline_mode=single),
        pl.BlockSpec((D_MODEL, D_MODEL), const, pipeline_mode=single),
        pl.BlockSpec((1, D_MODEL), const),
        pl.BlockSpec((1, D_MODEL), const),
    ]
    return pl.pallas_call(
        _out_kernel,
        grid=(t // tm,),
        in_specs=in_specs,
        out_specs=pl.BlockSpec((tm, D_MODEL), lambda i: (i, 0)),
        out_shape=jax.ShapeDtypeStruct((t, D_MODEL), F32),
        compiler_params=pltpu.CompilerParams(
            dimension_semantics=("parallel",), vmem_limit_bytes=VMEM_LIMIT),
        name="outproj",
    )(ya, yb, p_all, p_all, x, w_pa, w_pb, w_out, ln_g, ln_b)


RELAYOUT_TN = 512
SRC_I = 3 * A_WIDTH
SRC_SHIFT_A = 2 * A_HEADS
SRC_BG = COL_BZ + SRC_SHIFT_A
SRC_SHIFT_B = SRC_SHIFT_A + GATE_RANK
GATE_SRC_ROWS = 4 * SUBLANES


def _relayout_kernel(src_ref, gif_ref, gb0_ref, gb1_ref, o_ref):
    j = pl.program_id(0)
    tn = RELAYOUT_TN
    n_main = COL_GATE // tn

    @pl.when(j < n_main)
    def _():
        o_ref[...] = src_ref[...].astype(BF16)

    @pl.when(j == n_main)
    def _():
        g = jnp.concatenate([gif_ref[...], gb0_ref[...], gb1_ref[...],
                             jnp.zeros((SUBLANES, src_ref.shape[1]), F32)], axis=0).astype(BF16)
        r = lax.broadcasted_iota(jnp.int32, (tn, GATE_SRC_ROWS), 0)
        c = lax.broadcasted_iota(jnp.int32, (tn, GATE_SRC_ROWS), 1)
        h, l = r // LANES, r % LANES
        sel = (((l == 0) & (c == h)) | ((l == 1) & (c == A_HEADS + h))
               | ((l >= 2) & (l < 2 + GATE_RANK) & (c == l + (2 * A_HEADS - 2))))
        o_ref[...] = _dot(sel.astype(BF16), g).astype(BF16)


def _relayout_w_in(w_in_t):
    n_src, k = w_in_t.shape
    tn = RELAYOUT_TN

    def src_row(j):
        shift = jnp.where(j < COL_AO // tn, 0, jnp.where(j < COL_BZ // tn, SRC_SHIFT_A, SRC_SHIFT_B))
        group = jnp.minimum(j * (tn // SUBLANES) + shift // SUBLANES, (n_src - tn) // SUBLANES)
        return group * SUBLANES

    return pl.pallas_call(
        _relayout_kernel,
        grid=(N_PROJ // tn,),
        in_specs=[pl.BlockSpec((pl.Element(tn), pl.Element(k)), lambda j: (src_row(j), 0)),
                  pl.BlockSpec((SUBLANES, k), lambda j: (SRC_I // SUBLANES, 0)),
                  pl.BlockSpec((SUBLANES, k), lambda j: (SRC_BG // SUBLANES, 0)),
                  pl.BlockSpec((SUBLANES, k), lambda j: (SRC_BG // SUBLANES + 1, 0))],
        out_specs=pl.BlockSpec((tn, k), lambda j: (j, 0)),
        out_shape=jax.ShapeDtypeStruct((N_PROJ, k), BF16),
        compiler_params=pltpu.CompilerParams(
            dimension_semantics=("parallel",), vmem_limit_bytes=VMEM_LIMIT),
        name="relayout",
    )(w_in_t, w_in_t, w_in_t, w_in_t)


def kernel(x_prompt, x_sample, state_mlstm_C, state_mlstm_n, state_mlstm_m, state_conv, state_gla_S,
           w_in, conv_w, conv_b, b_i, b_f, a_norm_g, w_gate_up, b_gate, b_norm_g, w_pa, w_pb, w_out,
           ln_g, ln_b):
    batch, seq, _ = x_prompt.shape
    nseq = x_sample.shape[0]
    assert w_in.shape[0] == 1, "single-layer step"
    d = 0

    def layer(a):
        return a.reshape(a.shape[1:])

    wp = _relayout_w_in(layer(w_in).T)
    wpa, wpb, wo = w_pa[d].astype(BF16), w_pb[d].astype(BF16), w_out[d].astype(BF16)
    cw = conv_w[d]
    cb = conv_b[d][None, :]
    gate_bias = jnp.zeros((A_HEADS, LANES), F32).at[:, 0].set(b_i[d]).at[:, 1].set(b_f[d]).reshape(1, -1)
    ang = a_norm_g[d][None, :]
    bng = b_norm_g[d][None, :]
    wg = w_gate_up[d]
    bgate = b_gate[d][None, :]
    lng, lnb = ln_g[d][None, :], ln_b[d][None, :]

    xp = x_prompt.reshape(batch * seq, D_MODEL)
    p_all = _proj(xp, wp, tm=1024, tn=1280)
    ya, p_c, p_n, p_m = _mlstm(p_all, batch, seq, cw, cb, gate_bias, ang)
    yb, p_s = _gla(p_all, batch, seq, wg, bgate, bng)
    y_prompt = _out(ya, yb, p_all, xp, wpa, wpb, wo, lng, lnb, tm=256).reshape(batch, seq, D_MODEL)
    p_conv = p_all.reshape(batch, seq, N_PROJ)[:, seq - (CONV_W - 1):, :2 * A_WIDTH]

    xs = x_sample.reshape(nseq, D_MODEL)
    ps_all = _proj(xs, wp, tm=nseq, tn=1280)
    conv_state = state_conv.reshape(nseq, (CONV_W - 1) * 2 * A_WIDTH)
    ya_s, yb_s, s_c, s_n, s_m, s_s = _decode(
        ps_all, conv_state, cw, cb, gate_bias, ang, wg, bgate, bng,
        layer(state_mlstm_C), state_mlstm_n.reshape(nseq, A_HEADS, 1, A_DK), layer(state_mlstm_m),
        layer(state_gla_S))
    y_sample = _out(ya_s, yb_s, ps_all, xs, wpa, wpb, wo, lng, lnb, tm=nseq).reshape(nseq, 1, D_MODEL)
    s_conv = jnp.concatenate([layer(state_conv)[:, 1:, :], ps_all[:, None, :2 * A_WIDTH]], axis=1)

    def stacked(a):
        return a.reshape((1,) + a.shape)

    return (y_prompt, y_sample,
            stacked(p_c), p_n.reshape(1, batch, A_HEADS, A_DK), stacked(p_m[:, :, 0, 0]),
            stacked(p_conv), stacked(p_s),
            stacked(s_c), s_n.reshape(1, nseq, A_HEADS, A_DK),
            stacked(s_m.reshape(nseq, A_HEADS, LANES)[:, :, 0]), stacked(s_conv), stacked(s_s))
```

```python
import functools

import jax
import jax.numpy as jnp
from jax import lax
from jax.experimental import pallas as pl
from jax.experimental.pallas import tpu as pltpu

F32 = jnp.float32
BF16 = jnp.bfloat16

D_MODEL = 2048
A_HEADS = 4
A_WIDTH = 1024
A_DK = 256
A_DV = 256
CONV_W = 4
B_HEADS = 4
B_WIDTH = 1024
B_KWIDTH = 512
B_DK = 128
B_DV = 256
GATE_RANK = 16
GATE_TAU = 16.0
ALPHA = 2.0 ** 0.25
LN_EPS = 1e-5
LOG2_E = 1.4426950408889634

LANES = 128
SUBLANES = 8
VMEM_LIMIT = 48 * 1024 * 1024

COL_Q = 0
COL_K = A_WIDTH
COL_AV = 2 * A_WIDTH
COL_AO = 3 * A_WIDTH
COL_AZ = 4 * A_WIDTH
COL_BQ = 5 * A_WIDTH
COL_BK = COL_BQ + B_KWIDTH
COL_BV = COL_BK + B_KWIDTH
COL_BZ = COL_BV + B_WIDTH
COL_GA = COL_BZ + B_WIDTH
COL_GB = COL_GA + D_MODEL
COL_GATE = COL_GB + D_MODEL
N_PROJ = COL_GATE + A_HEADS * LANES

MLSTM_CHUNK = 256
MLSTM_HEADS_PER_STEP = 4
GLA_STEP = 256
GLA_BLOCK = 16
GLA_HEADS_PER_STEP = 4


def _dot(a, b):
    return jnp.dot(a, b, preferred_element_type=F32)


def _dot_nt(a, b):
    return lax.dot_general(a, b, (((1,), (1,)), ((), ())), preferred_element_type=F32)


def _dot_tn(a, b):
    return lax.dot_general(a, b, (((0,), (0,)), ((), ())), preferred_element_type=F32)


def _mask_dot(mask_bf16, x):
    hi = x.astype(BF16)
    r1 = x - hi.astype(F32)
    mid = r1.astype(BF16)
    lo = (r1 - mid.astype(F32)).astype(BF16)
    return _dot(mask_bf16, hi) + _dot(mask_bf16, mid) + _dot(mask_bf16, lo)


def _log_sigmoid(z):
    return jnp.minimum(z, 0.0) - jnp.log(1.0 + jnp.exp(-jnp.abs(z)))


def _sigmoid(z):
    return 1.0 / (1.0 + jnp.exp(-z))


def _silu(z):
    return z * _sigmoid(z)


def _proj_kernel(x_ref, w_ref, o_ref, xb_ref):
    @pl.when(pl.program_id(1) == 0)
    def _():
        xb_ref[...] = x_ref[...].astype(BF16)

    o_ref[...] = _dot_nt(xb_ref[...], w_ref[...])


def _proj(x, w_t, tm, tn):
    t, k = x.shape
    n = w_t.shape[0]
    return pl.pallas_call(
        _proj_kernel,
        grid=(t // tm, n // tn),
        in_specs=[pl.BlockSpec((tm, k), lambda i, j: (i, 0)),
                  pl.BlockSpec((tn, k), lambda i, j: (j, 0))],
        out_specs=pl.BlockSpec((tm, tn), lambda i, j: (i, j)),
        out_shape=jax.ShapeDtypeStruct((t, n), F32),
        scratch_shapes=[pltpu.VMEM((tm, k), BF16)],
        compiler_params=pltpu.CompilerParams(
            dimension_semantics=("parallel", "arbitrary"), vmem_limit_bytes=VMEM_LIMIT),
        name="proj",
    )(x, w_t)


def _mlstm_heads(qp_ref, kp_ref, v_ref, ao_ref, az_ref, g_ref, cwq_ref, cwk_ref, cbq_ref, cbk_ref,
                 gb_ref, ng_ref, c_ref, n_ref, m_ref, tailq_ref, tailk_ref, live, after_head):
    L = MLSTM_CHUNK
    row8 = lax.broadcasted_iota(jnp.int32, (SUBLANES, 1), 0)

    def conv_silu(x_ref, tail_ref, w_ref, b_ref):
        x = x_ref[...]
        tail = tail_ref[...]
        acc = b_ref[...] + x * w_ref[CONV_W - 1:CONV_W, :]
        for j in range(1, CONV_W):
            xs = pltpu.roll(x, j, axis=0)
            head = jnp.where(row8 < j, pltpu.roll(tail, j, axis=0), xs[0:SUBLANES, :])
            xs = jnp.concatenate([head, xs[SUBLANES:, :]], axis=0)
            acc = acc + xs * w_ref[CONV_W - 1 - j:CONV_W - j, :]
        tail_ref[...] = jnp.where(live, x[L - SUBLANES:, :], tail)
        return _silu(acc)

    q_all = conv_silu(qp_ref, tailq_ref, cwq_ref, cbq_ref)
    k_all = conv_silu(kp_ref, tailk_ref, cwk_ref, cbk_ref) * (A_DK ** -0.5)

    lane = lax.broadcasted_iota(jnp.int32, (L, LANES), 1)
    row = lax.broadcasted_iota(jnp.int32, (L, L), 0)
    col = lax.broadcasted_iota(jnp.int32, (L, L), 1)
    causal = row >= col
    causal_b = causal.astype(BF16)

    ya = []
    for hh in range(A_HEADS):
        cols = slice(hh * A_DK, (hh + 1) * A_DK)
        q, k, v = q_all[:, cols], k_all[:, cols], v_ref[:, cols]

        g = g_ref[:, hh * LANES:(hh + 1) * LANES] + gb_ref[:, hh * LANES:(hh + 1) * LANES]
        g2 = jnp.where(lane == 1, _log_sigmoid(g), g)
        cum = _mask_dot(causal_b, g2)
        x2 = jnp.where(lane == 1, cum, g2)
        x2t = x2.T
        itil_col, b_col = x2[:, 0:1], x2[:, 1:2]
        itil_row, b_row = x2t[0:1, :], x2t[1:2, :]

        m_prev = m_ref[0, hh, 0:1, 0:1]
        dmat = jnp.where(causal, b_col - b_row + itil_row, -jnp.inf)
        inter = b_col + m_prev
        m_t = jnp.maximum(inter, jnp.max(dmat, axis=1, keepdims=True))
        w = jnp.exp(dmat - m_t)
        decay = jnp.exp(inter - m_t)

        qb, kb, vb = q.astype(BF16), k.astype(BF16), v.astype(BF16)
        c_old = c_ref[0, hh]
        n_old = n_ref[0, hh]
        s = _dot_nt(qb, kb) * w
        num = decay * _dot(qb, c_old.astype(BF16)) + _dot(s.astype(BF16), vb)
        den = decay * jnp.sum(q * n_old, axis=1, keepdims=True) + jnp.sum(s, axis=1, keepdims=True)
        h = num / jnp.maximum(jnp.abs(den), jnp.exp(-m_t))

        m_new = m_t[L - 1:L, :]
        b_last = b_col[L - 1:L, :]
        wk = jnp.exp(b_last - b_col + itil_col - m_new)
        dec = jnp.exp(b_last + m_prev - m_new)
        kw = k * wk
        c_ref[0, hh] = jnp.where(live, dec * c_old + _dot(kw.T.astype(BF16), vb), c_old)
        n_ref[0, hh] = jnp.where(live, dec * n_old + jnp.sum(kw, axis=0, keepdims=True), n_old)
        m_ref[0, hh] = jnp.broadcast_to(jnp.where(live, m_new, m_prev), (1, LANES))

        mu = jnp.mean(h, axis=1, keepdims=True)
        hc = h - mu
        var = jnp.mean(hc * hc, axis=1, keepdims=True)
        hn = hc * lax.rsqrt(var + LN_EPS) * ng_ref[:, cols]
        ya.append((hn * _sigmoid(ao_ref[:, cols]) * _silu(az_ref[:, cols])).astype(BF16))
        after_head(hh)
    return ya


def _gla_gate_log(bg_tile, wg, bgate):
    lr = bg_tile[:, 2:2 + GATE_RANK]
    z = _dot(lr.astype(BF16), wg.astype(BF16)) + bgate
    return _log_sigmoid(z) / GATE_TAU


def _gla_heads(q_ref, k_ref, v_ref, bz_ref, g_ref, wg_ref, bgate_ref, ng_ref, st_ref, ds_ref, sb_ref, live,
               after_head):
    L = GLA_STEP
    B = GLA_BLOCK
    nb = L // B
    row = lax.broadcasted_iota(jnp.int32, (L, L), 0)
    col = lax.broadcasted_iota(jnp.int32, (L, L), 1)
    same = (row // B) == (col // B)
    tri_b = (same & (row >= col)).astype(BF16)
    same_b = same.astype(BF16)
    t_in = lax.broadcasted_iota(jnp.int32, (L, 1), 0) % B
    s_lane = lax.broadcasted_iota(jnp.int32, (1, LANES), 1)

    yb = []
    for hh in range(B_HEADS):
        kcols = slice(hh * B_DK, (hh + 1) * B_DK)
        vcols = slice(hh * B_DV, (hh + 1) * B_DV)
        loga = _gla_gate_log(g_ref[:, hh * LANES:(hh + 1) * LANES], wg_ref[:, kcols], bgate_ref[:, kcols])
        bc = _mask_dot(tri_b, loga)
        bl = _mask_dot(same_b, loga)
        k = k_ref[:, kcols]
        qs = q_ref[:, kcols] * (B_DK ** -0.5)
        qt = (qs * jnp.exp(bc)).astype(BF16)
        kt = (k * jnp.exp(bl - bc)).astype(BF16)
        eb = jnp.exp(bl)
        vb = v_ref[:, vcols].astype(BF16)

        bc3, q3, k3 = (x.reshape(nb, B, B_DK) for x in (bc * LOG2_E, qs, k))
        groups = B // SUBLANES
        bcg = [bc3[:, g * SUBLANES:(g + 1) * SUBLANES, :] for g in range(groups)]
        qg = [q3[:, g * SUBLANES:(g + 1) * SUBLANES, :] for g in range(groups)]
        ag = [jnp.zeros((nb, SUBLANES, LANES), F32) for _ in range(groups)]
        for s in range(B):
            bc_s, k_s = bc3[:, s:s + 1, :], k3[:, s:s + 1, :]
            for g in range(s // SUBLANES, groups):
                e = jnp.exp2(jnp.minimum(bcg[g] - bc_s, 0.0))
                a_col = jnp.sum(qg[g] * k_s * e, axis=2, keepdims=True)
                ag[g] = jnp.where(s_lane == s, a_col, ag[g])
        a = jnp.concatenate(ag, axis=1).reshape(L, LANES)
        ab = jnp.where(t_in >= s_lane, a, 0.0)[:, 0:B].astype(BF16)

        for j in range(nb):
            ds_ref[hh, j] = _dot_tn(vb[j * B:(j + 1) * B, :], kt[j * B:(j + 1) * B, :])
        st0 = st_ref[hh]
        st = st0
        for j in range(nb):
            sb_ref[hh, j] = st.astype(BF16)
            st = st * eb[j * B:j * B + 1, :] + ds_ref[hh, j]
        st_ref[hh] = jnp.where(live, st, st0)
        o = jnp.concatenate(
            [_dot(ab[j * B:(j + 1) * B, :], vb[j * B:(j + 1) * B, :])
             + _dot_nt(qt[j * B:(j + 1) * B, :], sb_ref[hh, j]) for j in range(nb)], axis=0)

        on = o * lax.rsqrt(jnp.mean(o * o, axis=1, keepdims=True) + LN_EPS) * ng_ref[:, vcols]
        yb.append((on * _silu(bz_ref[:, vcols])).astype(BF16))
        after_head(hh)
    return yb


def _recur_kernel(qp_ref, kp_ref, av_ref, ao_ref, az_ref, g_ref, cwq_ref, cwk_ref, cbq_ref, cbk_ref,
                  gbias_ref, ang_ref, bq_ref, bk_ref, bv_ref, bz_ref, wg_ref, bgate_ref, bng_ref,
                  ga_ref, gb_ref, wpa_ref, wpb_ref,
                  merged_ref, c_ref, n_ref, m_ref, s_out_ref,
                  tailq_ref, tailk_ref, st_ref, ds_ref, sb_ref, ya_prev_ref, yb_prev_ref, *, n_chunks):
    g = pl.program_id(0)
    live = g < pl.num_programs(0) - 1
    c = g % n_chunks

    @pl.when(live & (c == 0))
    def _():
        for ref in (c_ref, n_ref, m_ref, tailq_ref, tailk_ref, st_ref):
            ref[...] = jnp.zeros_like(ref)

    @pl.when(g == 0)
    def _():
        ya_prev_ref[...] = jnp.zeros_like(ya_prev_ref)
        yb_prev_ref[...] = jnp.zeros_like(yb_prev_ref)

    n_slices = A_HEADS + B_HEADS
    width = D_MODEL // n_slices
    ya_prev = ya_prev_ref[...]
    yb_prev = yb_prev_ref[...]

    def merge_slice(i):
        cols = slice(i * width, (i + 1) * width)
        pa = _dot(ya_prev, wpa_ref[:, cols])
        pb = _dot(yb_prev, wpb_ref[:, cols])
        merged_ref[:, cols] = (_sigmoid(ga_ref[:, cols]) * pa
                               + _sigmoid(gb_ref[:, cols]) * pb).astype(merged_ref.dtype)

    ya = _mlstm_heads(qp_ref, kp_ref, av_ref, ao_ref, az_ref, g_ref, cwq_ref, cwk_ref, cbq_ref, cbk_ref,
                      gbias_ref, ang_ref, c_ref, n_ref, m_ref, tailq_ref, tailk_ref, live, merge_slice)
    yb = _gla_heads(bq_ref, bk_ref, bv_ref, bz_ref, g_ref, wg_ref, bgate_ref, bng_ref,
                    st_ref, ds_ref, sb_ref, live, lambda h: merge_slice(A_HEADS + h))

    for h in range(A_HEADS):
        ya_prev_ref[:, h * A_DV:(h + 1) * A_DV] = ya[h]
    for h in range(B_HEADS):
        yb_prev_ref[:, h * B_DV:(h + 1) * B_DV] = yb[h]

    @pl.when(live & (c == n_chunks - 1))
    def _():
        for hh in range(B_HEADS):
            s_out_ref[0, hh] = st_ref[hh].T


RECUR_VMEM_LIMIT = 56 * 1024 * 1024


def _recur(p_all, batch, seq, conv_w, conv_b, gate_bias, a_norm_g, w_gate_up, b_gate, b_norm_g, w_pa, w_pb):
    L = MLSTM_CHUNK
    assert GLA_STEP == L
    nc = seq // L
    t = batch * seq
    n_steps = batch * nc + 1

    def cur(g):
        return jnp.minimum(g, n_steps - 2)

    def prev(g):
        return jnp.maximum(g - 1, 0)

    def pcol(base, width, chunk=cur):
        return pl.BlockSpec((L, width), lambda g: (chunk(g), base // width))

    def const(shape, col=0, **kw):
        return pl.BlockSpec(shape, lambda g: (0, col), **kw)

    def state(*dims):
        return pl.BlockSpec((1,) + dims, lambda g: (cur(g) // nc,) + (0,) * len(dims))

    in_specs = [
        pcol(COL_Q, A_WIDTH), pcol(COL_K, A_WIDTH), pcol(COL_AV, A_WIDTH), pcol(COL_AO, A_WIDTH),
        pcol(COL_AZ, A_WIDTH), pcol(COL_GATE, A_HEADS * LANES),
        const((CONV_W, A_WIDTH)), const((CONV_W, A_WIDTH), 1), const((1, A_WIDTH)), const((1, A_WIDTH), 1),
        const((1, A_HEADS * LANES)), const((1, A_WIDTH)),
        pcol(COL_BQ, B_KWIDTH), pcol(COL_BK, B_KWIDTH), pcol(COL_BV, B_WIDTH), pcol(COL_BZ, B_WIDTH),
        const((GATE_RANK, B_KWIDTH)), const((1, B_KWIDTH)), const((1, B_WIDTH)),
        pcol(COL_GA, D_MODEL, prev), pcol(COL_GB, D_MODEL, prev),
        const((A_WIDTH, D_MODEL), pipeline_mode=pl.Buffered(1)),
        const((B_WIDTH, D_MODEL), pipeline_mode=pl.Buffered(1)),
    ]
    out_specs = [
        pl.BlockSpec((L, D_MODEL), lambda g: (prev(g), 0)),
        state(A_HEADS, A_DK, A_DV), state(A_HEADS, 1, A_DK), state(A_HEADS, 1, LANES),
        state(B_HEADS, B_DK, B_DV),
    ]
    out_shape = [
        jax.ShapeDtypeStruct((t, D_MODEL), BF16),
        jax.ShapeDtypeStruct((batch, A_HEADS, A_DK, A_DV), F32),
        jax.ShapeDtypeStruct((batch, A_HEADS, 1, A_DK), F32),
        jax.ShapeDtypeStruct((batch, A_HEADS, 1, LANES), F32),
        jax.ShapeDtypeStruct((batch, B_HEADS, B_DK, B_DV), F32),
    ]
    nb = L // GLA_BLOCK
    return pl.pallas_call(
        functools.partial(_recur_kernel, n_chunks=nc),
        grid=(n_steps,),
        in_specs=in_specs,
        out_specs=out_specs,
        out_shape=out_shape,
        scratch_shapes=[pltpu.VMEM((SUBLANES, A_WIDTH), F32),
                        pltpu.VMEM((SUBLANES, A_WIDTH), F32),
                        pltpu.VMEM((B_HEADS, B_DV, B_DK), F32),
                        pltpu.VMEM((B_HEADS, nb, B_DV, B_DK), F32),
                        pltpu.VMEM((B_HEADS, nb, B_DV, B_DK), BF16),
                        pltpu.VMEM((L, A_WIDTH), BF16),
                        pltpu.VMEM((L, B_WIDTH), BF16)],
        compiler_params=pltpu.CompilerParams(
            dimension_semantics=("arbitrary",), vmem_limit_bytes=RECUR_VMEM_LIMIT),
        name="recur",
    )(p_all, p_all, p_all, p_all, p_all, p_all, conv_w, conv_w, conv_b, conv_b, gate_bias, a_norm_g,
      p_all, p_all, p_all, p_all, w_gate_up, b_gate, b_norm_g, p_all, p_all, w_pa, w_pb)


DEC_TOKENS = 16


def _decode_kernel(qp_ref, kp_ref, av_ref, ao_ref, az_ref, bq_ref, bk_ref, bv_ref, bz_ref, g_ref,
                   sq0_ref, sq1_ref, sq2_ref, sk0_ref, sk1_ref, sk2_ref,
                   cwq_ref, cwk_ref, cbq_ref, cbk_ref, gb_ref, ang_ref, wg_ref, bgate_ref, bng_ref,
                   c_ref, n_ref, m_ref, s_ref,
                   ya_ref, yb_ref, c_out_ref, n_out_ref, m_out_ref, s_out_ref):
    TB = DEC_TOKENS
    h_idx = pl.program_id(1)

    def conv_silu(s0, s1, s2, x, w_ref, b_ref):
        acc = b_ref[...] + s0[...] * w_ref[0:1, :]
        acc = acc + s1[...] * w_ref[1:2, :]
        acc = acc + s2[...] * w_ref[2:3, :]
        acc = acc + x[...] * w_ref[3:4, :]
        return _silu(acc)

    q = conv_silu(sq0_ref, sq1_ref, sq2_ref, qp_ref, cwq_ref, cbq_ref)
    k = conv_silu(sk0_ref, sk1_ref, sk2_ref, kp_ref, cwk_ref, cbk_ref) * (A_DK ** -0.5)
    v = av_ref[...]
    g = g_ref[...]
    gbias = gb_ref[...]
    itil = g[:, 0:1] + gbias[:, 0:1]
    logf = _log_sigmoid(g[:, 1:2] + gbias[:, 1:2])
    lane_h = lax.broadcasted_iota(jnp.int32, (TB, A_HEADS), 1)
    m_prev = jnp.sum(jnp.where(lane_h == h_idx, m_ref[...], 0.0), axis=1, keepdims=True)
    inter = logf + m_prev
    m_t = jnp.maximum(inter, itil)
    w = jnp.exp(itil - m_t)
    decay = jnp.exp(inter - m_t)
    n_old = n_ref[:, 0, 0, :]
    s = jnp.sum(q * k, axis=1, keepdims=True) * w
    den = decay * jnp.sum(q * n_old, axis=1, keepdims=True) + s
    scale = 1.0 / jnp.maximum(jnp.abs(den), jnp.exp(-m_t))
    kw = k * w
    n_out_ref[:, 0, 0, :] = decay * n_old + kw
    m_out_ref[...] = jnp.broadcast_to(m_t, (TB, LANES))

    rows8 = lax.broadcasted_iota(jnp.int32, (TB, 1), 0)
    qb, kwb, vb = q.astype(BF16), kw.astype(BF16), v.astype(BF16)
    h_rows = []
    for t in range(TB):
        c_old = c_ref[t, 0]
        sel = rows8 == t
        qc = _dot(qb, c_old.astype(BF16))[t:t + 1, :]
        h_rows.append((decay[t:t + 1, :] * qc + s[t:t + 1, :] * v[t:t + 1, :]) * scale[t:t + 1, :])
        outer = _dot_tn(jnp.where(sel, kwb, jnp.zeros_like(kwb)), vb)
        c_out_ref[t, 0] = decay[t:t + 1, :] * c_old + outer
    h = jnp.concatenate(h_rows, axis=0)
    mu = jnp.mean(h, axis=1, keepdims=True)
    hc = h - mu
    var = jnp.mean(hc * hc, axis=1, keepdims=True)
    hn = hc * lax.rsqrt(var + LN_EPS) * ang_ref[...]
    ya_ref[...] = (hn * _sigmoid(ao_ref[...]) * _silu(az_ref[...])).astype(ya_ref.dtype)

    gq = bq_ref[...] * (B_DK ** -0.5)
    gk = bk_ref[...]
    gv = bv_ref[...]
    loga = _gla_gate_log(g, wg_ref[...], bgate_ref[...])
    eb = jnp.exp(loga)
    a = jnp.sum(gq * gk, axis=1, keepdims=True)
    ebt = jnp.concatenate([eb, jnp.zeros((LANES - TB, B_DK), F32)], axis=0).T
    qeb, gkb, gvb = (gq * eb).astype(BF16), gk.astype(BF16), gv.astype(BF16)
    o_rows = []
    for t in range(TB):
        s_old = s_ref[t, 0]
        sel = rows8 == t
        o_rows.append(_dot(qeb, s_old.astype(BF16))[t:t + 1, :] + a[t:t + 1, :] * gv[t:t + 1, :])
        outer = _dot_tn(jnp.where(sel, gkb, jnp.zeros_like(gkb)), gvb)
        s_out_ref[t, 0] = ebt[:, t:t + 1] * s_old + outer
    o = jnp.concatenate(o_rows, axis=0)
    on = o * lax.rsqrt(jnp.mean(o * o, axis=1, keepdims=True) + LN_EPS) * bng_ref[...]
    yb_ref[...] = (on * _silu(bz_ref[...])).astype(yb_ref.dtype)


def _decode(p_s, conv_state, conv_w, conv_b, gate_bias, a_norm_g, w_gate_up, b_gate, b_norm_g,
            c_state, n_state, m_state, s_state):
    nseq = p_s.shape[0]
    TB = DEC_TOKENS

    def pcol(base, width):
        return pl.BlockSpec((TB, width), lambda i, h: (i, base // width + h))

    def cstate(j, base):
        return pl.BlockSpec((TB, A_DK), lambda i, h: (i, (j * 2 * A_WIDTH + base) // A_DK + h))

    in_specs = [
        pcol(COL_Q, A_DK), pcol(COL_K, A_DK), pcol(COL_AV, A_DV), pcol(COL_AO, A_DV), pcol(COL_AZ, A_DV),
        pcol(COL_BQ, B_DK), pcol(COL_BK, B_DK), pcol(COL_BV, B_DV), pcol(COL_BZ, B_DV),
        pcol(COL_GATE, LANES),
        cstate(0, COL_Q), cstate(1, COL_Q), cstate(2, COL_Q),
        cstate(0, COL_K), cstate(1, COL_K), cstate(2, COL_K),
        pl.BlockSpec((CONV_W, A_DK), lambda i, h: (0, h)),
        pl.BlockSpec((CONV_W, A_DK), lambda i, h: (0, A_HEADS + h)),
        pl.BlockSpec((1, A_DK), lambda i, h: (0, h)),
        pl.BlockSpec((1, A_DK), lambda i, h: (0, A_HEADS + h)),
        pl.BlockSpec((1, LANES), lambda i, h: (0, h)),
        pl.BlockSpec((1, A_DV), lambda i, h: (0, h)),
        pl.BlockSpec((GATE_RANK, B_DK), lambda i, h: (0, h)),
        pl.BlockSpec((1, B_DK), lambda i, h: (0, h)),
        pl.BlockSpec((1, B_DV), lambda i, h: (0, h)),
        pl.BlockSpec((TB, 1, A_DK, A_DV), lambda i, h: (i, h, 0, 0)),
        pl.BlockSpec((TB, 1, 1, A_DK), lambda i, h: (i, h, 0, 0)),
        pl.BlockSpec((TB, A_HEADS), lambda i, h: (i, 0)),
        pl.BlockSpec((TB, 1, B_DK, B_DV), lambda i, h: (i, h, 0, 0)),
    ]
    out_specs = [
        pl.BlockSpec((TB, A_DV), lambda i, h: (i, h)),
        pl.BlockSpec((TB, B_DV), lambda i, h: (i, h)),
        pl.BlockSpec((TB, 1, A_DK, A_DV), lambda i, h: (i, h, 0, 0)),
        pl.BlockSpec((TB, 1, 1, A_DK), lambda i, h: (i, h, 0, 0)),
        pl.BlockSpec((TB, LANES), lambda i, h: (i, h)),
        pl.BlockSpec((TB, 1, B_DK, B_DV), lambda i, h: (i, h, 0, 0)),
    ]
    out_shape = [
        jax.ShapeDtypeStruct((nseq, A_WIDTH), BF16),
        jax.ShapeDtypeStruct((nseq, B_WIDTH), BF16),
        jax.ShapeDtypeStruct((nseq, A_HEADS, A_DK, A_DV), F32),
        jax.ShapeDtypeStruct((nseq, A_HEADS, 1, A_DK), F32),
        jax.ShapeDtypeStruct((nseq, A_HEADS * LANES), F32),
        jax.ShapeDtypeStruct((nseq, B_HEADS, B_DK, B_DV), F32),
    ]
    return pl.pallas_call(
        _decode_kernel,
        grid=(nseq // TB, A_HEADS),
        in_specs=in_specs,
        out_specs=out_specs,
        out_shape=out_shape,
        compiler_params=pltpu.CompilerParams(
            dimension_semantics=("parallel", "parallel"), vmem_limit_bytes=VMEM_LIMIT),
        name="decode",
    )(p_s, p_s, p_s, p_s, p_s, p_s, p_s, p_s, p_s, p_s,
      conv_state, conv_state, conv_state, conv_state, conv_state, conv_state,
      conv_w, conv_w, conv_b, conv_b, gate_bias, a_norm_g, w_gate_up, b_gate, b_norm_g,
      c_state, n_state, m_state, s_state)


def _out_kernel(ya_ref, yb_ref, ga_ref, gb_ref, x_ref, wpa_ref, wpb_ref, wo_ref, lng_ref, lnb_ref, o_ref):
    pa = _dot(ya_ref[...], wpa_ref[...])
    pb = _dot(yb_ref[...], wpb_ref[...])
    merged = _sigmoid(ga_ref[...]) * pa + _sigmoid(gb_ref[...]) * pb
    y = _dot(merged.astype(BF16), wo_ref[...])
    r = ALPHA * x_ref[...] + y
    mu = jnp.mean(r, axis=1, keepdims=True)
    rc = r - mu
    var = jnp.mean(rc * rc, axis=1, keepdims=True)
    o_ref[...] = rc * lax.rsqrt(var + LN_EPS) * lng_ref[...] + lnb_ref[...]


def _out(ya, yb, p_all, x, w_pa, w_pb, w_out, ln_g, ln_b, tm):
    t = x.shape[0]
    const = lambda i: (0, 0)
    single = pl.Buffered(1)
    in_specs = [
        pl.BlockSpec((tm, A_WIDTH), lambda i: (i, 0)),
        pl.BlockSpec((tm, B_WIDTH), lambda i: (i, 0)),
        pl.BlockSpec((tm, D_MODEL), lambda i: (i, COL_GA // D_MODEL)),
        pl.BlockSpec((tm, D_MODEL), lambda i: (i, COL_GB // D_MODEL)),
        pl.BlockSpec((tm, D_MODEL), lambda i: (i, 0)),
        pl.BlockSpec((A_WIDTH, D_MODEL), const, pipeline_mode=single),
        pl.BlockSpec((B_WIDTH, D_MODEL), const, pipeline_mode=single),
        pl.BlockSpec((D_MODEL, D_MODEL), const, pipeline_mode=single),
        pl.BlockSpec((1, D_MODEL), const),
        pl.BlockSpec((1, D_MODEL), const),
    ]
    return pl.pallas_call(
        _out_kernel,
        grid=(t // tm,),
        in_specs=in_specs,
        out_specs=pl.BlockSpec((tm, D_MODEL), lambda i: (i, 0)),
        out_shape=jax.ShapeDtypeStruct((t, D_MODEL), F32),
        compiler_params=pltpu.CompilerParams(
            dimension_semantics=("parallel",), vmem_limit_bytes=VMEM_LIMIT),
        name="outproj",
    )(ya, yb, p_all, p_all, x, w_pa, w_pb, w_out, ln_g, ln_b)


def _final_kernel(merged_ref, x_ref, wo_ref, lng_ref, lnb_ref, o_ref):
    r = ALPHA * x_ref[...] + _dot(merged_ref[...], wo_ref[...])
    mu = jnp.mean(r, axis=1, keepdims=True)
    rc = r - mu
    var = jnp.mean(rc * rc, axis=1, keepdims=True)
    o_ref[...] = rc * lax.rsqrt(var + LN_EPS) * lng_ref[...] + lnb_ref[...]


def _final(merged, x, w_out, ln_g, ln_b, tm):
    t = x.shape[0]
    const = lambda i: (0, 0)
    return pl.pallas_call(
        _final_kernel,
        grid=(t // tm,),
        in_specs=[pl.BlockSpec((tm, D_MODEL), lambda i: (i, 0)),
                  pl.BlockSpec((tm, D_MODEL), lambda i: (i, 0)),
                  pl.BlockSpec((D_MODEL, D_MODEL), const, pipeline_mode=pl.Buffered(1)),
                  pl.BlockSpec((1, D_MODEL), const),
                  pl.BlockSpec((1, D_MODEL), const)],
        out_specs=pl.BlockSpec((tm, D_MODEL), lambda i: (i, 0)),
        out_shape=jax.ShapeDtypeStruct((t, D_MODEL), F32),
        compiler_params=pltpu.CompilerParams(
            dimension_semantics=("parallel",), vmem_limit_bytes=VMEM_LIMIT),
        name="final",
    )(merged, x, w_out, ln_g, ln_b)


RELAYOUT_TN = 512
SRC_I = 3 * A_WIDTH
SRC_SHIFT_A = 2 * A_HEADS
SRC_BG = COL_BZ + SRC_SHIFT_A
SRC_SHIFT_B = SRC_SHIFT_A + GATE_RANK
GATE_SRC_ROWS = 4 * SUBLANES


def _relayout_kernel(src_ref, gif_ref, gb0_ref, gb1_ref, o_ref):
    j = pl.program_id(0)
    tn = RELAYOUT_TN
    n_main = COL_GATE // tn

    @pl.when(j < n_main)
    def _():
        o_ref[...] = src_ref[...].astype(BF16)

    @pl.when(j == n_main)
    def _():
        g = jnp.concatenate([gif_ref[...], gb0_ref[...], gb1_ref[...],
                             jnp.zeros((SUBLANES, src_ref.shape[1]), F32)], axis=0).astype(BF16)
        r = lax.broadcasted_iota(jnp.int32, (tn, GATE_SRC_ROWS), 0)
        c = lax.broadcasted_iota(jnp.int32, (tn, GATE_SRC_ROWS), 1)
        h, l = r // LANES, r % LANES
        sel = (((l == 0) & (c == h)) | ((l == 1) & (c == A_HEADS + h))
               | ((l >= 2) & (l < 2 + GATE_RANK) & (c == l + (2 * A_HEADS - 2))))
        o_ref[...] = _dot(sel.astype(BF16), g).astype(BF16)


def _relayout_w_in(w_in_t):
    n_src, k = w_in_t.shape
    tn = RELAYOUT_TN

    def src_row(j):
        shift = jnp.where(j < COL_AO // tn, 0, jnp.where(j < COL_BZ // tn, SRC_SHIFT_A, SRC_SHIFT_B))
        group = jnp.minimum(j * (tn // SUBLANES) + shift // SUBLANES, (n_src - tn) // SUBLANES)
        return group * SUBLANES

    return pl.pallas_call(
        _relayout_kernel,
        grid=(N_PROJ // tn,),
        in_specs=[pl.BlockSpec((pl.Element(tn), pl.Element(k)), lambda j: (src_row(j), 0)),
                  pl.BlockSpec((SUBLANES, k), lambda j: (SRC_I // SUBLANES, 0)),
                  pl.BlockSpec((SUBLANES, k), lambda j: (SRC_BG // SUBLANES, 0)),
                  pl.BlockSpec((SUBLANES, k), lambda j: (SRC_BG // SUBLANES + 1, 0))],
        out_specs=pl.BlockSpec((tn, k), lambda j: (j, 0)),
        out_shape=jax.ShapeDtypeStruct((N_PROJ, k), BF16),
        compiler_params=pltpu.CompilerParams(
            dimension_semantics=("parallel",), vmem_limit_bytes=VMEM_LIMIT),
        name="relayout",
    )(w_in_t, w_in_t, w_in_t, w_in_t)


def kernel(x_prompt, x_sample, state_mlstm_C, state_mlstm_n, state_mlstm_m, state_conv, state_gla_S,
           w_in, conv_w, conv_b, b_i, b_f, a_norm_g, w_gate_up, b_gate, b_norm_g, w_pa, w_pb, w_out,
           ln_g, ln_b):
    batch, seq, _ = x_prompt.shape
    nseq = x_sample.shape[0]
    assert w_in.shape[0] == 1, "single-layer step"
    d = 0

    def layer(a):
        return a.reshape(a.shape[1:])

    wp = _relayout_w_in(layer(w_in).T)
    wpa, wpb, wo = w_pa[d].astype(BF16), w_pb[d].astype(BF16), w_out[d].astype(BF16)
    cw = conv_w[d]
    cb = conv_b[d][None, :]
    gate_bias = jnp.zeros((A_HEADS, LANES), F32).at[:, 0].set(b_i[d]).at[:, 1].set(b_f[d]).reshape(1, -1)
    ang = a_norm_g[d][None, :]
    bng = b_norm_g[d][None, :]
    wg = w_gate_up[d]
    bgate = b_gate[d][None, :]
    lng, lnb = ln_g[d][None, :], ln_b[d][None, :]

    xp = x_prompt.reshape(batch * seq, D_MODEL)
    p_all = _proj(xp, wp, tm=1024, tn=1280)
    merged, p_c, p_n, p_m, p_s = _recur(p_all, batch, seq, cw, cb, gate_bias, ang, wg, bgate, bng, wpa, wpb)
    y_prompt = _final(merged, xp, wo, lng, lnb, tm=512).reshape(batch, seq, D_MODEL)
    p_conv = p_all.reshape(batch, seq, N_PROJ)[:, seq - (CONV_W - 1):, :2 * A_WIDTH]

    xs = x_sample.reshape(nseq, D_MODEL)
    ps_all = _proj(xs, wp, tm=nseq, tn=1280)
    conv_state = state_conv.reshape(nseq, (CONV_W - 1) * 2 * A_WIDTH)
    ya_s, yb_s, s_c, s_n, s_m, s_s = _decode(
        ps_all, conv_state, cw, cb, gate_bias, ang, wg, bgate, bng,
        layer(state_mlstm_C), state_mlstm_n.reshape(nseq, A_HEADS, 1, A_DK), layer(state_mlstm_m),
        layer(state_gla_S))
    y_sample = _out(ya_s, yb_s, ps_all, xs, wpa, wpb, wo, lng, lnb, tm=nseq).reshape(nseq, 1, D_MODEL)
    s_conv = jnp.concatenate([layer(state_conv)[:, 1:, :], ps_all[:, None, :2 * A_WIDTH]], axis=1)

    def stacked(a):
        return a.reshape((1,) + a.shape)

    return (y_prompt, y_sample,
            stacked(p_c), p_n.reshape(1, batch, A_HEADS, A_DK), stacked(p_m[:, :, 0, 0]),
            stacked(p_conv), stacked(p_s),
            stacked(s_c), s_n.reshape(1, nseq, A_HEADS, A_DK),
            stacked(s_m.reshape(nseq, A_HEADS, LANES)[:, :, 0]), stacked(s_conv), stacked(s_s))
```

```python
import functools

import jax
import jax.numpy as jnp
from jax import lax
from jax.experimental import pallas as pl
from jax.experimental.pallas import tpu as pltpu

F32 = jnp.float32
BF16 = jnp.bfloat16

D_MODEL = 2048
A_HEADS = 4
A_WIDTH = 1024
A_DK = 256
A_DV = 256
CONV_W = 4
B_HEADS = 4
B_WIDTH = 1024
B_KWIDTH = 512
B_DK = 128
B_DV = 256
GATE_RANK = 16
GATE_TAU = 16.0
ALPHA = 2.0 ** 0.25
LN_EPS = 1e-5
LOG2_E = 1.4426950408889634

LANES = 128
SUBLANES = 8
VMEM_LIMIT = 48 * 1024 * 1024

COL_Q = 0
COL_K = A_WIDTH
COL_AV = 2 * A_WIDTH
COL_AO = 3 * A_WIDTH
COL_AZ = 4 * A_WIDTH
COL_BQ = 5 * A_WIDTH
COL_BK = COL_BQ + B_KWIDTH
COL_BV = COL_BK + B_KWIDTH
COL_BZ = COL_BV + B_WIDTH
COL_GA = COL_BZ + B_WIDTH
COL_GB = COL_GA + D_MODEL
COL_GATE = COL_GB + D_MODEL
N_PROJ = COL_GATE + A_HEADS * LANES

MLSTM_CHUNK = 256
MLSTM_HEADS_PER_STEP = 4
GLA_STEP = 256
GLA_BLOCK = 16
GLA_HEADS_PER_STEP = 4


def _dot(a, b):
    return jnp.dot(a, b, preferred_element_type=F32)


def _dot_nt(a, b):
    return lax.dot_general(a, b, (((1,), (1,)), ((), ())), preferred_element_type=F32)


def _dot_tn(a, b):
    return lax.dot_general(a, b, (((0,), (0,)), ((), ())), preferred_element_type=F32)


def _mask_dot(mask_bf16, x):
    hi = x.astype(BF16)
    r1 = x - hi.astype(F32)
    mid = r1.astype(BF16)
    lo = (r1 - mid.astype(F32)).astype(BF16)
    return _dot(mask_bf16, hi) + _dot(mask_bf16, mid) + _dot(mask_bf16, lo)


def _log_sigmoid(z):
    return jnp.minimum(z, 0.0) - jnp.log(1.0 + jnp.exp(-jnp.abs(z)))


def _sigmoid(z):
    return 1.0 / (1.0 + jnp.exp(-z))


def _silu(z):
    return z * _sigmoid(z)


def _proj_kernel(x_ref, w_ref, o_ref, xb_ref):
    @pl.when(pl.program_id(1) == 0)
    def _():
        xb_ref[...] = x_ref[...].astype(BF16)

    o_ref[...] = _dot_nt(xb_ref[...], w_ref[...])


def _proj(x, w_t, tm, tn):
    t, k = x.shape
    n = w_t.shape[0]
    return pl.pallas_call(
        _proj_kernel,
        grid=(t // tm, n // tn),
        in_specs=[pl.BlockSpec((tm, k), lambda i, j: (i, 0)),
                  pl.BlockSpec((tn, k), lambda i, j: (j, 0))],
        out_specs=pl.BlockSpec((tm, tn), lambda i, j: (i, j)),
        out_shape=jax.ShapeDtypeStruct((t, n), F32),
        scratch_shapes=[pltpu.VMEM((tm, k), BF16)],
        compiler_params=pltpu.CompilerParams(
            dimension_semantics=("parallel", "arbitrary"), vmem_limit_bytes=VMEM_LIMIT),
        name="proj",
    )(x, w_t)


def _mlstm_kernel(qp_ref, kp_ref, v_ref, ao_ref, az_ref, g_ref, cwq_ref, cwk_ref, cbq_ref, cbk_ref,
                  gb_ref, ng_ref, ya_ref, c_ref, n_ref, m_ref, tailq_ref, tailk_ref):
    L = MLSTM_CHUNK
    c = pl.program_id(2)

    @pl.when(c == 0)
    def _():
        c_ref[...] = jnp.zeros_like(c_ref)
        n_ref[...] = jnp.zeros_like(n_ref)
        m_ref[...] = jnp.zeros_like(m_ref)
        tailq_ref[...] = jnp.zeros_like(tailq_ref)
        tailk_ref[...] = jnp.zeros_like(tailk_ref)

    row8 = lax.broadcasted_iota(jnp.int32, (SUBLANES, 1), 0)

    def conv_silu(x_ref, tail_ref, w_ref, b_ref):
        x = x_ref[...]
        tail = tail_ref[...]
        acc = b_ref[...] + x * w_ref[CONV_W - 1:CONV_W, :]
        for j in range(1, CONV_W):
            xs = pltpu.roll(x, j, axis=0)
            head = jnp.where(row8 < j, pltpu.roll(tail, j, axis=0), xs[0:SUBLANES, :])
            xs = jnp.concatenate([head, xs[SUBLANES:, :]], axis=0)
            acc = acc + xs * w_ref[CONV_W - 1 - j:CONV_W - j, :]
        tail_ref[...] = x[L - SUBLANES:, :]
        return _silu(acc)

    q_all = conv_silu(qp_ref, tailq_ref, cwq_ref, cbq_ref)
    k_all = conv_silu(kp_ref, tailk_ref, cwk_ref, cbk_ref) * (A_DK ** -0.5)

    lane = lax.broadcasted_iota(jnp.int32, (L, LANES), 1)
    row = lax.broadcasted_iota(jnp.int32, (L, L), 0)
    col = lax.broadcasted_iota(jnp.int32, (L, L), 1)
    causal = row >= col
    causal_b = causal.astype(BF16)

    for hh in range(MLSTM_HEADS_PER_STEP):
        cols = slice(hh * A_DK, (hh + 1) * A_DK)
        q, k, v = q_all[:, cols], k_all[:, cols], v_ref[:, cols]

        g = g_ref[:, hh * LANES:(hh + 1) * LANES] + gb_ref[:, hh * LANES:(hh + 1) * LANES]
        g2 = jnp.where(lane == 1, _log_sigmoid(g), g)
        cum = _mask_dot(causal_b, g2)
        x2 = jnp.where(lane == 1, cum, g2)
        x2t = x2.T
        itil_col, b_col = x2[:, 0:1], x2[:, 1:2]
        itil_row, b_row = x2t[0:1, :], x2t[1:2, :]

        m_prev = m_ref[0, hh, 0:1, 0:1]
        dmat = jnp.where(causal, b_col - b_row + itil_row, -jnp.inf)
        inter = b_col + m_prev
        m_t = jnp.maximum(inter, jnp.max(dmat, axis=1, keepdims=True))
        w = jnp.exp(dmat - m_t)
        decay = jnp.exp(inter - m_t)

        qb, kb, vb = q.astype(BF16), k.astype(BF16), v.astype(BF16)
        c_old = c_ref[0, hh]
        n_old = n_ref[0, hh]
        s = _dot_nt(qb, kb) * w
        num = decay * _dot(qb, c_old.astype(BF16)) + _dot(s.astype(BF16), vb)
        den = decay * jnp.sum(q * n_old, axis=1, keepdims=True) + jnp.sum(s, axis=1, keepdims=True)
        inv = 1.0 / jnp.maximum(jnp.abs(den), jnp.exp(-m_t))

        m_new = m_t[L - 1:L, :]
        b_last = b_col[L - 1:L, :]
        wk = jnp.exp(b_last - b_col + itil_col - m_new)
        dec = jnp.exp(b_last + m_prev - m_new)
        kw = k * wk
        c_ref[0, hh] = dec * c_old + _dot(kw.T.astype(BF16), vb)
        n_ref[0, hh] = dec * n_old + jnp.sum(kw, axis=0, keepdims=True)
        m_ref[0, hh] = jnp.broadcast_to(m_new, (1, LANES))

        mu = jnp.mean(num, axis=1, keepdims=True)
        hc = num - mu
        var = jnp.mean(hc * hc, axis=1, keepdims=True)
        hn = hc * (inv * lax.rsqrt(var * (inv * inv) + LN_EPS)) * ng_ref[:, cols]
        ya_ref[:, cols] = (hn * _sigmoid(ao_ref[:, cols]) * _silu(az_ref[:, cols])).astype(ya_ref.dtype)


def _mlstm(p_all, batch, seq, conv_w, conv_b, gate_bias, a_norm_g):
    L = MLSTM_CHUNK
    hps = MLSTM_HEADS_PER_STEP
    nc = seq // L
    t = batch * seq
    wd = hps * A_DK
    k_off = A_WIDTH // wd

    def rows(b, h, c):
        return b * nc + c

    def pcol(base):
        return pl.BlockSpec((L, wd), lambda b, h, c: (rows(b, h, c), base // wd + h))

    in_specs = [
        pcol(COL_Q), pcol(COL_K), pcol(COL_AV), pcol(COL_AO), pcol(COL_AZ),
        pl.BlockSpec((L, hps * LANES), lambda b, h, c: (rows(b, h, c), COL_GATE // (hps * LANES) + h)),
        pl.BlockSpec((CONV_W, wd), lambda b, h, c: (0, h)),
        pl.BlockSpec((CONV_W, wd), lambda b, h, c: (0, k_off + h)),
        pl.BlockSpec((1, wd), lambda b, h, c: (0, h)),
        pl.BlockSpec((1, wd), lambda b, h, c: (0, k_off + h)),
        pl.BlockSpec((1, hps * LANES), lambda b, h, c: (0, h)),
        pl.BlockSpec((1, wd), lambda b, h, c: (0, h)),
    ]
    out_specs = [
        pl.BlockSpec((L, wd), lambda b, h, c: (rows(b, h, c), h)),
        pl.BlockSpec((1, hps, A_DK, A_DV), lambda b, h, c: (b, h, 0, 0)),
        pl.BlockSpec((1, hps, 1, A_DK), lambda b, h, c: (b, h, 0, 0)),
        pl.BlockSpec((1, hps, 1, LANES), lambda b, h, c: (b, h, 0, 0)),
    ]
    out_shape = [
        jax.ShapeDtypeStruct((t, A_WIDTH), BF16),
        jax.ShapeDtypeStruct((batch, A_HEADS, A_DK, A_DV), F32),
        jax.ShapeDtypeStruct((batch, A_HEADS, 1, A_DK), F32),
        jax.ShapeDtypeStruct((batch, A_HEADS, 1, LANES), F32),
    ]
    return pl.pallas_call(
        _mlstm_kernel,
        grid=(batch, A_HEADS // hps, nc),
        in_specs=in_specs,
        out_specs=out_specs,
        out_shape=out_shape,
        scratch_shapes=[pltpu.VMEM((SUBLANES, wd), F32),
                        pltpu.VMEM((SUBLANES, wd), F32)],
        compiler_params=pltpu.CompilerParams(
            dimension_semantics=("parallel", "parallel", "arbitrary"), vmem_limit_bytes=VMEM_LIMIT),
        name="mlstm",
    )(p_all, p_all, p_all, p_all, p_all, p_all, conv_w, conv_w, conv_b, conv_b, gate_bias, a_norm_g)


def _gla_gate_log(bg_tile, wg, bgate):
    lr = bg_tile[:, 2:2 + GATE_RANK]
    z = _dot(lr.astype(BF16), wg.astype(BF16)) + bgate
    return _log_sigmoid(z) / GATE_TAU


def _gla_kernel(q_ref, k_ref, v_ref, bz_ref, g_ref, wg_ref, bgate_ref, ng_ref,
                yb_ref, s_out_ref, st_ref, ds_ref, sb_ref):
    L = GLA_STEP
    B = GLA_BLOCK
    nb = L // B
    c = pl.program_id(2)

    @pl.when(c == 0)
    def _():
        st_ref[...] = jnp.zeros_like(st_ref)

    row = lax.broadcasted_iota(jnp.int32, (L, L), 0)
    col = lax.broadcasted_iota(jnp.int32, (L, L), 1)
    same = (row // B) == (col // B)
    tri_b = (same & (row >= col)).astype(BF16)
    same_b = same.astype(BF16)
    t_in = lax.broadcasted_iota(jnp.int32, (L, 1), 0) % B
    s_lane = lax.broadcasted_iota(jnp.int32, (1, LANES), 1)

    for hh in range(GLA_HEADS_PER_STEP):
        kcols = slice(hh * B_DK, (hh + 1) * B_DK)
        vcols = slice(hh * B_DV, (hh + 1) * B_DV)
        loga = _gla_gate_log(g_ref[:, hh * LANES:(hh + 1) * LANES], wg_ref[:, kcols], bgate_ref[:, kcols])
        bc = _mask_dot(tri_b, loga)
        bl = _mask_dot(same_b, loga)
        k = k_ref[:, kcols]
        qs = q_ref[:, kcols] * (B_DK ** -0.5)
        qt = (qs * jnp.exp(bc)).astype(BF16)
        kt = (k * jnp.exp(bl - bc)).astype(BF16)
        eb = jnp.exp(bl)
        vb = v_ref[:, vcols].astype(BF16)

        bc3, q3, k3 = (x.reshape(nb, B, B_DK) for x in (bc * LOG2_E, qs, k))
        groups = B // SUBLANES
        bcg = [bc3[:, g * SUBLANES:(g + 1) * SUBLANES, :] for g in range(groups)]
        qg = [q3[:, g * SUBLANES:(g + 1) * SUBLANES, :] for g in range(groups)]
        ag = [jnp.zeros((nb, SUBLANES, LANES), F32) for _ in range(groups)]
        for s in range(B):
            bc_s, k_s = bc3[:, s:s + 1, :], k3[:, s:s + 1, :]
            for g in range(s // SUBLANES, groups):
                e = jnp.exp2(bcg[g] - bc_s)
                a_col = jnp.sum(qg[g] * k_s * e, axis=2, keepdims=True)
                ag[g] = jnp.where(s_lane == s, a_col, ag[g])
        a = jnp.concatenate(ag, axis=1).reshape(L, LANES)
        ab = jnp.where(t_in >= s_lane, a, 0.0)[:, 0:B].astype(BF16)

        for j in range(nb):
            ds_ref[hh, j] = _dot_tn(vb[j * B:(j + 1) * B, :], kt[j * B:(j + 1) * B, :])
        st = st_ref[hh]
        for j in range(nb):
            sb_ref[hh, j] = st.astype(BF16)
            st = st * eb[j * B:j * B + 1, :] + ds_ref[hh, j]
        st_ref[hh] = st
        o = jnp.concatenate(
            [_dot(ab[j * B:(j + 1) * B, :], vb[j * B:(j + 1) * B, :])
             + _dot_nt(qt[j * B:(j + 1) * B, :], sb_ref[hh, j]) for j in range(nb)], axis=0)

        on = o * lax.rsqrt(jnp.mean(o * o, axis=1, keepdims=True) + LN_EPS) * ng_ref[:, vcols]
        yb_ref[:, vcols] = (on * _silu(bz_ref[:, vcols])).astype(yb_ref.dtype)

    @pl.when(c == pl.num_programs(2) - 1)
    def _():
        for hh in range(GLA_HEADS_PER_STEP):
            s_out_ref[0, hh] = st_ref[hh].T


def _gla(p_all, batch, seq, w_gate_up, b_gate, b_norm_g):
    L = GLA_STEP
    hps = GLA_HEADS_PER_STEP
    nc = seq // L
    t = batch * seq
    kw, vw = hps * B_DK, hps * B_DV

    def rows(b, h, c):
        return b * nc + c

    in_specs = [
        pl.BlockSpec((L, kw), lambda b, h, c: (rows(b, h, c), COL_BQ // kw + h)),
        pl.BlockSpec((L, kw), lambda b, h, c: (rows(b, h, c), COL_BK // kw + h)),
        pl.BlockSpec((L, vw), lambda b, h, c: (rows(b, h, c), COL_BV // vw + h)),
        pl.BlockSpec((L, vw), lambda b, h, c: (rows(b, h, c), COL_BZ // vw + h)),
        pl.BlockSpec((L, hps * LANES), lambda b, h, c: (rows(b, h, c), COL_GATE // (hps * LANES) + h)),
        pl.BlockSpec((GATE_RANK, kw), lambda b, h, c: (0, h)),
        pl.BlockSpec((1, kw), lambda b, h, c: (0, h)),
        pl.BlockSpec((1, vw), lambda b, h, c: (0, h)),
    ]
    out_specs = [
        pl.BlockSpec((L, vw), lambda b, h, c: (rows(b, h, c), h)),
        pl.BlockSpec((1, hps, B_DK, B_DV), lambda b, h, c: (b, h, 0, 0)),
    ]
    out_shape = [
        jax.ShapeDtypeStruct((t, B_WIDTH), BF16),
        jax.ShapeDtypeStruct((batch, B_HEADS, B_DK, B_DV), F32),
    ]
    return pl.pallas_call(
        _gla_kernel,
        grid=(batch, B_HEADS // hps, nc),
        in_specs=in_specs,
        out_specs=out_specs,
        out_shape=out_shape,
        scratch_shapes=[pltpu.VMEM((hps, B_DV, B_DK), F32),
                        pltpu.VMEM((hps, L // GLA_BLOCK, B_DV, B_DK), F32),
                        pltpu.VMEM((hps, L // GLA_BLOCK, B_DV, B_DK), BF16)],
        compiler_params=pltpu.CompilerParams(
            dimension_semantics=("parallel", "parallel", "arbitrary"), vmem_limit_bytes=VMEM_LIMIT),
        name="gla",
    )(p_all, p_all, p_all, p_all, p_all, w_gate_up, b_gate, b_norm_g)


DEC_TOKENS = 16


def _decode_kernel(qp_ref, kp_ref, av_ref, ao_ref, az_ref, bq_ref, bk_ref, bv_ref, bz_ref, g_ref,
                   sq0_ref, sq1_ref, sq2_ref, sk0_ref, sk1_ref, sk2_ref,
                   cwq_ref, cwk_ref, cbq_ref, cbk_ref, gb_ref, ang_ref, wg_ref, bgate_ref, bng_ref,
                   c_ref, n_ref, m_ref, s_ref,
                   ya_ref, yb_ref, c_out_ref, n_out_ref, m_out_ref, s_out_ref):
    TB = DEC_TOKENS
    h_idx = pl.program_id(1)

    def conv_silu(s0, s1, s2, x, w_ref, b_ref):
        acc = b_ref[...] + s0[...] * w_ref[0:1, :]
        acc = acc + s1[...] * w_ref[1:2, :]
        acc = acc + s2[...] * w_ref[2:3, :]
        acc = acc + x[...] * w_ref[3:4, :]
        return _silu(acc)

    q = conv_silu(sq0_ref, sq1_ref, sq2_ref, qp_ref, cwq_ref, cbq_ref)
    k = conv_silu(sk0_ref, sk1_ref, sk2_ref, kp_ref, cwk_ref, cbk_ref) * (A_DK ** -0.5)
    v = av_ref[...]
    g = g_ref[...]
    gbias = gb_ref[...]
    itil = g[:, 0:1] + gbias[:, 0:1]
    logf = _log_sigmoid(g[:, 1:2] + gbias[:, 1:2])
    lane_h = lax.broadcasted_iota(jnp.int32, (TB, A_HEADS), 1)
    m_prev = jnp.sum(jnp.where(lane_h == h_idx, m_ref[...], 0.0), axis=1, keepdims=True)
    inter = logf + m_prev
    m_t = jnp.maximum(inter, itil)
    w = jnp.exp(itil - m_t)
    decay = jnp.exp(inter - m_t)
    n_old = n_ref[:, 0, 0, :]
    s = jnp.sum(q * k, axis=1, keepdims=True) * w
    den = decay * jnp.sum(q * n_old, axis=1, keepdims=True) + s
    scale = 1.0 / jnp.maximum(jnp.abs(den), jnp.exp(-m_t))
    kw = k * w
    n_out_ref[:, 0, 0, :] = decay * n_old + kw
    m_out_ref[...] = jnp.broadcast_to(m_t, (TB, LANES))

    rows8 = lax.broadcasted_iota(jnp.int32, (TB, 1), 0)
    qb, kwb, vb = q.astype(BF16), kw.astype(BF16), v.astype(BF16)
    h_rows = []
    for t in range(TB):
        c_old = c_ref[t, 0]
        sel = rows8 == t
        qc = _dot(qb, c_old.astype(BF16))[t:t + 1, :]
        h_rows.append((decay[t:t + 1, :] * qc + s[t:t + 1, :] * v[t:t + 1, :]) * scale[t:t + 1, :])
        outer = _dot_tn(jnp.where(sel, kwb, jnp.zeros_like(kwb)), vb)
        c_out_ref[t, 0] = decay[t:t + 1, :] * c_old + outer
    h = jnp.concatenate(h_rows, axis=0)
    mu = jnp.mean(h, axis=1, keepdims=True)
    hc = h - mu
    var = jnp.mean(hc * hc, axis=1, keepdims=True)
    hn = hc * lax.rsqrt(var + LN_EPS) * ang_ref[...]
    ya_ref[...] = (hn * _sigmoid(ao_ref[...]) * _silu(az_ref[...])).astype(ya_ref.dtype)

    gq = bq_ref[...] * (B_DK ** -0.5)
    gk = bk_ref[...]
    gv = bv_ref[...]
    loga = _gla_gate_log(g, wg_ref[...], bgate_ref[...])
    eb = jnp.exp(loga)
    a = jnp.sum(gq * gk, axis=1, keepdims=True)
    ebt = jnp.concatenate([eb, jnp.zeros((LANES - TB, B_DK), F32)], axis=0).T
    qeb, gkb, gvb = (gq * eb).astype(BF16), gk.astype(BF16), gv.astype(BF16)
    o_rows = []
    for t in range(TB):
        s_old = s_ref[t, 0]
        sel = rows8 == t
        o_rows.append(_dot(qeb, s_old.astype(BF16))[t:t + 1, :] + a[t:t + 1, :] * gv[t:t + 1, :])
        outer = _dot_tn(jnp.where(sel, gkb, jnp.zeros_like(gkb)), gvb)
        s_out_ref[t, 0] = ebt[:, t:t + 1] * s_old + outer
    o = jnp.concatenate(o_rows, axis=0)
    on = o * lax.rsqrt(jnp.mean(o * o, axis=1, keepdims=True) + LN_EPS) * bng_ref[...]
    yb_ref[...] = (on * _silu(bz_ref[...])).astype(yb_ref.dtype)


def _decode(p_s, conv_state, conv_w, conv_b, gate_bias, a_norm_g, w_gate_up, b_gate, b_norm_g,
            c_state, n_state, m_state, s_state):
    nseq = p_s.shape[0]
    TB = DEC_TOKENS

    def pcol(base, width):
        return pl.BlockSpec((TB, width), lambda i, h: (i, base // width + h))

    def cstate(j, base):
        return pl.BlockSpec((TB, A_DK), lambda i, h: (i, (j * 2 * A_WIDTH + base) // A_DK + h))

    in_specs = [
        pcol(COL_Q, A_DK), pcol(COL_K, A_DK), pcol(COL_AV, A_DV), pcol(COL_AO, A_DV), pcol(COL_AZ, A_DV),
        pcol(COL_BQ, B_DK), pcol(COL_BK, B_DK), pcol(COL_BV, B_DV), pcol(COL_BZ, B_DV),
        pcol(COL_GATE, LANES),
        cstate(0, COL_Q), cstate(1, COL_Q), cstate(2, COL_Q),
        cstate(0, COL_K), cstate(1, COL_K), cstate(2, COL_K),
        pl.BlockSpec((CONV_W, A_DK), lambda i, h: (0, h)),
        pl.BlockSpec((CONV_W, A_DK), lambda i, h: (0, A_HEADS + h)),
        pl.BlockSpec((1, A_DK), lambda i, h: (0, h)),
        pl.BlockSpec((1, A_DK), lambda i, h: (0, A_HEADS + h)),
        pl.BlockSpec((1, LANES), lambda i, h: (0, h)),
        pl.BlockSpec((1, A_DV), lambda i, h: (0, h)),
        pl.BlockSpec((GATE_RANK, B_DK), lambda i, h: (0, h)),
        pl.BlockSpec((1, B_DK), lambda i, h: (0, h)),
        pl.BlockSpec((1, B_DV), lambda i, h: (0, h)),
        pl.BlockSpec((TB, 1, A_DK, A_DV), lambda i, h: (i, h, 0, 0)),
        pl.BlockSpec((TB, 1, 1, A_DK), lambda i, h: (i, h, 0, 0)),
        pl.BlockSpec((TB, A_HEADS), lambda i, h: (i, 0)),
        pl.BlockSpec((TB, 1, B_DK, B_DV), lambda i, h: (i, h, 0, 0)),
    ]
    out_specs = [
        pl.BlockSpec((TB, A_DV), lambda i, h: (i, h)),
        pl.BlockSpec((TB, B_DV), lambda i, h: (i, h)),
        pl.BlockSpec((TB, 1, A_DK, A_DV), lambda i, h: (i, h, 0, 0)),
        pl.BlockSpec((TB, 1, 1, A_DK), lambda i, h: (i, h, 0, 0)),
        pl.BlockSpec((TB, LANES), lambda i, h: (i, h)),
        pl.BlockSpec((TB, 1, B_DK, B_DV), lambda i, h: (i, h, 0, 0)),
    ]
    out_shape = [
        jax.ShapeDtypeStruct((nseq, A_WIDTH), BF16),
        jax.ShapeDtypeStruct((nseq, B_WIDTH), BF16),
        jax.ShapeDtypeStruct((nseq, A_HEADS, A_DK, A_DV), F32),
        jax.ShapeDtypeStruct((nseq, A_HEADS, 1, A_DK), F32),
        jax.ShapeDtypeStruct((nseq, A_HEADS * LANES), F32),
        jax.ShapeDtypeStruct((nseq, B_HEADS, B_DK, B_DV), F32),
    ]
    return pl.pallas_call(
        _decode_kernel,
        grid=(nseq // TB, A_HEADS),
        in_specs=in_specs,
        out_specs=out_specs,
        out_shape=out_shape,
        compiler_params=pltpu.CompilerParams(
            dimension_semantics=("parallel", "parallel"), vmem_limit_bytes=VMEM_LIMIT),
        name="decode",
    )(p_s, p_s, p_s, p_s, p_s, p_s, p_s, p_s, p_s, p_s,
      conv_state, conv_state, conv_state, conv_state, conv_state, conv_state,
      conv_w, conv_w, conv_b, conv_b, gate_bias, a_norm_g, w_gate_up, b_gate, b_norm_g,
      c_state, n_state, m_state, s_state)


def _out_kernel(ya_ref, yb_ref, ga_ref, gb_ref, x_ref, wpa_ref, wpb_ref, wo_ref, lng_ref, lnb_ref, o_ref):
    pa = _dot(ya_ref[...], wpa_ref[...])
    pb = _dot(yb_ref[...], wpb_ref[...])
    merged = _sigmoid(ga_ref[...]) * pa + _sigmoid(gb_ref[...]) * pb
    y = _dot(merged.astype(BF16), wo_ref[...])
    r = ALPHA * x_ref[...] + y
    mu = jnp.mean(r, axis=1, keepdims=True)
    rc = r - mu
    var = jnp.mean(rc * rc, axis=1, keepdims=True)
    o_ref[...] = rc * lax.rsqrt(var + LN_EPS) * lng_ref[...] + lnb_ref[...]


def _out(ya, yb, p_all, x, w_pa, w_pb, w_out, ln_g, ln_b, tm):
    t = x.shape[0]
    const = lambda i: (0, 0)
    single = pl.Buffered(1)
    in_specs = [
        pl.BlockSpec((tm, A_WIDTH), lambda i: (i, 0)),
        pl.BlockSpec((tm, B_WIDTH), lambda i: (i, 0)),
        pl.BlockSpec((tm, D_MODEL), lambda i: (i, COL_GA // D_MODEL)),
        pl.BlockSpec((tm, D_MODEL), lambda i: (i, COL_GB // D_MODEL)),
        pl.BlockSpec((tm, D_MODEL), lambda i: (i, 0)),
        pl.BlockSpec((A_WIDTH, D_MODEL), const, pipeline_mode=single),
        pl.BlockSpec((B_WIDTH, D_MODEL), const, pipeline_mode=single),
        pl.BlockSpec((D_MODEL, D_MODEL), const, pipeline_mode=single),
        pl.BlockSpec((1, D_MODEL), const),
        pl.BlockSpec((1, D_MODEL), const),
    ]
    return pl.pallas_call(
        _out_kernel,
        grid=(t // tm,),
        in_specs=in_specs,
        out_specs=pl.BlockSpec((tm, D_MODEL), lambda i: (i, 0)),
        out_shape=jax.ShapeDtypeStruct((t, D_MODEL), F32),
        compiler_params=pltpu.CompilerParams(
            dimension_semantics=("parallel",), vmem_limit_bytes=VMEM_LIMIT),
        name="outproj",
    )(ya, yb, p_all, p_all, x, w_pa, w_pb, w_out, ln_g, ln_b)


RELAYOUT_TN = 512
SRC_I = 3 * A_WIDTH
SRC_SHIFT_A = 2 * A_HEADS
SRC_BG = COL_BZ + SRC_SHIFT_A
SRC_SHIFT_B = SRC_SHIFT_A + GATE_RANK
GATE_SRC_ROWS = 4 * SUBLANES


def _relayout_kernel(src_ref, gif_ref, gb0_ref, gb1_ref, o_ref):
    j = pl.program_id(0)
    tn = RELAYOUT_TN
    n_main = COL_GATE // tn

    @pl.when(j < n_main)
    def _():
        o_ref[...] = src_ref[...].astype(BF16)

    @pl.when(j == n_main)
    def _():
        g = jnp.concatenate([gif_ref[...], gb0_ref[...], gb1_ref[...],
                             jnp.zeros((SUBLANES, src_ref.shape[1]), F32)], axis=0).astype(BF16)
        r = lax.broadcasted_iota(jnp.int32, (tn, GATE_SRC_ROWS), 0)
        c = lax.broadcasted_iota(jnp.int32, (tn, GATE_SRC_ROWS), 1)
        h, l = r // LANES, r % LANES
        sel = (((l == 0) & (c == h)) | ((l == 1) & (c == A_HEADS + h))
               | ((l >= 2) & (l < 2 + GATE_RANK) & (c == l + (2 * A_HEADS - 2))))
        o_ref[...] = _dot(sel.astype(BF16), g).astype(BF16)


def _relayout_w_in(w_in_t):
    n_src, k = w_in_t.shape
    tn = RELAYOUT_TN

    def src_row(j):
        shift = jnp.where(j < COL_AO // tn, 0, jnp.where(j < COL_BZ // tn, SRC_SHIFT_A, SRC_SHIFT_B))
        group = jnp.minimum(j * (tn // SUBLANES) + shift // SUBLANES, (n_src - tn) // SUBLANES)
        return group * SUBLANES

    return pl.pallas_call(
        _relayout_kernel,
        grid=(N_PROJ // tn,),
        in_specs=[pl.BlockSpec((pl.Element(tn), pl.Element(k)), lambda j: (src_row(j), 0)),
                  pl.BlockSpec((SUBLANES, k), lambda j: (SRC_I // SUBLANES, 0)),
                  pl.BlockSpec((SUBLANES, k), lambda j: (SRC_BG // SUBLANES, 0)),
                  pl.BlockSpec((SUBLANES, k), lambda j: (SRC_BG // SUBLANES + 1, 0))],
        out_specs=pl.BlockSpec((tn, k), lambda j: (j, 0)),
        out_shape=jax.ShapeDtypeStruct((N_PROJ, k), BF16),
        compiler_params=pltpu.CompilerParams(
            dimension_semantics=("parallel",), vmem_limit_bytes=VMEM_LIMIT),
        name="relayout",
    )(w_in_t, w_in_t, w_in_t, w_in_t)


def kernel(x_prompt, x_sample, state_mlstm_C, state_mlstm_n, state_mlstm_m, state_conv, state_gla_S,
           w_in, conv_w, conv_b, b_i, b_f, a_norm_g, w_gate_up, b_gate, b_norm_g, w_pa, w_pb, w_out,
           ln_g, ln_b):
    batch, seq, _ = x_prompt.shape
    nseq = x_sample.shape[0]
    assert w_in.shape[0] == 1, "single-layer step"
    d = 0

    def layer(a):
        return a.reshape(a.shape[1:])

    wp = _relayout_w_in(layer(w_in).T)
    wpa, wpb, wo = w_pa[d].astype(BF16), w_pb[d].astype(BF16), w_out[d].astype(BF16)
    cw = conv_w[d]
    cb = conv_b[d][None, :]
    gate_bias = jnp.zeros((A_HEADS, LANES), F32).at[:, 0].set(b_i[d]).at[:, 1].set(b_f[d]).reshape(1, -1)
    ang = a_norm_g[d][None, :]
    bng = b_norm_g[d][None, :]
    wg = w_gate_up[d]
    bgate = b_gate[d][None, :]
    lng, lnb = ln_g[d][None, :], ln_b[d][None, :]

    xp = x_prompt.reshape(batch * seq, D_MODEL)
    p_all = _proj(xp, wp, tm=1024, tn=1280)
    ya, p_c, p_n, p_m = _mlstm(p_all, batch, seq, cw, cb, gate_bias, ang)
    yb, p_s = _gla(p_all, batch, seq, wg, bgate, bng)
    y_prompt = _out(ya, yb, p_all, xp, wpa, wpb, wo, lng, lnb, tm=256).reshape(batch, seq, D_MODEL)
    p_conv = p_all.reshape(batch, seq, N_PROJ)[:, seq - (CONV_W - 1):, :2 * A_WIDTH]

    xs = x_sample.reshape(nseq, D_MODEL)
    ps_all = _proj(xs, wp, tm=nseq, tn=1280)
    conv_state = state_conv.reshape(nseq, (CONV_W - 1) * 2 * A_WIDTH)
    ya_s, yb_s, s_c, s_n, s_m, s_s = _decode(
        ps_all, conv_state, cw, cb, gate_bias, ang, wg, bgate, bng,
        layer(state_mlstm_C), state_mlstm_n.reshape(nseq, A_HEADS, 1, A_DK), layer(state_mlstm_m),
        layer(state_gla_S))
    y_sample = _out(ya_s, yb_s, ps_all, xs, wpa, wpb, wo, lng, lnb, tm=nseq).reshape(nseq, 1, D_MODEL)
    s_conv = jnp.concatenate([layer(state_conv)[:, 1:, :], ps_all[:, None, :2 * A_WIDTH]], axis=1)

    def stacked(a):
        return a.reshape((1,) + a.shape)

    return (y_prompt, y_sample,
            stacked(p_c), p_n.reshape(1, batch, A_HEADS, A_DK), stacked(p_m[:, :, 0, 0]),
            stacked(p_conv), stacked(p_s),
            stacked(s_c), s_n.reshape(1, nseq, A_HEADS, A_DK),
            stacked(s_m.reshape(nseq, A_HEADS, LANES)[:, :, 0]), stacked(s_conv), stacked(s_s))
```

```python
import functools

import jax
import jax.numpy as jnp
from jax import lax
from jax.experimental import pallas as pl
from jax.experimental.pallas import tpu as pltpu

F32 = jnp.float32
BF16 = jnp.bfloat16

D_MODEL = 2048
A_HEADS = 4
A_WIDTH = 1024
A_DK = 256
A_DV = 256
CONV_W = 4
B_HEADS = 4
B_WIDTH = 1024
B_KWIDTH = 512
B_DK = 128
B_DV = 256
GATE_RANK = 16
GATE_TAU = 16.0
ALPHA = 2.0 ** 0.25
LN_EPS = 1e-5
LOG2_E = 1.4426950408889634
LN_2 = 0.6931471805599453

LANES = 128
SUBLANES = 8
VMEM_LIMIT = 48 * 1024 * 1024

COL_Q = 0
COL_K = A_WIDTH
COL_BQ = 2 * A_WIDTH
COL_BK = COL_BQ + B_KWIDTH
COL_AV = COL_BK + B_KWIDTH
COL_AO = COL_AV + A_WIDTH
COL_AZ = COL_AO + A_WIDTH
COL_BV = COL_AZ + A_WIDTH
COL_BZ = COL_BV + B_WIDTH
COL_GA = COL_BZ + B_WIDTH
COL_GB = COL_GA + D_MODEL
COL_GATE = COL_GB + D_MODEL
N_PROJ = COL_GATE + A_HEADS * LANES

SRC_QK = 0
SRC_AV = 2 * A_WIDTH
SRC_I = SRC_AV + A_WIDTH
SRC_AO = SRC_I + 2 * A_HEADS
SRC_BQ = SRC_AO + 2 * A_WIDTH
SRC_BV = SRC_BQ + 2 * B_KWIDTH
SRC_BG = SRC_BV + B_WIDTH
SRC_BZ = SRC_BG + GATE_RANK
RELAYOUT_RUNS = ((COL_Q, COL_BQ, SRC_QK), (COL_BQ, COL_AV, SRC_BQ), (COL_AV, COL_AO, SRC_AV),
                 (COL_AO, COL_BV, SRC_AO), (COL_BV, COL_BZ, SRC_BV), (COL_BZ, COL_GATE, SRC_BZ))

MLSTM_CHUNK = 256
MLSTM_HEADS_PER_STEP = 4
GLA_STEP = 256
GLA_BLOCK = 16
GLA_HEADS_PER_STEP = 4


def _dot(a, b):
    return jnp.dot(a, b, preferred_element_type=F32)


def _dot_nt(a, b):
    return lax.dot_general(a, b, (((1,), (1,)), ((), ())), preferred_element_type=F32)


def _dot_tn(a, b):
    return lax.dot_general(a, b, (((0,), (0,)), ((), ())), preferred_element_type=F32)


def _mask_dot(mask_bf16, x):
    hi = x.astype(BF16)
    r1 = x - hi.astype(F32)
    mid = r1.astype(BF16)
    lo = (r1 - mid.astype(F32)).astype(BF16)
    return _dot(mask_bf16, hi) + _dot(mask_bf16, mid) + _dot(mask_bf16, lo)


def _log_sigmoid(z):
    return jnp.minimum(z, 0.0) - LN_2 * jnp.log2(1.0 + jnp.exp2(jnp.abs(z) * (-LOG2_E)))


def _sigmoid(z):
    return 1.0 / (1.0 + jnp.exp2(z * (-LOG2_E)))


def _silu(z):
    return z * _sigmoid(z)


def _proj_kernel(x_ref, w_ref, o_ref, xb_ref):
    @pl.when(pl.program_id(1) == 0)
    def _():
        xb_ref[...] = x_ref[...].astype(BF16)

    o_ref[...] = _dot_nt(xb_ref[...], w_ref[...])


def _proj(x, w_t, tm, tn):
    t, k = x.shape
    n = w_t.shape[0]
    return pl.pallas_call(
        _proj_kernel,
        grid=(t // tm, n // tn),
        in_specs=[pl.BlockSpec((tm, k), lambda i, j: (i, 0)),
                  pl.BlockSpec((tn, k), lambda i, j: (j, 0))],
        out_specs=pl.BlockSpec((tm, tn), lambda i, j: (i, j)),
        out_shape=jax.ShapeDtypeStruct((t, n), F32),
        scratch_shapes=[pltpu.VMEM((tm, k), BF16)],
        compiler_params=pltpu.CompilerParams(
            dimension_semantics=("parallel", "arbitrary"), vmem_limit_bytes=VMEM_LIMIT),
        name="proj",
    )(x, w_t)


def _mlstm_kernel(qp_ref, kp_ref, v_ref, ao_ref, az_ref, g_ref, cwq_ref, cwk_ref, cbq_ref, cbk_ref,
                  gb_ref, ng_ref, ya_ref, c_ref, n_ref, m_ref, tailq_ref, tailk_ref):
    L = MLSTM_CHUNK
    c = pl.program_id(2)

    @pl.when(c == 0)
    def _():
        c_ref[...] = jnp.zeros_like(c_ref)
        n_ref[...] = jnp.zeros_like(n_ref)
        m_ref[...] = jnp.zeros_like(m_ref)
        tailq_ref[...] = jnp.zeros_like(tailq_ref)
        tailk_ref[...] = jnp.zeros_like(tailk_ref)

    sub = lax.broadcasted_iota(jnp.int32, (1, SUBLANES, 1), 1)

    def conv_silu(x_ref, tail_ref, w_ref, b_ref):
        x = x_ref[...]
        width = x.shape[1]
        x3 = jnp.concatenate([tail_ref[...], x], axis=0).reshape(L // SUBLANES + 1, SUBLANES, width)
        acc = b_ref[...] + x * w_ref[CONV_W - 1:CONV_W, :]
        for j in range(1, CONV_W):
            rot = pltpu.roll(x3, j, axis=1)
            xs = jnp.where(sub < j, rot[:-1], rot[1:]).reshape(L, width)
            acc = acc + xs * w_ref[CONV_W - 1 - j:CONV_W - j, :]
        tail_ref[...] = x[L - SUBLANES:, :]
        return _silu(acc)

    q_all = conv_silu(qp_ref, tailq_ref, cwq_ref, cbq_ref)
    k_all = conv_silu(kp_ref, tailk_ref, cwk_ref, cbk_ref) * (A_DK ** -0.5)

    lane = lax.broadcasted_iota(jnp.int32, (L, LANES), 1)
    row = lax.broadcasted_iota(jnp.int32, (L, L), 0)
    col = lax.broadcasted_iota(jnp.int32, (L, L), 1)
    causal = row >= col
    causal_b = causal.astype(BF16)

    for hh in range(MLSTM_HEADS_PER_STEP):
        cols = slice(hh * A_DK, (hh + 1) * A_DK)
        q, k, v = q_all[:, cols], k_all[:, cols], v_ref[:, cols]

        g = g_ref[:, hh * LANES:(hh + 1) * LANES] + gb_ref[:, hh * LANES:(hh + 1) * LANES]
        g2 = jnp.where(lane == 1, _log_sigmoid(g), g)
        cum = _mask_dot(causal_b, g2)
        x2 = jnp.where(lane == 1, cum, g2)
        x2t = x2.T
        itil_col, b_col = x2[:, 0:1], x2[:, 1:2]
        itil_row, b_row = x2t[0:1, :], x2t[1:2, :]

        m_prev = m_ref[0, hh, 0:1, 0:1]
        dmat = jnp.where(causal, b_col - b_row + itil_row, -jnp.inf)
        inter = b_col + m_prev
        m_t = jnp.maximum(inter, jnp.max(dmat, axis=1, keepdims=True))
        w = jnp.exp(dmat - m_t)
        decay = jnp.exp(inter - m_t)

        qb, kb, vb = q.astype(BF16), k.astype(BF16), v.astype(BF16)
        c_old = c_ref[0, hh]
        n_old = n_ref[0, hh]
        s = _dot_nt(qb, kb) * w
        num = decay * _dot(qb, c_old.astype(BF16)) + _dot(s.astype(BF16), vb)
        den = decay * jnp.sum(q * n_old, axis=1, keepdims=True) + jnp.sum(s, axis=1, keepdims=True)
        inv = 1.0 / jnp.maximum(jnp.abs(den), jnp.exp(-m_t))

        m_new = m_t[L - 1:L, :]
        b_last = b_col[L - 1:L, :]
        wk = jnp.exp(b_last - b_col + itil_col - m_new)
        dec = jnp.exp(b_last + m_prev - m_new)
        kw = k * wk
        c_ref[0, hh] = dec * c_old + _dot(kw.T.astype(BF16), vb)
        n_ref[0, hh] = dec * n_old + jnp.sum(kw, axis=0, keepdims=True)
        m_ref[0, hh] = jnp.broadcast_to(m_new, (1, LANES))

        mu = jnp.mean(num, axis=1, keepdims=True)
        hc = num - mu
        var = jnp.mean(hc * hc, axis=1, keepdims=True)
        hn = hc * (inv * lax.rsqrt(var * (inv * inv) + LN_EPS)) * ng_ref[:, cols]
        ya_ref[:, cols] = (hn * _sigmoid(ao_ref[:, cols]) * _silu(az_ref[:, cols])).astype(ya_ref.dtype)


def _mlstm(p_all, batch, seq, conv_w, conv_b, gate_bias, a_norm_g):
    L = MLSTM_CHUNK
    hps = MLSTM_HEADS_PER_STEP
    nc = seq // L
    t = batch * seq
    wd = hps * A_DK
    k_off = A_WIDTH // wd

    def rows(b, h, c):
        return b * nc + c

    def pcol(base):
        return pl.BlockSpec((L, wd), lambda b, h, c: (rows(b, h, c), base // wd + h))

    in_specs = [
        pcol(COL_Q), pcol(COL_K), pcol(COL_AV), pcol(COL_AO), pcol(COL_AZ),
        pl.BlockSpec((L, hps * LANES), lambda b, h, c: (rows(b, h, c), COL_GATE // (hps * LANES) + h)),
        pl.BlockSpec((CONV_W, wd), lambda b, h, c: (0, h)),
        pl.BlockSpec((CONV_W, wd), lambda b, h, c: (0, k_off + h)),
        pl.BlockSpec((1, wd), lambda b, h, c: (0, h)),
        pl.BlockSpec((1, wd), lambda b, h, c: (0, k_off + h)),
        pl.BlockSpec((1, hps * LANES), lambda b, h, c: (0, h)),
        pl.BlockSpec((1, wd), lambda b, h, c: (0, h)),
    ]
    out_specs = [
        pl.BlockSpec((L, wd), lambda b, h, c: (rows(b, h, c), h)),
        pl.BlockSpec((1, hps, A_DK, A_DV), lambda b, h, c: (b, h, 0, 0)),
        pl.BlockSpec((1, hps, 1, A_DK), lambda b, h, c: (b, h, 0, 0)),
        pl.BlockSpec((1, hps, 1, LANES), lambda b, h, c: (b, h, 0, 0)),
    ]
    out_shape = [
        jax.ShapeDtypeStruct((t, A_WIDTH), BF16),
        jax.ShapeDtypeStruct((batch, A_HEADS, A_DK, A_DV), F32),
        jax.ShapeDtypeStruct((batch, A_HEADS, 1, A_DK), F32),
        jax.ShapeDtypeStruct((batch, A_HEADS, 1, LANES), F32),
    ]
    return pl.pallas_call(
        _mlstm_kernel,
        grid=(batch, A_HEADS // hps, nc),
        in_specs=in_specs,
        out_specs=out_specs,
        out_shape=out_shape,
        scratch_shapes=[pltpu.VMEM((SUBLANES, wd), F32),
                        pltpu.VMEM((SUBLANES, wd), F32)],
        compiler_params=pltpu.CompilerParams(
            dimension_semantics=("parallel", "parallel", "arbitrary"), vmem_limit_bytes=VMEM_LIMIT),
        name="mlstm",
    )(p_all, p_all, p_all, p_all, p_all, p_all, conv_w, conv_w, conv_b, conv_b, gate_bias, a_norm_g)


def _gla_gate_log(bg_tile, wg, bgate):
    lr = bg_tile[:, 2:2 + GATE_RANK]
    z = _dot(lr.astype(BF16), wg.astype(BF16)) + bgate
    return _log_sigmoid(z) / GATE_TAU


def _gla_kernel(q_ref, k_ref, v_ref, bz_ref, g_ref, wg_ref, bgate_ref, ng_ref,
                yb_ref, s_out_ref, st_ref, ds_ref, sb_ref):
    L = GLA_STEP
    B = GLA_BLOCK
    nb = L // B
    c = pl.program_id(2)

    @pl.when(c == 0)
    def _():
        st_ref[...] = jnp.zeros_like(st_ref)

    row = lax.broadcasted_iota(jnp.int32, (L, L), 0)
    col = lax.broadcasted_iota(jnp.int32, (L, L), 1)
    same = (row // B) == (col // B)
    tri_b = (same & (row >= col)).astype(BF16)
    same_b = same.astype(BF16)
    t_in = lax.broadcasted_iota(jnp.int32, (L, 1), 0) % B
    s_lane = lax.broadcasted_iota(jnp.int32, (1, LANES), 1)

    for hh in range(GLA_HEADS_PER_STEP):
        kcols = slice(hh * B_DK, (hh + 1) * B_DK)
        vcols = slice(hh * B_DV, (hh + 1) * B_DV)
        loga = _gla_gate_log(g_ref[:, hh * LANES:(hh + 1) * LANES], wg_ref[:, kcols], bgate_ref[:, kcols])
        bc = _mask_dot(tri_b, loga)
        bl = _mask_dot(same_b, loga)
        k = k_ref[:, kcols]
        qs = q_ref[:, kcols] * (B_DK ** -0.5)
        qt = (qs * jnp.exp(bc)).astype(BF16)
        kt = (k * jnp.exp(bl - bc)).astype(BF16)
        eb = jnp.exp(bl)
        vb = v_ref[:, vcols].astype(BF16)

        bc3, q3, k3 = (x.reshape(nb, B, B_DK) for x in (bc * LOG2_E, qs, k))
        groups = B // SUBLANES
        bcg = [bc3[:, g * SUBLANES:(g + 1) * SUBLANES, :] for g in range(groups)]
        qg = [q3[:, g * SUBLANES:(g + 1) * SUBLANES, :] for g in range(groups)]
        ag = [jnp.zeros((nb, SUBLANES, LANES), F32) for _ in range(groups)]
        for s in range(B):
            bc_s, k_s = bc3[:, s:s + 1, :], k3[:, s:s + 1, :]
            for g in range(s // SUBLANES, groups):
                e = jnp.exp2(bcg[g] - bc_s)
                a_col = jnp.sum(qg[g] * k_s * e, axis=2, keepdims=True)
                ag[g] = jnp.where(s_lane == s, a_col, ag[g])
        a = jnp.concatenate(ag, axis=1).reshape(L, LANES)
        ab = jnp.where(t_in >= s_lane, a, 0.0)[:, 0:B].astype(BF16)

        for j in range(nb):
            ds_ref[hh, j] = _dot_tn(vb[j * B:(j + 1) * B, :], kt[j * B:(j + 1) * B, :])
        st = st_ref[hh]
        for j in range(nb):
            sb_ref[hh, j] = st.astype(BF16)
            st = st * eb[j * B:j * B + 1, :] + ds_ref[hh, j]
        st_ref[hh] = st
        o = jnp.concatenate(
            [_dot(ab[j * B:(j + 1) * B, :], vb[j * B:(j + 1) * B, :])
             + _dot_nt(qt[j * B:(j + 1) * B, :], sb_ref[hh, j]) for j in range(nb)], axis=0)

        on = o * lax.rsqrt(jnp.mean(o * o, axis=1, keepdims=True) + LN_EPS) * ng_ref[:, vcols]
        yb_ref[:, vcols] = (on * _silu(bz_ref[:, vcols])).astype(yb_ref.dtype)

    @pl.when(c == pl.num_programs(2) - 1)
    def _():
        for hh in range(GLA_HEADS_PER_STEP):
            s_out_ref[0, hh] = st_ref[hh].T


def _gla(p_all, batch, seq, w_gate_up, b_gate, b_norm_g):
    L = GLA_STEP
    hps = GLA_HEADS_PER_STEP
    nc = seq // L
    t = batch * seq
    kw, vw = hps * B_DK, hps * B_DV

    def rows(b, h, c):
        return b * nc + c

    in_specs = [
        pl.BlockSpec((L, kw), lambda b, h, c: (rows(b, h, c), COL_BQ // kw + h)),
        pl.BlockSpec((L, kw), lambda b, h, c: (rows(b, h, c), COL_BK // kw + h)),
        pl.BlockSpec((L, vw), lambda b, h, c: (rows(b, h, c), COL_BV // vw + h)),
        pl.BlockSpec((L, vw), lambda b, h, c: (rows(b, h, c), COL_BZ // vw + h)),
        pl.BlockSpec((L, hps * LANES), lambda b, h, c: (rows(b, h, c), COL_GATE // (hps * LANES) + h)),
        pl.BlockSpec((GATE_RANK, kw), lambda b, h, c: (0, h)),
        pl.BlockSpec((1, kw), lambda b, h, c: (0, h)),
        pl.BlockSpec((1, vw), lambda b, h, c: (0, h)),
    ]
    out_specs = [
        pl.BlockSpec((L, vw), lambda b, h, c: (rows(b, h, c), h)),
        pl.BlockSpec((1, hps, B_DK, B_DV), lambda b, h, c: (b, h, 0, 0)),
    ]
    out_shape = [
        jax.ShapeDtypeStruct((t, B_WIDTH), BF16),
        jax.ShapeDtypeStruct((batch, B_HEADS, B_DK, B_DV), F32),
    ]
    return pl.pallas_call(
        _gla_kernel,
        grid=(batch, B_HEADS // hps, nc),
        in_specs=in_specs,
        out_specs=out_specs,
        out_shape=out_shape,
        scratch_shapes=[pltpu.VMEM((hps, B_DV, B_DK), F32),
                        pltpu.VMEM((hps, L // GLA_BLOCK, B_DV, B_DK), F32),
                        pltpu.VMEM((hps, L // GLA_BLOCK, B_DV, B_DK), BF16)],
        compiler_params=pltpu.CompilerParams(
            dimension_semantics=("parallel", "parallel", "arbitrary"), vmem_limit_bytes=VMEM_LIMIT),
        name="gla",
    )(p_all, p_all, p_all, p_all, p_all, w_gate_up, b_gate, b_norm_g)


DEC_TOKENS = 16


def _decode_kernel(qp_ref, kp_ref, av_ref, ao_ref, az_ref, bq_ref, bk_ref, bv_ref, bz_ref, g_ref,
                   sq0_ref, sq1_ref, sq2_ref, sk0_ref, sk1_ref, sk2_ref,
                   cwq_ref, cwk_ref, cbq_ref, cbk_ref, gb_ref, ang_ref, wg_ref, bgate_ref, bng_ref,
                   c_ref, n_ref, m_ref, s_ref,
                   ya_ref, yb_ref, c_out_ref, n_out_ref, m_out_ref, s_out_ref):
    TB = DEC_TOKENS
    h_idx = pl.program_id(1)

    def conv_silu(s0, s1, s2, x, w_ref, b_ref):
        acc = b_ref[...] + s0[...] * w_ref[0:1, :]
        acc = acc + s1[...] * w_ref[1:2, :]
        acc = acc + s2[...] * w_ref[2:3, :]
        acc = acc + x[...] * w_ref[3:4, :]
        return _silu(acc)

    q = conv_silu(sq0_ref, sq1_ref, sq2_ref, qp_ref, cwq_ref, cbq_ref)
    k = conv_silu(sk0_ref, sk1_ref, sk2_ref, kp_ref, cwk_ref, cbk_ref) * (A_DK ** -0.5)
    v = av_ref[...]
    g = g_ref[...]
    gbias = gb_ref[...]
    itil = g[:, 0:1] + gbias[:, 0:1]
    logf = _log_sigmoid(g[:, 1:2] + gbias[:, 1:2])
    lane_h = lax.broadcasted_iota(jnp.int32, (TB, A_HEADS), 1)
    m_prev = jnp.sum(jnp.where(lane_h == h_idx, m_ref[...], 0.0), axis=1, keepdims=True)
    inter = logf + m_prev
    m_t = jnp.maximum(inter, itil)
    w = jnp.exp(itil - m_t)
    decay = jnp.exp(inter - m_t)
    n_old = n_ref[:, 0, 0, :]
    s = jnp.sum(q * k, axis=1, keepdims=True) * w
    den = decay * jnp.sum(q * n_old, axis=1, keepdims=True) + s
    scale = 1.0 / jnp.maximum(jnp.abs(den), jnp.exp(-m_t))
    kw = k * w
    n_out_ref[:, 0, 0, :] = decay * n_old + kw
    m_out_ref[...] = jnp.broadcast_to(m_t, (TB, LANES))

    rows8 = lax.broadcasted_iota(jnp.int32, (TB, 1), 0)
    qb, kwb, vb = q.astype(BF16), kw.astype(BF16), v.astype(BF16)
    h_rows = []
    for t in range(TB):
        c_old = c_ref[t, 0]
        sel = rows8 == t
        qc = _dot(qb, c_old.astype(BF16))[t:t + 1, :]
        h_rows.append((decay[t:t + 1, :] * qc + s[t:t + 1, :] * v[t:t + 1, :]) * scale[t:t + 1, :])
        outer = _dot_tn(jnp.where(sel, kwb, jnp.zeros_like(kwb)), vb)
        c_out_ref[t, 0] = decay[t:t + 1, :] * c_old + outer
    h = jnp.concatenate(h_rows, axis=0)
    mu = jnp.mean(h, axis=1, keepdims=True)
    hc = h - mu
    var = jnp.mean(hc * hc, axis=1, keepdims=True)
    hn = hc * lax.rsqrt(var + LN_EPS) * ang_ref[...]
    ya_ref[...] = (hn * _sigmoid(ao_ref[...]) * _silu(az_ref[...])).astype(ya_ref.dtype)

    gq = bq_ref[...] * (B_DK ** -0.5)
    gk = bk_ref[...]
    gv = bv_ref[...]
    loga = _gla_gate_log(g, wg_ref[...], bgate_ref[...])
    eb = jnp.exp(loga)
    a = jnp.sum(gq * gk, axis=1, keepdims=True)
    ebt = jnp.concatenate([eb, jnp.zeros((LANES - TB, B_DK), F32)], axis=0).T
    qeb, gkb, gvb = (gq * eb).astype(BF16), gk.astype(BF16), gv.astype(BF16)
    o_rows = []
    for t in range(TB):
        s_old = s_ref[t, 0]
        sel = rows8 == t
        o_rows.append(_dot(qeb, s_old.astype(BF16))[t:t + 1, :] + a[t:t + 1, :] * gv[t:t + 1, :])
        outer = _dot_tn(jnp.where(sel, gkb, jnp.zeros_like(gkb)), gvb)
        s_out_ref[t, 0] = ebt[:, t:t + 1] * s_old + outer
    o = jnp.concatenate(o_rows, axis=0)
    on = o * lax.rsqrt(jnp.mean(o * o, axis=1, keepdims=True) + LN_EPS) * bng_ref[...]
    yb_ref[...] = (on * _silu(bz_ref[...])).astype(yb_ref.dtype)


def _decode(p_s, conv_state, conv_w, conv_b, gate_bias, a_norm_g, w_gate_up, b_gate, b_norm_g,
            c_state, n_state, m_state, s_state):
    nseq = p_s.shape[0]
    TB = DEC_TOKENS

    def pcol(base, width):
        return pl.BlockSpec((TB, width), lambda i, h: (i, base // width + h))

    def cstate(j, base):
        return pl.BlockSpec((TB, A_DK), lambda i, h: (i, (j * 2 * A_WIDTH + base) // A_DK + h))

    in_specs = [
        pcol(COL_Q, A_DK), pcol(COL_K, A_DK), pcol(COL_AV, A_DV), pcol(COL_AO, A_DV), pcol(COL_AZ, A_DV),
        pcol(COL_BQ, B_DK), pcol(COL_BK, B_DK), pcol(COL_BV, B_DV), pcol(COL_BZ, B_DV),
        pcol(COL_GATE, LANES),
        cstate(0, COL_Q), cstate(1, COL_Q), cstate(2, COL_Q),
        cstate(0, COL_K), cstate(1, COL_K), cstate(2, COL_K),
        pl.BlockSpec((CONV_W, A_DK), lambda i, h: (0, h)),
        pl.BlockSpec((CONV_W, A_DK), lambda i, h: (0, A_HEADS + h)),
        pl.BlockSpec((1, A_DK), lambda i, h: (0, h)),
        pl.BlockSpec((1, A_DK), lambda i, h: (0, A_HEADS + h)),
        pl.BlockSpec((1, LANES), lambda i, h: (0, h)),
        pl.BlockSpec((1, A_DV), lambda i, h: (0, h)),
        pl.BlockSpec((GATE_RANK, B_DK), lambda i, h: (0, h)),
        pl.BlockSpec((1, B_DK), lambda i, h: (0, h)),
        pl.BlockSpec((1, B_DV), lambda i, h: (0, h)),
        pl.BlockSpec((TB, 1, A_DK, A_DV), lambda i, h: (i, h, 0, 0)),
        pl.BlockSpec((TB, 1, 1, A_DK), lambda i, h: (i, h, 0, 0)),
        pl.BlockSpec((TB, A_HEADS), lambda i, h: (i, 0)),
        pl.BlockSpec((TB, 1, B_DK, B_DV), lambda i, h: (i, h, 0, 0)),
    ]
    out_specs = [
        pl.BlockSpec((TB, A_DV), lambda i, h: (i, h)),
        pl.BlockSpec((TB, B_DV), lambda i, h: (i, h)),
        pl.BlockSpec((TB, 1, A_DK, A_DV), lambda i, h: (i, h, 0, 0)),
        pl.BlockSpec((TB, 1, 1, A_DK), lambda i, h: (i, h, 0, 0)),
        pl.BlockSpec((TB, LANES), lambda i, h: (i, h)),
        pl.BlockSpec((TB, 1, B_DK, B_DV), lambda i, h: (i, h, 0, 0)),
    ]
    out_shape = [
        jax.ShapeDtypeStruct((nseq, A_WIDTH), BF16),
        jax.ShapeDtypeStruct((nseq, B_WIDTH), BF16),
        jax.ShapeDtypeStruct((nseq, A_HEADS, A_DK, A_DV), F32),
        jax.ShapeDtypeStruct((nseq, A_HEADS, 1, A_DK), F32),
        jax.ShapeDtypeStruct((nseq, A_HEADS * LANES), F32),
        jax.ShapeDtypeStruct((nseq, B_HEADS, B_DK, B_DV), F32),
    ]
    return pl.pallas_call(
        _decode_kernel,
        grid=(nseq // TB, A_HEADS),
        in_specs=in_specs,
        out_specs=out_specs,
        out_shape=out_shape,
        compiler_params=pltpu.CompilerParams(
            dimension_semantics=("parallel", "parallel"), vmem_limit_bytes=VMEM_LIMIT),
        name="decode",
    )(p_s, p_s, p_s, p_s, p_s, p_s, p_s, p_s, p_s, p_s,
      conv_state, conv_state, conv_state, conv_state, conv_state, conv_state,
      conv_w, conv_w, conv_b, conv_b, gate_bias, a_norm_g, w_gate_up, b_gate, b_norm_g,
      c_state, n_state, m_state, s_state)


def _out_kernel(ya_ref, yb_ref, ga_ref, gb_ref, x_ref, wpa_ref, wpb_ref, wo_ref, lng_ref, lnb_ref, o_ref):
    pa = _dot(ya_ref[...], wpa_ref[...])
    pb = _dot(yb_ref[...], wpb_ref[...])
    merged = _sigmoid(ga_ref[...]) * pa + _sigmoid(gb_ref[...]) * pb
    y = _dot(merged.astype(BF16), wo_ref[...])
    r = ALPHA * x_ref[...] + y
    mu = jnp.mean(r, axis=1, keepdims=True)
    rc = r - mu
    var = jnp.mean(rc * rc, axis=1, keepdims=True)
    o_ref[...] = rc * lax.rsqrt(var + LN_EPS) * lng_ref[...] + lnb_ref[...]


def _out(ya, yb, p_all, x, w_pa, w_pb, w_out, ln_g, ln_b, tm):
    t = x.shape[0]
    const = lambda i: (0, 0)
    single = pl.Buffered(1)
    in_specs = [
        pl.BlockSpec((tm, A_WIDTH), lambda i: (i, 0)),
        pl.BlockSpec((tm, B_WIDTH), lambda i: (i, 0)),
        pl.BlockSpec((tm, D_MODEL), lambda i: (i, COL_GA // D_MODEL)),
        pl.BlockSpec((tm, D_MODEL), lambda i: (i, COL_GB // D_MODEL)),
        pl.BlockSpec((tm, D_MODEL), lambda i: (i, 0)),
        pl.BlockSpec((A_WIDTH, D_MODEL), const, pipeline_mode=single),
        pl.BlockSpec((B_WIDTH, D_MODEL), const, pipeline_mode=single),
        pl.BlockSpec((D_MODEL, D_MODEL), const, pipeline_mode=single),
        pl.BlockSpec((1, D_MODEL), const),
        pl.BlockSpec((1, D_MODEL), const),
    ]
    return pl.pallas_call(
        _out_kernel,
        grid=(t // tm,),
        in_specs=in_specs,
        out_specs=pl.BlockSpec((tm, D_MODEL), lambda i: (i, 0)),
        out_shape=jax.ShapeDtypeStruct((t, D_MODEL), F32),
        compiler_params=pltpu.CompilerParams(
            dimension_semantics=("parallel",), vmem_limit_bytes=VMEM_LIMIT),
        name="outproj",
    )(ya, yb, p_all, p_all, x, w_pa, w_pb, w_out, ln_g, ln_b)


RELAYOUT_TN = 512
GATE_SRC_ROWS = 4 * SUBLANES


def _relayout_kernel(src_ref, gif_ref, gb0_ref, gb1_ref, o_ref):
    j = pl.program_id(0)
    tn = RELAYOUT_TN
    n_main = COL_GATE // tn

    @pl.when(j < n_main)
    def _():
        o_ref[...] = src_ref[...].astype(BF16)

    @pl.when(j == n_main)
    def _():
        g = jnp.concatenate([gif_ref[...], gb0_ref[...], gb1_ref[...],
                             jnp.zeros((SUBLANES, src_ref.shape[1]), F32)], axis=0).astype(BF16)
        r = lax.broadcasted_iota(jnp.int32, (tn, GATE_SRC_ROWS), 0)
        c = lax.broadcasted_iota(jnp.int32, (tn, GATE_SRC_ROWS), 1)
        h, l = r // LANES, r % LANES
        sel = (((l == 0) & (c == h)) | ((l == 1) & (c == A_HEADS + h))
               | ((l >= 2) & (l < 2 + GATE_RANK) & (c == l + (2 * A_HEADS - 2))))
        o_ref[...] = _dot(sel.astype(BF16), g).astype(BF16)


def _relayout_w_in(w_in_t):
    n_src, k = w_in_t.shape
    tn = RELAYOUT_TN

    def src_row(j):
        shift = 0
        for first, last, src in RELAYOUT_RUNS:
            assert first % tn == 0 and last % tn == 0 and (src - first) % SUBLANES == 0
            shift = jnp.where((j >= first // tn) & (j < last // tn), (src - first) // SUBLANES, shift)
        group = jnp.minimum(j * (tn // SUBLANES) + shift, (n_src - tn) // SUBLANES)
        return group * SUBLANES

    return pl.pallas_call(
        _relayout_kernel,
        grid=(N_PROJ // tn,),
        in_specs=[pl.BlockSpec((pl.Element(tn), pl.Element(k)), lambda j: (src_row(j), 0)),
                  pl.BlockSpec((SUBLANES, k), lambda j: (SRC_I // SUBLANES, 0)),
                  pl.BlockSpec((SUBLANES, k), lambda j: (SRC_BG // SUBLANES, 0)),
                  pl.BlockSpec((SUBLANES, k), lambda j: (SRC_BG // SUBLANES + 1, 0))],
        out_specs=pl.BlockSpec((tn, k), lambda j: (j, 0)),
        out_shape=jax.ShapeDtypeStruct((N_PROJ, k), BF16),
        compiler_params=pltpu.CompilerParams(
            dimension_semantics=("parallel",), vmem_limit_bytes=VMEM_LIMIT),
        name="relayout",
    )(w_in_t, w_in_t, w_in_t, w_in_t)


def kernel(x_prompt, x_sample, state_mlstm_C, state_mlstm_n, state_mlstm_m, state_conv, state_gla_S,
           w_in, conv_w, conv_b, b_i, b_f, a_norm_g, w_gate_up, b_gate, b_norm_g, w_pa, w_pb, w_out,
           ln_g, ln_b):
    batch, seq, _ = x_prompt.shape
    nseq = x_sample.shape[0]
    assert w_in.shape[0] == 1, "single-layer step"
    d = 0

    def layer(a):
        return a.reshape(a.shape[1:])

    wp = _relayout_w_in(layer(w_in).T)
    wpa, wpb, wo = w_pa[d].astype(BF16), w_pb[d].astype(BF16), w_out[d].astype(BF16)
    cw = conv_w[d]
    cb = conv_b[d][None, :]
    gate_bias = jnp.zeros((A_HEADS, LANES), F32).at[:, 0].set(b_i[d]).at[:, 1].set(b_f[d]).reshape(1, -1)
    ang = a_norm_g[d][None, :]
    bng = b_norm_g[d][None, :]
    wg = w_gate_up[d]
    bgate = b_gate[d][None, :]
    lng, lnb = ln_g[d][None, :], ln_b[d][None, :]

    xp = x_prompt.reshape(batch * seq, D_MODEL)
    p_all = _proj(xp, wp, tm=1024, tn=1280)
    ya, p_c, p_n, p_m = _mlstm(p_all, batch, seq, cw, cb, gate_bias, ang)
    yb, p_s = _gla(p_all, batch, seq, wg, bgate, bng)
    y_prompt = _out(ya, yb, p_all, xp, wpa, wpb, wo, lng, lnb, tm=256).reshape(batch, seq, D_MODEL)
    p_conv = p_all.reshape(batch, seq, N_PROJ)[:, seq - (CONV_W - 1):, :2 * A_WIDTH]

    xs = x_sample.reshape(nseq, D_MODEL)
    ps_all = _proj(xs, wp, tm=nseq, tn=1280)
    conv_state = state_conv.reshape(nseq, (CONV_W - 1) * 2 * A_WIDTH)
    ya_s, yb_s, s_c, s_n, s_m, s_s = _decode(
        ps_all, conv_state, cw, cb, gate_bias, ang, wg, bgate, bng,
        layer(state_mlstm_C), state_mlstm_n.reshape(nseq, A_HEADS, 1, A_DK), layer(state_mlstm_m),
        layer(state_gla_S))
    y_sample = _out(ya_s, yb_s, ps_all, xs, wpa, wpb, wo, lng, lnb, tm=nseq).reshape(nseq, 1, D_MODEL)
    s_conv = jnp.concatenate([layer(state_conv)[:, 1:, :], ps_all[:, None, :2 * A_WIDTH]], axis=1)

    def stacked(a):
        return a.reshape((1,) + a.shape)

    return (y_prompt, y_sample,
            stacked(p_c), p_n.reshape(1, batch, A_HEADS, A_DK), stacked(p_m[:, :, 0, 0]),
            stacked(p_conv), stacked(p_s),
            stacked(s_c), s_n.reshape(1, nseq, A_HEADS, A_DK),
            stacked(s_m.reshape(nseq, A_HEADS, LANES)[:, :, 0]), stacked(s_conv), stacked(s_s))
```

```python
import functools

import jax
import jax.numpy as jnp
from jax import lax
from jax.experimental import pallas as pl
from jax.experimental.pallas import tpu as pltpu

F32 = jnp.float32
BF16 = jnp.bfloat16

D_MODEL = 2048
A_HEADS = 4
A_WIDTH = 1024
A_DK = 256
A_DV = 256
CONV_W = 4
B_HEADS = 4
B_WIDTH = 1024
B_KWIDTH = 512
B_DK = 128
B_DV = 256
GATE_RANK = 16
GATE_TAU = 16.0
ALPHA = 2.0 ** 0.25
LN_EPS = 1e-5
LOG2_E = 1.4426950408889634
LN_2 = 0.6931471805599453

LANES = 128
SUBLANES = 8
VMEM_LIMIT = 48 * 1024 * 1024

COL_Q = 0
COL_K = A_WIDTH
COL_BQ = 2 * A_WIDTH
COL_BK = COL_BQ + B_KWIDTH
COL_AV = COL_BK + B_KWIDTH
COL_AO = COL_AV + A_WIDTH
COL_AZ = COL_AO + A_WIDTH
COL_BV = COL_AZ + A_WIDTH
COL_BZ = COL_BV + B_WIDTH
COL_GA = COL_BZ + B_WIDTH
COL_GB = COL_GA + D_MODEL
COL_GATE = COL_GB + D_MODEL
N_PROJ = COL_GATE + A_HEADS * LANES

SRC_QK = 0
SRC_AV = 2 * A_WIDTH
SRC_I = SRC_AV + A_WIDTH
SRC_AO = SRC_I + 2 * A_HEADS
SRC_BQ = SRC_AO + 2 * A_WIDTH
SRC_BV = SRC_BQ + 2 * B_KWIDTH
SRC_BG = SRC_BV + B_WIDTH
SRC_BZ = SRC_BG + GATE_RANK
RELAYOUT_RUNS = ((COL_Q, COL_BQ, SRC_QK), (COL_BQ, COL_AV, SRC_BQ), (COL_AV, COL_AO, SRC_AV),
                 (COL_AO, COL_BV, SRC_AO), (COL_BV, COL_BZ, SRC_BV), (COL_BZ, COL_GATE, SRC_BZ))

MLSTM_CHUNK = 256
MLSTM_HEADS_PER_STEP = 4
GLA_STEP = 256
GLA_BLOCK = 16
GLA_HEADS_PER_STEP = 4


def _dot(a, b):
    return jnp.dot(a, b, preferred_element_type=F32)


def _dot_nt(a, b):
    return lax.dot_general(a, b, (((1,), (1,)), ((), ())), preferred_element_type=F32)


def _dot_tn(a, b):
    return lax.dot_general(a, b, (((0,), (0,)), ((), ())), preferred_element_type=F32)


def _mask_dot(mask_bf16, x):
    hi = x.astype(BF16)
    r1 = x - hi.astype(F32)
    mid = r1.astype(BF16)
    lo = (r1 - mid.astype(F32)).astype(BF16)
    return _dot(mask_bf16, hi) + _dot(mask_bf16, mid) + _dot(mask_bf16, lo)


def _log_sigmoid(z):
    return jnp.minimum(z, 0.0) - LN_2 * jnp.log2(1.0 + jnp.exp2(jnp.abs(z) * (-LOG2_E)))


def _sigmoid(z):
    return 1.0 / (1.0 + jnp.exp2(z * (-LOG2_E)))


def _silu(z):
    return z * _sigmoid(z)


def _proj_kernel(x_ref, w_ref, o_ref, xb_ref):
    @pl.when(pl.program_id(1) == 0)
    def _():
        xb_ref[...] = x_ref[...].astype(BF16)

    o_ref[...] = _dot_nt(xb_ref[...], w_ref[...])


def _proj(x, w_t, tm, tn):
    t, k = x.shape
    n = w_t.shape[0]
    return pl.pallas_call(
        _proj_kernel,
        grid=(t // tm, n // tn),
        in_specs=[pl.BlockSpec((tm, k), lambda i, j: (i, 0)),
                  pl.BlockSpec((tn, k), lambda i, j: (j, 0))],
        out_specs=pl.BlockSpec((tm, tn), lambda i, j: (i, j)),
        out_shape=jax.ShapeDtypeStruct((t, n), F32),
        scratch_shapes=[pltpu.VMEM((tm, k), BF16)],
        compiler_params=pltpu.CompilerParams(
            dimension_semantics=("parallel", "arbitrary"), vmem_limit_bytes=VMEM_LIMIT),
        name="proj",
    )(x, w_t)


def _mlstm_kernel(qp_ref, kp_ref, v_ref, ao_ref, az_ref, g_ref, cwq_ref, cwk_ref, cbq_ref, cbk_ref,
                  gb_ref, ng_ref, *rest, n_chunks):
    dec_in, (ya_ref, c_ref, n_ref, m_ref), dec_out, (tailq_ref, tailk_ref) = (
        rest[:21], rest[21:25], rest[25:29], rest[29:])
    L = MLSTM_CHUNK
    c = pl.program_id(2)

    @pl.when(c == 0)
    def _():
        c_ref[...] = jnp.zeros_like(c_ref)
        n_ref[...] = jnp.zeros_like(n_ref)
        m_ref[...] = jnp.zeros_like(m_ref)
        tailq_ref[...] = jnp.zeros_like(tailq_ref)
        tailk_ref[...] = jnp.zeros_like(tailk_ref)

    _mlstm_decode_step((pl.program_id(0) * n_chunks + c) % A_HEADS, *dec_in, *dec_out)

    sub = lax.broadcasted_iota(jnp.int32, (1, SUBLANES, 1), 1)

    def conv_silu(x_ref, tail_ref, w_ref, b_ref):
        x = x_ref[...]
        width = x.shape[1]
        x3 = jnp.concatenate([tail_ref[...], x], axis=0).reshape(L // SUBLANES + 1, SUBLANES, width)
        acc = b_ref[...] + x * w_ref[CONV_W - 1:CONV_W, :]
        for j in range(1, CONV_W):
            rot = pltpu.roll(x3, j, axis=1)
            xs = jnp.where(sub < j, rot[:-1], rot[1:]).reshape(L, width)
            acc = acc + xs * w_ref[CONV_W - 1 - j:CONV_W - j, :]
        tail_ref[...] = x[L - SUBLANES:, :]
        return _silu(acc)

    q_all = conv_silu(qp_ref, tailq_ref, cwq_ref, cbq_ref)
    k_all = conv_silu(kp_ref, tailk_ref, cwk_ref, cbk_ref) * (A_DK ** -0.5)

    lane = lax.broadcasted_iota(jnp.int32, (L, LANES), 1)
    row = lax.broadcasted_iota(jnp.int32, (L, L), 0)
    col = lax.broadcasted_iota(jnp.int32, (L, L), 1)
    causal = row >= col
    causal_b = causal.astype(BF16)

    for hh in range(MLSTM_HEADS_PER_STEP):
        cols = slice(hh * A_DK, (hh + 1) * A_DK)
        q, k, v = q_all[:, cols], k_all[:, cols], v_ref[:, cols]

        g = g_ref[:, hh * LANES:(hh + 1) * LANES] + gb_ref[:, hh * LANES:(hh + 1) * LANES]
        g2 = jnp.where(lane == 1, _log_sigmoid(g), g)
        cum = _mask_dot(causal_b, g2)
        x2 = jnp.where(lane == 1, cum, g2)
        x2t = x2.T
        itil_col, b_col = x2[:, 0:1], x2[:, 1:2]
        itil_row, b_row = x2t[0:1, :], x2t[1:2, :]

        m_prev = m_ref[0, hh, 0:1, 0:1]
        dmat = jnp.where(causal, b_col - b_row + itil_row, -jnp.inf)
        inter = b_col + m_prev
        m_t = jnp.maximum(inter, jnp.max(dmat, axis=1, keepdims=True))
        w = jnp.exp(dmat - m_t)
        decay = jnp.exp(inter - m_t)

        qb, kb, vb = q.astype(BF16), k.astype(BF16), v.astype(BF16)
        c_old = c_ref[0, hh]
        n_old = n_ref[0, hh]
        s = _dot_nt(qb, kb) * w
        num = decay * _dot(qb, c_old.astype(BF16)) + _dot(s.astype(BF16), vb)
        den = decay * jnp.sum(q * n_old, axis=1, keepdims=True) + jnp.sum(s, axis=1, keepdims=True)
        inv = 1.0 / jnp.maximum(jnp.abs(den), jnp.exp(-m_t))

        m_new = m_t[L - 1:L, :]
        b_last = b_col[L - 1:L, :]
        wk = jnp.exp(b_last - b_col + itil_col - m_new)
        dec = jnp.exp(b_last + m_prev - m_new)
        kw = k * wk
        c_ref[0, hh] = dec * c_old + _dot(kw.T.astype(BF16), vb)
        n_ref[0, hh] = dec * n_old + jnp.sum(kw, axis=0, keepdims=True)
        m_ref[0, hh] = jnp.broadcast_to(m_new, (1, LANES))

        mu = jnp.mean(num, axis=1, keepdims=True)
        hc = num - mu
        var = jnp.mean(hc * hc, axis=1, keepdims=True)
        hn = hc * (inv * lax.rsqrt(var * (inv * inv) + LN_EPS)) * ng_ref[:, cols]
        ya_ref[:, cols] = (hn * _sigmoid(ao_ref[:, cols]) * _silu(az_ref[:, cols])).astype(ya_ref.dtype)


def _mlstm(p_all, batch, seq, conv_w, conv_b, gate_bias, a_norm_g, p_s, conv_state, c_state, n_state, m_state):
    L = MLSTM_CHUNK
    hps = MLSTM_HEADS_PER_STEP
    nc = seq // L
    t = batch * seq
    wd = hps * A_DK
    k_off = A_WIDTH // wd

    def rows(b, h, c):
        return b * nc + c

    def pcol(base):
        return pl.BlockSpec((L, wd), lambda b, h, c: (rows(b, h, c), base // wd + h))

    in_specs = [
        pcol(COL_Q), pcol(COL_K), pcol(COL_AV), pcol(COL_AO), pcol(COL_AZ),
        pl.BlockSpec((L, hps * LANES), lambda b, h, c: (rows(b, h, c), COL_GATE // (hps * LANES) + h)),
        pl.BlockSpec((CONV_W, wd), lambda b, h, c: (0, h)),
        pl.BlockSpec((CONV_W, wd), lambda b, h, c: (0, k_off + h)),
        pl.BlockSpec((1, wd), lambda b, h, c: (0, h)),
        pl.BlockSpec((1, wd), lambda b, h, c: (0, k_off + h)),
        pl.BlockSpec((1, hps * LANES), lambda b, h, c: (0, h)),
        pl.BlockSpec((1, wd), lambda b, h, c: (0, h)),
    ]
    out_specs = [
        pl.BlockSpec((L, wd), lambda b, h, c: (rows(b, h, c), h)),
        pl.BlockSpec((1, hps, A_DK, A_DV), lambda b, h, c: (b, h, 0, 0)),
        pl.BlockSpec((1, hps, 1, A_DK), lambda b, h, c: (b, h, 0, 0)),
        pl.BlockSpec((1, hps, 1, LANES), lambda b, h, c: (b, h, 0, 0)),
    ]
    out_shape = [
        jax.ShapeDtypeStruct((t, A_WIDTH), BF16),
        jax.ShapeDtypeStruct((batch, A_HEADS, A_DK, A_DV), F32),
        jax.ShapeDtypeStruct((batch, A_HEADS, 1, A_DK), F32),
        jax.ShapeDtypeStruct((batch, A_HEADS, 1, LANES), F32),
    ]

    assert hps == A_HEADS
    nseq = p_s.shape[0]
    TB = DEC_TOKENS
    dstep = _decode_step(batch, nc, nseq)

    def dcol(base, width):
        return pl.BlockSpec((TB, width), lambda b, h, c: (dstep(b, h, c)[0], base // width + dstep(b, h, c)[1]))

    def dconv(j, base):
        return pl.BlockSpec((TB, A_DK), lambda b, h, c: (dstep(b, h, c)[0],
                                                        (j * 2 * A_WIDTH + base) // A_DK + dstep(b, h, c)[1]))

    def dhead(shape, off=0):
        return pl.BlockSpec(shape, lambda b, h, c: (0, off + dstep(b, h, c)[1]))

    def dstate(*dims):
        return pl.BlockSpec((TB, 1) + dims, lambda b, h, c: dstep(b, h, c) + (0,) * len(dims))

    in_specs += [
        dcol(COL_Q, A_DK), dcol(COL_K, A_DK), dcol(COL_AV, A_DV), dcol(COL_AO, A_DV), dcol(COL_AZ, A_DV),
        dcol(COL_GATE, LANES),
        dconv(0, 0), dconv(1, 0), dconv(2, 0), dconv(0, A_WIDTH), dconv(1, A_WIDTH), dconv(2, A_WIDTH),
        dhead((CONV_W, A_DK)), dhead((CONV_W, A_DK), A_HEADS), dhead((1, A_DK)), dhead((1, A_DK), A_HEADS),
        dhead((1, LANES)), dhead((1, A_DV)),
        dstate(A_DK, A_DV), dstate(1, A_DK),
        pl.BlockSpec((TB, A_HEADS), lambda b, h, c: (dstep(b, h, c)[0], 0)),
    ]
    out_specs += [
        pl.BlockSpec((TB, A_DV), lambda b, h, c: dstep(b, h, c)),
        dstate(A_DK, A_DV), dstate(1, A_DK),
        pl.BlockSpec((TB, LANES), lambda b, h, c: dstep(b, h, c)),
    ]
    out_shape += [
        jax.ShapeDtypeStruct((nseq, A_WIDTH), BF16),
        jax.ShapeDtypeStruct((nseq, A_HEADS, A_DK, A_DV), F32),
        jax.ShapeDtypeStruct((nseq, A_HEADS, 1, A_DK), F32),
        jax.ShapeDtypeStruct((nseq, A_HEADS * LANES), F32),
    ]
    return pl.pallas_call(
        functools.partial(_mlstm_kernel, n_chunks=nc),
        grid=(batch, A_HEADS // hps, nc),
        in_specs=in_specs,
        out_specs=out_specs,
        out_shape=out_shape,
        scratch_shapes=[pltpu.VMEM((SUBLANES, wd), F32),
                        pltpu.VMEM((SUBLANES, wd), F32)],
        compiler_params=pltpu.CompilerParams(
            dimension_semantics=("parallel", "parallel", "arbitrary"), vmem_limit_bytes=VMEM_LIMIT),
        name="mlstm",
    )(p_all, p_all, p_all, p_all, p_all, p_all, conv_w, conv_w, conv_b, conv_b, gate_bias, a_norm_g,
      p_s, p_s, p_s, p_s, p_s, p_s,
      conv_state, conv_state, conv_state, conv_state, conv_state, conv_state,
      conv_w, conv_w, conv_b, conv_b, gate_bias, a_norm_g, c_state, n_state, m_state)


def _gla_gate_log(bg_tile, wg, bgate):
    lr = bg_tile[:, 2:2 + GATE_RANK]
    z = _dot(lr.astype(BF16), wg.astype(BF16)) + bgate
    return _log_sigmoid(z) / GATE_TAU


def _gla_kernel(q_ref, k_ref, v_ref, bz_ref, g_ref, wg_ref, bgate_ref, ng_ref, *rest):
    dec_in, (yb_ref, s_out_ref), dec_out, (st_ref, ds_ref, sb_ref) = rest[:9], rest[9:11], rest[11:13], rest[13:]
    L = GLA_STEP
    B = GLA_BLOCK
    nb = L // B
    c = pl.program_id(2)

    @pl.when(c == 0)
    def _():
        st_ref[...] = jnp.zeros_like(st_ref)

    _gla_decode_step(*dec_in, *dec_out)

    row = lax.broadcasted_iota(jnp.int32, (L, L), 0)
    col = lax.broadcasted_iota(jnp.int32, (L, L), 1)
    same = (row // B) == (col // B)
    tri_b = (same & (row >= col)).astype(BF16)
    same_b = same.astype(BF16)
    t_in = lax.broadcasted_iota(jnp.int32, (L, 1), 0) % B
    s_lane = lax.broadcasted_iota(jnp.int32, (1, LANES), 1)

    for hh in range(GLA_HEADS_PER_STEP):
        kcols = slice(hh * B_DK, (hh + 1) * B_DK)
        vcols = slice(hh * B_DV, (hh + 1) * B_DV)
        loga = _gla_gate_log(g_ref[:, hh * LANES:(hh + 1) * LANES], wg_ref[:, kcols], bgate_ref[:, kcols])
        bc = _mask_dot(tri_b, loga)
        bl = _mask_dot(same_b, loga)
        k = k_ref[:, kcols]
        qs = q_ref[:, kcols] * (B_DK ** -0.5)
        qt = (qs * jnp.exp(bc)).astype(BF16)
        kt = (k * jnp.exp(bl - bc)).astype(BF16)
        eb = jnp.exp(bl)
        vb = v_ref[:, vcols].astype(BF16)

        bc3, q3, k3 = (x.reshape(nb, B, B_DK) for x in (bc * LOG2_E, qs, k))
        groups = B // SUBLANES
        bcg = [bc3[:, g * SUBLANES:(g + 1) * SUBLANES, :] for g in range(groups)]
        qg = [q3[:, g * SUBLANES:(g + 1) * SUBLANES, :] for g in range(groups)]
        ag = [jnp.zeros((nb, SUBLANES, LANES), F32) for _ in range(groups)]
        for s in range(B):
            bc_s, k_s = bc3[:, s:s + 1, :], k3[:, s:s + 1, :]
            for g in range(s // SUBLANES, groups):
                e = jnp.exp2(bcg[g] - bc_s)
                a_col = jnp.sum(qg[g] * k_s * e, axis=2, keepdims=True)
                ag[g] = jnp.where(s_lane == s, a_col, ag[g])
        a = jnp.concatenate(ag, axis=1).reshape(L, LANES)
        ab = jnp.where(t_in >= s_lane, a, 0.0)[:, 0:B].astype(BF16)

        for j in range(nb):
            ds_ref[hh, j] = _dot_tn(vb[j * B:(j + 1) * B, :], kt[j * B:(j + 1) * B, :])
        st = st_ref[hh]
        for j in range(nb):
            sb_ref[hh, j] = st.astype(BF16)
            st = st * eb[j * B:j * B + 1, :] + ds_ref[hh, j]
        st_ref[hh] = st
        o = jnp.concatenate(
            [_dot(ab[j * B:(j + 1) * B, :], vb[j * B:(j + 1) * B, :])
             + _dot_nt(qt[j * B:(j + 1) * B, :], sb_ref[hh, j]) for j in range(nb)], axis=0)

        on = o * lax.rsqrt(jnp.mean(o * o, axis=1, keepdims=True) + LN_EPS) * ng_ref[:, vcols]
        yb_ref[:, vcols] = (on * _silu(bz_ref[:, vcols])).astype(yb_ref.dtype)

    @pl.when(c == pl.num_programs(2) - 1)
    def _():
        for hh in range(GLA_HEADS_PER_STEP):
            s_out_ref[0, hh] = st_ref[hh].T


def _gla(p_all, batch, seq, w_gate_up, b_gate, b_norm_g, p_s, s_state):
    L = GLA_STEP
    hps = GLA_HEADS_PER_STEP
    nc = seq // L
    t = batch * seq
    kw, vw = hps * B_DK, hps * B_DV

    def rows(b, h, c):
        return b * nc + c

    in_specs = [
        pl.BlockSpec((L, kw), lambda b, h, c: (rows(b, h, c), COL_BQ // kw + h)),
        pl.BlockSpec((L, kw), lambda b, h, c: (rows(b, h, c), COL_BK // kw + h)),
        pl.BlockSpec((L, vw), lambda b, h, c: (rows(b, h, c), COL_BV // vw + h)),
        pl.BlockSpec((L, vw), lambda b, h, c: (rows(b, h, c), COL_BZ // vw + h)),
        pl.BlockSpec((L, hps * LANES), lambda b, h, c: (rows(b, h, c), COL_GATE // (hps * LANES) + h)),
        pl.BlockSpec((GATE_RANK, kw), lambda b, h, c: (0, h)),
        pl.BlockSpec((1, kw), lambda b, h, c: (0, h)),
        pl.BlockSpec((1, vw), lambda b, h, c: (0, h)),
    ]
    out_specs = [
        pl.BlockSpec((L, vw), lambda b, h, c: (rows(b, h, c), h)),
        pl.BlockSpec((1, hps, B_DK, B_DV), lambda b, h, c: (b, h, 0, 0)),
    ]
    out_shape = [
        jax.ShapeDtypeStruct((t, B_WIDTH), BF16),
        jax.ShapeDtypeStruct((batch, B_HEADS, B_DK, B_DV), F32),
    ]

    assert hps == B_HEADS
    nseq = p_s.shape[0]
    TB = DEC_TOKENS
    dstep = _decode_step(batch, nc, nseq)

    def dcol(base, width):
        return pl.BlockSpec((TB, width), lambda b, h, c: (dstep(b, h, c)[0], base // width + dstep(b, h, c)[1]))

    def dhead(shape):
        return pl.BlockSpec(shape, lambda b, h, c: (0, dstep(b, h, c)[1]))

    dstate = pl.BlockSpec((TB, 1, B_DK, B_DV), lambda b, h, c: dstep(b, h, c) + (0, 0))
    in_specs += [
        dcol(COL_BQ, B_DK), dcol(COL_BK, B_DK), dcol(COL_BV, B_DV), dcol(COL_BZ, B_DV), dcol(COL_GATE, LANES),
        dhead((GATE_RANK, B_DK)), dhead((1, B_DK)), dhead((1, B_DV)), dstate,
    ]
    out_specs += [pl.BlockSpec((TB, B_DV), lambda b, h, c: dstep(b, h, c)), dstate]
    out_shape += [
        jax.ShapeDtypeStruct((nseq, B_WIDTH), BF16),
        jax.ShapeDtypeStruct((nseq, B_HEADS, B_DK, B_DV), F32),
    ]
    return pl.pallas_call(
        _gla_kernel,
        grid=(batch, B_HEADS // hps, nc),
        in_specs=in_specs,
        out_specs=out_specs,
        out_shape=out_shape,
        scratch_shapes=[pltpu.VMEM((hps, B_DV, B_DK), F32),
                        pltpu.VMEM((hps, L // GLA_BLOCK, B_DV, B_DK), F32),
                        pltpu.VMEM((hps, L // GLA_BLOCK, B_DV, B_DK), BF16)],
        compiler_params=pltpu.CompilerParams(
            dimension_semantics=("parallel", "parallel", "arbitrary"), vmem_limit_bytes=VMEM_LIMIT),
        name="gla",
    )(p_all, p_all, p_all, p_all, p_all, w_gate_up, b_gate, b_norm_g,
      p_s, p_s, p_s, p_s, p_s, w_gate_up, b_gate, b_norm_g, s_state)


DEC_TOKENS = 16


def _decode_step(batch, nc, nseq):
    assert batch * nc == (nseq // DEC_TOKENS) * A_HEADS and A_HEADS == B_HEADS

    def step(b, h, c):
        s = b * nc + c
        return s // A_HEADS, s % A_HEADS
    return step


def _mlstm_decode_step(h_idx, qp_ref, kp_ref, av_ref, ao_ref, az_ref, g_ref,
                       sq0_ref, sq1_ref, sq2_ref, sk0_ref, sk1_ref, sk2_ref,
                       cwq_ref, cwk_ref, cbq_ref, cbk_ref, gb_ref, ang_ref, c_ref, n_ref, m_ref,
                       ya_ref, c_out_ref, n_out_ref, m_out_ref):
    TB = DEC_TOKENS

    def conv_silu(s0, s1, s2, x, w_ref, b_ref):
        acc = b_ref[...] + s0[...] * w_ref[0:1, :]
        acc = acc + s1[...] * w_ref[1:2, :]
        acc = acc + s2[...] * w_ref[2:3, :]
        acc = acc + x[...] * w_ref[3:4, :]
        return _silu(acc)

    q = conv_silu(sq0_ref, sq1_ref, sq2_ref, qp_ref, cwq_ref, cbq_ref)
    k = conv_silu(sk0_ref, sk1_ref, sk2_ref, kp_ref, cwk_ref, cbk_ref) * (A_DK ** -0.5)
    v = av_ref[...]
    g = g_ref[...]
    gbias = gb_ref[...]
    itil = g[:, 0:1] + gbias[:, 0:1]
    logf = _log_sigmoid(g[:, 1:2] + gbias[:, 1:2])
    lane_h = lax.broadcasted_iota(jnp.int32, (TB, A_HEADS), 1)
    m_prev = jnp.sum(jnp.where(lane_h == h_idx, m_ref[...], 0.0), axis=1, keepdims=True)
    inter = logf + m_prev
    m_t = jnp.maximum(inter, itil)
    w = jnp.exp(itil - m_t)
    decay = jnp.exp(inter - m_t)
    n_old = n_ref[:, 0, 0, :]
    s = jnp.sum(q * k, axis=1, keepdims=True) * w
    den = decay * jnp.sum(q * n_old, axis=1, keepdims=True) + s
    scale = 1.0 / jnp.maximum(jnp.abs(den), jnp.exp(-m_t))
    kw = k * w
    n_out_ref[:, 0, 0, :] = decay * n_old + kw
    m_out_ref[...] = jnp.broadcast_to(m_t, (TB, LANES))

    rows = lax.broadcasted_iota(jnp.int32, (TB, 1), 0)
    qb, kwb, vb = q.astype(BF16), kw.astype(BF16), v.astype(BF16)
    h_rows = []
    for t in range(TB):
        c_old = c_ref[t, 0]
        qc = _dot(qb, c_old.astype(BF16))[t:t + 1, :]
        h_rows.append((decay[t:t + 1, :] * qc + s[t:t + 1, :] * v[t:t + 1, :]) * scale[t:t + 1, :])
        outer = _dot_tn(jnp.where(rows == t, kwb, jnp.zeros_like(kwb)), vb)
        c_out_ref[t, 0] = decay[t:t + 1, :] * c_old + outer
    h = jnp.concatenate(h_rows, axis=0)
    mu = jnp.mean(h, axis=1, keepdims=True)
    hc = h - mu
    var = jnp.mean(hc * hc, axis=1, keepdims=True)
    hn = hc * lax.rsqrt(var + LN_EPS) * ang_ref[...]
    ya_ref[...] = (hn * _sigmoid(ao_ref[...]) * _silu(az_ref[...])).astype(ya_ref.dtype)


def _gla_decode_step(bq_ref, bk_ref, bv_ref, bz_ref, g_ref, wg_ref, bgate_ref, bng_ref, s_ref,
                     yb_ref, s_out_ref):
    TB = DEC_TOKENS
    gq = bq_ref[...] * (B_DK ** -0.5)
    gk = bk_ref[...]
    gv = bv_ref[...]
    loga = _gla_gate_log(g_ref[...], wg_ref[...], bgate_ref[...])
    eb = jnp.exp(loga)
    a = jnp.sum(gq * gk, axis=1, keepdims=True)
    ebt = jnp.concatenate([eb, jnp.zeros((LANES - TB, B_DK), F32)], axis=0).T
    rows = lax.broadcasted_iota(jnp.int32, (TB, 1), 0)
    qeb, gkb, gvb = (gq * eb).astype(BF16), gk.astype(BF16), gv.astype(BF16)
    o_rows = []
    for t in range(TB):
        s_old = s_ref[t, 0]
        o_rows.append(_dot(qeb, s_old.astype(BF16))[t:t + 1, :] + a[t:t + 1, :] * gv[t:t + 1, :])
        outer = _dot_tn(jnp.where(rows == t, gkb, jnp.zeros_like(gkb)), gvb)
        s_out_ref[t, 0] = ebt[:, t:t + 1] * s_old + outer
    o = jnp.concatenate(o_rows, axis=0)
    on = o * lax.rsqrt(jnp.mean(o * o, axis=1, keepdims=True) + LN_EPS) * bng_ref[...]
    yb_ref[...] = (on * _silu(bz_ref[...])).astype(yb_ref.dtype)


def _out_kernel(ya_ref, yb_ref, ga_ref, gb_ref, x_ref, wpa_ref, wpb_ref, wo_ref, lng_ref, lnb_ref, o_ref):
    pa = _dot(ya_ref[...], wpa_ref[...])
    pb = _dot(yb_ref[...], wpb_ref[...])
    merged = _sigmoid(ga_ref[...]) * pa + _sigmoid(gb_ref[...]) * pb
    y = _dot(merged.astype(BF16), wo_ref[...])
    r = ALPHA * x_ref[...] + y
    mu = jnp.mean(r, axis=1, keepdims=True)
    rc = r - mu
    var = jnp.mean(rc * rc, axis=1, keepdims=True)
    o_ref[...] = rc * lax.rsqrt(var + LN_EPS) * lng_ref[...] + lnb_ref[...]


def _out(ya, yb, p_all, x, w_pa, w_pb, w_out, ln_g, ln_b, tm):
    t = x.shape[0]
    const = lambda i: (0, 0)
    single = pl.Buffered(1)
    in_specs = [
        pl.BlockSpec((tm, A_WIDTH), lambda i: (i, 0)),
        pl.BlockSpec((tm, B_WIDTH), lambda i: (i, 0)),
        pl.BlockSpec((tm, D_MODEL), lambda i: (i, COL_GA // D_MODEL)),
        pl.BlockSpec((tm, D_MODEL), lambda i: (i, COL_GB // D_MODEL)),
        pl.BlockSpec((tm, D_MODEL), lambda i: (i, 0)),
        pl.BlockSpec((A_WIDTH, D_MODEL), const, pipeline_mode=single),
        pl.BlockSpec((B_WIDTH, D_MODEL), const, pipeline_mode=single),
        pl.BlockSpec((D_MODEL, D_MODEL), const, pipeline_mode=single),
        pl.BlockSpec((1, D_MODEL), const),
        pl.BlockSpec((1, D_MODEL), const),
    ]
    return pl.pallas_call(
        _out_kernel,
        grid=(t // tm,),
        in_specs=in_specs,
        out_specs=pl.BlockSpec((tm, D_MODEL), lambda i: (i, 0)),
        out_shape=jax.ShapeDtypeStruct((t, D_MODEL), F32),
        compiler_params=pltpu.CompilerParams(
            dimension_semantics=("parallel",), vmem_limit_bytes=VMEM_LIMIT),
        name="outproj",
    )(ya, yb, p_all, p_all, x, w_pa, w_pb, w_out, ln_g, ln_b)


RELAYOUT_TN = 512
GATE_SRC_ROWS = 4 * SUBLANES


def _relayout_kernel(src_ref, gif_ref, gb0_ref, gb1_ref, o_ref):
    j = pl.program_id(0)
    tn = RELAYOUT_TN
    n_main = COL_GATE // tn

    @pl.when(j < n_main)
    def _():
        o_ref[...] = src_ref[...].astype(BF16)

    @pl.when(j == n_main)
    def _():
        g = jnp.concatenate([gif_ref[...], gb0_ref[...], gb1_ref[...],
                             jnp.zeros((SUBLANES, src_ref.shape[1]), F32)], axis=0).astype(BF16)
        r = lax.broadcasted_iota(jnp.int32, (tn, GATE_SRC_ROWS), 0)
        c = lax.broadcasted_iota(jnp.int32, (tn, GATE_SRC_ROWS), 1)
        h, l = r // LANES, r % LANES
        sel = (((l == 0) & (c == h)) | ((l == 1) & (c == A_HEADS + h))
               | ((l >= 2) & (l < 2 + GATE_RANK) & (c == l + (2 * A_HEADS - 2))))
        o_ref[...] = _dot(sel.astype(BF16), g).astype(BF16)


def _relayout_w_in(w_in_t):
    n_src, k = w_in_t.shape
    tn = RELAYOUT_TN

    def src_row(j):
        shift = 0
        for first, last, src in RELAYOUT_RUNS:
            assert first % tn == 0 and last % tn == 0 and (src - first) % SUBLANES == 0
            shift = jnp.where((j >= first // tn) & (j < last // tn), (src - first) // SUBLANES, shift)
        group = jnp.minimum(j * (tn // SUBLANES) + shift, (n_src - tn) // SUBLANES)
        return group * SUBLANES

    return pl.pallas_call(
        _relayout_kernel,
        grid=(N_PROJ // tn,),
        in_specs=[pl.BlockSpec((pl.Element(tn), pl.Element(k)), lambda j: (src_row(j), 0)),
                  pl.BlockSpec((SUBLANES, k), lambda j: (SRC_I // SUBLANES, 0)),
                  pl.BlockSpec((SUBLANES, k), lambda j: (SRC_BG // SUBLANES, 0)),
                  pl.BlockSpec((SUBLANES, k), lambda j: (SRC_BG // SUBLANES + 1, 0))],
        out_specs=pl.BlockSpec((tn, k), lambda j: (j, 0)),
        out_shape=jax.ShapeDtypeStruct((N_PROJ, k), BF16),
        compiler_params=pltpu.CompilerParams(
            dimension_semantics=("parallel",), vmem_limit_bytes=VMEM_LIMIT),
        name="relayout",
    )(w_in_t, w_in_t, w_in_t, w_in_t)


def kernel(x_prompt, x_sample, state_mlstm_C, state_mlstm_n, state_mlstm_m, state_conv, state_gla_S,
           w_in, conv_w, conv_b, b_i, b_f, a_norm_g, w_gate_up, b_gate, b_norm_g, w_pa, w_pb, w_out,
           ln_g, ln_b):
    batch, seq, _ = x_prompt.shape
    nseq = x_sample.shape[0]
    assert w_in.shape[0] == 1, "single-layer step"
    d = 0

    def layer(a):
        return a.reshape(a.shape[1:])

    wp = _relayout_w_in(layer(w_in).T)
    wpa, wpb, wo = w_pa[d].astype(BF16), w_pb[d].astype(BF16), w_out[d].astype(BF16)
    cw = conv_w[d]
    cb = conv_b[d][None, :]
    gate_bias = jnp.zeros((A_HEADS, LANES), F32).at[:, 0].set(b_i[d]).at[:, 1].set(b_f[d]).reshape(1, -1)
    ang = a_norm_g[d][None, :]
    bng = b_norm_g[d][None, :]
    wg = w_gate_up[d]
    bgate = b_gate[d][None, :]
    lng, lnb = ln_g[d][None, :], ln_b[d][None, :]

    xp = x_prompt.reshape(batch * seq, D_MODEL)
    xs = x_sample.reshape(nseq, D_MODEL)
    p_all = _proj(xp, wp, tm=1024, tn=1280)
    ps_all = _proj(xs, wp, tm=nseq, tn=1280)

    conv_state = state_conv.reshape(nseq, (CONV_W - 1) * 2 * A_WIDTH)
    ya, p_c, p_n, p_m, ya_s, s_c, s_n, s_m = _mlstm(
        p_all, batch, seq, cw, cb, gate_bias, ang, ps_all, conv_state,
        layer(state_mlstm_C), state_mlstm_n.reshape(nseq, A_HEADS, 1, A_DK), layer(state_mlstm_m))
    yb, p_s, yb_s, s_s = _gla(p_all, batch, seq, wg, bgate, bng, ps_all, layer(state_gla_S))

    y_prompt = _out(ya, yb, p_all, xp, wpa, wpb, wo, lng, lnb, tm=256).reshape(batch, seq, D_MODEL)
    y_sample = _out(ya_s, yb_s, ps_all, xs, wpa, wpb, wo, lng, lnb, tm=nseq).reshape(nseq, 1, D_MODEL)
    p_conv = p_all.reshape(batch, seq, N_PROJ)[:, seq - (CONV_W - 1):, :2 * A_WIDTH]
    s_conv = jnp.concatenate([layer(state_conv)[:, 1:, :], ps_all[:, None, :2 * A_WIDTH]], axis=1)

    def stacked(a):
        return a.reshape((1,) + a.shape)

    return (y_prompt, y_sample,
            stacked(p_c), p_n.reshape(1, batch, A_HEADS, A_DK), stacked(p_m[:, :, 0, 0]),
            stacked(p_conv), stacked(p_s),
            stacked(s_c), s_n.reshape(1, nseq, A_HEADS, A_DK),
            stacked(s_m.reshape(nseq, A_HEADS, LANES)[:, :, 0]), stacked(s_conv), stacked(s_s))
```

```python
import functools

import jax
import jax.numpy as jnp
from jax import lax
from jax.experimental import pallas as pl
from jax.experimental.pallas import tpu as pltpu

F32 = jnp.float32
BF16 = jnp.bfloat16

D_MODEL = 2048
A_HEADS = 4
A_WIDTH = 1024
A_DK = 256
A_DV = 256
CONV_W = 4
B_HEADS = 4
B_WIDTH = 1024
B_KWIDTH = 512
B_DK = 128
B_DV = 256
GATE_RANK = 16
GATE_TAU = 16.0
ALPHA = 2.0 ** 0.25
LN_EPS = 1e-5
LOG2_E = 1.4426950408889634
LN_2 = 0.6931471805599453

LANES = 128
SUBLANES = 8
VMEM_LIMIT = 48 * 1024 * 1024

COL_Q = 0
COL_K = A_WIDTH
COL_BQ = 2 * A_WIDTH
COL_BK = COL_BQ + B_KWIDTH
COL_AV = COL_BK + B_KWIDTH
COL_AO = COL_AV + A_WIDTH
COL_AZ = COL_AO + A_WIDTH
COL_BV = COL_AZ + A_WIDTH
COL_BZ = COL_BV + B_WIDTH
COL_GA = COL_BZ + B_WIDTH
COL_GB = COL_GA + D_MODEL
COL_GATE = COL_GB + D_MODEL
N_PROJ = COL_GATE + A_HEADS * LANES

SRC_QK = 0
SRC_AV = 2 * A_WIDTH
SRC_I = SRC_AV + A_WIDTH
SRC_AO = SRC_I + 2 * A_HEADS
SRC_BQ = SRC_AO + 2 * A_WIDTH
SRC_BV = SRC_BQ + 2 * B_KWIDTH
SRC_BG = SRC_BV + B_WIDTH
SRC_BZ = SRC_BG + GATE_RANK
RELAYOUT_RUNS = ((COL_Q, COL_BQ, SRC_QK), (COL_BQ, COL_AV, SRC_BQ), (COL_AV, COL_AO, SRC_AV),
                 (COL_AO, COL_BV, SRC_AO), (COL_BV, COL_BZ, SRC_BV), (COL_BZ, COL_GATE, SRC_BZ))

MLSTM_CHUNK = 256
MLSTM_HEADS_PER_STEP = 4
GLA_STEP = 256
GLA_BLOCK = 16
GLA_HEADS_PER_STEP = 4


def _dot(a, b):
    return jnp.dot(a, b, preferred_element_type=F32)


def _dot_nt(a, b):
    return lax.dot_general(a, b, (((1,), (1,)), ((), ())), preferred_element_type=F32)


def _dot_tn(a, b):
    return lax.dot_general(a, b, (((0,), (0,)), ((), ())), preferred_element_type=F32)


def _mask_dot(mask_bf16, x):
    hi = x.astype(BF16)
    r1 = x - hi.astype(F32)
    mid = r1.astype(BF16)
    lo = (r1 - mid.astype(F32)).astype(BF16)
    return _dot(mask_bf16, hi) + _dot(mask_bf16, mid) + _dot(mask_bf16, lo)


def _log_sigmoid(z):
    return jnp.minimum(z, 0.0) - LN_2 * jnp.log2(1.0 + jnp.exp2(jnp.abs(z) * (-LOG2_E)))


def _sigmoid(z):
    return 1.0 / (1.0 + jnp.exp2(z * (-LOG2_E)))


def _silu(z):
    return z * _sigmoid(z)


def _proj_kernel(x_ref, w_ref, o_ref, xb_ref):
    @pl.when(pl.program_id(1) == 0)
    def _():
        xb_ref[...] = x_ref[...].astype(BF16)

    o_ref[...] = _dot_nt(xb_ref[...], w_ref[...])


def _proj(x, w_t, tm, tn):
    t, k = x.shape
    n = w_t.shape[0]
    return pl.pallas_call(
        _proj_kernel,
        grid=(t // tm, n // tn),
        in_specs=[pl.BlockSpec((tm, k), lambda i, j: (i, 0)),
                  pl.BlockSpec((tn, k), lambda i, j: (j, 0))],
        out_specs=pl.BlockSpec((tm, tn), lambda i, j: (i, j)),
        out_shape=jax.ShapeDtypeStruct((t, n), F32),
        scratch_shapes=[pltpu.VMEM((tm, k), BF16)],
        compiler_params=pltpu.CompilerParams(
            dimension_semantics=("parallel", "arbitrary"), vmem_limit_bytes=VMEM_LIMIT),
        name="proj",
    )(x, w_t)


def _mlstm_kernel(qp_ref, kp_ref, v_ref, ao_ref, az_ref, g_ref, cwq_ref, cwk_ref, cbq_ref, cbk_ref,
                  gb_ref, ng_ref, *rest, n_chunks):
    dec_in, (ya_ref, c_ref, n_ref, m_ref), dec_out, (tailq_ref, tailk_ref) = (
        rest[:21], rest[21:25], rest[25:29], rest[29:])
    L = MLSTM_CHUNK
    c = pl.program_id(2)

    @pl.when(c == 0)
    def _():
        c_ref[...] = jnp.zeros_like(c_ref)
        n_ref[...] = jnp.zeros_like(n_ref)
        m_ref[...] = jnp.zeros_like(m_ref)
        tailq_ref[...] = jnp.zeros_like(tailq_ref)
        tailk_ref[...] = jnp.zeros_like(tailk_ref)

    _mlstm_decode_step((pl.program_id(0) * n_chunks + c) % A_HEADS, *dec_in, *dec_out)

    sub = lax.broadcasted_iota(jnp.int32, (1, SUBLANES, 1), 1)

    def conv_silu(x_ref, tail_ref, w_ref, b_ref):
        x = x_ref[...]
        width = x.shape[1]
        x3 = jnp.concatenate([tail_ref[...], x], axis=0).reshape(L // SUBLANES + 1, SUBLANES, width)
        acc = b_ref[...] + x * w_ref[CONV_W - 1:CONV_W, :]
        for j in range(1, CONV_W):
            rot = pltpu.roll(x3, j, axis=1)
            xs = jnp.where(sub < j, rot[:-1], rot[1:]).reshape(L, width)
            acc = acc + xs * w_ref[CONV_W - 1 - j:CONV_W - j, :]
        tail_ref[...] = x[L - SUBLANES:, :]
        return _silu(acc)

    q_all = conv_silu(qp_ref, tailq_ref, cwq_ref, cbq_ref)
    k_all = conv_silu(kp_ref, tailk_ref, cwk_ref, cbk_ref) * (A_DK ** -0.5)

    lane = lax.broadcasted_iota(jnp.int32, (L, LANES), 1)
    row = lax.broadcasted_iota(jnp.int32, (L, L), 0)
    col = lax.broadcasted_iota(jnp.int32, (L, L), 1)
    causal = row >= col
    causal_b = causal.astype(BF16)

    for hh in range(MLSTM_HEADS_PER_STEP):
        cols = slice(hh * A_DK, (hh + 1) * A_DK)
        q, k, v = q_all[:, cols], k_all[:, cols], v_ref[:, cols]

        g = g_ref[:, hh * LANES:(hh + 1) * LANES] + gb_ref[:, hh * LANES:(hh + 1) * LANES]
        g2 = jnp.where(lane == 1, _log_sigmoid(g), g)
        cum = _mask_dot(causal_b, g2)
        x2 = jnp.where(lane == 1, cum, g2)
        x2t = x2.T
        itil_col, b_col = x2[:, 0:1], x2[:, 1:2]
        itil_row, b_row = x2t[0:1, :], x2t[1:2, :]

        m_prev = m_ref[0, hh, 0:1, 0:1]
        dmat = jnp.where(causal, b_col - b_row + itil_row, -jnp.inf)
        inter = b_col + m_prev
        m_t = jnp.maximum(inter, jnp.max(dmat, axis=1, keepdims=True))
        w = jnp.exp(dmat - m_t)
        decay = jnp.exp(inter - m_t)

        qb, kb, vb = q.astype(BF16), k.astype(BF16), v.astype(BF16)
        c_old = c_ref[0, hh]
        n_old = n_ref[0, hh]
        s = _dot_nt(qb, kb) * w
        num = decay * _dot(qb, c_old.astype(BF16)) + _dot(s.astype(BF16), vb)
        den = decay * jnp.sum(q * n_old, axis=1, keepdims=True) + jnp.sum(s, axis=1, keepdims=True)
        inv = 1.0 / jnp.maximum(jnp.abs(den), jnp.exp(-m_t))

        m_new = m_t[L - 1:L, :]
        b_last = b_col[L - 1:L, :]
        wk = jnp.exp(b_last - b_col + itil_col - m_new)
        dec = jnp.exp(b_last + m_prev - m_new)
        kw = k * wk
        c_ref[0, hh] = dec * c_old + _dot(kw.T.astype(BF16), vb)
        n_ref[0, hh] = dec * n_old + jnp.sum(kw, axis=0, keepdims=True)
        m_ref[0, hh] = jnp.broadcast_to(m_new, (1, LANES))

        mu = jnp.mean(num, axis=1, keepdims=True)
        hc = num - mu
        var = jnp.mean(hc * hc, axis=1, keepdims=True)
        hn = hc * (inv * lax.rsqrt(var * (inv * inv) + LN_EPS)) * ng_ref[:, cols]
        ya_ref[:, cols] = (hn * _sigmoid(ao_ref[:, cols]) * _silu(az_ref[:, cols])).astype(ya_ref.dtype)


def _mlstm(p_all, batch, seq, conv_w, conv_b, gate_bias, a_norm_g, p_s, conv_state, c_state, n_state, m_state):
    L = MLSTM_CHUNK
    hps = MLSTM_HEADS_PER_STEP
    nc = seq // L
    t = batch * seq
    wd = hps * A_DK
    k_off = A_WIDTH // wd

    def rows(b, h, c):
        return b * nc + c

    def pcol(base):
        return pl.BlockSpec((L, wd), lambda b, h, c: (rows(b, h, c), base // wd + h))

    in_specs = [
        pcol(COL_Q), pcol(COL_K), pcol(COL_AV), pcol(COL_AO), pcol(COL_AZ),
        pl.BlockSpec((L, hps * LANES), lambda b, h, c: (rows(b, h, c), COL_GATE // (hps * LANES) + h)),
        pl.BlockSpec((CONV_W, wd), lambda b, h, c: (0, h)),
        pl.BlockSpec((CONV_W, wd), lambda b, h, c: (0, k_off + h)),
        pl.BlockSpec((1, wd), lambda b, h, c: (0, h)),
        pl.BlockSpec((1, wd), lambda b, h, c: (0, k_off + h)),
        pl.BlockSpec((1, hps * LANES), lambda b, h, c: (0, h)),
        pl.BlockSpec((1, wd), lambda b, h, c: (0, h)),
    ]
    out_specs = [
        pl.BlockSpec((L, wd), lambda b, h, c: (rows(b, h, c), h)),
        pl.BlockSpec((1, hps, A_DK, A_DV), lambda b, h, c: (b, h, 0, 0)),
        pl.BlockSpec((1, hps, 1, A_DK), lambda b, h, c: (b, h, 0, 0)),
        pl.BlockSpec((1, hps, 1, LANES), lambda b, h, c: (b, h, 0, 0)),
    ]
    out_shape = [
        jax.ShapeDtypeStruct((t, A_WIDTH), BF16),
        jax.ShapeDtypeStruct((batch, A_HEADS, A_DK, A_DV), F32),
        jax.ShapeDtypeStruct((batch, A_HEADS, 1, A_DK), F32),
        jax.ShapeDtypeStruct((batch, A_HEADS, 1, LANES), F32),
    ]

    assert hps == A_HEADS
    nseq = p_s.shape[0]
    TB = DEC_TOKENS
    dstep = _decode_step(batch, nc, nseq)

    def dcol(base, width):
        return pl.BlockSpec((TB, width), lambda b, h, c: (dstep(b, h, c)[0], base // width + dstep(b, h, c)[1]))

    def dconv(j, base):
        return pl.BlockSpec((TB, A_DK), lambda b, h, c: (dstep(b, h, c)[0],
                                                        (j * 2 * A_WIDTH + base) // A_DK + dstep(b, h, c)[1]))

    def dhead(shape, off=0):
        return pl.BlockSpec(shape, lambda b, h, c: (0, off + dstep(b, h, c)[1]))

    def dstate(*dims):
        return pl.BlockSpec((TB, 1) + dims, lambda b, h, c: dstep(b, h, c) + (0,) * len(dims))

    in_specs += [
        dcol(COL_Q, A_DK), dcol(COL_K, A_DK), dcol(COL_AV, A_DV), dcol(COL_AO, A_DV), dcol(COL_AZ, A_DV),
        dcol(COL_GATE, LANES),
        dconv(0, 0), dconv(1, 0), dconv(2, 0), dconv(0, A_WIDTH), dconv(1, A_WIDTH), dconv(2, A_WIDTH),
        dhead((CONV_W, A_DK)), dhead((CONV_W, A_DK), A_HEADS), dhead((1, A_DK)), dhead((1, A_DK), A_HEADS),
        dhead((1, LANES)), dhead((1, A_DV)),
        dstate(A_DK, A_DV), dstate(1, A_DK),
        pl.BlockSpec((TB, A_HEADS), lambda b, h, c: (dstep(b, h, c)[0], 0)),
    ]
    out_specs += [
        pl.BlockSpec((TB, A_DV), lambda b, h, c: dstep(b, h, c)),
        dstate(A_DK, A_DV), dstate(1, A_DK),
        pl.BlockSpec((TB, LANES), lambda b, h, c: dstep(b, h, c)),
    ]
    out_shape += [
        jax.ShapeDtypeStruct((nseq, A_WIDTH), BF16),
        jax.ShapeDtypeStruct((nseq, A_HEADS, A_DK, A_DV), F32),
        jax.ShapeDtypeStruct((nseq, A_HEADS, 1, A_DK), F32),
        jax.ShapeDtypeStruct((nseq, A_HEADS * LANES), F32),
    ]
    return pl.pallas_call(
        functools.partial(_mlstm_kernel, n_chunks=nc),
        grid=(batch, A_HEADS // hps, nc),
        in_specs=in_specs,
        out_specs=out_specs,
        out_shape=out_shape,
        scratch_shapes=[pltpu.VMEM((SUBLANES, wd), F32),
                        pltpu.VMEM((SUBLANES, wd), F32)],
        compiler_params=pltpu.CompilerParams(
            dimension_semantics=("parallel", "parallel", "arbitrary"), vmem_limit_bytes=VMEM_LIMIT),
        name="mlstm",
    )(p_all, p_all, p_all, p_all, p_all, p_all, conv_w, conv_w, conv_b, conv_b, gate_bias, a_norm_g,
      p_s, p_s, p_s, p_s, p_s, p_s,
      conv_state, conv_state, conv_state, conv_state, conv_state, conv_state,
      conv_w, conv_w, conv_b, conv_b, gate_bias, a_norm_g, c_state, n_state, m_state)


def _gla_gate_log(bg_tile, wg, bgate):
    lr = bg_tile[:, 2:2 + GATE_RANK]
    z = _dot(lr.astype(BF16), wg.astype(BF16)) + bgate
    return _log_sigmoid(z) / GATE_TAU


def _gla_kernel(q_ref, k_ref, v_ref, bz_ref, g_ref, wg_ref, bgate_ref, ng_ref, *rest):
    dec_in, w_in, (yb_ref, s_out_ref), dec_out, w_out, (st_ref, ds_ref, sb_ref) = (
        rest[:9], rest[9:12], rest[12:14], rest[14:16], rest[16:19], rest[19:])
    L = GLA_STEP
    B = GLA_BLOCK
    nb = L // B
    c = pl.program_id(2)

    @pl.when(c == 0)
    def _():
        st_ref[...] = jnp.zeros_like(st_ref)

    _gla_decode_step(*dec_in, *dec_out)
    for src_ref, dst_ref in zip(w_in, w_out):
        dst_ref[...] = src_ref[...].astype(BF16)

    row = lax.broadcasted_iota(jnp.int32, (L, L), 0)
    col = lax.broadcasted_iota(jnp.int32, (L, L), 1)
    same = (row // B) == (col // B)
    tri_b = (same & (row >= col)).astype(BF16)
    same_b = same.astype(BF16)
    t_in = lax.broadcasted_iota(jnp.int32, (L, 1), 0) % B
    s_lane = lax.broadcasted_iota(jnp.int32, (1, LANES), 1)

    for hh in range(GLA_HEADS_PER_STEP):
        kcols = slice(hh * B_DK, (hh + 1) * B_DK)
        vcols = slice(hh * B_DV, (hh + 1) * B_DV)
        loga = _gla_gate_log(g_ref[:, hh * LANES:(hh + 1) * LANES], wg_ref[:, kcols], bgate_ref[:, kcols])
        bc = _mask_dot(tri_b, loga)
        bl = _mask_dot(same_b, loga)
        k = k_ref[:, kcols]
        qs = q_ref[:, kcols] * (B_DK ** -0.5)
        qt = (qs * jnp.exp(bc)).astype(BF16)
        kt = (k * jnp.exp(bl - bc)).astype(BF16)
        eb = jnp.exp(bl)
        vb = v_ref[:, vcols].astype(BF16)

        bc3, q3, k3 = (x.reshape(nb, B, B_DK) for x in (bc * LOG2_E, qs, k))
        groups = B // SUBLANES
        bcg = [bc3[:, g * SUBLANES:(g + 1) * SUBLANES, :] for g in range(groups)]
        qg = [q3[:, g * SUBLANES:(g + 1) * SUBLANES, :] for g in range(groups)]
        ag = [jnp.zeros((nb, SUBLANES, LANES), F32) for _ in range(groups)]
        for s in range(B):
            bc_s, k_s = bc3[:, s:s + 1, :], k3[:, s:s + 1, :]
            for g in range(s // SUBLANES, groups):
                e = jnp.exp2(bcg[g] - bc_s)
                a_col = jnp.sum(qg[g] * k_s * e, axis=2, keepdims=True)
                ag[g] = jnp.where(s_lane == s, a_col, ag[g])
        a = jnp.concatenate(ag, axis=1).reshape(L, LANES)
        ab = jnp.where(t_in >= s_lane, a, 0.0)[:, 0:B].astype(BF16)

        for j in range(nb):
            ds_ref[hh, j] = _dot_tn(vb[j * B:(j + 1) * B, :], kt[j * B:(j + 1) * B, :])
        st = st_ref[hh]
        for j in range(nb):
            sb_ref[hh, j] = st.astype(BF16)
            st = st * eb[j * B:j * B + 1, :] + ds_ref[hh, j]
        st_ref[hh] = st
        o = jnp.concatenate(
            [_dot(ab[j * B:(j + 1) * B, :], vb[j * B:(j + 1) * B, :])
             + _dot_nt(qt[j * B:(j + 1) * B, :], sb_ref[hh, j]) for j in range(nb)], axis=0)

        on = o * lax.rsqrt(jnp.mean(o * o, axis=1, keepdims=True) + LN_EPS) * ng_ref[:, vcols]
        yb_ref[:, vcols] = (on * _silu(bz_ref[:, vcols])).astype(yb_ref.dtype)

    @pl.when(c == pl.num_programs(2) - 1)
    def _():
        for hh in range(GLA_HEADS_PER_STEP):
            s_out_ref[0, hh] = st_ref[hh].T


def _gla(p_all, batch, seq, w_gate_up, b_gate, b_norm_g, p_s, s_state, cast_weights):
    L = GLA_STEP
    hps = GLA_HEADS_PER_STEP
    nc = seq // L
    t = batch * seq
    kw, vw = hps * B_DK, hps * B_DV

    def rows(b, h, c):
        return b * nc + c

    in_specs = [
        pl.BlockSpec((L, kw), lambda b, h, c: (rows(b, h, c), COL_BQ // kw + h)),
        pl.BlockSpec((L, kw), lambda b, h, c: (rows(b, h, c), COL_BK // kw + h)),
        pl.BlockSpec((L, vw), lambda b, h, c: (rows(b, h, c), COL_BV // vw + h)),
        pl.BlockSpec((L, vw), lambda b, h, c: (rows(b, h, c), COL_BZ // vw + h)),
        pl.BlockSpec((L, hps * LANES), lambda b, h, c: (rows(b, h, c), COL_GATE // (hps * LANES) + h)),
        pl.BlockSpec((GATE_RANK, kw), lambda b, h, c: (0, h)),
        pl.BlockSpec((1, kw), lambda b, h, c: (0, h)),
        pl.BlockSpec((1, vw), lambda b, h, c: (0, h)),
    ]
    out_specs = [
        pl.BlockSpec((L, vw), lambda b, h, c: (rows(b, h, c), h)),
        pl.BlockSpec((1, hps, B_DK, B_DV), lambda b, h, c: (b, h, 0, 0)),
    ]
    out_shape = [
        jax.ShapeDtypeStruct((t, B_WIDTH), BF16),
        jax.ShapeDtypeStruct((batch, B_HEADS, B_DK, B_DV), F32),
    ]

    assert hps == B_HEADS
    nseq = p_s.shape[0]
    TB = DEC_TOKENS
    dstep = _decode_step(batch, nc, nseq)

    def dcol(base, width):
        return pl.BlockSpec((TB, width), lambda b, h, c: (dstep(b, h, c)[0], base // width + dstep(b, h, c)[1]))

    def dhead(shape):
        return pl.BlockSpec(shape, lambda b, h, c: (0, dstep(b, h, c)[1]))

    dstate = pl.BlockSpec((TB, 1, B_DK, B_DV), lambda b, h, c: dstep(b, h, c) + (0, 0))
    in_specs += [
        dcol(COL_BQ, B_DK), dcol(COL_BK, B_DK), dcol(COL_BV, B_DV), dcol(COL_BZ, B_DV), dcol(COL_GATE, LANES),
        dhead((GATE_RANK, B_DK)), dhead((1, B_DK)), dhead((1, B_DV)), dstate,
    ]
    out_specs += [pl.BlockSpec((TB, B_DV), lambda b, h, c: dstep(b, h, c)), dstate]
    out_shape += [
        jax.ShapeDtypeStruct((nseq, B_WIDTH), BF16),
        jax.ShapeDtypeStruct((nseq, B_HEADS, B_DK, B_DV), F32),
    ]

    n_steps = batch * nc
    for wmat in cast_weights:
        slab = pl.BlockSpec((wmat.shape[0] // n_steps, wmat.shape[1]), lambda b, h, c: (rows(b, h, c), 0))
        in_specs.append(slab)
        out_specs.append(slab)
        out_shape.append(jax.ShapeDtypeStruct(wmat.shape, BF16))
    return pl.pallas_call(
        _gla_kernel,
        grid=(batch, B_HEADS // hps, nc),
        in_specs=in_specs,
        out_specs=out_specs,
        out_shape=out_shape,
        scratch_shapes=[pltpu.VMEM((hps, B_DV, B_DK), F32),
                        pltpu.VMEM((hps, L // GLA_BLOCK, B_DV, B_DK), F32),
                        pltpu.VMEM((hps, L // GLA_BLOCK, B_DV, B_DK), BF16)],
        compiler_params=pltpu.CompilerParams(
            dimension_semantics=("parallel", "parallel", "arbitrary"), vmem_limit_bytes=VMEM_LIMIT),
        name="gla",
    )(p_all, p_all, p_all, p_all, p_all, w_gate_up, b_gate, b_norm_g,
      p_s, p_s, p_s, p_s, p_s, w_gate_up, b_gate, b_norm_g, s_state, *cast_weights)


DEC_TOKENS = 16


def _decode_step(batch, nc, nseq):
    assert batch * nc == (nseq // DEC_TOKENS) * A_HEADS and A_HEADS == B_HEADS

    def step(b, h, c):
        s = b * nc + c
        return s // A_HEADS, s % A_HEADS
    return step


def _mlstm_decode_step(h_idx, qp_ref, kp_ref, av_ref, ao_ref, az_ref, g_ref,
                       sq0_ref, sq1_ref, sq2_ref, sk0_ref, sk1_ref, sk2_ref,
                       cwq_ref, cwk_ref, cbq_ref, cbk_ref, gb_ref, ang_ref, c_ref, n_ref, m_ref,
                       ya_ref, c_out_ref, n_out_ref, m_out_ref):
    TB = DEC_TOKENS

    def conv_silu(s0, s1, s2, x, w_ref, b_ref):
        acc = b_ref[...] + s0[...] * w_ref[0:1, :]
        acc = acc + s1[...] * w_ref[1:2, :]
        acc = acc + s2[...] * w_ref[2:3, :]
        acc = acc + x[...] * w_ref[3:4, :]
        return _silu(acc)

    q = conv_silu(sq0_ref, sq1_ref, sq2_ref, qp_ref, cwq_ref, cbq_ref)
    k = conv_silu(sk0_ref, sk1_ref, sk2_ref, kp_ref, cwk_ref, cbk_ref) * (A_DK ** -0.5)
    v = av_ref[...]
    g = g_ref[...]
    gbias = gb_ref[...]
    itil = g[:, 0:1] + gbias[:, 0:1]
    logf = _log_sigmoid(g[:, 1:2] + gbias[:, 1:2])
    lane_h = lax.broadcasted_iota(jnp.int32, (TB, A_HEADS), 1)
    m_prev = jnp.sum(jnp.where(lane_h == h_idx, m_ref[...], 0.0), axis=1, keepdims=True)
    inter = logf + m_prev
    m_t = jnp.maximum(inter, itil)
    w = jnp.exp(itil - m_t)
    decay = jnp.exp(inter - m_t)
    n_old = n_ref[:, 0, 0, :]
    s = jnp.sum(q * k, axis=1, keepdims=True) * w
    den = decay * jnp.sum(q * n_old, axis=1, keepdims=True) + s
    scale = 1.0 / jnp.maximum(jnp.abs(den), jnp.exp(-m_t))
    kw = k * w
    n_out_ref[:, 0, 0, :] = decay * n_old + kw
    m_out_ref[...] = jnp.broadcast_to(m_t, (TB, LANES))

    rows = lax.broadcasted_iota(jnp.int32, (TB, 1), 0)
    qb, kwb, vb = q.astype(BF16), kw.astype(BF16), v.astype(BF16)
    h_rows = []
    for t in range(TB):
        c_old = c_ref[t, 0]
        qc = _dot(qb, c_old.astype(BF16))[t:t + 1, :]
        h_rows.append((decay[t:t + 1, :] * qc + s[t:t + 1, :] * v[t:t + 1, :]) * scale[t:t + 1, :])
        outer = _dot_tn(jnp.where(rows == t, kwb, jnp.zeros_like(kwb)), vb)
        c_out_ref[t, 0] = decay[t:t + 1, :] * c_old + outer
    h = jnp.concatenate(h_rows, axis=0)
    mu = jnp.mean(h, axis=1, keepdims=True)
    hc = h - mu
    var = jnp.mean(hc * hc, axis=1, keepdims=True)
    hn = hc * lax.rsqrt(var + LN_EPS) * ang_ref[...]
    ya_ref[...] = (hn * _sigmoid(ao_ref[...]) * _silu(az_ref[...])).astype(ya_ref.dtype)


def _gla_decode_step(bq_ref, bk_ref, bv_ref, bz_ref, g_ref, wg_ref, bgate_ref, bng_ref, s_ref,
                     yb_ref, s_out_ref):
    TB = DEC_TOKENS
    gq = bq_ref[...] * (B_DK ** -0.5)
    gk = bk_ref[...]
    gv = bv_ref[...]
    loga = _gla_gate_log(g_ref[...], wg_ref[...], bgate_ref[...])
    eb = jnp.exp(loga)
    a = jnp.sum(gq * gk, axis=1, keepdims=True)
    ebt = jnp.concatenate([eb, jnp.zeros((LANES - TB, B_DK), F32)], axis=0).T
    rows = lax.broadcasted_iota(jnp.int32, (TB, 1), 0)
    qeb, gkb, gvb = (gq * eb).astype(BF16), gk.astype(BF16), gv.astype(BF16)
    o_rows = []
    for t in range(TB):
        s_old = s_ref[t, 0]
        o_rows.append(_dot(qeb, s_old.astype(BF16))[t:t + 1, :] + a[t:t + 1, :] * gv[t:t + 1, :])
        outer = _dot_tn(jnp.where(rows == t, gkb, jnp.zeros_like(gkb)), gvb)
        s_out_ref[t, 0] = ebt[:, t:t + 1] * s_old + outer
    o = jnp.concatenate(o_rows, axis=0)
    on = o * lax.rsqrt(jnp.mean(o * o, axis=1, keepdims=True) + LN_EPS) * bng_ref[...]
    yb_ref[...] = (on * _silu(bz_ref[...])).astype(yb_ref.dtype)


def _out_kernel(ya_ref, yb_ref, ga_ref, gb_ref, x_ref, wpa_ref, wpb_ref, wo_ref, lng_ref, lnb_ref, o_ref):
    pa = _dot(ya_ref[...], wpa_ref[...])
    pb = _dot(yb_ref[...], wpb_ref[...])
    merged = _sigmoid(ga_ref[...]) * pa + _sigmoid(gb_ref[...]) * pb
    y = _dot(merged.astype(BF16), wo_ref[...])
    r = ALPHA * x_ref[...] + y
    mu = jnp.mean(r, axis=1, keepdims=True)
    rc = r - mu
    var = jnp.mean(rc * rc, axis=1, keepdims=True)
    o_ref[...] = rc * lax.rsqrt(var + LN_EPS) * lng_ref[...] + lnb_ref[...]


def _out(ya, yb, p_all, x, w_pa, w_pb, w_out, ln_g, ln_b, tm):
    t = x.shape[0]
    const = lambda i: (0, 0)
    single = pl.Buffered(1)
    in_specs = [
        pl.BlockSpec((tm, A_WIDTH), lambda i: (i, 0)),
        pl.BlockSpec((tm, B_WIDTH), lambda i: (i, 0)),
        pl.BlockSpec((tm, D_MODEL), lambda i: (i, COL_GA // D_MODEL)),
        pl.BlockSpec((tm, D_MODEL), lambda i: (i, COL_GB // D_MODEL)),
        pl.BlockSpec((tm, D_MODEL), lambda i: (i, 0)),
        pl.BlockSpec((A_WIDTH, D_MODEL), const, pipeline_mode=single),
        pl.BlockSpec((B_WIDTH, D_MODEL), const, pipeline_mode=single),
        pl.BlockSpec((D_MODEL, D_MODEL), const, pipeline_mode=single),
        pl.BlockSpec((1, D_MODEL), const),
        pl.BlockSpec((1, D_MODEL), const),
    ]
    return pl.pallas_call(
        _out_kernel,
        grid=(t // tm,),
        in_specs=in_specs,
        out_specs=pl.BlockSpec((tm, D_MODEL), lambda i: (i, 0)),
        out_shape=jax.ShapeDtypeStruct((t, D_MODEL), F32),
        compiler_params=pltpu.CompilerParams(
            dimension_semantics=("parallel",), vmem_limit_bytes=VMEM_LIMIT),
        name="outproj",
    )(ya, yb, p_all, p_all, x, w_pa, w_pb, w_out, ln_g, ln_b)


RELAYOUT_TN = 512
GATE_SRC_ROWS = 4 * SUBLANES


def _relayout_kernel(src_ref, gif_ref, gb0_ref, gb1_ref, o_ref):
    j = pl.program_id(0)
    tn = RELAYOUT_TN
    n_main = COL_GATE // tn

    @pl.when(j < n_main)
    def _():
        o_ref[...] = src_ref[...].astype(BF16)

    @pl.when(j == n_main)
    def _():
        g = jnp.concatenate([gif_ref[...], gb0_ref[...], gb1_ref[...],
                             jnp.zeros((SUBLANES, src_ref.shape[1]), F32)], axis=0).astype(BF16)
        r = lax.broadcasted_iota(jnp.int32, (tn, GATE_SRC_ROWS), 0)
        c = lax.broadcasted_iota(jnp.int32, (tn, GATE_SRC_ROWS), 1)
        h, l = r // LANES, r % LANES
        sel = (((l == 0) & (c == h)) | ((l == 1) & (c == A_HEADS + h))
               | ((l >= 2) & (l < 2 + GATE_RANK) & (c == l + (2 * A_HEADS - 2))))
        o_ref[...] = _dot(sel.astype(BF16), g).astype(BF16)


def _relayout_w_in(w_in_t):
    n_src, k = w_in_t.shape
    tn = RELAYOUT_TN

    def src_row(j):
        shift = 0
        for first, last, src in RELAYOUT_RUNS:
            assert first % tn == 0 and last % tn == 0 and (src - first) % SUBLANES == 0
            shift = jnp.where((j >= first // tn) & (j < last // tn), (src - first) // SUBLANES, shift)
        group = jnp.minimum(j * (tn // SUBLANES) + shift, (n_src - tn) // SUBLANES)
        return group * SUBLANES

    return pl.pallas_call(
        _relayout_kernel,
        grid=(N_PROJ // tn,),
        in_specs=[pl.BlockSpec((pl.Element(tn), pl.Element(k)), lambda j: (src_row(j), 0)),
                  pl.BlockSpec((SUBLANES, k), lambda j: (SRC_I // SUBLANES, 0)),
                  pl.BlockSpec((SUBLANES, k), lambda j: (SRC_BG // SUBLANES, 0)),
                  pl.BlockSpec((SUBLANES, k), lambda j: (SRC_BG // SUBLANES + 1, 0))],
        out_specs=pl.BlockSpec((tn, k), lambda j: (j, 0)),
        out_shape=jax.ShapeDtypeStruct((N_PROJ, k), BF16),
        compiler_params=pltpu.CompilerParams(
            dimension_semantics=("parallel",), vmem_limit_bytes=VMEM_LIMIT),
        name="relayout",
    )(w_in_t, w_in_t, w_in_t, w_in_t)


def kernel(x_prompt, x_sample, state_mlstm_C, state_mlstm_n, state_mlstm_m, state_conv, state_gla_S,
           w_in, conv_w, conv_b, b_i, b_f, a_norm_g, w_gate_up, b_gate, b_norm_g, w_pa, w_pb, w_out,
           ln_g, ln_b):
    batch, seq, _ = x_prompt.shape
    nseq = x_sample.shape[0]
    assert w_in.shape[0] == 1, "single-layer step"
    d = 0

    def layer(a):
        return a.reshape(a.shape[1:])

    wp = _relayout_w_in(layer(w_in).T)
    cw = conv_w[d]
    cb = conv_b[d][None, :]
    gate_bias = jnp.zeros((A_HEADS, LANES), F32).at[:, 0].set(b_i[d]).at[:, 1].set(b_f[d]).reshape(1, -1)
    ang = a_norm_g[d][None, :]
    bng = b_norm_g[d][None, :]
    wg = w_gate_up[d]
    bgate = b_gate[d][None, :]
    lng, lnb = ln_g[d][None, :], ln_b[d][None, :]

    xp = x_prompt.reshape(batch * seq, D_MODEL)
    xs = x_sample.reshape(nseq, D_MODEL)
    p_all = _proj(xp, wp, tm=1024, tn=1280)
    ps_all = _proj(xs, wp, tm=nseq, tn=1280)

    conv_state = state_conv.reshape(nseq, (CONV_W - 1) * 2 * A_WIDTH)
    ya, p_c, p_n, p_m, ya_s, s_c, s_n, s_m = _mlstm(
        p_all, batch, seq, cw, cb, gate_bias, ang, ps_all, conv_state,
        layer(state_mlstm_C), state_mlstm_n.reshape(nseq, A_HEADS, 1, A_DK), layer(state_mlstm_m))
    yb, p_s, yb_s, s_s, wpa, wpb, wo = _gla(p_all, batch, seq, wg, bgate, bng, ps_all, layer(state_gla_S),
                                            (layer(w_pa), layer(w_pb), layer(w_out)))

    y_prompt = _out(ya, yb, p_all, xp, wpa, wpb, wo, lng, lnb, tm=256).reshape(batch, seq, D_MODEL)
    y_sample = _out(ya_s, yb_s, ps_all, xs, wpa, wpb, wo, lng, lnb, tm=nseq).reshape(nseq, 1, D_MODEL)
    p_conv = p_all.reshape(batch, seq, N_PROJ)[:, seq - (CONV_W - 1):, :2 * A_WIDTH]
    s_conv = jnp.concatenate([layer(state_conv)[:, 1:, :], ps_all[:, None, :2 * A_WIDTH]], axis=1)

    def stacked(a):
        return a.reshape((1,) + a.shape)

    return (y_prompt, y_sample,
            stacked(p_c), p_n.reshape(1, batch, A_HEADS, A_DK), stacked(p_m[:, :, 0, 0]),
            stacked(p_conv), stacked(p_s),
            stacked(s_c), s_n.reshape(1, nseq, A_HEADS, A_DK),
            stacked(s_m.reshape(nseq, A_HEADS, LANES)[:, :, 0]), stacked(s_conv), stacked(s_s))
```

```python
import functools

import jax
import jax.numpy as jnp
from jax import lax
from jax.experimental import pallas as pl
from jax.experimental.pallas import tpu as pltpu

F32 = jnp.float32
BF16 = jnp.bfloat16

D_MODEL = 2048
A_HEADS = 4
A_WIDTH = 1024
A_DK = 256
A_DV = 256
CONV_W = 4
B_HEADS = 4
B_WIDTH = 1024
B_KWIDTH = 512
B_DK = 128
B_DV = 256
GATE_RANK = 16
GATE_TAU = 16.0
ALPHA = 2.0 ** 0.25
LN_EPS = 1e-5
LOG2_E = 1.4426950408889634
LN_2 = 0.6931471805599453

LANES = 128
SUBLANES = 8
VMEM_LIMIT = 48 * 1024 * 1024

COL_Q = 0
COL_K = A_WIDTH
COL_BQ = 2 * A_WIDTH
COL_BK = COL_BQ + B_KWIDTH
COL_AV = COL_BK + B_KWIDTH
COL_AO = COL_AV + A_WIDTH
COL_AZ = COL_AO + A_WIDTH
COL_BV = COL_AZ + A_WIDTH
COL_BZ = COL_BV + B_WIDTH
COL_GA = COL_BZ + B_WIDTH
COL_GB = COL_GA + D_MODEL
COL_GATE = COL_GB + D_MODEL
N_PROJ = COL_GATE + A_HEADS * LANES

SRC_QK = 0
SRC_AV = 2 * A_WIDTH
SRC_I = SRC_AV + A_WIDTH
SRC_AO = SRC_I + 2 * A_HEADS
SRC_BQ = SRC_AO + 2 * A_WIDTH
SRC_BV = SRC_BQ + 2 * B_KWIDTH
SRC_BG = SRC_BV + B_WIDTH
SRC_BZ = SRC_BG + GATE_RANK
RELAYOUT_RUNS = ((COL_Q, COL_BQ, SRC_QK), (COL_BQ, COL_AV, SRC_BQ), (COL_AV, COL_AO, SRC_AV),
                 (COL_AO, COL_BV, SRC_AO), (COL_BV, COL_BZ, SRC_BV), (COL_BZ, COL_GATE, SRC_BZ))

MLSTM_CHUNK = 256
MLSTM_HEADS_PER_STEP = 4
GLA_STEP = 256
GLA_BLOCK = 16
GLA_HEADS_PER_STEP = 4


def _dot(a, b):
    return jnp.dot(a, b, preferred_element_type=F32)


def _dot_nt(a, b):
    return lax.dot_general(a, b, (((1,), (1,)), ((), ())), preferred_element_type=F32)


def _dot_tn(a, b):
    return lax.dot_general(a, b, (((0,), (0,)), ((), ())), preferred_element_type=F32)


def _mask_dot(mask_bf16, x):
    hi = x.astype(BF16)
    r1 = x - hi.astype(F32)
    mid = r1.astype(BF16)
    lo = (r1 - mid.astype(F32)).astype(BF16)
    return _dot(mask_bf16, hi) + _dot(mask_bf16, mid) + _dot(mask_bf16, lo)


def _log_sigmoid(z):
    return jnp.minimum(z, 0.0) - LN_2 * jnp.log2(1.0 + jnp.exp2(jnp.abs(z) * (-LOG2_E)))


def _sigmoid(z):
    return 1.0 / (1.0 + jnp.exp2(z * (-LOG2_E)))


def _silu(z):
    return z * _sigmoid(z)


def _proj_kernel(x_ref, w_ref, o_ref, xb_ref):
    @pl.when(pl.program_id(1) == 0)
    def _():
        xb_ref[...] = x_ref[...].astype(BF16)

    o_ref[...] = _dot_nt(xb_ref[...], w_ref[...])


def _proj(x, w_t, tm, tn):
    t, k = x.shape
    n = w_t.shape[0]
    return pl.pallas_call(
        _proj_kernel,
        grid=(t // tm, n // tn),
        in_specs=[pl.BlockSpec((tm, k), lambda i, j: (i, 0)),
                  pl.BlockSpec((tn, k), lambda i, j: (j, 0))],
        out_specs=pl.BlockSpec((tm, tn), lambda i, j: (i, j)),
        out_shape=jax.ShapeDtypeStruct((t, n), F32),
        scratch_shapes=[pltpu.VMEM((tm, k), BF16)],
        compiler_params=pltpu.CompilerParams(
            dimension_semantics=("parallel", "arbitrary"), vmem_limit_bytes=VMEM_LIMIT),
        name="proj",
    )(x, w_t)


def _mlstm_kernel(qp_ref, kp_ref, v_ref, ao_ref, az_ref, g_ref, cwq_ref, cwk_ref, cbq_ref, cbk_ref,
                  gb_ref, ng_ref, *rest, n_chunks):
    dec_in, (ya_ref, c_ref, n_ref, m_ref), dec_out, (tailq_ref, tailk_ref) = (
        rest[:21], rest[21:25], rest[25:29], rest[29:])
    L = MLSTM_CHUNK
    c = pl.program_id(2)

    @pl.when(c == 0)
    def _():
        c_ref[...] = jnp.zeros_like(c_ref)
        n_ref[...] = jnp.zeros_like(n_ref)
        m_ref[...] = jnp.zeros_like(m_ref)
        tailq_ref[...] = jnp.zeros_like(tailq_ref)
        tailk_ref[...] = jnp.zeros_like(tailk_ref)

    _mlstm_decode_step((pl.program_id(0) * n_chunks + c) % A_HEADS, *dec_in, *dec_out)

    sub = lax.broadcasted_iota(jnp.int32, (1, SUBLANES, 1), 1)

    def conv_silu(x_ref, tail_ref, w_ref, b_ref):
        x = x_ref[...]
        width = x.shape[1]
        x3 = jnp.concatenate([tail_ref[...], x], axis=0).reshape(L // SUBLANES + 1, SUBLANES, width)
        acc = b_ref[...] + x * w_ref[CONV_W - 1:CONV_W, :]
        for j in range(1, CONV_W):
            rot = pltpu.roll(x3, j, axis=1)
            xs = jnp.where(sub < j, rot[:-1], rot[1:]).reshape(L, width)
            acc = acc + xs * w_ref[CONV_W - 1 - j:CONV_W - j, :]
        tail_ref[...] = x[L - SUBLANES:, :]
        return _silu(acc)

    q_all = conv_silu(qp_ref, tailq_ref, cwq_ref, cbq_ref)
    k_all = conv_silu(kp_ref, tailk_ref, cwk_ref, cbk_ref) * (A_DK ** -0.5)

    lane = lax.broadcasted_iota(jnp.int32, (L, LANES), 1)
    row = lax.broadcasted_iota(jnp.int32, (L, L), 0)
    col = lax.broadcasted_iota(jnp.int32, (L, L), 1)
    causal = row >= col
    causal_b = causal.astype(BF16)

    for hh in range(MLSTM_HEADS_PER_STEP):
        cols = slice(hh * A_DK, (hh + 1) * A_DK)
        q, k, v = q_all[:, cols], k_all[:, cols], v_ref[:, cols]

        g = g_ref[:, hh * LANES:(hh + 1) * LANES] + gb_ref[:, hh * LANES:(hh + 1) * LANES]
        g2 = jnp.where(lane == 1, _log_sigmoid(g), g)
        cum = _mask_dot(causal_b, g2)
        x2 = jnp.where(lane == 1, cum, g2)
        x2t = x2.T
        itil_col, b_col = x2[:, 0:1], x2[:, 1:2]
        itil_row, b_row = x2t[0:1, :], x2t[1:2, :]

        m_prev = m_ref[0, hh, 0:1, 0:1]
        dmat = jnp.where(causal, b_col - b_row + itil_row, -jnp.inf)
        inter = b_col + m_prev
        m_t = jnp.maximum(inter, jnp.max(dmat, axis=1, keepdims=True))
        w = jnp.exp(dmat - m_t)
        decay = jnp.exp(inter - m_t)

        qb, kb, vb = q.astype(BF16), k.astype(BF16), v.astype(BF16)
        c_old = c_ref[0, hh]
        n_old = n_ref[0, hh]
        s = _dot_nt(qb, kb) * w
        num = decay * _dot(qb, c_old.astype(BF16)) + _dot(s.astype(BF16), vb)
        den = decay * jnp.sum(q * n_old, axis=1, keepdims=True) + jnp.sum(s, axis=1, keepdims=True)
        inv = 1.0 / jnp.maximum(jnp.abs(den), jnp.exp(-m_t))

        m_new = m_t[L - 1:L, :]
        b_last = b_col[L - 1:L, :]
        wk = jnp.exp(b_last - b_col + itil_col - m_new)
        dec = jnp.exp(b_last + m_prev - m_new)
        kw = k * wk
        c_ref[0, hh] = dec * c_old + _dot(kw.T.astype(BF16), vb)
        n_ref[0, hh] = dec * n_old + jnp.sum(kw, axis=0, keepdims=True)
        m_ref[0, hh] = jnp.broadcast_to(m_new, (1, LANES))

        mu = jnp.mean(num, axis=1, keepdims=True)
        hc = num - mu
        var = jnp.mean(hc * hc, axis=1, keepdims=True)
        hn = hc * (inv * lax.rsqrt(var * (inv * inv) + LN_EPS)) * ng_ref[:, cols]
        ya_ref[:, cols] = (hn * _sigmoid(ao_ref[:, cols]) * _silu(az_ref[:, cols])).astype(ya_ref.dtype)


def _mlstm(p_all, batch, seq, conv_w, conv_b, gate_bias, a_norm_g, p_s, conv_state, c_state, n_state, m_state):
    L = MLSTM_CHUNK
    hps = MLSTM_HEADS_PER_STEP
    nc = seq // L
    t = batch * seq
    wd = hps * A_DK
    k_off = A_WIDTH // wd

    def rows(b, h, c):
        return b * nc + c

    def pcol(base):
        return pl.BlockSpec((L, wd), lambda b, h, c: (rows(b, h, c), base // wd + h))

    in_specs = [
        pcol(COL_Q), pcol(COL_K), pcol(COL_AV), pcol(COL_AO), pcol(COL_AZ),
        pl.BlockSpec((L, hps * LANES), lambda b, h, c: (rows(b, h, c), COL_GATE // (hps * LANES) + h)),
        pl.BlockSpec((CONV_W, wd), lambda b, h, c: (0, h)),
        pl.BlockSpec((CONV_W, wd), lambda b, h, c: (0, k_off + h)),
        pl.BlockSpec((1, wd), lambda b, h, c: (0, h)),
        pl.BlockSpec((1, wd), lambda b, h, c: (0, k_off + h)),
        pl.BlockSpec((1, hps * LANES), lambda b, h, c: (0, h)),
        pl.BlockSpec((1, wd), lambda b, h, c: (0, h)),
    ]
    out_specs = [
        pl.BlockSpec((L, wd), lambda b, h, c: (rows(b, h, c), h)),
        pl.BlockSpec((1, hps, A_DK, A_DV), lambda b, h, c: (b, h, 0, 0)),
        pl.BlockSpec((1, hps, 1, A_DK), lambda b, h, c: (b, h, 0, 0)),
        pl.BlockSpec((1, hps, 1, LANES), lambda b, h, c: (b, h, 0, 0)),
    ]
    out_shape = [
        jax.ShapeDtypeStruct((t, A_WIDTH), BF16),
        jax.ShapeDtypeStruct((batch, A_HEADS, A_DK, A_DV), F32),
        jax.ShapeDtypeStruct((batch, A_HEADS, 1, A_DK), F32),
        jax.ShapeDtypeStruct((batch, A_HEADS, 1, LANES), F32),
    ]

    assert hps == A_HEADS
    nseq = p_s.shape[0]
    TB = DEC_TOKENS
    dstep = _decode_step(batch, nc, nseq)

    def dcol(base, width):
        return pl.BlockSpec((TB, width), lambda b, h, c: (dstep(b, h, c)[0], base // width + dstep(b, h, c)[1]))

    def dconv(j, base):
        return pl.BlockSpec((TB, A_DK), lambda b, h, c: (dstep(b, h, c)[0],
                                                        (j * 2 * A_WIDTH + base) // A_DK + dstep(b, h, c)[1]))

    def dhead(shape, off=0):
        return pl.BlockSpec(shape, lambda b, h, c: (0, off + dstep(b, h, c)[1]))

    def dstate(*dims):
        return pl.BlockSpec((TB, 1) + dims, lambda b, h, c: dstep(b, h, c) + (0,) * len(dims))

    in_specs += [
        dcol(COL_Q, A_DK), dcol(COL_K, A_DK), dcol(COL_AV, A_DV), dcol(COL_AO, A_DV), dcol(COL_AZ, A_DV),
        dcol(COL_GATE, LANES),
        dconv(0, 0), dconv(1, 0), dconv(2, 0), dconv(0, A_WIDTH), dconv(1, A_WIDTH), dconv(2, A_WIDTH),
        dhead((CONV_W, A_DK)), dhead((CONV_W, A_DK), A_HEADS), dhead((1, A_DK)), dhead((1, A_DK), A_HEADS),
        dhead((1, LANES)), dhead((1, A_DV)),
        dstate(A_DK, A_DV), dstate(1, A_DK),
        pl.BlockSpec((TB, A_HEADS), lambda b, h, c: (dstep(b, h, c)[0], 0)),
    ]
    out_specs += [
        pl.BlockSpec((TB, A_DV), lambda b, h, c: dstep(b, h, c)),
        dstate(A_DK, A_DV), dstate(1, A_DK),
        pl.BlockSpec((TB, LANES), lambda b, h, c: dstep(b, h, c)),
    ]
    out_shape += [
        jax.ShapeDtypeStruct((nseq, A_WIDTH), BF16),
        jax.ShapeDtypeStruct((nseq, A_HEADS, A_DK, A_DV), F32),
        jax.ShapeDtypeStruct((nseq, A_HEADS, 1, A_DK), F32),
        jax.ShapeDtypeStruct((nseq, A_HEADS * LANES), F32),
    ]
    return pl.pallas_call(
        functools.partial(_mlstm_kernel, n_chunks=nc),
        grid=(batch, A_HEADS // hps, nc),
        in_specs=in_specs,
        out_specs=out_specs,
        out_shape=out_shape,
        scratch_shapes=[pltpu.VMEM((SUBLANES, wd), F32),
                        pltpu.VMEM((SUBLANES, wd), F32)],
        compiler_params=pltpu.CompilerParams(
            dimension_semantics=("parallel", "parallel", "arbitrary"), vmem_limit_bytes=VMEM_LIMIT),
        name="mlstm",
    )(p_all, p_all, p_all, p_all, p_all, p_all, conv_w, conv_w, conv_b, conv_b, gate_bias, a_norm_g,
      p_s, p_s, p_s, p_s, p_s, p_s,
      conv_state, conv_state, conv_state, conv_state, conv_state, conv_state,
      conv_w, conv_w, conv_b, conv_b, gate_bias, a_norm_g, c_state, n_state, m_state)


def _gla_gate_log(bg_tile, wg, bgate):
    lr = bg_tile[:, 2:2 + GATE_RANK]
    z = _dot(lr.astype(BF16), wg.astype(BF16)) + bgate
    return _log_sigmoid(z) / GATE_TAU


def _gla_kernel(q_ref, k_ref, v_ref, bz_ref, g_ref, wg_ref, bgate_ref, ng_ref, *rest):
    dec_in, w_in, (yb_ref, s_out_ref), dec_out, w_out, (st_ref, ds_ref, sb_ref) = (
        rest[:9], rest[9:12], rest[12:14], rest[14:16], rest[16:19], rest[19:])
    L = GLA_STEP
    B = GLA_BLOCK
    nb = L // B
    c = pl.program_id(2)

    @pl.when(c == 0)
    def _():
        st_ref[...] = jnp.zeros_like(st_ref)

    _gla_decode_step(*dec_in, *dec_out)
    for src_ref, dst_ref in zip(w_in, w_out):
        dst_ref[...] = src_ref[...].astype(BF16)

    row = lax.broadcasted_iota(jnp.int32, (L, L), 0)
    col = lax.broadcasted_iota(jnp.int32, (L, L), 1)
    same = (row // B) == (col // B)
    tri_b = (same & (row >= col)).astype(BF16)
    same_b = same.astype(BF16)
    t_in = lax.broadcasted_iota(jnp.int32, (L, 1), 0) % B
    s_lane = lax.broadcasted_iota(jnp.int32, (1, LANES), 1)

    for hh in range(GLA_HEADS_PER_STEP):
        kcols = slice(hh * B_DK, (hh + 1) * B_DK)
        vcols = slice(hh * B_DV, (hh + 1) * B_DV)
        loga = _gla_gate_log(g_ref[:, hh * LANES:(hh + 1) * LANES], wg_ref[:, kcols], bgate_ref[:, kcols])
        bc = _mask_dot(tri_b, loga)
        bl = _mask_dot(same_b, loga)
        k = k_ref[:, kcols]
        qs = q_ref[:, kcols] * (B_DK ** -0.5)
        qt = (qs * jnp.exp(bc)).astype(BF16)
        kt = (k * jnp.exp(bl - bc)).astype(BF16)
        eb = jnp.exp(bl)
        vb = v_ref[:, vcols].astype(BF16)

        bc3, q3, k3 = (x.reshape(nb, B, B_DK) for x in (bc * LOG2_E, qs, k))
        groups = B // SUBLANES
        bcg = [bc3[:, g * SUBLANES:(g + 1) * SUBLANES, :] for g in range(groups)]
        qg = [q3[:, g * SUBLANES:(g + 1) * SUBLANES, :] for g in range(groups)]
        ag = [jnp.zeros((nb, SUBLANES, LANES), F32) for _ in range(groups)]
        for s in range(B):
            bc_s, k_s = bc3[:, s:s + 1, :], k3[:, s:s + 1, :]
            for g in range(s // SUBLANES, groups):
                e = jnp.exp2(bcg[g] - bc_s)
                a_col = jnp.sum(qg[g] * k_s * e, axis=2, keepdims=True)
                ag[g] = jnp.where(s_lane == s, a_col, ag[g])
        a = jnp.concatenate(ag, axis=1).reshape(L, LANES)
        ab = jnp.where(t_in >= s_lane, a, 0.0)[:, 0:B].astype(BF16)

        for j in range(nb):
            ds_ref[hh, j] = _dot_tn(vb[j * B:(j + 1) * B, :], kt[j * B:(j + 1) * B, :])
        st = st_ref[hh]
        for j in range(nb):
            sb_ref[hh, j] = st.astype(BF16)
            st = st * eb[j * B:j * B + 1, :] + ds_ref[hh, j]
        st_ref[hh] = st
        o = jnp.concatenate(
            [_dot(ab[j * B:(j + 1) * B, :], vb[j * B:(j + 1) * B, :])
             + _dot_nt(qt[j * B:(j + 1) * B, :], sb_ref[hh, j]) for j in range(nb)], axis=0)

        on = o * lax.rsqrt(jnp.mean(o * o, axis=1, keepdims=True) + LN_EPS) * ng_ref[:, vcols]
        yb_ref[:, vcols] = (on * _silu(bz_ref[:, vcols])).astype(yb_ref.dtype)

    @pl.when(c == pl.num_programs(2) - 1)
    def _():
        for hh in range(GLA_HEADS_PER_STEP):
            s_out_ref[0, hh] = st_ref[hh].T


def _gla(p_all, batch, seq, w_gate_up, b_gate, b_norm_g, p_s, s_state, cast_weights):
    L = GLA_STEP
    hps = GLA_HEADS_PER_STEP
    nc = seq // L
    t = batch * seq
    kw, vw = hps * B_DK, hps * B_DV

    def rows(b, h, c):
        return b * nc + c

    in_specs = [
        pl.BlockSpec((L, kw), lambda b, h, c: (rows(b, h, c), COL_BQ // kw + h)),
        pl.BlockSpec((L, kw), lambda b, h, c: (rows(b, h, c), COL_BK // kw + h)),
        pl.BlockSpec((L, vw), lambda b, h, c: (rows(b, h, c), COL_BV // vw + h)),
        pl.BlockSpec((L, vw), lambda b, h, c: (rows(b, h, c), COL_BZ // vw + h)),
        pl.BlockSpec((L, hps * LANES), lambda b, h, c: (rows(b, h, c), COL_GATE // (hps * LANES) + h)),
        pl.BlockSpec((GATE_RANK, kw), lambda b, h, c: (0, h)),
        pl.BlockSpec((1, kw), lambda b, h, c: (0, h)),
        pl.BlockSpec((1, vw), lambda b, h, c: (0, h)),
    ]
    out_specs = [
        pl.BlockSpec((L, vw), lambda b, h, c: (rows(b, h, c), h)),
        pl.BlockSpec((1, hps, B_DK, B_DV), lambda b, h, c: (b, h, 0, 0)),
    ]
    out_shape = [
        jax.ShapeDtypeStruct((t, B_WIDTH), BF16),
        jax.ShapeDtypeStruct((batch, B_HEADS, B_DK, B_DV), F32),
    ]

    assert hps == B_HEADS
    nseq = p_s.shape[0]
    TB = DEC_TOKENS
    dstep = _decode_step(batch, nc, nseq)

    def dcol(base, width):
        return pl.BlockSpec((TB, width), lambda b, h, c: (dstep(b, h, c)[0], base // width + dstep(b, h, c)[1]))

    def dhead(shape):
        return pl.BlockSpec(shape, lambda b, h, c: (0, dstep(b, h, c)[1]))

    dstate = pl.BlockSpec((TB, 1, B_DK, B_DV), lambda b, h, c: dstep(b, h, c) + (0, 0))
    in_specs += [
        dcol(COL_BQ, B_DK), dcol(COL_BK, B_DK), dcol(COL_BV, B_DV), dcol(COL_BZ, B_DV), dcol(COL_GATE, LANES),
        dhead((GATE_RANK, B_DK)), dhead((1, B_DK)), dhead((1, B_DV)), dstate,
    ]
    out_specs += [pl.BlockSpec((TB, B_DV), lambda b, h, c: dstep(b, h, c)), dstate]
    out_shape += [
        jax.ShapeDtypeStruct((nseq, B_WIDTH), BF16),
        jax.ShapeDtypeStruct((nseq, B_HEADS, B_DK, B_DV), F32),
    ]

    n_steps = batch * nc
    for wmat in cast_weights:
        slab = pl.BlockSpec((wmat.shape[0] // n_steps, wmat.shape[1]), lambda b, h, c: (rows(b, h, c), 0))
        in_specs.append(slab)
        out_specs.append(slab)
        out_shape.append(jax.ShapeDtypeStruct(wmat.shape, BF16))
    return pl.pallas_call(
        _gla_kernel,
        grid=(batch, B_HEADS // hps, nc),
        in_specs=in_specs,
        out_specs=out_specs,
        out_shape=out_shape,
        scratch_shapes=[pltpu.VMEM((hps, B_DV, B_DK), F32),
                        pltpu.VMEM((hps, L // GLA_BLOCK, B_DV, B_DK), F32),
                        pltpu.VMEM((hps, L // GLA_BLOCK, B_DV, B_DK), BF16)],
        compiler_params=pltpu.CompilerParams(
            dimension_semantics=("parallel", "parallel", "arbitrary"), vmem_limit_bytes=VMEM_LIMIT),
        name="gla",
    )(p_all, p_all, p_all, p_all, p_all, w_gate_up, b_gate, b_norm_g,
      p_s, p_s, p_s, p_s, p_s, w_gate_up, b_gate, b_norm_g, s_state, *cast_weights)


DEC_TOKENS = 16


def _decode_step(batch, nc, nseq):
    assert batch * nc == (nseq // DEC_TOKENS) * A_HEADS and A_HEADS == B_HEADS

    def step(b, h, c):
        s = b * nc + c
        return s // A_HEADS, s % A_HEADS
    return step


def _mlstm_decode_step(h_idx, qp_ref, kp_ref, av_ref, ao_ref, az_ref, g_ref,
                       sq0_ref, sq1_ref, sq2_ref, sk0_ref, sk1_ref, sk2_ref,
                       cwq_ref, cwk_ref, cbq_ref, cbk_ref, gb_ref, ang_ref, c_ref, n_ref, m_ref,
                       ya_ref, c_out_ref, n_out_ref, m_out_ref):
    TB = DEC_TOKENS

    def conv_silu(s0, s1, s2, x, w_ref, b_ref):
        acc = b_ref[...] + s0[...] * w_ref[0:1, :]
        acc = acc + s1[...] * w_ref[1:2, :]
        acc = acc + s2[...] * w_ref[2:3, :]
        acc = acc + x[...] * w_ref[3:4, :]
        return _silu(acc)

    q = conv_silu(sq0_ref, sq1_ref, sq2_ref, qp_ref, cwq_ref, cbq_ref)
    k = conv_silu(sk0_ref, sk1_ref, sk2_ref, kp_ref, cwk_ref, cbk_ref) * (A_DK ** -0.5)
    v = av_ref[...]
    g = g_ref[...]
    gbias = gb_ref[...]
    itil = g[:, 0:1] + gbias[:, 0:1]
    logf = _log_sigmoid(g[:, 1:2] + gbias[:, 1:2])
    lane_h = lax.broadcasted_iota(jnp.int32, (TB, A_HEADS), 1)
    m_prev = jnp.sum(jnp.where(lane_h == h_idx, m_ref[...], 0.0), axis=1, keepdims=True)
    inter = logf + m_prev
    m_t = jnp.maximum(inter, itil)
    w = jnp.exp(itil - m_t)
    decay = jnp.exp(inter - m_t)
    n_old = n_ref[:, 0, 0, :]
    s = jnp.sum(q * k, axis=1, keepdims=True) * w
    den = decay * jnp.sum(q * n_old, axis=1, keepdims=True) + s
    scale = 1.0 / jnp.maximum(jnp.abs(den), jnp.exp(-m_t))
    kw = k * w
    n_out_ref[:, 0, 0, :] = decay * n_old + kw
    m_out_ref[...] = jnp.broadcast_to(m_t, (TB, LANES))

    rows = lax.broadcasted_iota(jnp.int32, (TB, 1), 0)
    qb, kwb, vb = q.astype(BF16), kw.astype(BF16), v.astype(BF16)
    h_rows = []
    for t in range(TB):
        c_old = c_ref[t, 0]
        qc = _dot(qb, c_old.astype(BF16))[t:t + 1, :]
        h_rows.append((decay[t:t + 1, :] * qc + s[t:t + 1, :] * v[t:t + 1, :]) * scale[t:t + 1, :])
        outer = _dot_tn(jnp.where(rows == t, kwb, jnp.zeros_like(kwb)), vb)
        c_out_ref[t, 0] = decay[t:t + 1, :] * c_old + outer
    h = jnp.concatenate(h_rows, axis=0)
    mu = jnp.mean(h, axis=1, keepdims=True)
    hc = h - mu
    var = jnp.mean(hc * hc, axis=1, keepdims=True)
    hn = hc * lax.rsqrt(var + LN_EPS) * ang_ref[...]
    ya_ref[...] = (hn * _sigmoid(ao_ref[...]) * _silu(az_ref[...])).astype(ya_ref.dtype)


def _gla_decode_step(bq_ref, bk_ref, bv_ref, bz_ref, g_ref, wg_ref, bgate_ref, bng_ref, s_ref,
                     yb_ref, s_out_ref):
    TB = DEC_TOKENS
    gq = bq_ref[...] * (B_DK ** -0.5)
    gk = bk_ref[...]
    gv = bv_ref[...]
    loga = _gla_gate_log(g_ref[...], wg_ref[...], bgate_ref[...])
    eb = jnp.exp(loga)
    a = jnp.sum(gq * gk, axis=1, keepdims=True)
    ebt = jnp.concatenate([eb, jnp.zeros((LANES - TB, B_DK), F32)], axis=0).T
    rows = lax.broadcasted_iota(jnp.int32, (TB, 1), 0)
    qeb, gkb, gvb = (gq * eb).astype(BF16), gk.astype(BF16), gv.astype(BF16)
    o_rows = []
    for t in range(TB):
        s_old = s_ref[t, 0]
        o_rows.append(_dot(qeb, s_old.astype(BF16))[t:t + 1, :] + a[t:t + 1, :] * gv[t:t + 1, :])
        outer = _dot_tn(jnp.where(rows == t, gkb, jnp.zeros_like(gkb)), gvb)
        s_out_ref[t, 0] = ebt[:, t:t + 1] * s_old + outer
    o = jnp.concatenate(o_rows, axis=0)
    on = o * lax.rsqrt(jnp.mean(o * o, axis=1, keepdims=True) + LN_EPS) * bng_ref[...]
    yb_ref[...] = (on * _silu(bz_ref[...])).astype(yb_ref.dtype)


def _out_kernel(ya_ref, yb_ref, ga_ref, gb_ref, x_ref, wpa_ref, wpb_ref, wo_ref, lng_ref, lnb_ref, o_ref):
    pa = _dot(ya_ref[...], wpa_ref[...])
    pb = _dot(yb_ref[...], wpb_ref[...])
    merged = _sigmoid(ga_ref[...]) * pa + _sigmoid(gb_ref[...]) * pb
    y = _dot(merged.astype(BF16), wo_ref[...])
    r = ALPHA * x_ref[...] + y
    mu = jnp.mean(r, axis=1, keepdims=True)
    rc = r - mu
    var = jnp.mean(rc * rc, axis=1, keepdims=True)
    o_ref[...] = rc * lax.rsqrt(var + LN_EPS) * lng_ref[...] + lnb_ref[...]


def _out(ya, yb, p_all, x, w_pa, w_pb, w_out, ln_g, ln_b, tm):
    t = x.shape[0]
    const = lambda i: (0, 0)
    single = pl.Buffered(1)
    in_specs = [
        pl.BlockSpec((tm, A_WIDTH), lambda i: (i, 0)),
        pl.BlockSpec((tm, B_WIDTH), lambda i: (i, 0)),
        pl.BlockSpec((tm, D_MODEL), lambda i: (i, COL_GA // D_MODEL)),
        pl.BlockSpec((tm, D_MODEL), lambda i: (i, COL_GB // D_MODEL)),
        pl.BlockSpec((tm, D_MODEL), lambda i: (i, 0)),
        pl.BlockSpec((A_WIDTH, D_MODEL), const, pipeline_mode=single),
        pl.BlockSpec((B_WIDTH, D_MODEL), const, pipeline_mode=single),
        pl.BlockSpec((D_MODEL, D_MODEL), const, pipeline_mode=single),
        pl.BlockSpec((1, D_MODEL), const),
        pl.BlockSpec((1, D_MODEL), const),
    ]
    return pl.pallas_call(
        _out_kernel,
        grid=(t // tm,),
        in_specs=in_specs,
        out_specs=pl.BlockSpec((tm, D_MODEL), lambda i: (i, 0)),
        out_shape=jax.ShapeDtypeStruct((t, D_MODEL), F32),
        compiler_params=pltpu.CompilerParams(
            dimension_semantics=("parallel",), vmem_limit_bytes=VMEM_LIMIT),
        name="outproj",
    )(ya, yb, p_all, p_all, x, w_pa, w_pb, w_out, ln_g, ln_b)


RELAYOUT_TN = 512
GATE_SRC_ROWS = 4 * SUBLANES


def _relayout_kernel(src_ref, gif_ref, gb0_ref, gb1_ref, xs_ref, o_ref, ps_ref):
    j = pl.program_id(0)
    tn = RELAYOUT_TN
    n_main = COL_GATE // tn

    @pl.when(j < n_main)
    def _():
        o_ref[...] = src_ref[...].astype(BF16)

    @pl.when(j == n_main)
    def _():
        g = jnp.concatenate([gif_ref[...], gb0_ref[...], gb1_ref[...],
                             jnp.zeros((SUBLANES, src_ref.shape[1]), F32)], axis=0).astype(BF16)
        r = lax.broadcasted_iota(jnp.int32, (tn, GATE_SRC_ROWS), 0)
        c = lax.broadcasted_iota(jnp.int32, (tn, GATE_SRC_ROWS), 1)
        h, l = r // LANES, r % LANES
        sel = (((l == 0) & (c == h)) | ((l == 1) & (c == A_HEADS + h))
               | ((l >= 2) & (l < 2 + GATE_RANK) & (c == l + (2 * A_HEADS - 2))))
        o_ref[...] = _dot(sel.astype(BF16), g).astype(BF16)

    ps_ref[...] = _dot_nt(xs_ref[...].astype(BF16), o_ref[...])


def _relayout_w_in(w_in_t, xs):
    n_src, k = w_in_t.shape
    ts = xs.shape[0]
    tn = RELAYOUT_TN

    def src_row(j):
        shift = 0
        for first, last, src in RELAYOUT_RUNS:
            assert first % tn == 0 and last % tn == 0 and (src - first) % SUBLANES == 0
            shift = jnp.where((j >= first // tn) & (j < last // tn), (src - first) // SUBLANES, shift)
        group = jnp.minimum(j * (tn // SUBLANES) + shift, (n_src - tn) // SUBLANES)
        return group * SUBLANES

    return pl.pallas_call(
        _relayout_kernel,
        grid=(N_PROJ // tn,),
        in_specs=[pl.BlockSpec((pl.Element(tn), pl.Element(k)), lambda j: (src_row(j), 0)),
                  pl.BlockSpec((SUBLANES, k), lambda j: (SRC_I // SUBLANES, 0)),
                  pl.BlockSpec((SUBLANES, k), lambda j: (SRC_BG // SUBLANES, 0)),
                  pl.BlockSpec((SUBLANES, k), lambda j: (SRC_BG // SUBLANES + 1, 0)),
                  pl.BlockSpec((ts, k), lambda j: (0, 0))],
        out_specs=[pl.BlockSpec((tn, k), lambda j: (j, 0)),
                   pl.BlockSpec((ts, tn), lambda j: (0, j))],
        out_shape=[jax.ShapeDtypeStruct((N_PROJ, k), BF16), jax.ShapeDtypeStruct((ts, N_PROJ), F32)],
        compiler_params=pltpu.CompilerParams(
            dimension_semantics=("parallel",), vmem_limit_bytes=VMEM_LIMIT),
        name="relayout",
    )(w_in_t, w_in_t, w_in_t, w_in_t, xs)


def kernel(x_prompt, x_sample, state_mlstm_C, state_mlstm_n, state_mlstm_m, state_conv, state_gla_S,
           w_in, conv_w, conv_b, b_i, b_f, a_norm_g, w_gate_up, b_gate, b_norm_g, w_pa, w_pb, w_out,
           ln_g, ln_b):
    batch, seq, _ = x_prompt.shape
    nseq = x_sample.shape[0]
    assert w_in.shape[0] == 1, "single-layer step"
    d = 0

    def layer(a):
        return a.reshape(a.shape[1:])

    xs = x_sample.reshape(nseq, D_MODEL)
    wp, ps_all = _relayout_w_in(layer(w_in).T, xs)
    cw = conv_w[d]
    cb = conv_b[d][None, :]
    gate_bias = jnp.zeros((A_HEADS, LANES), F32).at[:, 0].set(b_i[d]).at[:, 1].set(b_f[d]).reshape(1, -1)
    ang = a_norm_g[d][None, :]
    bng = b_norm_g[d][None, :]
    wg = w_gate_up[d]
    bgate = b_gate[d][None, :]
    lng, lnb = ln_g[d][None, :], ln_b[d][None, :]

    xp = x_prompt.reshape(batch * seq, D_MODEL)
    p_all = _proj(xp, wp, tm=1024, tn=1280)

    conv_state = state_conv.reshape(nseq, (CONV_W - 1) * 2 * A_WIDTH)
    ya, p_c, p_n, p_m, ya_s, s_c, s_n, s_m = _mlstm(
        p_all, batch, seq, cw, cb, gate_bias, ang, ps_all, conv_state,
        layer(state_mlstm_C), state_mlstm_n.reshape(nseq, A_HEADS, 1, A_DK), layer(state_mlstm_m))
    yb, p_s, yb_s, s_s, wpa, wpb, wo = _gla(p_all, batch, seq, wg, bgate, bng, ps_all, layer(state_gla_S),
                                            (layer(w_pa), layer(w_pb), layer(w_out)))

    y_prompt = _out(ya, yb, p_all, xp, wpa, wpb, wo, lng, lnb, tm=256).reshape(batch, seq, D_MODEL)
    y_sample = _out(ya_s, yb_s, ps_all, xs, wpa, wpb, wo, lng, lnb, tm=nseq).reshape(nseq, 1, D_MODEL)
    p_conv = p_all.reshape(batch, seq, N_PROJ)[:, seq - (CONV_W - 1):, :2 * A_WIDTH]
    s_conv = jnp.concatenate([layer(state_conv)[:, 1:, :], ps_all[:, None, :2 * A_WIDTH]], axis=1)

    def stacked(a):
        return a.reshape((1,) + a.shape)

    return (y_prompt, y_sample,
            stacked(p_c), p_n.reshape(1, batch, A_HEADS, A_DK), stacked(p_m[:, :, 0, 0]),
            stacked(p_conv), stacked(p_s),
            stacked(s_c), s_n.reshape(1, nseq, A_HEADS, A_DK),
            stacked(s_m.reshape(nseq, A_HEADS, LANES)[:, :, 0]), stacked(s_conv), stacked(s_s))
```

```python
import functools

import jax
import jax.numpy as jnp
from jax import lax
from jax.experimental import pallas as pl
from jax.experimental.pallas import tpu as pltpu

F32 = jnp.float32
BF16 = jnp.bfloat16

D_MODEL = 2048
A_HEADS = 4
A_WIDTH = 1024
A_DK = 256
A_DV = 256
CONV_W = 4
B_HEADS = 4
B_WIDTH = 1024
B_KWIDTH = 512
B_DK = 128
B_DV = 256
GATE_RANK = 16
GATE_TAU = 16.0
ALPHA = 2.0 ** 0.25
LN_EPS = 1e-5
LOG2_E = 1.4426950408889634
LN_2 = 0.6931471805599453

LANES = 128
SUBLANES = 8
VMEM_LIMIT = 48 * 1024 * 1024

COL_Q = 0
COL_K = A_WIDTH
COL_BQ = 2 * A_WIDTH
COL_BK = COL_BQ + B_KWIDTH
COL_AV = COL_BK + B_KWIDTH
COL_AO = COL_AV + A_WIDTH
COL_AZ = COL_AO + A_WIDTH
COL_BV = COL_AZ + A_WIDTH
COL_BZ = COL_BV + B_WIDTH
COL_GA = COL_BZ + B_WIDTH
COL_GB = COL_GA + D_MODEL
COL_GATE = COL_GB + D_MODEL
GATE_LANE_I = 0
GATE_LANE_F = A_HEADS
GATE_LANE_LR = 2 * A_HEADS
MXU_COLS = 256
N_PROJ = COL_GATE + MXU_COLS
PROJ_TN = N_PROJ // 7
assert PROJ_TN * 7 == N_PROJ and PROJ_TN % MXU_COLS == 0
PROJ_VMEM_LIMIT = 52 * 1024 * 1024

SRC_QK = 0
SRC_AV = 2 * A_WIDTH
SRC_I = SRC_AV + A_WIDTH
SRC_AO = SRC_I + 2 * A_HEADS
SRC_BQ = SRC_AO + 2 * A_WIDTH
SRC_BV = SRC_BQ + 2 * B_KWIDTH
SRC_BG = SRC_BV + B_WIDTH
SRC_BZ = SRC_BG + GATE_RANK
RELAYOUT_RUNS = ((COL_Q, COL_BQ, SRC_QK), (COL_BQ, COL_AV, SRC_BQ), (COL_AV, COL_AO, SRC_AV),
                 (COL_AO, COL_BV, SRC_AO), (COL_BV, COL_BZ, SRC_BV), (COL_BZ, COL_GATE, SRC_BZ))

MLSTM_CHUNK = 256
MLSTM_HEADS_PER_STEP = 4
GLA_STEP = 256
GLA_BLOCK = 16
GLA_HEADS_PER_STEP = 4


def _dot(a, b):
    return jnp.dot(a, b, preferred_element_type=F32)


def _dot_nt(a, b):
    return lax.dot_general(a, b, (((1,), (1,)), ((), ())), preferred_element_type=F32)


def _dot_tn(a, b):
    return lax.dot_general(a, b, (((0,), (0,)), ((), ())), preferred_element_type=F32)


def _mask_dot(mask_bf16, x):
    hi = x.astype(BF16)
    r1 = x - hi.astype(F32)
    mid = r1.astype(BF16)
    lo = (r1 - mid.astype(F32)).astype(BF16)
    return _dot(mask_bf16, hi) + _dot(mask_bf16, mid) + _dot(mask_bf16, lo)


def _log_sigmoid(z):
    return jnp.minimum(z, 0.0) - LN_2 * jnp.log2(1.0 + jnp.exp2(jnp.abs(z) * (-LOG2_E)))


def _sigmoid(z):
    return 1.0 / (1.0 + jnp.exp2(z * (-LOG2_E)))


def _silu(z):
    return z * _sigmoid(z)


def _proj_kernel(x_ref, w_ref, o_ref, xb_ref):
    @pl.when(pl.program_id(1) == 0)
    def _():
        xb_ref[...] = x_ref[...].astype(BF16)

    o_ref[...] = _dot_nt(xb_ref[...], w_ref[...])


def _proj(x, w_t, tm, tn):
    t, k = x.shape
    n = w_t.shape[0]
    return pl.pallas_call(
        _proj_kernel,
        grid=(t // tm, n // tn),
        in_specs=[pl.BlockSpec((tm, k), lambda i, j: (i, 0)),
                  pl.BlockSpec((tn, k), lambda i, j: (j, 0))],
        out_specs=pl.BlockSpec((tm, tn), lambda i, j: (i, j)),
        out_shape=jax.ShapeDtypeStruct((t, n), F32),
        scratch_shapes=[pltpu.VMEM((tm, k), BF16)],
        compiler_params=pltpu.CompilerParams(
            dimension_semantics=("parallel", "arbitrary"), vmem_limit_bytes=PROJ_VMEM_LIMIT),
        name="proj",
    )(x, w_t)


def _mlstm_kernel(qp_ref, kp_ref, v_ref, ao_ref, az_ref, g_ref, cwq_ref, cwk_ref, cbq_ref, cbk_ref,
                  gb_ref, ng_ref, *rest, n_chunks):
    dec_in, (ya_ref, c_ref, n_ref, m_ref), dec_out, (tailq_ref, tailk_ref) = (
        rest[:21], rest[21:25], rest[25:29], rest[29:])
    L = MLSTM_CHUNK
    c = pl.program_id(2)

    @pl.when(c == 0)
    def _():
        c_ref[...] = jnp.zeros_like(c_ref)
        n_ref[...] = jnp.zeros_like(n_ref)
        m_ref[...] = jnp.zeros_like(m_ref)
        tailq_ref[...] = jnp.zeros_like(tailq_ref)
        tailk_ref[...] = jnp.zeros_like(tailk_ref)

    _mlstm_decode_step((pl.program_id(0) * n_chunks + c) % A_HEADS, *dec_in, *dec_out)

    sub = lax.broadcasted_iota(jnp.int32, (1, SUBLANES, 1), 1)

    def conv_silu(x_ref, tail_ref, w_ref, b_ref):
        x = x_ref[...]
        width = x.shape[1]
        x3 = jnp.concatenate([tail_ref[...], x], axis=0).reshape(L // SUBLANES + 1, SUBLANES, width)
        acc = b_ref[...] + x * w_ref[CONV_W - 1:CONV_W, :]
        for j in range(1, CONV_W):
            rot = pltpu.roll(x3, j, axis=1)
            xs = jnp.where(sub < j, rot[:-1], rot[1:]).reshape(L, width)
            acc = acc + xs * w_ref[CONV_W - 1 - j:CONV_W - j, :]
        tail_ref[...] = x[L - SUBLANES:, :]
        return _silu(acc)

    q_all = conv_silu(qp_ref, tailq_ref, cwq_ref, cbq_ref)
    k_all = conv_silu(kp_ref, tailk_ref, cwk_ref, cbk_ref) * (A_DK ** -0.5)

    lane = lax.broadcasted_iota(jnp.int32, (L, LANES), 1)
    row = lax.broadcasted_iota(jnp.int32, (L, L), 0)
    col = lax.broadcasted_iota(jnp.int32, (L, L), 1)
    causal = row >= col
    causal_b = causal.astype(BF16)

    g = g_ref[...] + gb_ref[...]
    is_f = (lane >= GATE_LANE_F) & (lane < GATE_LANE_F + A_HEADS)
    g2 = jnp.where(is_f, _log_sigmoid(g), g)
    x2 = jnp.where(is_f, _mask_dot(causal_b, g2), g2)
    x2t = x2.T

    for hh in range(MLSTM_HEADS_PER_STEP):
        cols = slice(hh * A_DK, (hh + 1) * A_DK)
        q, k, v = q_all[:, cols], k_all[:, cols], v_ref[:, cols]
        li, lf = GATE_LANE_I + hh, GATE_LANE_F + hh
        itil_col, b_col = x2[:, li:li + 1], x2[:, lf:lf + 1]
        itil_row, b_row = x2t[li:li + 1, :], x2t[lf:lf + 1, :]

        m_prev = m_ref[0, hh, 0:1, 0:1]
        dmat = jnp.where(causal, b_col - b_row + itil_row, -jnp.inf)
        inter = b_col + m_prev
        m_t = jnp.maximum(inter, jnp.max(dmat, axis=1, keepdims=True))
        w = jnp.exp(dmat - m_t)
        decay = jnp.exp(inter - m_t)

        qb, kb, vb = q.astype(BF16), k.astype(BF16), v.astype(BF16)
        c_old = c_ref[0, hh]
        n_old = n_ref[0, hh]
        s = _dot_nt(qb, kb) * w
        num = decay * _dot(qb, c_old.astype(BF16)) + _dot(s.astype(BF16), vb)
        den = decay * jnp.sum(q * n_old, axis=1, keepdims=True) + jnp.sum(s, axis=1, keepdims=True)
        inv = 1.0 / jnp.maximum(jnp.abs(den), jnp.exp(-m_t))

        m_new = m_t[L - 1:L, :]
        b_last = b_col[L - 1:L, :]
        wk = jnp.exp(b_last - b_col + itil_col - m_new)
        dec = jnp.exp(b_last + m_prev - m_new)
        kw = k * wk
        c_ref[0, hh] = dec * c_old + _dot(kw.T.astype(BF16), vb)
        n_ref[0, hh] = dec * n_old + jnp.sum(kw, axis=0, keepdims=True)
        m_ref[0, hh] = jnp.broadcast_to(m_new, (1, LANES))

        mu = jnp.mean(num, axis=1, keepdims=True)
        hc = num - mu
        var = jnp.mean(hc * hc, axis=1, keepdims=True)
        hn = hc * (inv * lax.rsqrt(var * (inv * inv) + LN_EPS)) * ng_ref[:, cols]
        ya_ref[:, cols] = (hn * _sigmoid(ao_ref[:, cols]) * _silu(az_ref[:, cols])).astype(ya_ref.dtype)


def _mlstm(p_all, batch, seq, conv_w, conv_b, gate_bias, a_norm_g, p_s, conv_state, c_state, n_state, m_state):
    L = MLSTM_CHUNK
    hps = MLSTM_HEADS_PER_STEP
    nc = seq // L
    t = batch * seq
    wd = hps * A_DK
    k_off = A_WIDTH // wd

    def rows(b, h, c):
        return b * nc + c

    def pcol(base):
        return pl.BlockSpec((L, wd), lambda b, h, c: (rows(b, h, c), base // wd + h))

    in_specs = [
        pcol(COL_Q), pcol(COL_K), pcol(COL_AV), pcol(COL_AO), pcol(COL_AZ),
        pl.BlockSpec((L, LANES), lambda b, h, c: (rows(b, h, c), COL_GATE // LANES)),
        pl.BlockSpec((CONV_W, wd), lambda b, h, c: (0, h)),
        pl.BlockSpec((CONV_W, wd), lambda b, h, c: (0, k_off + h)),
        pl.BlockSpec((1, wd), lambda b, h, c: (0, h)),
        pl.BlockSpec((1, wd), lambda b, h, c: (0, k_off + h)),
        pl.BlockSpec((1, LANES), lambda b, h, c: (0, 0)),
        pl.BlockSpec((1, wd), lambda b, h, c: (0, h)),
    ]
    out_specs = [
        pl.BlockSpec((L, wd), lambda b, h, c: (rows(b, h, c), h)),
        pl.BlockSpec((1, hps, A_DK, A_DV), lambda b, h, c: (b, h, 0, 0)),
        pl.BlockSpec((1, hps, 1, A_DK), lambda b, h, c: (b, h, 0, 0)),
        pl.BlockSpec((1, hps, 1, LANES), lambda b, h, c: (b, h, 0, 0)),
    ]
    out_shape = [
        jax.ShapeDtypeStruct((t, A_WIDTH), BF16),
        jax.ShapeDtypeStruct((batch, A_HEADS, A_DK, A_DV), F32),
        jax.ShapeDtypeStruct((batch, A_HEADS, 1, A_DK), F32),
        jax.ShapeDtypeStruct((batch, A_HEADS, 1, LANES), F32),
    ]

    assert hps == A_HEADS
    nseq = p_s.shape[0]
    TB = DEC_TOKENS
    dstep = _decode_step(batch, nc, nseq)

    def dcol(base, width):
        return pl.BlockSpec((TB, width), lambda b, h, c: (dstep(b, h, c)[0], base // width + dstep(b, h, c)[1]))

    def dconv(j, base):
        return pl.BlockSpec((TB, A_DK), lambda b, h, c: (dstep(b, h, c)[0],
                                                        (j * 2 * A_WIDTH + base) // A_DK + dstep(b, h, c)[1]))

    def dhead(shape, off=0):
        return pl.BlockSpec(shape, lambda b, h, c: (0, off + dstep(b, h, c)[1]))

    def dstate(*dims):
        return pl.BlockSpec((TB, 1) + dims, lambda b, h, c: dstep(b, h, c) + (0,) * len(dims))

    in_specs += [
        dcol(COL_Q, A_DK), dcol(COL_K, A_DK), dcol(COL_AV, A_DV), dcol(COL_AO, A_DV), dcol(COL_AZ, A_DV),
        pl.BlockSpec((TB, LANES), lambda b, h, c: (dstep(b, h, c)[0], COL_GATE // LANES)),
        dconv(0, 0), dconv(1, 0), dconv(2, 0), dconv(0, A_WIDTH), dconv(1, A_WIDTH), dconv(2, A_WIDTH),
        dhead((CONV_W, A_DK)), dhead((CONV_W, A_DK), A_HEADS), dhead((1, A_DK)), dhead((1, A_DK), A_HEADS),
        pl.BlockSpec((1, LANES), lambda b, h, c: (0, 0)), dhead((1, A_DV)),
        dstate(A_DK, A_DV), dstate(1, A_DK),
        pl.BlockSpec((TB, A_HEADS), lambda b, h, c: (dstep(b, h, c)[0], 0)),
    ]
    out_specs += [
        pl.BlockSpec((TB, A_DV), lambda b, h, c: dstep(b, h, c)),
        dstate(A_DK, A_DV), dstate(1, A_DK),
        pl.BlockSpec((TB, LANES), lambda b, h, c: dstep(b, h, c)),
    ]
    out_shape += [
        jax.ShapeDtypeStruct((nseq, A_WIDTH), BF16),
        jax.ShapeDtypeStruct((nseq, A_HEADS, A_DK, A_DV), F32),
        jax.ShapeDtypeStruct((nseq, A_HEADS, 1, A_DK), F32),
        jax.ShapeDtypeStruct((nseq, A_HEADS * LANES), F32),
    ]
    return pl.pallas_call(
        functools.partial(_mlstm_kernel, n_chunks=nc),
        grid=(batch, A_HEADS // hps, nc),
        in_specs=in_specs,
        out_specs=out_specs,
        out_shape=out_shape,
        scratch_shapes=[pltpu.VMEM((SUBLANES, wd), F32),
                        pltpu.VMEM((SUBLANES, wd), F32)],
        compiler_params=pltpu.CompilerParams(
            dimension_semantics=("parallel", "parallel", "arbitrary"), vmem_limit_bytes=VMEM_LIMIT),
        name="mlstm",
    )(p_all, p_all, p_all, p_all, p_all, p_all, conv_w, conv_w, conv_b, conv_b, gate_bias, a_norm_g,
      p_s, p_s, p_s, p_s, p_s, p_s,
      conv_state, conv_state, conv_state, conv_state, conv_state, conv_state,
      conv_w, conv_w, conv_b, conv_b, gate_bias, a_norm_g, c_state, n_state, m_state)


def _gla_gate_log(bg_tile, wg, bgate):
    lr = bg_tile[:, GATE_LANE_LR:GATE_LANE_LR + GATE_RANK]
    z = _dot(lr.astype(BF16), wg.astype(BF16)) + bgate
    return _log_sigmoid(z) / GATE_TAU


def _gla_kernel(q_ref, k_ref, v_ref, bz_ref, g_ref, wg_ref, bgate_ref, ng_ref, *rest):
    dec_in, w_in, (yb_ref, s_out_ref), dec_out, w_out, (st_ref, ds_ref, sb_ref) = (
        rest[:9], rest[9:12], rest[12:14], rest[14:16], rest[16:19], rest[19:])
    L = GLA_STEP
    B = GLA_BLOCK
    nb = L // B
    c = pl.program_id(2)

    @pl.when(c == 0)
    def _():
        st_ref[...] = jnp.zeros_like(st_ref)

    _gla_decode_step(*dec_in, *dec_out)
    for src_ref, dst_ref in zip(w_in, w_out):
        dst_ref[...] = src_ref[...].astype(BF16)

    row = lax.broadcasted_iota(jnp.int32, (L, L), 0)
    col = lax.broadcasted_iota(jnp.int32, (L, L), 1)
    same = (row // B) == (col // B)
    tri_b = (same & (row >= col)).astype(BF16)
    same_b = same.astype(BF16)
    t_in = lax.broadcasted_iota(jnp.int32, (L, 1), 0) % B
    s_lane = lax.broadcasted_iota(jnp.int32, (1, LANES), 1)

    for hh in range(GLA_HEADS_PER_STEP):
        kcols = slice(hh * B_DK, (hh + 1) * B_DK)
        vcols = slice(hh * B_DV, (hh + 1) * B_DV)
        loga = _gla_gate_log(g_ref[...], wg_ref[:, kcols], bgate_ref[:, kcols])
        bc = _mask_dot(tri_b, loga)
        bl = _mask_dot(same_b, loga)
        k = k_ref[:, kcols]
        qs = q_ref[:, kcols] * (B_DK ** -0.5)
        qt = (qs * jnp.exp(bc)).astype(BF16)
        kt = (k * jnp.exp(bl - bc)).astype(BF16)
        eb = jnp.exp(bl)
        vb = v_ref[:, vcols].astype(BF16)

        bc3, q3, k3 = (x.reshape(nb, B, B_DK) for x in (bc * LOG2_E, qs, k))
        groups = B // SUBLANES
        bcg = [bc3[:, g * SUBLANES:(g + 1) * SUBLANES, :] for g in range(groups)]
        qg = [q3[:, g * SUBLANES:(g + 1) * SUBLANES, :] for g in range(groups)]
        ag = [jnp.zeros((nb, SUBLANES, LANES), F32) for _ in range(groups)]
        for s in range(B):
            bc_s, k_s = bc3[:, s:s + 1, :], k3[:, s:s + 1, :]
            for g in range(s // SUBLANES, groups):
                e = jnp.exp2(bcg[g] - bc_s)
                a_col = jnp.sum(qg[g] * k_s * e, axis=2, keepdims=True)
                ag[g] = jnp.where(s_lane == s, a_col, ag[g])
        a = jnp.concatenate(ag, axis=1).reshape(L, LANES)
        ab = jnp.where(t_in >= s_lane, a, 0.0)[:, 0:B].astype(BF16)

        for j in range(nb):
            ds_ref[hh, j] = _dot_tn(vb[j * B:(j + 1) * B, :], kt[j * B:(j + 1) * B, :])
        st = st_ref[hh]
        for j in range(nb):
            sb_ref[hh, j] = st.astype(BF16)
            st = st * eb[j * B:j * B + 1, :] + ds_ref[hh, j]
        st_ref[hh] = st
        o = jnp.concatenate(
            [_dot(ab[j * B:(j + 1) * B, :], vb[j * B:(j + 1) * B, :])
             + _dot_nt(qt[j * B:(j + 1) * B, :], sb_ref[hh, j]) for j in range(nb)], axis=0)

        on = o * lax.rsqrt(jnp.mean(o * o, axis=1, keepdims=True) + LN_EPS) * ng_ref[:, vcols]
        yb_ref[:, vcols] = (on * _silu(bz_ref[:, vcols])).astype(yb_ref.dtype)

    @pl.when(c == pl.num_programs(2) - 1)
    def _():
        for hh in range(GLA_HEADS_PER_STEP):
            s_out_ref[0, hh] = st_ref[hh].T


def _gla(p_all, batch, seq, w_gate_up, b_gate, b_norm_g, p_s, s_state, cast_weights):
    L = GLA_STEP
    hps = GLA_HEADS_PER_STEP
    nc = seq // L
    t = batch * seq
    kw, vw = hps * B_DK, hps * B_DV

    def rows(b, h, c):
        return b * nc + c

    in_specs = [
        pl.BlockSpec((L, kw), lambda b, h, c: (rows(b, h, c), COL_BQ // kw + h)),
        pl.BlockSpec((L, kw), lambda b, h, c: (rows(b, h, c), COL_BK // kw + h)),
        pl.BlockSpec((L, vw), lambda b, h, c: (rows(b, h, c), COL_BV // vw + h)),
        pl.BlockSpec((L, vw), lambda b, h, c: (rows(b, h, c), COL_BZ // vw + h)),
        pl.BlockSpec((L, LANES), lambda b, h, c: (rows(b, h, c), COL_GATE // LANES)),
        pl.BlockSpec((GATE_RANK, kw), lambda b, h, c: (0, h)),
        pl.BlockSpec((1, kw), lambda b, h, c: (0, h)),
        pl.BlockSpec((1, vw), lambda b, h, c: (0, h)),
    ]
    out_specs = [
        pl.BlockSpec((L, vw), lambda b, h, c: (rows(b, h, c), h)),
        pl.BlockSpec((1, hps, B_DK, B_DV), lambda b, h, c: (b, h, 0, 0)),
    ]
    out_shape = [
        jax.ShapeDtypeStruct((t, B_WIDTH), BF16),
        jax.ShapeDtypeStruct((batch, B_HEADS, B_DK, B_DV), F32),
    ]

    assert hps == B_HEADS
    nseq = p_s.shape[0]
    TB = DEC_TOKENS
    dstep = _decode_step(batch, nc, nseq)

    def dcol(base, width):
        return pl.BlockSpec((TB, width), lambda b, h, c: (dstep(b, h, c)[0], base // width + dstep(b, h, c)[1]))

    def dhead(shape):
        return pl.BlockSpec(shape, lambda b, h, c: (0, dstep(b, h, c)[1]))

    dstate = pl.BlockSpec((TB, 1, B_DK, B_DV), lambda b, h, c: dstep(b, h, c) + (0, 0))
    in_specs += [
        dcol(COL_BQ, B_DK), dcol(COL_BK, B_DK), dcol(COL_BV, B_DV), dcol(COL_BZ, B_DV),
        pl.BlockSpec((TB, LANES), lambda b, h, c: (dstep(b, h, c)[0], COL_GATE // LANES)),
        dhead((GATE_RANK, B_DK)), dhead((1, B_DK)), dhead((1, B_DV)), dstate,
    ]
    out_specs += [pl.BlockSpec((TB, B_DV), lambda b, h, c: dstep(b, h, c)), dstate]
    out_shape += [
        jax.ShapeDtypeStruct((nseq, B_WIDTH), BF16),
        jax.ShapeDtypeStruct((nseq, B_HEADS, B_DK, B_DV), F32),
    ]

    n_steps = batch * nc
    for wmat in cast_weights:
        slab = pl.BlockSpec((wmat.shape[0] // n_steps, wmat.shape[1]), lambda b, h, c: (rows(b, h, c), 0))
        in_specs.append(slab)
        out_specs.append(slab)
        out_shape.append(jax.ShapeDtypeStruct(wmat.shape, BF16))
    return pl.pallas_call(
        _gla_kernel,
        grid=(batch, B_HEADS // hps, nc),
        in_specs=in_specs,
        out_specs=out_specs,
        out_shape=out_shape,
        scratch_shapes=[pltpu.VMEM((hps, B_DV, B_DK), F32),
                        pltpu.VMEM((hps, L // GLA_BLOCK, B_DV, B_DK), F32),
                        pltpu.VMEM((hps, L // GLA_BLOCK, B_DV, B_DK), BF16)],
        compiler_params=pltpu.CompilerParams(
            dimension_semantics=("parallel", "parallel", "arbitrary"), vmem_limit_bytes=VMEM_LIMIT),
        name="gla",
    )(p_all, p_all, p_all, p_all, p_all, w_gate_up, b_gate, b_norm_g,
      p_s, p_s, p_s, p_s, p_s, w_gate_up, b_gate, b_norm_g, s_state, *cast_weights)


DEC_TOKENS = 16


def _decode_step(batch, nc, nseq):
    assert batch * nc == (nseq // DEC_TOKENS) * A_HEADS and A_HEADS == B_HEADS

    def step(b, h, c):
        s = b * nc + c
        return s // A_HEADS, s % A_HEADS
    return step


def _mlstm_decode_step(h_idx, qp_ref, kp_ref, av_ref, ao_ref, az_ref, g_ref,
                       sq0_ref, sq1_ref, sq2_ref, sk0_ref, sk1_ref, sk2_ref,
                       cwq_ref, cwk_ref, cbq_ref, cbk_ref, gb_ref, ang_ref, c_ref, n_ref, m_ref,
                       ya_ref, c_out_ref, n_out_ref, m_out_ref):
    TB = DEC_TOKENS

    def conv_silu(s0, s1, s2, x, w_ref, b_ref):
        acc = b_ref[...] + s0[...] * w_ref[0:1, :]
        acc = acc + s1[...] * w_ref[1:2, :]
        acc = acc + s2[...] * w_ref[2:3, :]
        acc = acc + x[...] * w_ref[3:4, :]
        return _silu(acc)

    q = conv_silu(sq0_ref, sq1_ref, sq2_ref, qp_ref, cwq_ref, cbq_ref)
    k = conv_silu(sk0_ref, sk1_ref, sk2_ref, kp_ref, cwk_ref, cbk_ref) * (A_DK ** -0.5)
    v = av_ref[...]
    g = g_ref[...]
    gbias = gb_ref[...]
    gate_lane = lax.broadcasted_iota(jnp.int32, (TB, LANES), 1)
    gsum = g + gbias

    def gate(lane0):
        return jnp.sum(jnp.where(gate_lane == lane0 + h_idx, gsum, 0.0), axis=1, keepdims=True)

    itil = gate(GATE_LANE_I)
    logf = _log_sigmoid(gate(GATE_LANE_F))
    lane_h = lax.broadcasted_iota(jnp.int32, (TB, A_HEADS), 1)
    m_prev = jnp.sum(jnp.where(lane_h == h_idx, m_ref[...], 0.0), axis=1, keepdims=True)
    inter = logf + m_prev
    m_t = jnp.maximum(inter, itil)
    w = jnp.exp(itil - m_t)
    decay = jnp.exp(inter - m_t)
    n_old = n_ref[:, 0, 0, :]
    s = jnp.sum(q * k, axis=1, keepdims=True) * w
    den = decay * jnp.sum(q * n_old, axis=1, keepdims=True) + s
    scale = 1.0 / jnp.maximum(jnp.abs(den), jnp.exp(-m_t))
    kw = k * w
    n_out_ref[:, 0, 0, :] = decay * n_old + kw
    m_out_ref[...] = jnp.broadcast_to(m_t, (TB, LANES))

    rows = lax.broadcasted_iota(jnp.int32, (TB, 1), 0)
    qb, kwb, vb = q.astype(BF16), kw.astype(BF16), v.astype(BF16)
    h_rows = []
    for t in range(TB):
        c_old = c_ref[t, 0]
        qc = _dot(qb, c_old.astype(BF16))[t:t + 1, :]
        h_rows.append((decay[t:t + 1, :] * qc + s[t:t + 1, :] * v[t:t + 1, :]) * scale[t:t + 1, :])
        outer = _dot_tn(jnp.where(rows == t, kwb, jnp.zeros_like(kwb)), vb)
        c_out_ref[t, 0] = decay[t:t + 1, :] * c_old + outer
    h = jnp.concatenate(h_rows, axis=0)
    mu = jnp.mean(h, axis=1, keepdims=True)
    hc = h - mu
    var = jnp.mean(hc * hc, axis=1, keepdims=True)
    hn = hc * lax.rsqrt(var + LN_EPS) * ang_ref[...]
    ya_ref[...] = (hn * _sigmoid(ao_ref[...]) * _silu(az_ref[...])).astype(ya_ref.dtype)


def _gla_decode_step(bq_ref, bk_ref, bv_ref, bz_ref, g_ref, wg_ref, bgate_ref, bng_ref, s_ref,
                     yb_ref, s_out_ref):
    TB = DEC_TOKENS
    gq = bq_ref[...] * (B_DK ** -0.5)
    gk = bk_ref[...]
    gv = bv_ref[...]
    loga = _gla_gate_log(g_ref[...], wg_ref[...], bgate_ref[...])
    eb = jnp.exp(loga)
    a = jnp.sum(gq * gk, axis=1, keepdims=True)
    ebt = jnp.concatenate([eb, jnp.zeros((LANES - TB, B_DK), F32)], axis=0).T
    rows = lax.broadcasted_iota(jnp.int32, (TB, 1), 0)
    qeb, gkb, gvb = (gq * eb).astype(BF16), gk.astype(BF16), gv.astype(BF16)
    o_rows = []
    for t in range(TB):
        s_old = s_ref[t, 0]
        o_rows.append(_dot(qeb, s_old.astype(BF16))[t:t + 1, :] + a[t:t + 1, :] * gv[t:t + 1, :])
        outer = _dot_tn(jnp.where(rows == t, gkb, jnp.zeros_like(gkb)), gvb)
        s_out_ref[t, 0] = ebt[:, t:t + 1] * s_old + outer
    o = jnp.concatenate(o_rows, axis=0)
    on = o * lax.rsqrt(jnp.mean(o * o, axis=1, keepdims=True) + LN_EPS) * bng_ref[...]
    yb_ref[...] = (on * _silu(bz_ref[...])).astype(yb_ref.dtype)


def _out_kernel(ya_ref, yb_ref, ga_ref, gb_ref, x_ref, wpa_ref, wpb_ref, wo_ref, lng_ref, lnb_ref, o_ref):
    pa = _dot(ya_ref[...], wpa_ref[...])
    pb = _dot(yb_ref[...], wpb_ref[...])
    merged = _sigmoid(ga_ref[...]) * pa + _sigmoid(gb_ref[...]) * pb
    y = _dot(merged.astype(BF16), wo_ref[...])
    r = ALPHA * x_ref[...] + y
    mu = jnp.mean(r, axis=1, keepdims=True)
    rc = r - mu
    var = jnp.mean(rc * rc, axis=1, keepdims=True)
    o_ref[...] = rc * lax.rsqrt(var + LN_EPS) * lng_ref[...] + lnb_ref[...]


def _out(ya, yb, p_all, x, w_pa, w_pb, w_out, ln_g, ln_b, tm):
    t = x.shape[0]
    const = lambda i: (0, 0)
    single = pl.Buffered(1)
    in_specs = [
        pl.BlockSpec((tm, A_WIDTH), lambda i: (i, 0)),
        pl.BlockSpec((tm, B_WIDTH), lambda i: (i, 0)),
        pl.BlockSpec((tm, D_MODEL), lambda i: (i, COL_GA // D_MODEL)),
        pl.BlockSpec((tm, D_MODEL), lambda i: (i, COL_GB // D_MODEL)),
        pl.BlockSpec((tm, D_MODEL), lambda i: (i, 0)),
        pl.BlockSpec((A_WIDTH, D_MODEL), const, pipeline_mode=single),
        pl.BlockSpec((B_WIDTH, D_MODEL), const, pipeline_mode=single),
        pl.BlockSpec((D_MODEL, D_MODEL), const, pipeline_mode=single),
        pl.BlockSpec((1, D_MODEL), const),
        pl.BlockSpec((1, D_MODEL), const),
    ]
    return pl.pallas_call(
        _out_kernel,
        grid=(t // tm,),
        in_specs=in_specs,
        out_specs=pl.BlockSpec((tm, D_MODEL), lambda i: (i, 0)),
        out_shape=jax.ShapeDtypeStruct((t, D_MODEL), F32),
        compiler_params=pltpu.CompilerParams(
            dimension_semantics=("parallel",), vmem_limit_bytes=VMEM_LIMIT),
        name="outproj",
    )(ya, yb, p_all, p_all, x, w_pa, w_pb, w_out, ln_g, ln_b)


RELAYOUT_TN = 512


def _relayout_kernel(src_ref, gif_ref, gb0_ref, gb1_ref, xs_ref, o_ref, ps_ref):
    j = pl.program_id(0)
    tn = RELAYOUT_TN
    n_main = COL_GATE // tn

    @pl.when(j < n_main)
    def _():
        o_ref[...] = src_ref[...].astype(BF16)

    @pl.when(j == n_main)
    def _():
        pad = jnp.zeros((tn - 2 * A_HEADS - GATE_RANK, src_ref.shape[1]), F32)
        o_ref[...] = jnp.concatenate([gif_ref[...], gb0_ref[...], gb1_ref[...], pad], axis=0).astype(BF16)

    ps_ref[...] = _dot_nt(xs_ref[...].astype(BF16), o_ref[...])


def _relayout_w_in(w_in_t, xs):
    n_src, k = w_in_t.shape
    ts = xs.shape[0]
    tn = RELAYOUT_TN

    def src_row(j):
        shift = 0
        for first, last, src in RELAYOUT_RUNS:
            assert first % tn == 0 and last % tn == 0 and (src - first) % SUBLANES == 0
            shift = jnp.where((j >= first // tn) & (j < last // tn), (src - first) // SUBLANES, shift)
        group = jnp.minimum(j * (tn // SUBLANES) + shift, (n_src - tn) // SUBLANES)
        return group * SUBLANES

    return pl.pallas_call(
        _relayout_kernel,
        grid=(pl.cdiv(N_PROJ, tn),),
        in_specs=[pl.BlockSpec((pl.Element(tn), pl.Element(k)), lambda j: (src_row(j), 0)),
                  pl.BlockSpec((SUBLANES, k), lambda j: (SRC_I // SUBLANES, 0)),
                  pl.BlockSpec((SUBLANES, k), lambda j: (SRC_BG // SUBLANES, 0)),
                  pl.BlockSpec((SUBLANES, k), lambda j: (SRC_BG // SUBLANES + 1, 0)),
                  pl.BlockSpec((ts, k), lambda j: (0, 0))],
        out_specs=[pl.BlockSpec((tn, k), lambda j: (j, 0)),
                   pl.BlockSpec((ts, tn), lambda j: (0, j))],
        out_shape=[jax.ShapeDtypeStruct((N_PROJ, k), BF16), jax.ShapeDtypeStruct((ts, N_PROJ), F32)],
        compiler_params=pltpu.CompilerParams(
            dimension_semantics=("parallel",), vmem_limit_bytes=VMEM_LIMIT),
        name="relayout",
    )(w_in_t, w_in_t, w_in_t, w_in_t, xs)


def kernel(x_prompt, x_sample, state_mlstm_C, state_mlstm_n, state_mlstm_m, state_conv, state_gla_S,
           w_in, conv_w, conv_b, b_i, b_f, a_norm_g, w_gate_up, b_gate, b_norm_g, w_pa, w_pb, w_out,
           ln_g, ln_b):
    batch, seq, _ = x_prompt.shape
    nseq = x_sample.shape[0]
    assert w_in.shape[0] == 1, "single-layer step"
    d = 0

    def layer(a):
        return a.reshape(a.shape[1:])

    xs = x_sample.reshape(nseq, D_MODEL)
    wp, ps_all = _relayout_w_in(layer(w_in).T, xs)
    cw = conv_w[d]
    cb = conv_b[d][None, :]
    gate_bias = jnp.concatenate([b_i[d], b_f[d], jnp.zeros((LANES - 2 * A_HEADS,), F32)])[None, :]
    ang = a_norm_g[d][None, :]
    bng = b_norm_g[d][None, :]
    wg = w_gate_up[d]
    bgate = b_gate[d][None, :]
    lng, lnb = ln_g[d][None, :], ln_b[d][None, :]

    xp = x_prompt.reshape(batch * seq, D_MODEL)
    p_all = _proj(xp, wp, tm=1024, tn=PROJ_TN)

    conv_state = state_conv.reshape(nseq, (CONV_W - 1) * 2 * A_WIDTH)
    ya, p_c, p_n, p_m, ya_s, s_c, s_n, s_m = _mlstm(
        p_all, batch, seq, cw, cb, gate_bias, ang, ps_all, conv_state,
        layer(state_mlstm_C), state_mlstm_n.reshape(nseq, A_HEADS, 1, A_DK), layer(state_mlstm_m))
    yb, p_s, yb_s, s_s, wpa, wpb, wo = _gla(p_all, batch, seq, wg, bgate, bng, ps_all, layer(state_gla_S),
                                            (layer(w_pa), layer(w_pb), layer(w_out)))

    y_prompt = _out(ya, yb, p_all, xp, wpa, wpb, wo, lng, lnb, tm=256).reshape(batch, seq, D_MODEL)
    y_sample = _out(ya_s, yb_s, ps_all, xs, wpa, wpb, wo, lng, lnb, tm=nseq).reshape(nseq, 1, D_MODEL)
    p_conv = p_all.reshape(batch, seq, N_PROJ)[:, seq - (CONV_W - 1):, :2 * A_WIDTH]
    s_conv = jnp.concatenate([layer(state_conv)[:, 1:, :], ps_all[:, None, :2 * A_WIDTH]], axis=1)

    def stacked(a):
        return a.reshape((1,) + a.shape)

    return (y_prompt, y_sample,
            stacked(p_c), p_n.reshape(1, batch, A_HEADS, A_DK), stacked(p_m[:, :, 0, 0]),
            stacked(p_conv), stacked(p_s),
            stacked(s_c), s_n.reshape(1, nseq, A_HEADS, A_DK),
            stacked(s_m.reshape(nseq, A_HEADS, LANES)[:, :, 0]), stacked(s_conv), stacked(s_s))
```

```python
import functools

import jax
import jax.numpy as jnp
from jax import lax
from jax.experimental import pallas as pl
from jax.experimental.pallas import tpu as pltpu

F32 = jnp.float32
BF16 = jnp.bfloat16

D_MODEL = 2048
A_HEADS = 4
A_WIDTH = 1024
A_DK = 256
A_DV = 256
CONV_W = 4
B_HEADS = 4
B_WIDTH = 1024
B_KWIDTH = 512
B_DK = 128
B_DV = 256
GATE_RANK = 16
GATE_TAU = 16.0
ALPHA = 2.0 ** 0.25
LN_EPS = 1e-5
LOG2_E = 1.4426950408889634
LN_2 = 0.6931471805599453

LANES = 128
SUBLANES = 8
VMEM_LIMIT = 48 * 1024 * 1024

COL_Q = 0
COL_K = A_WIDTH
COL_BQ = 2 * A_WIDTH
COL_BK = COL_BQ + B_KWIDTH
COL_AV = COL_BK + B_KWIDTH
COL_AO = COL_AV + A_WIDTH
COL_AZ = COL_AO + A_WIDTH
COL_BV = COL_AZ + A_WIDTH
COL_BZ = COL_BV + B_WIDTH
COL_GA = COL_BZ + B_WIDTH
COL_GB = COL_GA + D_MODEL
COL_GATE = COL_GB + D_MODEL
GATE_LANE_I = 0
GATE_LANE_F = A_HEADS
GATE_LANE_LR = 2 * A_HEADS
MXU_COLS = 256
N_PROJ = COL_GATE + MXU_COLS
PROJ_TN = N_PROJ // 7
assert PROJ_TN * 7 == N_PROJ and PROJ_TN % MXU_COLS == 0
PROJ_VMEM_LIMIT = 52 * 1024 * 1024

SRC_QK = 0
SRC_AV = 2 * A_WIDTH
SRC_I = SRC_AV + A_WIDTH
SRC_AO = SRC_I + 2 * A_HEADS
SRC_BQ = SRC_AO + 2 * A_WIDTH
SRC_BV = SRC_BQ + 2 * B_KWIDTH
SRC_BG = SRC_BV + B_WIDTH
SRC_BZ = SRC_BG + GATE_RANK
RELAYOUT_RUNS = ((COL_Q, COL_BQ, SRC_QK), (COL_BQ, COL_AV, SRC_BQ), (COL_AV, COL_AO, SRC_AV),
                 (COL_AO, COL_BV, SRC_AO), (COL_BV, COL_BZ, SRC_BV), (COL_BZ, COL_GATE, SRC_BZ))

MLSTM_CHUNK = 256
MLSTM_HEADS_PER_STEP = 4
GLA_STEP = 256
GLA_BLOCK = 16
GLA_HEADS_PER_STEP = 4


def _dot(a, b):
    return jnp.dot(a, b, preferred_element_type=F32)


def _dot_nt(a, b):
    return lax.dot_general(a, b, (((1,), (1,)), ((), ())), preferred_element_type=F32)


def _dot_tn(a, b):
    return lax.dot_general(a, b, (((0,), (0,)), ((), ())), preferred_element_type=F32)


def _mask_dot(mask_bf16, x):
    hi = x.astype(BF16)
    r1 = x - hi.astype(F32)
    mid = r1.astype(BF16)
    lo = (r1 - mid.astype(F32)).astype(BF16)
    return _dot(mask_bf16, hi) + _dot(mask_bf16, mid) + _dot(mask_bf16, lo)


def _log_sigmoid(z):
    return jnp.minimum(z, 0.0) - LN_2 * jnp.log2(1.0 + jnp.exp2(jnp.abs(z) * (-LOG2_E)))


def _sigmoid(z):
    return 1.0 / (1.0 + jnp.exp2(z * (-LOG2_E)))


def _silu(z):
    return z * _sigmoid(z)


def _proj_kernel(x_ref, w_ref, o_ref, xb_ref):
    @pl.when(pl.program_id(1) == 0)
    def _():
        xb_ref[...] = x_ref[...].astype(BF16)

    o_ref[...] = _dot_nt(xb_ref[...], w_ref[...])


def _proj(x, w_t, tm, tn):
    t, k = x.shape
    n = w_t.shape[0]
    return pl.pallas_call(
        _proj_kernel,
        grid=(t // tm, n // tn),
        in_specs=[pl.BlockSpec((tm, k), lambda i, j: (i, 0)),
                  pl.BlockSpec((tn, k), lambda i, j: (j, 0))],
        out_specs=pl.BlockSpec((tm, tn), lambda i, j: (i, j)),
        out_shape=jax.ShapeDtypeStruct((t, n), F32),
        scratch_shapes=[pltpu.VMEM((tm, k), BF16)],
        compiler_params=pltpu.CompilerParams(
            dimension_semantics=("parallel", "arbitrary"), vmem_limit_bytes=PROJ_VMEM_LIMIT),
        name="proj",
    )(x, w_t)


def _mlstm_kernel(qp_ref, kp_ref, v_ref, ao_ref, az_ref, g_ref, cwq_ref, cwk_ref, cbq_ref, cbk_ref,
                  gb_ref, ng_ref, *rest, n_chunks):
    dec_in, (ya_ref, c_ref, n_ref, m_ref), dec_out, (tailq_ref, tailk_ref) = (
        rest[:21], rest[21:25], rest[25:29], rest[29:])
    L = MLSTM_CHUNK
    c = pl.program_id(2)

    @pl.when(c == 0)
    def _():
        c_ref[...] = jnp.zeros_like(c_ref)
        n_ref[...] = jnp.zeros_like(n_ref)
        m_ref[...] = jnp.zeros_like(m_ref)
        tailq_ref[...] = jnp.zeros_like(tailq_ref)
        tailk_ref[...] = jnp.zeros_like(tailk_ref)

    _mlstm_decode_step((pl.program_id(0) * n_chunks + c) % A_HEADS, *dec_in, *dec_out)

    sub = lax.broadcasted_iota(jnp.int32, (1, SUBLANES, 1), 1)

    def conv_silu(x_ref, tail_ref, w_ref, b_ref):
        x = x_ref[...]
        width = x.shape[1]
        x3 = jnp.concatenate([tail_ref[...], x], axis=0).reshape(L // SUBLANES + 1, SUBLANES, width)
        acc = b_ref[...] + x * w_ref[CONV_W - 1:CONV_W, :]
        for j in range(1, CONV_W):
            rot = pltpu.roll(x3, j, axis=1)
            xs = jnp.where(sub < j, rot[:-1], rot[1:]).reshape(L, width)
            acc = acc + xs * w_ref[CONV_W - 1 - j:CONV_W - j, :]
        tail_ref[...] = x[L - SUBLANES:, :]
        return _silu(acc)

    q_all = conv_silu(qp_ref, tailq_ref, cwq_ref, cbq_ref)
    k_all = conv_silu(kp_ref, tailk_ref, cwk_ref, cbk_ref) * (A_DK ** -0.5)

    lane = lax.broadcasted_iota(jnp.int32, (L, LANES), 1)
    row = lax.broadcasted_iota(jnp.int32, (L, L), 0)
    col = lax.broadcasted_iota(jnp.int32, (L, L), 1)
    causal = row >= col
    causal_b = causal.astype(BF16)

    g = g_ref[...] + gb_ref[...]
    is_f = (lane >= GATE_LANE_F) & (lane < GATE_LANE_F + A_HEADS)
    g2 = jnp.where(is_f, _log_sigmoid(g), g)
    x2 = jnp.where(is_f, _mask_dot(causal_b, g2), g2)
    x2t = x2.T

    for hh in range(MLSTM_HEADS_PER_STEP):
        cols = slice(hh * A_DK, (hh + 1) * A_DK)
        q, k, v = q_all[:, cols], k_all[:, cols], v_ref[:, cols]
        li, lf = GATE_LANE_I + hh, GATE_LANE_F + hh
        itil_col, b_col = x2[:, li:li + 1], x2[:, lf:lf + 1]
        itil_row, b_row = x2t[li:li + 1, :], x2t[lf:lf + 1, :]

        m_prev = m_ref[0, hh, 0:1, 0:1]
        dmat = jnp.where(causal, b_col - b_row + itil_row, -jnp.inf)
        inter = b_col + m_prev
        m_t = jnp.maximum(inter, jnp.max(dmat, axis=1, keepdims=True))
        w = jnp.exp(dmat - m_t)
        decay = jnp.exp(inter - m_t)

        qb, kb, vb = q.astype(BF16), k.astype(BF16), v.astype(BF16)
        c_old = c_ref[0, hh]
        n_old = n_ref[0, hh]
        s = _dot_nt(qb, kb) * w
        num = decay * _dot(qb, c_old.astype(BF16)) + _dot(s.astype(BF16), vb)
        den = decay * jnp.sum(q * n_old, axis=1, keepdims=True) + jnp.sum(s, axis=1, keepdims=True)
        inv = 1.0 / jnp.maximum(jnp.abs(den), jnp.exp(-m_t))

        m_new = m_t[L - 1:L, :]
        b_last = b_col[L - 1:L, :]
        wk = jnp.exp(b_last - b_col + itil_col - m_new)
        dec = jnp.exp(b_last + m_prev - m_new)
        kw = k * wk
        c_ref[0, hh] = dec * c_old + _dot(kw.T.astype(BF16), vb)
        n_ref[0, hh] = dec * n_old + jnp.sum(kw, axis=0, keepdims=True)
        m_ref[0, hh] = jnp.broadcast_to(m_new, (1, LANES))

        mu = jnp.mean(num, axis=1, keepdims=True)
        hc = num - mu
        var = jnp.mean(hc * hc, axis=1, keepdims=True)
        hn = hc * (inv * lax.rsqrt(var * (inv * inv) + LN_EPS)) * ng_ref[:, cols]
        ya_ref[:, cols] = (hn * _sigmoid(ao_ref[:, cols]) * _silu(az_ref[:, cols])).astype(ya_ref.dtype)


def _mlstm(p_all, batch, seq, conv_w, conv_b, gate_bias, a_norm_g, p_s, conv_state, c_state, n_state, m_state):
    L = MLSTM_CHUNK
    hps = MLSTM_HEADS_PER_STEP
    nc = seq // L
    t = batch * seq
    wd = hps * A_DK
    k_off = A_WIDTH // wd

    def rows(b, h, c):
        return b * nc + c

    def pcol(base):
        return pl.BlockSpec((L, wd), lambda b, h, c: (rows(b, h, c), base // wd + h))

    in_specs = [
        pcol(COL_Q), pcol(COL_K), pcol(COL_AV), pcol(COL_AO), pcol(COL_AZ),
        pl.BlockSpec((L, LANES), lambda b, h, c: (rows(b, h, c), COL_GATE // LANES)),
        pl.BlockSpec((CONV_W, wd), lambda b, h, c: (0, h)),
        pl.BlockSpec((CONV_W, wd), lambda b, h, c: (0, k_off + h)),
        pl.BlockSpec((1, wd), lambda b, h, c: (0, h)),
        pl.BlockSpec((1, wd), lambda b, h, c: (0, k_off + h)),
        pl.BlockSpec((1, LANES), lambda b, h, c: (0, 0)),
        pl.BlockSpec((1, wd), lambda b, h, c: (0, h)),
    ]
    out_specs = [
        pl.BlockSpec((L, wd), lambda b, h, c: (rows(b, h, c), h)),
        pl.BlockSpec((1, hps, A_DK, A_DV), lambda b, h, c: (b, h, 0, 0)),
        pl.BlockSpec((1, hps, 1, A_DK), lambda b, h, c: (b, h, 0, 0)),
        pl.BlockSpec((1, hps, 1, LANES), lambda b, h, c: (b, h, 0, 0)),
    ]
    out_shape = [
        jax.ShapeDtypeStruct((t, A_WIDTH), BF16),
        jax.ShapeDtypeStruct((batch, A_HEADS, A_DK, A_DV), F32),
        jax.ShapeDtypeStruct((batch, A_HEADS, 1, A_DK), F32),
        jax.ShapeDtypeStruct((batch, A_HEADS, 1, LANES), F32),
    ]

    assert hps == A_HEADS
    nseq = p_s.shape[0]
    TB = DEC_TOKENS
    dstep = _decode_step(batch, nc, nseq)

    def dcol(base, width):
        return pl.BlockSpec((TB, width), lambda b, h, c: (dstep(b, h, c)[0], base // width + dstep(b, h, c)[1]))

    def dconv(j, base):
        return pl.BlockSpec((TB, A_DK), lambda b, h, c: (dstep(b, h, c)[0],
                                                        (j * 2 * A_WIDTH + base) // A_DK + dstep(b, h, c)[1]))

    def dhead(shape, off=0):
        return pl.BlockSpec(shape, lambda b, h, c: (0, off + dstep(b, h, c)[1]))

    def dstate(*dims):
        return pl.BlockSpec((TB, 1) + dims, lambda b, h, c: dstep(b, h, c) + (0,) * len(dims))

    in_specs += [
        dcol(COL_Q, A_DK), dcol(COL_K, A_DK), dcol(COL_AV, A_DV), dcol(COL_AO, A_DV), dcol(COL_AZ, A_DV),
        pl.BlockSpec((TB, LANES), lambda b, h, c: (dstep(b, h, c)[0], COL_GATE // LANES)),
        dconv(0, 0), dconv(1, 0), dconv(2, 0), dconv(0, A_WIDTH), dconv(1, A_WIDTH), dconv(2, A_WIDTH),
        dhead((CONV_W, A_DK)), dhead((CONV_W, A_DK), A_HEADS), dhead((1, A_DK)), dhead((1, A_DK), A_HEADS),
        pl.BlockSpec((1, LANES), lambda b, h, c: (0, 0)), dhead((1, A_DV)),
        dstate(A_DK, A_DV), dstate(1, A_DK),
        pl.BlockSpec((TB, A_HEADS), lambda b, h, c: (dstep(b, h, c)[0], 0)),
    ]
    out_specs += [
        pl.BlockSpec((TB, A_DV), lambda b, h, c: dstep(b, h, c)),
        dstate(A_DK, A_DV), dstate(1, A_DK),
        pl.BlockSpec((TB, LANES), lambda b, h, c: dstep(b, h, c)),
    ]
    out_shape += [
        jax.ShapeDtypeStruct((nseq, A_WIDTH), BF16),
        jax.ShapeDtypeStruct((nseq, A_HEADS, A_DK, A_DV), F32),
        jax.ShapeDtypeStruct((nseq, A_HEADS, 1, A_DK), F32),
        jax.ShapeDtypeStruct((nseq, A_HEADS * LANES), F32),
    ]
    return pl.pallas_call(
        functools.partial(_mlstm_kernel, n_chunks=nc),
        grid=(batch, A_HEADS // hps, nc),
        in_specs=in_specs,
        out_specs=out_specs,
        out_shape=out_shape,
        scratch_shapes=[pltpu.VMEM((SUBLANES, wd), F32),
                        pltpu.VMEM((SUBLANES, wd), F32)],
        compiler_params=pltpu.CompilerParams(
            dimension_semantics=("parallel", "parallel", "arbitrary"), vmem_limit_bytes=VMEM_LIMIT),
        name="mlstm",
    )(p_all, p_all, p_all, p_all, p_all, p_all, conv_w, conv_w, conv_b, conv_b, gate_bias, a_norm_g,
      p_s, p_s, p_s, p_s, p_s, p_s,
      conv_state, conv_state, conv_state, conv_state, conv_state, conv_state,
      conv_w, conv_w, conv_b, conv_b, gate_bias, a_norm_g, c_state, n_state, m_state)


def _gla_gate_log(bg_tile, wg, bgate):
    lr = bg_tile[:, GATE_LANE_LR:GATE_LANE_LR + GATE_RANK]
    z = _dot(lr.astype(BF16), wg.astype(BF16)) + bgate
    return _log_sigmoid(z) / GATE_TAU


def _gla_kernel(q_ref, k_ref, v_ref, bz_ref, g_ref, wg_ref, bgate_ref, ng_ref, *rest):
    dec_in, w_in, (yb_ref, s_out_ref), dec_out, w_out, (st_ref, ds_ref, sb_ref) = (
        rest[:9], rest[9:12], rest[12:14], rest[14:16], rest[16:19], rest[19:])
    L = GLA_STEP
    B = GLA_BLOCK
    nb = L // B
    c = pl.program_id(2)

    @pl.when(c == 0)
    def _():
        st_ref[...] = jnp.zeros_like(st_ref)

    _gla_decode_step(*dec_in, *dec_out)
    for src_ref, dst_ref in zip(w_in, w_out):
        dst_ref[...] = src_ref[...].astype(BF16)

    row = lax.broadcasted_iota(jnp.int32, (L, L), 0)
    col = lax.broadcasted_iota(jnp.int32, (L, L), 1)
    same = (row // B) == (col // B)
    tri_b = (same & (row >= col)).astype(BF16)
    same_b = same.astype(BF16)
    t_in = lax.broadcasted_iota(jnp.int32, (L, 1), 0) % B
    s_lane = lax.broadcasted_iota(jnp.int32, (1, LANES), 1)

    for hh in range(GLA_HEADS_PER_STEP):
        kcols = slice(hh * B_DK, (hh + 1) * B_DK)
        vcols = slice(hh * B_DV, (hh + 1) * B_DV)
        loga = _gla_gate_log(g_ref[...], wg_ref[:, kcols], bgate_ref[:, kcols])
        bc = _mask_dot(tri_b, loga)
        bl = _mask_dot(same_b, loga)
        k = k_ref[:, kcols]
        qs = q_ref[:, kcols] * (B_DK ** -0.5)
        qt = (qs * jnp.exp(bc)).astype(BF16)
        kt = (k * jnp.exp(bl - bc)).astype(BF16)
        eb = jnp.exp(bl)
        vb = v_ref[:, vcols].astype(BF16)

        bc3, q3, k3 = (x.reshape(nb, B, B_DK) for x in (bc * LOG2_E, qs, k))
        groups = B // SUBLANES
        bcg = [bc3[:, g * SUBLANES:(g + 1) * SUBLANES, :] for g in range(groups)]
        qg = [q3[:, g * SUBLANES:(g + 1) * SUBLANES, :] for g in range(groups)]
        ag = [jnp.zeros((nb, SUBLANES, LANES), F32) for _ in range(groups)]
        for s in range(B):
            bc_s, k_s = bc3[:, s:s + 1, :], k3[:, s:s + 1, :]
            for g in range(s // SUBLANES, groups):
                e = jnp.exp2(bcg[g] - bc_s)
                a_col = jnp.sum(qg[g] * k_s * e, axis=2, keepdims=True)
                ag[g] = jnp.where(s_lane == s, a_col, ag[g])
        a = jnp.concatenate(ag, axis=1).reshape(L, LANES)
        ab = jnp.where(t_in >= s_lane, a, 0.0)[:, 0:B].astype(BF16)

        for j in range(nb):
            ds_ref[hh, j] = _dot_tn(vb[j * B:(j + 1) * B, :], kt[j * B:(j + 1) * B, :])
        st = st_ref[hh]
        for j in range(nb):
            sb_ref[hh, j] = st.astype(BF16)
            st = st * eb[j * B:j * B + 1, :] + ds_ref[hh, j]
        st_ref[hh] = st
        o = jnp.concatenate(
            [_dot(ab[j * B:(j + 1) * B, :], vb[j * B:(j + 1) * B, :])
             + _dot_nt(qt[j * B:(j + 1) * B, :], sb_ref[hh, j]) for j in range(nb)], axis=0)

        on = o * lax.rsqrt(jnp.mean(o * o, axis=1, keepdims=True) + LN_EPS) * ng_ref[:, vcols]
        yb_ref[:, vcols] = (on * _silu(bz_ref[:, vcols])).astype(yb_ref.dtype)

    @pl.when(c == pl.num_programs(2) - 1)
    def _():
        for hh in range(GLA_HEADS_PER_STEP):
            s_out_ref[0, hh] = st_ref[hh].T


def _gla(p_all, batch, seq, w_gate_up, b_gate, b_norm_g, p_s, s_state, cast_weights):
    L = GLA_STEP
    hps = GLA_HEADS_PER_STEP
    nc = seq // L
    t = batch * seq
    kw, vw = hps * B_DK, hps * B_DV

    def rows(b, h, c):
        return b * nc + c

    in_specs = [
        pl.BlockSpec((L, kw), lambda b, h, c: (rows(b, h, c), COL_BQ // kw + h)),
        pl.BlockSpec((L, kw), lambda b, h, c: (rows(b, h, c), COL_BK // kw + h)),
        pl.BlockSpec((L, vw), lambda b, h, c: (rows(b, h, c), COL_BV // vw + h)),
        pl.BlockSpec((L, vw), lambda b, h, c: (rows(b, h, c), COL_BZ // vw + h)),
        pl.BlockSpec((L, LANES), lambda b, h, c: (rows(b, h, c), COL_GATE // LANES)),
        pl.BlockSpec((GATE_RANK, kw), lambda b, h, c: (0, h)),
        pl.BlockSpec((1, kw), lambda b, h, c: (0, h)),
        pl.BlockSpec((1, vw), lambda b, h, c: (0, h)),
    ]
    out_specs = [
        pl.BlockSpec((L, vw), lambda b, h, c: (rows(b, h, c), h)),
        pl.BlockSpec((1, hps, B_DK, B_DV), lambda b, h, c: (b, h, 0, 0)),
    ]
    out_shape = [
        jax.ShapeDtypeStruct((t, B_WIDTH), BF16),
        jax.ShapeDtypeStruct((batch, B_HEADS, B_DK, B_DV), F32),
    ]

    assert hps == B_HEADS
    nseq = p_s.shape[0]
    TB = DEC_TOKENS
    dstep = _decode_step(batch, nc, nseq)

    def dcol(base, width):
        return pl.BlockSpec((TB, width), lambda b, h, c: (dstep(b, h, c)[0], base // width + dstep(b, h, c)[1]))

    def dhead(shape):
        return pl.BlockSpec(shape, lambda b, h, c: (0, dstep(b, h, c)[1]))

    dstate = pl.BlockSpec((TB, 1, B_DK, B_DV), lambda b, h, c: dstep(b, h, c) + (0, 0))
    in_specs += [
        dcol(COL_BQ, B_DK), dcol(COL_BK, B_DK), dcol(COL_BV, B_DV), dcol(COL_BZ, B_DV),
        pl.BlockSpec((TB, LANES), lambda b, h, c: (dstep(b, h, c)[0], COL_GATE // LANES)),
        dhead((GATE_RANK, B_DK)), dhead((1, B_DK)), dhead((1, B_DV)), dstate,
    ]
    out_specs += [pl.BlockSpec((TB, B_DV), lambda b, h, c: dstep(b, h, c)), dstate]
    out_shape += [
        jax.ShapeDtypeStruct((nseq, B_WIDTH), BF16),
        jax.ShapeDtypeStruct((nseq, B_HEADS, B_DK, B_DV), F32),
    ]

    n_steps = batch * nc
    for wmat in cast_weights:
        slab = pl.BlockSpec((wmat.shape[0] // n_steps, wmat.shape[1]), lambda b, h, c: (rows(b, h, c), 0))
        in_specs.append(slab)
        out_specs.append(slab)
        out_shape.append(jax.ShapeDtypeStruct(wmat.shape, BF16))
    return pl.pallas_call(
        _gla_kernel,
        grid=(batch, B_HEADS // hps, nc),
        in_specs=in_specs,
        out_specs=out_specs,
        out_shape=out_shape,
        scratch_shapes=[pltpu.VMEM((hps, B_DV, B_DK), F32),
                        pltpu.VMEM((hps, L // GLA_BLOCK, B_DV, B_DK), F32),
                        pltpu.VMEM((hps, L // GLA_BLOCK, B_DV, B_DK), BF16)],
        compiler_params=pltpu.CompilerParams(
            dimension_semantics=("parallel", "parallel", "arbitrary"), vmem_limit_bytes=VMEM_LIMIT),
        name="gla",
    )(p_all, p_all, p_all, p_all, p_all, w_gate_up, b_gate, b_norm_g,
      p_s, p_s, p_s, p_s, p_s, w_gate_up, b_gate, b_norm_g, s_state, *cast_weights)


DEC_TOKENS = 16


def _decode_step(batch, nc, nseq):
    assert batch * nc == (nseq // DEC_TOKENS) * A_HEADS and A_HEADS == B_HEADS

    def step(b, h, c):
        s = b * nc + c
        return s // A_HEADS, s % A_HEADS
    return step


def _mlstm_decode_step(h_idx, qp_ref, kp_ref, av_ref, ao_ref, az_ref, g_ref,
                       sq0_ref, sq1_ref, sq2_ref, sk0_ref, sk1_ref, sk2_ref,
                       cwq_ref, cwk_ref, cbq_ref, cbk_ref, gb_ref, ang_ref, c_ref, n_ref, m_ref,
                       ya_ref, c_out_ref, n_out_ref, m_out_ref):
    TB = DEC_TOKENS

    def conv_silu(s0, s1, s2, x, w_ref, b_ref):
        acc = b_ref[...] + s0[...] * w_ref[0:1, :]
        acc = acc + s1[...] * w_ref[1:2, :]
        acc = acc + s2[...] * w_ref[2:3, :]
        acc = acc + x[...] * w_ref[3:4, :]
        return _silu(acc)

    q = conv_silu(sq0_ref, sq1_ref, sq2_ref, qp_ref, cwq_ref, cbq_ref)
    k = conv_silu(sk0_ref, sk1_ref, sk2_ref, kp_ref, cwk_ref, cbk_ref) * (A_DK ** -0.5)
    v = av_ref[...]
    g = g_ref[...]
    gbias = gb_ref[...]
    gate_lane = lax.broadcasted_iota(jnp.int32, (TB, LANES), 1)
    gsum = g + gbias

    def gate(lane0):
        return jnp.sum(jnp.where(gate_lane == lane0 + h_idx, gsum, 0.0), axis=1, keepdims=True)

    itil = gate(GATE_LANE_I)
    logf = _log_sigmoid(gate(GATE_LANE_F))
    lane_h = lax.broadcasted_iota(jnp.int32, (TB, A_HEADS), 1)
    m_prev = jnp.sum(jnp.where(lane_h == h_idx, m_ref[...], 0.0), axis=1, keepdims=True)
    inter = logf + m_prev
    m_t = jnp.maximum(inter, itil)
    w = jnp.exp(itil - m_t)
    decay = jnp.exp(inter - m_t)
    n_old = n_ref[:, 0, 0, :]
    s = jnp.sum(q * k, axis=1, keepdims=True) * w
    den = decay * jnp.sum(q * n_old, axis=1, keepdims=True) + s
    scale = 1.0 / jnp.maximum(jnp.abs(den), jnp.exp(-m_t))
    kw = k * w
    n_out_ref[:, 0, 0, :] = decay * n_old + kw
    m_out_ref[...] = jnp.broadcast_to(m_t, (TB, LANES))

    rows = lax.broadcasted_iota(jnp.int32, (TB, 1), 0)
    qb, kwb, vb = q.astype(BF16), kw.astype(BF16), v.astype(BF16)
    h_rows = []
    for t in range(TB):
        c_old = c_ref[t, 0]
        qc = _dot(qb, c_old.astype(BF16))[t:t + 1, :]
        h_rows.append((decay[t:t + 1, :] * qc + s[t:t + 1, :] * v[t:t + 1, :]) * scale[t:t + 1, :])
        outer = _dot_tn(jnp.where(rows == t, kwb, jnp.zeros_like(kwb)), vb)
        c_out_ref[t, 0] = decay[t:t + 1, :] * c_old + outer
    h = jnp.concatenate(h_rows, axis=0)
    mu = jnp.mean(h, axis=1, keepdims=True)
    hc = h - mu
    var = jnp.mean(hc * hc, axis=1, keepdims=True)
    hn = hc * lax.rsqrt(var + LN_EPS) * ang_ref[...]
    ya_ref[...] = (hn * _sigmoid(ao_ref[...]) * _silu(az_ref[...])).astype(ya_ref.dtype)


def _gla_decode_step(bq_ref, bk_ref, bv_ref, bz_ref, g_ref, wg_ref, bgate_ref, bng_ref, s_ref,
                     yb_ref, s_out_ref):
    TB = DEC_TOKENS
    gq = bq_ref[...] * (B_DK ** -0.5)
    gk = bk_ref[...]
    gv = bv_ref[...]
    loga = _gla_gate_log(g_ref[...], wg_ref[...], bgate_ref[...])
    eb = jnp.exp(loga)
    a = jnp.sum(gq * gk, axis=1, keepdims=True)
    ebt = jnp.concatenate([eb, jnp.zeros((LANES - TB, B_DK), F32)], axis=0).T
    rows = lax.broadcasted_iota(jnp.int32, (TB, 1), 0)
    qeb, gkb, gvb = (gq * eb).astype(BF16), gk.astype(BF16), gv.astype(BF16)
    o_rows = []
    for t in range(TB):
        s_old = s_ref[t, 0]
        o_rows.append(_dot(qeb, s_old.astype(BF16))[t:t + 1, :] + a[t:t + 1, :] * gv[t:t + 1, :])
        outer = _dot_tn(jnp.where(rows == t, gkb, jnp.zeros_like(gkb)), gvb)
        s_out_ref[t, 0] = ebt[:, t:t + 1] * s_old + outer
    o = jnp.concatenate(o_rows, axis=0)
    on = o * lax.rsqrt(jnp.mean(o * o, axis=1, keepdims=True) + LN_EPS) * bng_ref[...]
    yb_ref[...] = (on * _silu(bz_ref[...])).astype(yb_ref.dtype)


def _out_kernel(ya_ref, yb_ref, ga_ref, gb_ref, x_ref, yas_ref, ybs_ref, gas_ref, gbs_ref, xs_ref,
                wpa_ref, wpb_ref, wo_ref, lng_ref, lnb_ref, o_ref, os_ref):
    def rows(ya, yb, ga, gb, x):
        pa = _dot(ya[...], wpa_ref[...])
        pb = _dot(yb[...], wpb_ref[...])
        merged = _sigmoid(ga[...]) * pa + _sigmoid(gb[...]) * pb
        r = ALPHA * x[...] + _dot(merged.astype(BF16), wo_ref[...])
        mu = jnp.mean(r, axis=1, keepdims=True)
        rc = r - mu
        var = jnp.mean(rc * rc, axis=1, keepdims=True)
        return rc * lax.rsqrt(var + LN_EPS) * lng_ref[...] + lnb_ref[...]

    last = pl.num_programs(0) - 1

    @pl.when(pl.program_id(0) < last)
    def _():
        o_ref[...] = rows(ya_ref, yb_ref, ga_ref, gb_ref, x_ref)

    @pl.when(pl.program_id(0) == last)
    def _():
        os_ref[...] = rows(yas_ref, ybs_ref, gas_ref, gbs_ref, xs_ref)


def _out(ya, yb, p_all, x, ya_s, yb_s, p_s, xs, w_pa, w_pb, w_out, ln_g, ln_b, tm):
    t, ts = x.shape[0], xs.shape[0]
    n_tiles = t // tm
    const = lambda i: (0, 0)
    single = pl.Buffered(1)

    def tile(i):
        return jnp.minimum(i, n_tiles - 1)

    in_specs = [
        pl.BlockSpec((tm, A_WIDTH), lambda i: (tile(i), 0)),
        pl.BlockSpec((tm, B_WIDTH), lambda i: (tile(i), 0)),
        pl.BlockSpec((tm, D_MODEL), lambda i: (tile(i), COL_GA // D_MODEL)),
        pl.BlockSpec((tm, D_MODEL), lambda i: (tile(i), COL_GB // D_MODEL)),
        pl.BlockSpec((tm, D_MODEL), lambda i: (tile(i), 0)),
        pl.BlockSpec((ts, A_WIDTH), const),
        pl.BlockSpec((ts, B_WIDTH), const),
        pl.BlockSpec((ts, D_MODEL), lambda i: (0, COL_GA // D_MODEL)),
        pl.BlockSpec((ts, D_MODEL), lambda i: (0, COL_GB // D_MODEL)),
        pl.BlockSpec((ts, D_MODEL), const),
        pl.BlockSpec((A_WIDTH, D_MODEL), const, pipeline_mode=single),
        pl.BlockSpec((B_WIDTH, D_MODEL), const, pipeline_mode=single),
        pl.BlockSpec((D_MODEL, D_MODEL), const, pipeline_mode=single),
        pl.BlockSpec((1, D_MODEL), const),
        pl.BlockSpec((1, D_MODEL), const),
    ]
    return pl.pallas_call(
        _out_kernel,
        grid=(n_tiles + 1,),
        in_specs=in_specs,
        out_specs=[pl.BlockSpec((tm, D_MODEL), lambda i: (tile(i), 0)),
                   pl.BlockSpec((ts, D_MODEL), const)],
        out_shape=[jax.ShapeDtypeStruct((t, D_MODEL), F32), jax.ShapeDtypeStruct((ts, D_MODEL), F32)],
        compiler_params=pltpu.CompilerParams(
            dimension_semantics=("arbitrary",), vmem_limit_bytes=VMEM_LIMIT),
        name="outproj",
    )(ya, yb, p_all, p_all, x, ya_s, yb_s, p_s, p_s, xs, w_pa, w_pb, w_out, ln_g, ln_b)


RELAYOUT_TN = 1024


def _relayout_kernel(src_ref, gif_ref, gb0_ref, gb1_ref, xs_ref, o_ref, ps_ref):
    j = pl.program_id(0)
    tn = RELAYOUT_TN
    n_main = COL_GATE // tn

    @pl.when(j < n_main)
    def _():
        o_ref[...] = src_ref[...].astype(BF16)

    @pl.when(j == n_main)
    def _():
        pad = jnp.zeros((tn - 2 * A_HEADS - GATE_RANK, src_ref.shape[1]), F32)
        o_ref[...] = jnp.concatenate([gif_ref[...], gb0_ref[...], gb1_ref[...], pad], axis=0).astype(BF16)

    ps_ref[...] = _dot_nt(xs_ref[...].astype(BF16), o_ref[...])


def _relayout_w_in(w_in_t, xs):
    n_src, k = w_in_t.shape
    ts = xs.shape[0]
    tn = RELAYOUT_TN

    def src_row(j):
        shift = 0
        for first, last, src in RELAYOUT_RUNS:
            assert first % tn == 0 and last % tn == 0 and (src - first) % SUBLANES == 0
            shift = jnp.where((j >= first // tn) & (j < last // tn), (src - first) // SUBLANES, shift)
        group = jnp.minimum(j * (tn // SUBLANES) + shift, (n_src - tn) // SUBLANES)
        return group * SUBLANES

    return pl.pallas_call(
        _relayout_kernel,
        grid=(pl.cdiv(N_PROJ, tn),),
        in_specs=[pl.BlockSpec((pl.Element(tn), pl.Element(k)), lambda j: (src_row(j), 0)),
                  pl.BlockSpec((SUBLANES, k), lambda j: (SRC_I // SUBLANES, 0)),
                  pl.BlockSpec((SUBLANES, k), lambda j: (SRC_BG // SUBLANES, 0)),
                  pl.BlockSpec((SUBLANES, k), lambda j: (SRC_BG // SUBLANES + 1, 0)),
                  pl.BlockSpec((ts, k), lambda j: (0, 0))],
        out_specs=[pl.BlockSpec((tn, k), lambda j: (j, 0)),
                   pl.BlockSpec((ts, tn), lambda j: (0, j))],
        out_shape=[jax.ShapeDtypeStruct((N_PROJ, k), BF16), jax.ShapeDtypeStruct((ts, N_PROJ), F32)],
        compiler_params=pltpu.CompilerParams(
            dimension_semantics=("parallel",), vmem_limit_bytes=VMEM_LIMIT),
        name="relayout",
    )(w_in_t, w_in_t, w_in_t, w_in_t, xs)


def kernel(x_prompt, x_sample, state_mlstm_C, state_mlstm_n, state_mlstm_m, state_conv, state_gla_S,
           w_in, conv_w, conv_b, b_i, b_f, a_norm_g, w_gate_up, b_gate, b_norm_g, w_pa, w_pb, w_out,
           ln_g, ln_b):
    batch, seq, _ = x_prompt.shape
    nseq = x_sample.shape[0]
    assert w_in.shape[0] == 1, "single-layer step"
    d = 0

    def layer(a):
        return a.reshape(a.shape[1:])

    xs = x_sample.reshape(nseq, D_MODEL)
    wp, ps_all = _relayout_w_in(layer(w_in).T, xs)
    cw = conv_w[d]
    cb = conv_b[d][None, :]
    gate_bias = jnp.concatenate([b_i[d], b_f[d], jnp.zeros((LANES - 2 * A_HEADS,), F32)])[None, :]
    ang = a_norm_g[d][None, :]
    bng = b_norm_g[d][None, :]
    wg = w_gate_up[d]
    bgate = b_gate[d][None, :]
    lng, lnb = ln_g[d][None, :], ln_b[d][None, :]

    xp = x_prompt.reshape(batch * seq, D_MODEL)
    p_all = _proj(xp, wp, tm=1024, tn=PROJ_TN)

    conv_state = state_conv.reshape(nseq, (CONV_W - 1) * 2 * A_WIDTH)
    ya, p_c, p_n, p_m, ya_s, s_c, s_n, s_m = _mlstm(
        p_all, batch, seq, cw, cb, gate_bias, ang, ps_all, conv_state,
        layer(state_mlstm_C), state_mlstm_n.reshape(nseq, A_HEADS, 1, A_DK), layer(state_mlstm_m))
    yb, p_s, yb_s, s_s, wpa, wpb, wo = _gla(p_all, batch, seq, wg, bgate, bng, ps_all, layer(state_gla_S),
                                            (layer(w_pa), layer(w_pb), layer(w_out)))

    y_prompt, y_sample = _out(ya, yb, p_all, xp, ya_s, yb_s, ps_all, xs, wpa, wpb, wo, lng, lnb, tm=256)
    y_prompt = y_prompt.reshape(batch, seq, D_MODEL)
    y_sample = y_sample.reshape(nseq, 1, D_MODEL)
    p_conv = p_all.reshape(batch, seq, N_PROJ)[:, seq - (CONV_W - 1):, :2 * A_WIDTH]
    s_conv = jnp.concatenate([layer(state_conv)[:, 1:, :], ps_all[:, None, :2 * A_WIDTH]], axis=1)

    def stacked(a):
        return a.reshape((1,) + a.shape)

    return (y_prompt, y_sample,
            stacked(p_c), p_n.reshape(1, batch, A_HEADS, A_DK), stacked(p_m[:, :, 0, 0]),
            stacked(p_conv), stacked(p_s),
            stacked(s_c), s_n.reshape(1, nseq, A_HEADS, A_DK),
            stacked(s_m.reshape(nseq, A_HEADS, LANES)[:, :, 0]), stacked(s_conv), stacked(s_s))
```

```python
import functools

import jax
import jax.numpy as jnp
from jax import lax
from jax.experimental import pallas as pl
from jax.experimental.pallas import tpu as pltpu

F32 = jnp.float32
BF16 = jnp.bfloat16

D_MODEL = 2048
A_HEADS = 4
A_WIDTH = 1024
A_DK = 256
A_DV = 256
CONV_W = 4
B_HEADS = 4
B_WIDTH = 1024
B_KWIDTH = 512
B_DK = 128
B_DV = 256
GATE_RANK = 16
GATE_TAU = 16.0
ALPHA = 2.0 ** 0.25
LN_EPS = 1e-5
LOG2_E = 1.4426950408889634
LN_2 = 0.6931471805599453

LANES = 128
SUBLANES = 8
VMEM_LIMIT = 48 * 1024 * 1024

COL_Q = 0
COL_K = A_WIDTH
COL_BQ = 2 * A_WIDTH
COL_BK = COL_BQ + B_KWIDTH
COL_AV = COL_BK + B_KWIDTH
COL_AO = COL_AV + A_WIDTH
COL_AZ = COL_AO + A_WIDTH
COL_BV = COL_AZ + A_WIDTH
COL_BZ = COL_BV + B_WIDTH
COL_GA = COL_BZ + B_WIDTH
COL_GB = COL_GA + D_MODEL
COL_GATE = COL_GB + D_MODEL
GATE_LANE_I = 0
GATE_LANE_F = A_HEADS
GATE_LANE_LR = 2 * A_HEADS
MXU_COLS = 256
N_PROJ = COL_GATE + MXU_COLS
PROJ_TN = N_PROJ // 7
assert PROJ_TN * 7 == N_PROJ and PROJ_TN % MXU_COLS == 0
PROJ_VMEM_LIMIT = 52 * 1024 * 1024

SRC_QK = 0
SRC_AV = 2 * A_WIDTH
SRC_I = SRC_AV + A_WIDTH
SRC_AO = SRC_I + 2 * A_HEADS
SRC_BQ = SRC_AO + 2 * A_WIDTH
SRC_BV = SRC_BQ + 2 * B_KWIDTH
SRC_BG = SRC_BV + B_WIDTH
SRC_BZ = SRC_BG + GATE_RANK
RELAYOUT_RUNS = ((COL_Q, COL_BQ, SRC_QK), (COL_BQ, COL_AV, SRC_BQ), (COL_AV, COL_AO, SRC_AV),
                 (COL_AO, COL_BV, SRC_AO), (COL_BV, COL_BZ, SRC_BV), (COL_BZ, COL_GATE, SRC_BZ))

MLSTM_CHUNK = 256
GLA_STEP = 256
GLA_BLOCK = 16


def _dot(a, b):
    return jnp.dot(a, b, preferred_element_type=F32)


def _dot_nt(a, b):
    return lax.dot_general(a, b, (((1,), (1,)), ((), ())), preferred_element_type=F32)


def _dot_tn(a, b):
    return lax.dot_general(a, b, (((0,), (0,)), ((), ())), preferred_element_type=F32)


def _mask_dot(mask_bf16, x):
    hi = x.astype(BF16)
    r1 = x - hi.astype(F32)
    mid = r1.astype(BF16)
    lo = (r1 - mid.astype(F32)).astype(BF16)
    return _dot(mask_bf16, hi) + _dot(mask_bf16, mid) + _dot(mask_bf16, lo)


def _log_sigmoid(z):
    return jnp.minimum(z, 0.0) - LN_2 * jnp.log2(1.0 + jnp.exp2(jnp.abs(z) * (-LOG2_E)))


def _sigmoid(z):
    return 1.0 / (1.0 + jnp.exp2(z * (-LOG2_E)))


def _silu(z):
    return z * _sigmoid(z)


def _proj_kernel(x_ref, w_ref, o_ref, xb_ref):
    @pl.when(pl.program_id(1) == 0)
    def _():
        xb_ref[...] = x_ref[...].astype(BF16)

    o_ref[...] = _dot_nt(xb_ref[...], w_ref[...])


def _proj(x, w_t, tm, tn):
    t, k = x.shape
    n = w_t.shape[0]
    return pl.pallas_call(
        _proj_kernel,
        grid=(t // tm, n // tn),
        in_specs=[pl.BlockSpec((tm, k), lambda i, j: (i, 0)),
                  pl.BlockSpec((tn, k), lambda i, j: (j, 0))],
        out_specs=pl.BlockSpec((tm, tn), lambda i, j: (i, j)),
        out_shape=jax.ShapeDtypeStruct((t, n), F32),
        scratch_shapes=[pltpu.VMEM((tm, k), BF16)],
        compiler_params=pltpu.CompilerParams(
            dimension_semantics=("parallel", "arbitrary"), vmem_limit_bytes=PROJ_VMEM_LIMIT),
        name="proj",
    )(x, w_t)


def _mlstm_kernel(qp_ref, kp_ref, v_ref, ao_ref, az_ref, g_ref, cwq_ref, cwk_ref, cbq_ref, cbk_ref,
                  gb_ref, ng_ref, *rest, n_chunks):
    dec_in, (ya_ref, c_ref, n_ref, m_ref), dec_out, (tailq_ref, tailk_ref) = (
        rest[:21], rest[21:25], rest[25:29], rest[29:])
    L = MLSTM_CHUNK
    c = pl.program_id(2)

    @pl.when(c == 0)
    def _():
        c_ref[...] = jnp.zeros_like(c_ref)
        n_ref[...] = jnp.zeros_like(n_ref)
        m_ref[...] = jnp.zeros_like(m_ref)
        tailq_ref[...] = jnp.zeros_like(tailq_ref)
        tailk_ref[...] = jnp.zeros_like(tailk_ref)

    _mlstm_decode_step((pl.program_id(0) * n_chunks + c) % A_HEADS, *dec_in, *dec_out)

    sub = lax.broadcasted_iota(jnp.int32, (1, SUBLANES, 1), 1)

    def conv_silu(x_ref, tail_ref, w_ref, b_ref):
        x = x_ref[...]
        width = x.shape[1]
        x3 = jnp.concatenate([tail_ref[...], x], axis=0).reshape(L // SUBLANES + 1, SUBLANES, width)
        acc = b_ref[...] + x * w_ref[CONV_W - 1:CONV_W, :]
        for j in range(1, CONV_W):
            rot = pltpu.roll(x3, j, axis=1)
            xs = jnp.where(sub < j, rot[:-1], rot[1:]).reshape(L, width)
            acc = acc + xs * w_ref[CONV_W - 1 - j:CONV_W - j, :]
        tail_ref[...] = x[L - SUBLANES:, :]
        return _silu(acc)

    q_all = conv_silu(qp_ref, tailq_ref, cwq_ref, cbq_ref)
    k_all = conv_silu(kp_ref, tailk_ref, cwk_ref, cbk_ref) * (A_DK ** -0.5)

    lane = lax.broadcasted_iota(jnp.int32, (L, LANES), 1)
    row = lax.broadcasted_iota(jnp.int32, (L, L), 0)
    col = lax.broadcasted_iota(jnp.int32, (L, L), 1)
    causal = row >= col
    causal_b = causal.astype(BF16)

    g = g_ref[...] + gb_ref[...]
    is_f = (lane >= GATE_LANE_F) & (lane < GATE_LANE_F + A_HEADS)
    g2 = jnp.where(is_f, _log_sigmoid(g), g)
    x2 = jnp.where(is_f, _mask_dot(causal_b, g2), g2)
    x2t = x2.T

    for hh in range(A_HEADS):
        cols = slice(hh * A_DK, (hh + 1) * A_DK)
        q, k, v = q_all[:, cols], k_all[:, cols], v_ref[:, cols]
        li, lf = GATE_LANE_I + hh, GATE_LANE_F + hh
        itil_col, b_col = x2[:, li:li + 1], x2[:, lf:lf + 1]
        itil_row, b_row = x2t[li:li + 1, :], x2t[lf:lf + 1, :]

        m_prev = m_ref[0, hh, 0:1, 0:1]
        dmat = jnp.where(causal, b_col - b_row + itil_row, -jnp.inf)
        inter = b_col + m_prev
        m_t = jnp.maximum(inter, jnp.max(dmat, axis=1, keepdims=True))
        w = jnp.exp(dmat - m_t)
        decay = jnp.exp(inter - m_t)

        qb, kb, vb = q.astype(BF16), k.astype(BF16), v.astype(BF16)
        c_old = c_ref[0, hh]
        n_old = n_ref[0, hh]
        s = _dot_nt(qb, kb) * w
        num = decay * _dot(qb, c_old.astype(BF16)) + _dot(s.astype(BF16), vb)
        den = decay * jnp.sum(q * n_old, axis=1, keepdims=True) + jnp.sum(s, axis=1, keepdims=True)
        inv = 1.0 / jnp.maximum(jnp.abs(den), jnp.exp(-m_t))

        m_new = m_t[L - 1:L, :]
        b_last = b_col[L - 1:L, :]
        wk = jnp.exp(b_last - b_col + itil_col - m_new)
        dec = jnp.exp(b_last + m_prev - m_new)
        kw = k * wk
        c_ref[0, hh] = dec * c_old + _dot(kw.T.astype(BF16), vb)
        n_ref[0, hh] = dec * n_old + jnp.sum(kw, axis=0, keepdims=True)
        m_ref[0, hh] = jnp.broadcast_to(m_new, (1, LANES))

        mu = jnp.mean(num, axis=1, keepdims=True)
        hc = num - mu
        var = jnp.mean(hc * hc, axis=1, keepdims=True)
        hn = hc * (inv * lax.rsqrt(var * (inv * inv) + LN_EPS)) * ng_ref[:, cols]
        ya_ref[:, cols] = (hn * _sigmoid(ao_ref[:, cols]) * _silu(az_ref[:, cols])).astype(ya_ref.dtype)


def _mlstm(p_all, batch, seq, conv_w, conv_b, gate_bias, a_norm_g, p_s, conv_state, c_state, n_state, m_state):
    L = MLSTM_CHUNK
    hps = A_HEADS
    nc = seq // L
    t = batch * seq
    wd = hps * A_DK
    k_off = A_WIDTH // wd

    def rows(b, h, c):
        return b * nc + c

    def pcol(base):
        return pl.BlockSpec((L, wd), lambda b, h, c: (rows(b, h, c), base // wd + h))

    in_specs = [
        pcol(COL_Q), pcol(COL_K), pcol(COL_AV), pcol(COL_AO), pcol(COL_AZ),
        pl.BlockSpec((L, LANES), lambda b, h, c: (rows(b, h, c), COL_GATE // LANES)),
        pl.BlockSpec((CONV_W, wd), lambda b, h, c: (0, h)),
        pl.BlockSpec((CONV_W, wd), lambda b, h, c: (0, k_off + h)),
        pl.BlockSpec((1, wd), lambda b, h, c: (0, h)),
        pl.BlockSpec((1, wd), lambda b, h, c: (0, k_off + h)),
        pl.BlockSpec((1, LANES), lambda b, h, c: (0, 0)),
        pl.BlockSpec((1, wd), lambda b, h, c: (0, h)),
    ]
    out_specs = [
        pl.BlockSpec((L, wd), lambda b, h, c: (rows(b, h, c), h)),
        pl.BlockSpec((1, hps, A_DK, A_DV), lambda b, h, c: (b, h, 0, 0)),
        pl.BlockSpec((1, hps, 1, A_DK), lambda b, h, c: (b, h, 0, 0)),
        pl.BlockSpec((1, hps, 1, LANES), lambda b, h, c: (b, h, 0, 0)),
    ]
    out_shape = [
        jax.ShapeDtypeStruct((t, A_WIDTH), BF16),
        jax.ShapeDtypeStruct((batch, A_HEADS, A_DK, A_DV), F32),
        jax.ShapeDtypeStruct((batch, A_HEADS, 1, A_DK), F32),
        jax.ShapeDtypeStruct((batch, A_HEADS, 1, LANES), F32),
    ]

    nseq = p_s.shape[0]
    TB = DEC_TOKENS
    dstep = _decode_step(batch, nc, nseq)

    def dcol(base, width):
        return pl.BlockSpec((TB, width), lambda b, h, c: (dstep(b, h, c)[0], base // width + dstep(b, h, c)[1]))

    def dconv(j, base):
        return pl.BlockSpec((TB, A_DK), lambda b, h, c: (dstep(b, h, c)[0],
                                                        (j * 2 * A_WIDTH + base) // A_DK + dstep(b, h, c)[1]))

    def dhead(shape, off=0):
        return pl.BlockSpec(shape, lambda b, h, c: (0, off + dstep(b, h, c)[1]))

    def dstate(*dims):
        return pl.BlockSpec((TB, 1) + dims, lambda b, h, c: dstep(b, h, c) + (0,) * len(dims))

    in_specs += [
        dcol(COL_Q, A_DK), dcol(COL_K, A_DK), dcol(COL_AV, A_DV), dcol(COL_AO, A_DV), dcol(COL_AZ, A_DV),
        pl.BlockSpec((TB, LANES), lambda b, h, c: (dstep(b, h, c)[0], COL_GATE // LANES)),
        dconv(0, 0), dconv(1, 0), dconv(2, 0), dconv(0, A_WIDTH), dconv(1, A_WIDTH), dconv(2, A_WIDTH),
        dhead((CONV_W, A_DK)), dhead((CONV_W, A_DK), A_HEADS), dhead((1, A_DK)), dhead((1, A_DK), A_HEADS),
        pl.BlockSpec((1, LANES), lambda b, h, c: (0, 0)), dhead((1, A_DV)),
        dstate(A_DK, A_DV), dstate(1, A_DK),
        pl.BlockSpec((TB, A_HEADS), lambda b, h, c: (dstep(b, h, c)[0], 0)),
    ]
    out_specs += [
        pl.BlockSpec((TB, A_DV), lambda b, h, c: dstep(b, h, c)),
        dstate(A_DK, A_DV), dstate(1, A_DK),
        pl.BlockSpec((TB, LANES), lambda b, h, c: dstep(b, h, c)),
    ]
    out_shape += [
        jax.ShapeDtypeStruct((nseq, A_WIDTH), BF16),
        jax.ShapeDtypeStruct((nseq, A_HEADS, A_DK, A_DV), F32),
        jax.ShapeDtypeStruct((nseq, A_HEADS, 1, A_DK), F32),
        jax.ShapeDtypeStruct((nseq, A_HEADS * LANES), F32),
    ]
    return pl.pallas_call(
        functools.partial(_mlstm_kernel, n_chunks=nc),
        grid=(batch, A_HEADS // hps, nc),
        in_specs=in_specs,
        out_specs=out_specs,
        out_shape=out_shape,
        scratch_shapes=[pltpu.VMEM((SUBLANES, wd), F32),
                        pltpu.VMEM((SUBLANES, wd), F32)],
        compiler_params=pltpu.CompilerParams(
            dimension_semantics=("parallel", "parallel", "arbitrary"), vmem_limit_bytes=VMEM_LIMIT),
        name="mlstm",
    )(p_all, p_all, p_all, p_all, p_all, p_all, conv_w, conv_w, conv_b, conv_b, gate_bias, a_norm_g,
      p_s, p_s, p_s, p_s, p_s, p_s,
      conv_state, conv_state, conv_state, conv_state, conv_state, conv_state,
      conv_w, conv_w, conv_b, conv_b, gate_bias, a_norm_g, c_state, n_state, m_state)


def _gla_gate_log(bg_tile, wg, bgate):
    lr = bg_tile[:, GATE_LANE_LR:GATE_LANE_LR + GATE_RANK]
    z = _dot(lr.astype(BF16), wg.astype(BF16)) + bgate
    return _log_sigmoid(z) / GATE_TAU


def _gla_kernel(q_ref, k_ref, v_ref, bz_ref, g_ref, wg_ref, bgate_ref, ng_ref, *rest):
    dec_in, w_in, (yb_ref, s_out_ref), dec_out, w_out, (st_ref, ds_ref, sb_ref) = (
        rest[:9], rest[9:12], rest[12:14], rest[14:16], rest[16:19], rest[19:])
    L = GLA_STEP
    B = GLA_BLOCK
    nb = L // B
    c = pl.program_id(2)

    @pl.when(c == 0)
    def _():
        st_ref[...] = jnp.zeros_like(st_ref)

    _gla_decode_step(*dec_in, *dec_out)
    for src_ref, dst_ref in zip(w_in, w_out):
        dst_ref[...] = src_ref[...].astype(BF16)

    row = lax.broadcasted_iota(jnp.int32, (L, L), 0)
    col = lax.broadcasted_iota(jnp.int32, (L, L), 1)
    tri_b = (((row // B) == (col // B)) & (row >= col)).astype(BF16)
    t_in = lax.broadcasted_iota(jnp.int32, (L, 1), 0) % B
    s_lane = lax.broadcasted_iota(jnp.int32, (1, LANES), 1)

    for hh in range(B_HEADS):
        kcols = slice(hh * B_DK, (hh + 1) * B_DK)
        vcols = slice(hh * B_DV, (hh + 1) * B_DV)
        loga = _gla_gate_log(g_ref[...], wg_ref[:, kcols], bgate_ref[:, kcols])
        bc = _mask_dot(tri_b, loga)
        bl = jnp.broadcast_to(bc.reshape(nb, B, B_DK)[:, B - 1:B, :],
                              (nb, B, B_DK)).reshape(L, B_DK)
        k = k_ref[:, kcols]
        qs = q_ref[:, kcols] * (B_DK ** -0.5)
        qt = (qs * jnp.exp(bc)).astype(BF16)
        kt = (k * jnp.exp(bl - bc)).astype(BF16)
        eb = jnp.exp(bl)
        vb = v_ref[:, vcols].astype(BF16)

        bc3, q3, k3 = (x.reshape(nb, B, B_DK) for x in (bc * LOG2_E, qs, k))
        groups = B // SUBLANES
        bcg = [bc3[:, g * SUBLANES:(g + 1) * SUBLANES, :] for g in range(groups)]
        qg = [q3[:, g * SUBLANES:(g + 1) * SUBLANES, :] for g in range(groups)]
        ag = [jnp.zeros((nb, SUBLANES, LANES), F32) for _ in range(groups)]
        for s in range(B):
            bc_s, k_s = bc3[:, s:s + 1, :], k3[:, s:s + 1, :]
            for g in range(s // SUBLANES, groups):
                e = jnp.exp2(bcg[g] - bc_s)
                a_col = jnp.sum(qg[g] * k_s * e, axis=2, keepdims=True)
                ag[g] = jnp.where(s_lane == s, a_col, ag[g])
        a = jnp.concatenate(ag, axis=1).reshape(L, LANES)
        ab = jnp.where(t_in >= s_lane, a, 0.0)[:, 0:B].astype(BF16)

        for j in range(nb):
            ds_ref[hh, j] = _dot_tn(vb[j * B:(j + 1) * B, :], kt[j * B:(j + 1) * B, :])
        st = st_ref[hh]
        for j in range(nb):
            sb_ref[hh, j] = st.astype(BF16)
            st = st * eb[j * B:j * B + 1, :] + ds_ref[hh, j]
        st_ref[hh] = st
        o = jnp.concatenate(
            [_dot(ab[j * B:(j + 1) * B, :], vb[j * B:(j + 1) * B, :])
             + _dot_nt(qt[j * B:(j + 1) * B, :], sb_ref[hh, j]) for j in range(nb)], axis=0)

        on = o * lax.rsqrt(jnp.mean(o * o, axis=1, keepdims=True) + LN_EPS) * ng_ref[:, vcols]
        yb_ref[:, vcols] = (on * _silu(bz_ref[:, vcols])).astype(yb_ref.dtype)

    @pl.when(c == pl.num_programs(2) - 1)
    def _():
        for hh in range(B_HEADS):
            s_out_ref[0, hh] = st_ref[hh].T


def _gla(p_all, batch, seq, w_gate_up, b_gate, b_norm_g, p_s, s_state, cast_weights):
    L = GLA_STEP
    hps = B_HEADS
    nc = seq // L
    t = batch * seq
    kw, vw = hps * B_DK, hps * B_DV

    def rows(b, h, c):
        return b * nc + c

    in_specs = [
        pl.BlockSpec((L, kw), lambda b, h, c: (rows(b, h, c), COL_BQ // kw + h)),
        pl.BlockSpec((L, kw), lambda b, h, c: (rows(b, h, c), COL_BK // kw + h)),
        pl.BlockSpec((L, vw), lambda b, h, c: (rows(b, h, c), COL_BV // vw + h)),
        pl.BlockSpec((L, vw), lambda b, h, c: (rows(b, h, c), COL_BZ // vw + h)),
        pl.BlockSpec((L, LANES), lambda b, h, c: (rows(b, h, c), COL_GATE // LANES)),
        pl.BlockSpec((GATE_RANK, kw), lambda b, h, c: (0, h)),
        pl.BlockSpec((1, kw), lambda b, h, c: (0, h)),
        pl.BlockSpec((1, vw), lambda b, h, c: (0, h)),
    ]
    out_specs = [
        pl.BlockSpec((L, vw), lambda b, h, c: (rows(b, h, c), h)),
        pl.BlockSpec((1, hps, B_DK, B_DV), lambda b, h, c: (b, h, 0, 0)),
    ]
    out_shape = [
        jax.ShapeDtypeStruct((t, B_WIDTH), BF16),
        jax.ShapeDtypeStruct((batch, B_HEADS, B_DK, B_DV), F32),
    ]

    nseq = p_s.shape[0]
    TB = DEC_TOKENS
    dstep = _decode_step(batch, nc, nseq)

    def dcol(base, width):
        return pl.BlockSpec((TB, width), lambda b, h, c: (dstep(b, h, c)[0], base // width + dstep(b, h, c)[1]))

    def dhead(shape):
        return pl.BlockSpec(shape, lambda b, h, c: (0, dstep(b, h, c)[1]))

    dstate = pl.BlockSpec((TB, 1, B_DK, B_DV), lambda b, h, c: dstep(b, h, c) + (0, 0))
    in_specs += [
        dcol(COL_BQ, B_DK), dcol(COL_BK, B_DK), dcol(COL_BV, B_DV), dcol(COL_BZ, B_DV),
        pl.BlockSpec((TB, LANES), lambda b, h, c: (dstep(b, h, c)[0], COL_GATE // LANES)),
        dhead((GATE_RANK, B_DK)), dhead((1, B_DK)), dhead((1, B_DV)), dstate,
    ]
    out_specs += [pl.BlockSpec((TB, B_DV), lambda b, h, c: dstep(b, h, c)), dstate]
    out_shape += [
        jax.ShapeDtypeStruct((nseq, B_WIDTH), BF16),
        jax.ShapeDtypeStruct((nseq, B_HEADS, B_DK, B_DV), F32),
    ]

    n_steps = batch * nc
    for wmat in cast_weights:
        slab = pl.BlockSpec((wmat.shape[0] // n_steps, wmat.shape[1]), lambda b, h, c: (rows(b, h, c), 0))
        in_specs.append(slab)
        out_specs.append(slab)
        out_shape.append(jax.ShapeDtypeStruct(wmat.shape, BF16))
    return pl.pallas_call(
        _gla_kernel,
        grid=(batch, B_HEADS // hps, nc),
        in_specs=in_specs,
        out_specs=out_specs,
        out_shape=out_shape,
        scratch_shapes=[pltpu.VMEM((hps, B_DV, B_DK), F32),
                        pltpu.VMEM((hps, L // GLA_BLOCK, B_DV, B_DK), F32),
                        pltpu.VMEM((hps, L // GLA_BLOCK, B_DV, B_DK), BF16)],
        compiler_params=pltpu.CompilerParams(
            dimension_semantics=("parallel", "parallel", "arbitrary"), vmem_limit_bytes=VMEM_LIMIT),
        name="gla",
    )(p_all, p_all, p_all, p_all, p_all, w_gate_up, b_gate, b_norm_g,
      p_s, p_s, p_s, p_s, p_s, w_gate_up, b_gate, b_norm_g, s_state, *cast_weights)


DEC_TOKENS = 16


def _decode_step(batch, nc, nseq):
    assert batch * nc == (nseq // DEC_TOKENS) * A_HEADS and A_HEADS == B_HEADS

    def step(b, h, c):
        s = b * nc + c
        return s // A_HEADS, s % A_HEADS
    return step


def _mlstm_decode_step(h_idx, qp_ref, kp_ref, av_ref, ao_ref, az_ref, g_ref,
                       sq0_ref, sq1_ref, sq2_ref, sk0_ref, sk1_ref, sk2_ref,
                       cwq_ref, cwk_ref, cbq_ref, cbk_ref, gb_ref, ang_ref, c_ref, n_ref, m_ref,
                       ya_ref, c_out_ref, n_out_ref, m_out_ref):
    TB = DEC_TOKENS

    def conv_silu(s0, s1, s2, x, w_ref, b_ref):
        acc = b_ref[...] + s0[...] * w_ref[0:1, :]
        acc = acc + s1[...] * w_ref[1:2, :]
        acc = acc + s2[...] * w_ref[2:3, :]
        acc = acc + x[...] * w_ref[3:4, :]
        return _silu(acc)

    q = conv_silu(sq0_ref, sq1_ref, sq2_ref, qp_ref, cwq_ref, cbq_ref)
    k = conv_silu(sk0_ref, sk1_ref, sk2_ref, kp_ref, cwk_ref, cbk_ref) * (A_DK ** -0.5)
    v = av_ref[...]
    g = g_ref[...]
    gbias = gb_ref[...]
    gate_lane = lax.broadcasted_iota(jnp.int32, (TB, LANES), 1)
    gsum = g + gbias

    def gate(lane0):
        return jnp.sum(jnp.where(gate_lane == lane0 + h_idx, gsum, 0.0), axis=1, keepdims=True)

    itil = gate(GATE_LANE_I)
    logf = _log_sigmoid(gate(GATE_LANE_F))
    lane_h = lax.broadcasted_iota(jnp.int32, (TB, A_HEADS), 1)
    m_prev = jnp.sum(jnp.where(lane_h == h_idx, m_ref[...], 0.0), axis=1, keepdims=True)
    inter = logf + m_prev
    m_t = jnp.maximum(inter, itil)
    w = jnp.exp(itil - m_t)
    decay = jnp.exp(inter - m_t)
    n_old = n_ref[:, 0, 0, :]
    s = jnp.sum(q * k, axis=1, keepdims=True) * w
    den = decay * jnp.sum(q * n_old, axis=1, keepdims=True) + s
    scale = 1.0 / jnp.maximum(jnp.abs(den), jnp.exp(-m_t))
    kw = k * w
    n_out_ref[:, 0, 0, :] = decay * n_old + kw
    m_out_ref[...] = jnp.broadcast_to(m_t, (TB, LANES))

    rows = lax.broadcasted_iota(jnp.int32, (TB, 1), 0)
    qb, kwb, vb = q.astype(BF16), kw.astype(BF16), v.astype(BF16)
    h_rows = []
    for t in range(TB):
        c_old = c_ref[t, 0]
        qc = _dot(qb, c_old.astype(BF16))[t:t + 1, :]
        h_rows.append((decay[t:t + 1, :] * qc + s[t:t + 1, :] * v[t:t + 1, :]) * scale[t:t + 1, :])
        outer = _dot_tn(jnp.where(rows == t, kwb, jnp.zeros_like(kwb)), vb)
        c_out_ref[t, 0] = decay[t:t + 1, :] * c_old + outer
    h = jnp.concatenate(h_rows, axis=0)
    mu = jnp.mean(h, axis=1, keepdims=True)
    hc = h - mu
    var = jnp.mean(hc * hc, axis=1, keepdims=True)
    hn = hc * lax.rsqrt(var + LN_EPS) * ang_ref[...]
    ya_ref[...] = (hn * _sigmoid(ao_ref[...]) * _silu(az_ref[...])).astype(ya_ref.dtype)


def _gla_decode_step(bq_ref, bk_ref, bv_ref, bz_ref, g_ref, wg_ref, bgate_ref, bng_ref, s_ref,
                     yb_ref, s_out_ref):
    TB = DEC_TOKENS
    gq = bq_ref[...] * (B_DK ** -0.5)
    gk = bk_ref[...]
    gv = bv_ref[...]
    loga = _gla_gate_log(g_ref[...], wg_ref[...], bgate_ref[...])
    eb = jnp.exp(loga)
    a = jnp.sum(gq * gk, axis=1, keepdims=True)
    ebt = jnp.concatenate([eb, jnp.zeros((LANES - TB, B_DK), F32)], axis=0).T
    rows = lax.broadcasted_iota(jnp.int32, (TB, 1), 0)
    qeb, gkb, gvb = (gq * eb).astype(BF16), gk.astype(BF16), gv.astype(BF16)
    o_rows = []
    for t in range(TB):
        s_old = s_ref[t, 0]
        o_rows.append(_dot(qeb, s_old.astype(BF16))[t:t + 1, :] + a[t:t + 1, :] * gv[t:t + 1, :])
        outer = _dot_tn(jnp.where(rows == t, gkb, jnp.zeros_like(gkb)), gvb)
        s_out_ref[t, 0] = ebt[:, t:t + 1] * s_old + outer
    o = jnp.concatenate(o_rows, axis=0)
    on = o * lax.rsqrt(jnp.mean(o * o, axis=1, keepdims=True) + LN_EPS) * bng_ref[...]
    yb_ref[...] = (on * _silu(bz_ref[...])).astype(yb_ref.dtype)


def _out_kernel(ya_ref, yb_ref, ga_ref, gb_ref, x_ref, yas_ref, ybs_ref, gas_ref, gbs_ref, xs_ref,
                wpa_ref, wpb_ref, wo_ref, lng_ref, lnb_ref, o_ref, os_ref):
    def rows(ya, yb, ga, gb, x):
        pa = _dot(ya[...], wpa_ref[...])
        pb = _dot(yb[...], wpb_ref[...])
        merged = _sigmoid(ga[...]) * pa + _sigmoid(gb[...]) * pb
        r = ALPHA * x[...] + _dot(merged.astype(BF16), wo_ref[...])
        mu = jnp.mean(r, axis=1, keepdims=True)
        rc = r - mu
        var = jnp.mean(rc * rc, axis=1, keepdims=True)
        return rc * lax.rsqrt(var + LN_EPS) * lng_ref[...] + lnb_ref[...]

    last = pl.num_programs(0) - 1

    @pl.when(pl.program_id(0) < last)
    def _():
        o_ref[...] = rows(ya_ref, yb_ref, ga_ref, gb_ref, x_ref)

    @pl.when(pl.program_id(0) == last)
    def _():
        os_ref[...] = rows(yas_ref, ybs_ref, gas_ref, gbs_ref, xs_ref)


def _out(ya, yb, p_all, x, ya_s, yb_s, p_s, xs, w_pa, w_pb, w_out, ln_g, ln_b, tm):
    t, ts = x.shape[0], xs.shape[0]
    n_tiles = t // tm
    const = lambda i: (0, 0)
    single = pl.Buffered(1)

    def tile(i):
        return jnp.minimum(i, n_tiles - 1)

    in_specs = [
        pl.BlockSpec((tm, A_WIDTH), lambda i: (tile(i), 0)),
        pl.BlockSpec((tm, B_WIDTH), lambda i: (tile(i), 0)),
        pl.BlockSpec((tm, D_MODEL), lambda i: (tile(i), COL_GA // D_MODEL)),
        pl.BlockSpec((tm, D_MODEL), lambda i: (tile(i), COL_GB // D_MODEL)),
        pl.BlockSpec((tm, D_MODEL), lambda i: (tile(i), 0)),
        pl.BlockSpec((ts, A_WIDTH), const),
        pl.BlockSpec((ts, B_WIDTH), const),
        pl.BlockSpec((ts, D_MODEL), lambda i: (0, COL_GA // D_MODEL)),
        pl.BlockSpec((ts, D_MODEL), lambda i: (0, COL_GB // D_MODEL)),
        pl.BlockSpec((ts, D_MODEL), const),
        pl.BlockSpec((A_WIDTH, D_MODEL), const, pipeline_mode=single),
        pl.BlockSpec((B_WIDTH, D_MODEL), const, pipeline_mode=single),
        pl.BlockSpec((D_MODEL, D_MODEL), const, pipeline_mode=single),
        pl.BlockSpec((1, D_MODEL), const),
        pl.BlockSpec((1, D_MODEL), const),
    ]
    return pl.pallas_call(
        _out_kernel,
        grid=(n_tiles + 1,),
        in_specs=in_specs,
        out_specs=[pl.BlockSpec((tm, D_MODEL), lambda i: (tile(i), 0)),
                   pl.BlockSpec((ts, D_MODEL), const)],
        out_shape=[jax.ShapeDtypeStruct((t, D_MODEL), F32), jax.ShapeDtypeStruct((ts, D_MODEL), F32)],
        compiler_params=pltpu.CompilerParams(
            dimension_semantics=("arbitrary",), vmem_limit_bytes=VMEM_LIMIT),
        name="outproj",
    )(ya, yb, p_all, p_all, x, ya_s, yb_s, p_s, p_s, xs, w_pa, w_pb, w_out, ln_g, ln_b)


RELAYOUT_TN = 1024


def _relayout_kernel(src_ref, gif_ref, gb0_ref, gb1_ref, xs_ref, o_ref, ps_ref):
    j = pl.program_id(0)
    tn = RELAYOUT_TN
    n_main = COL_GATE // tn

    @pl.when(j < n_main)
    def _():
        o_ref[...] = src_ref[...].astype(BF16)

    @pl.when(j == n_main)
    def _():
        pad = jnp.zeros((tn - 2 * A_HEADS - GATE_RANK, src_ref.shape[1]), F32)
        o_ref[...] = jnp.concatenate([gif_ref[...], gb0_ref[...], gb1_ref[...], pad], axis=0).astype(BF16)

    ps_ref[...] = _dot_nt(xs_ref[...].astype(BF16), o_ref[...])


def _relayout_w_in(w_in_t, xs):
    n_src, k = w_in_t.shape
    ts = xs.shape[0]
    tn = RELAYOUT_TN

    def src_row(j):
        shift = 0
        for first, last, src in RELAYOUT_RUNS:
            assert first % tn == 0 and last % tn == 0 and (src - first) % SUBLANES == 0
            shift = jnp.where((j >= first // tn) & (j < last // tn), (src - first) // SUBLANES, shift)
        group = jnp.minimum(j * (tn // SUBLANES) + shift, (n_src - tn) // SUBLANES)
        return group * SUBLANES

    return pl.pallas_call(
        _relayout_kernel,
        grid=(pl.cdiv(N_PROJ, tn),),
        in_specs=[pl.BlockSpec((pl.Element(tn), pl.Element(k)), lambda j: (src_row(j), 0)),
                  pl.BlockSpec((SUBLANES, k), lambda j: (SRC_I // SUBLANES, 0)),
                  pl.BlockSpec((SUBLANES, k), lambda j: (SRC_BG // SUBLANES, 0)),
                  pl.BlockSpec((SUBLANES, k), lambda j: (SRC_BG // SUBLANES + 1, 0)),
                  pl.BlockSpec((ts, k), lambda j: (0, 0))],
        out_specs=[pl.BlockSpec((tn, k), lambda j: (j, 0)),
                   pl.BlockSpec((ts, tn), lambda j: (0, j))],
        out_shape=[jax.ShapeDtypeStruct((N_PROJ, k), BF16), jax.ShapeDtypeStruct((ts, N_PROJ), F32)],
        compiler_params=pltpu.CompilerParams(
            dimension_semantics=("parallel",), vmem_limit_bytes=VMEM_LIMIT),
        name="relayout",
    )(w_in_t, w_in_t, w_in_t, w_in_t, xs)


def kernel(x_prompt, x_sample, state_mlstm_C, state_mlstm_n, state_mlstm_m, state_conv, state_gla_S,
           w_in, conv_w, conv_b, b_i, b_f, a_norm_g, w_gate_up, b_gate, b_norm_g, w_pa, w_pb, w_out,
           ln_g, ln_b):
    batch, seq, _ = x_prompt.shape
    nseq = x_sample.shape[0]
    assert w_in.shape[0] == 1, "single-layer step"
    d = 0

    def layer(a):
        return a.reshape(a.shape[1:])

    xs = x_sample.reshape(nseq, D_MODEL)
    wp, ps_all = _relayout_w_in(layer(w_in).T, xs)
    cw = conv_w[d]
    cb = conv_b[d][None, :]
    gate_bias = jnp.concatenate([b_i[d], b_f[d], jnp.zeros((LANES - 2 * A_HEADS,), F32)])[None, :]
    ang = a_norm_g[d][None, :]
    bng = b_norm_g[d][None, :]
    wg = w_gate_up[d]
    bgate = b_gate[d][None, :]
    lng, lnb = ln_g[d][None, :], ln_b[d][None, :]

    xp = x_prompt.reshape(batch * seq, D_MODEL)
    p_all = _proj(xp, wp, tm=1024, tn=PROJ_TN)

    conv_state = state_conv.reshape(nseq, (CONV_W - 1) * 2 * A_WIDTH)
    ya, p_c, p_n, p_m, ya_s, s_c, s_n, s_m = _mlstm(
        p_all, batch, seq, cw, cb, gate_bias, ang, ps_all, conv_state,
        layer(state_mlstm_C), state_mlstm_n.reshape(nseq, A_HEADS, 1, A_DK), layer(state_mlstm_m))
    yb, p_s, yb_s, s_s, wpa, wpb, wo = _gla(p_all, batch, seq, wg, bgate, bng, ps_all, layer(state_gla_S),
                                            (layer(w_pa), layer(w_pb), layer(w_out)))

    y_prompt, y_sample = _out(ya, yb, p_all, xp, ya_s, yb_s, ps_all, xs, wpa, wpb, wo, lng, lnb, tm=256)
    y_prompt = y_prompt.reshape(batch, seq, D_MODEL)
    y_sample = y_sample.reshape(nseq, 1, D_MODEL)
    p_conv = p_all.reshape(batch, seq, N_PROJ)[:, seq - (CONV_W - 1):, :2 * A_WIDTH]
    s_conv = jnp.concatenate([layer(state_conv)[:, 1:, :], ps_all[:, None, :2 * A_WIDTH]], axis=1)

    def stacked(a):
        return a.reshape((1,) + a.shape)

    return (y_prompt, y_sample,
            stacked(p_c), p_n.reshape(1, batch, A_HEADS, A_DK), stacked(p_m[:, :, 0, 0]),
            stacked(p_conv), stacked(p_s),
            stacked(s_c), s_n.reshape(1, nseq, A_HEADS, A_DK),
            stacked(s_m.reshape(nseq, A_HEADS, LANES)[:, :, 0]), stacked(s_conv), stacked(s_s))
```

```python
import functools

import jax
import jax.numpy as jnp
from jax import lax
from jax.experimental import pallas as pl
from jax.experimental.pallas import tpu as pltpu

F32 = jnp.float32
BF16 = jnp.bfloat16

D_MODEL = 2048
A_HEADS = 4
A_WIDTH = 1024
A_DK = 256
A_DV = 256
CONV_W = 4
B_HEADS = 4
B_WIDTH = 1024
B_KWIDTH = 512
B_DK = 128
B_DV = 256
GATE_RANK = 16
GATE_TAU = 16.0
ALPHA = 2.0 ** 0.25
LN_EPS = 1e-5
LOG2_E = 1.4426950408889634
LN_2 = 0.6931471805599453

LANES = 128
SUBLANES = 8
VMEM_LIMIT = 48 * 1024 * 1024

COL_Q = 0
COL_K = A_WIDTH
COL_BQ = 2 * A_WIDTH
COL_BK = COL_BQ + B_KWIDTH
COL_AV = COL_BK + B_KWIDTH
COL_AO = COL_AV + A_WIDTH
COL_AZ = COL_AO + A_WIDTH
COL_BV = COL_AZ + A_WIDTH
COL_BZ = COL_BV + B_WIDTH
COL_GA = COL_BZ + B_WIDTH
COL_GB = COL_GA + D_MODEL
COL_GATE = COL_GB + D_MODEL
GATE_LANE_I = 0
GATE_LANE_F = A_HEADS
GATE_LANE_LR = 2 * A_HEADS
MXU_COLS = 256
N_PROJ = COL_GATE + MXU_COLS
PROJ_TN = N_PROJ // 7
assert PROJ_TN * 7 == N_PROJ and PROJ_TN % MXU_COLS == 0
PROJ_VMEM_LIMIT = 52 * 1024 * 1024

SRC_QK = 0
SRC_AV = 2 * A_WIDTH
SRC_I = SRC_AV + A_WIDTH
SRC_AO = SRC_I + 2 * A_HEADS
SRC_BQ = SRC_AO + 2 * A_WIDTH
SRC_BV = SRC_BQ + 2 * B_KWIDTH
SRC_BG = SRC_BV + B_WIDTH
SRC_BZ = SRC_BG + GATE_RANK
RELAYOUT_RUNS = ((COL_Q, COL_BQ, SRC_QK), (COL_BQ, COL_AV, SRC_BQ), (COL_AV, COL_AO, SRC_AV),
                 (COL_AO, COL_BV, SRC_AO), (COL_BV, COL_BZ, SRC_BV), (COL_BZ, COL_GATE, SRC_BZ))

MLSTM_CHUNK = 256
GLA_STEP = 256
GLA_BLOCK = 16


def _dot(a, b):
    return jnp.dot(a, b, preferred_element_type=F32)


def _dot_nt(a, b):
    return lax.dot_general(a, b, (((1,), (1,)), ((), ())), preferred_element_type=F32)


def _dot_tn(a, b):
    return lax.dot_general(a, b, (((0,), (0,)), ((), ())), preferred_element_type=F32)


def _mask_dot(mask_bf16, x):
    hi = x.astype(BF16)
    r1 = x - hi.astype(F32)
    mid = r1.astype(BF16)
    lo = (r1 - mid.astype(F32)).astype(BF16)
    return _dot(mask_bf16, hi) + _dot(mask_bf16, mid) + _dot(mask_bf16, lo)


def _log_sigmoid(z):
    return jnp.minimum(z, 0.0) - LN_2 * jnp.log2(1.0 + jnp.exp2(jnp.abs(z) * (-LOG2_E)))


def _sigmoid(z):
    return 1.0 / (1.0 + jnp.exp2(z * (-LOG2_E)))


def _silu(z):
    return z * _sigmoid(z)


def _proj_kernel(x_ref, w_ref, o_ref, xb_ref):
    @pl.when(pl.program_id(1) == 0)
    def _():
        xb_ref[...] = x_ref[...].astype(BF16)

    o_ref[...] = _dot_nt(xb_ref[...], w_ref[...])


def _proj(x, w_t, tm, tn):
    t, k = x.shape
    n = w_t.shape[0]
    return pl.pallas_call(
        _proj_kernel,
        grid=(t // tm, n // tn),
        in_specs=[pl.BlockSpec((tm, k), lambda i, j: (i, 0)),
                  pl.BlockSpec((tn, k), lambda i, j: (j, 0))],
        out_specs=pl.BlockSpec((tm, tn), lambda i, j: (i, j)),
        out_shape=jax.ShapeDtypeStruct((t, n), F32),
        scratch_shapes=[pltpu.VMEM((tm, k), BF16)],
        compiler_params=pltpu.CompilerParams(
            dimension_semantics=("parallel", "arbitrary"), vmem_limit_bytes=PROJ_VMEM_LIMIT),
        name="proj",
    )(x, w_t)


def _mlstm_kernel(qp_ref, kp_ref, v_ref, ao_ref, az_ref, g_ref, cwq_ref, cwk_ref, cbq_ref, cbk_ref,
                  gb_ref, ng_ref, *rest, n_chunks):
    dec_in, (ya_ref, c_ref, n_ref, m_ref), dec_out, (tailq_ref, tailk_ref) = (
        rest[:21], rest[21:25], rest[25:29], rest[29:])
    L = MLSTM_CHUNK
    c = pl.program_id(2)

    @pl.when(c == 0)
    def _():
        c_ref[...] = jnp.zeros_like(c_ref)
        n_ref[...] = jnp.zeros_like(n_ref)
        m_ref[...] = jnp.zeros_like(m_ref)
        tailq_ref[...] = jnp.zeros_like(tailq_ref)
        tailk_ref[...] = jnp.zeros_like(tailk_ref)

    _mlstm_decode_step((pl.program_id(0) * n_chunks + c) % A_HEADS, *dec_in, *dec_out)

    sub = lax.broadcasted_iota(jnp.int32, (1, SUBLANES, 1), 1)

    def conv_silu(x_ref, tail_ref, w_ref, b_ref):
        x = x_ref[...]
        width = x.shape[1]
        x3 = jnp.concatenate([tail_ref[...], x], axis=0).reshape(L // SUBLANES + 1, SUBLANES, width)
        acc = b_ref[...] + x * w_ref[CONV_W - 1:CONV_W, :]
        for j in range(1, CONV_W):
            rot = pltpu.roll(x3, j, axis=1)
            xs = jnp.where(sub < j, rot[:-1], rot[1:]).reshape(L, width)
            acc = acc + xs * w_ref[CONV_W - 1 - j:CONV_W - j, :]
        tail_ref[...] = x[L - SUBLANES:, :]
        return _silu(acc)

    q_all = conv_silu(qp_ref, tailq_ref, cwq_ref, cbq_ref)
    k_all = conv_silu(kp_ref, tailk_ref, cwk_ref, cbk_ref) * (A_DK ** -0.5)

    lane = lax.broadcasted_iota(jnp.int32, (L, LANES), 1)
    row = lax.broadcasted_iota(jnp.int32, (L, L), 0)
    col = lax.broadcasted_iota(jnp.int32, (L, L), 1)
    causal = row >= col
    causal_b = causal.astype(BF16)

    g = g_ref[...] + gb_ref[...]
    is_f = (lane >= GATE_LANE_F) & (lane < GATE_LANE_F + A_HEADS)
    g2 = jnp.where(is_f, _log_sigmoid(g), g)
    x2 = jnp.where(is_f, _mask_dot(causal_b, g2), g2)
    x2t = x2.T

    for hh in range(A_HEADS):
        cols = slice(hh * A_DK, (hh + 1) * A_DK)
        q, k, v = q_all[:, cols], k_all[:, cols], v_ref[:, cols]
        li, lf = GATE_LANE_I + hh, GATE_LANE_F + hh
        itil_col, b_col = x2[:, li:li + 1], x2[:, lf:lf + 1]
        itil_row, b_row = x2t[li:li + 1, :], x2t[lf:lf + 1, :]

        m_prev = m_ref[0, hh, 0:1, 0:1]
        dmat = jnp.where(causal, b_col - b_row + itil_row, -jnp.inf)
        inter = b_col + m_prev
        m_t = jnp.maximum(inter, jnp.max(dmat, axis=1, keepdims=True))
        w = jnp.exp(dmat - m_t)
        decay = jnp.exp(inter - m_t)

        qb, kb, vb = q.astype(BF16), k.astype(BF16), v.astype(BF16)
        c_old = c_ref[0, hh]
        n_old = n_ref[0, hh]
        s = _dot_nt(qb, kb) * w
        num = decay * _dot(qb, c_old.astype(BF16)) + _dot(s.astype(BF16), vb)
        den = decay * jnp.sum(q * n_old, axis=1, keepdims=True) + jnp.sum(s, axis=1, keepdims=True)
        inv = 1.0 / jnp.maximum(jnp.abs(den), jnp.exp(-m_t))

        m_new = m_t[L - 1:L, :]
        b_last = b_col[L - 1:L, :]
        wk = jnp.exp(b_last - b_col + itil_col - m_new)
        dec = jnp.exp(b_last + m_prev - m_new)
        kw = k * wk
        c_ref[0, hh] = dec * c_old + _dot_tn(kw.astype(BF16), vb)
        n_ref[0, hh] = dec * n_old + jnp.sum(kw, axis=0, keepdims=True)
        m_ref[0, hh] = jnp.broadcast_to(m_new, (1, LANES))

        mu = jnp.mean(num, axis=1, keepdims=True)
        hc = num - mu
        var = jnp.mean(hc * hc, axis=1, keepdims=True)
        hn = hc * (inv * lax.rsqrt(var * (inv * inv) + LN_EPS)) * ng_ref[:, cols]
        ya_ref[:, cols] = (hn * _sigmoid(ao_ref[:, cols]) * _silu(az_ref[:, cols])).astype(ya_ref.dtype)


def _mlstm(p_all, batch, seq, conv_w, conv_b, gate_bias, a_norm_g, p_s, conv_state, c_state, n_state, m_state):
    L = MLSTM_CHUNK
    hps = A_HEADS
    nc = seq // L
    t = batch * seq
    wd = hps * A_DK
    k_off = A_WIDTH // wd

    def rows(b, h, c):
        return b * nc + c

    def pcol(base):
        return pl.BlockSpec((L, wd), lambda b, h, c: (rows(b, h, c), base // wd + h))

    in_specs = [
        pcol(COL_Q), pcol(COL_K), pcol(COL_AV), pcol(COL_AO), pcol(COL_AZ),
        pl.BlockSpec((L, LANES), lambda b, h, c: (rows(b, h, c), COL_GATE // LANES)),
        pl.BlockSpec((CONV_W, wd), lambda b, h, c: (0, h)),
        pl.BlockSpec((CONV_W, wd), lambda b, h, c: (0, k_off + h)),
        pl.BlockSpec((1, wd), lambda b, h, c: (0, h)),
        pl.BlockSpec((1, wd), lambda b, h, c: (0, k_off + h)),
        pl.BlockSpec((1, LANES), lambda b, h, c: (0, 0)),
        pl.BlockSpec((1, wd), lambda b, h, c: (0, h)),
    ]
    out_specs = [
        pl.BlockSpec((L, wd), lambda b, h, c: (rows(b, h, c), h)),
        pl.BlockSpec((1, hps, A_DK, A_DV), lambda b, h, c: (b, h, 0, 0)),
        pl.BlockSpec((1, hps, 1, A_DK), lambda b, h, c: (b, h, 0, 0)),
        pl.BlockSpec((1, hps, 1, LANES), lambda b, h, c: (b, h, 0, 0)),
    ]
    out_shape = [
        jax.ShapeDtypeStruct((t, A_WIDTH), BF16),
        jax.ShapeDtypeStruct((batch, A_HEADS, A_DK, A_DV), F32),
        jax.ShapeDtypeStruct((batch, A_HEADS, 1, A_DK), F32),
        jax.ShapeDtypeStruct((batch, A_HEADS, 1, LANES), F32),
    ]

    nseq = p_s.shape[0]
    TB = DEC_TOKENS
    dstep = _decode_step(batch, nc, nseq)

    def dcol(base, width):
        return pl.BlockSpec((TB, width), lambda b, h, c: (dstep(b, h, c)[0], base // width + dstep(b, h, c)[1]))

    def dconv(j, base):
        return pl.BlockSpec((TB, A_DK), lambda b, h, c: (dstep(b, h, c)[0],
                                                        (j * 2 * A_WIDTH + base) // A_DK + dstep(b, h, c)[1]))

    def dhead(shape, off=0):
        return pl.BlockSpec(shape, lambda b, h, c: (0, off + dstep(b, h, c)[1]))

    def dstate(*dims):
        return pl.BlockSpec((TB, 1) + dims, lambda b, h, c: dstep(b, h, c) + (0,) * len(dims))

    in_specs += [
        dcol(COL_Q, A_DK), dcol(COL_K, A_DK), dcol(COL_AV, A_DV), dcol(COL_AO, A_DV), dcol(COL_AZ, A_DV),
        pl.BlockSpec((TB, LANES), lambda b, h, c: (dstep(b, h, c)[0], COL_GATE // LANES)),
        dconv(0, 0), dconv(1, 0), dconv(2, 0), dconv(0, A_WIDTH), dconv(1, A_WIDTH), dconv(2, A_WIDTH),
        dhead((CONV_W, A_DK)), dhead((CONV_W, A_DK), A_HEADS), dhead((1, A_DK)), dhead((1, A_DK), A_HEADS),
        pl.BlockSpec((1, LANES), lambda b, h, c: (0, 0)), dhead((1, A_DV)),
        dstate(A_DK, A_DV), dstate(1, A_DK),
        pl.BlockSpec((TB, A_HEADS), lambda b, h, c: (dstep(b, h, c)[0], 0)),
    ]
    out_specs += [
        pl.BlockSpec((TB, A_DV), lambda b, h, c: dstep(b, h, c)),
        dstate(A_DK, A_DV), dstate(1, A_DK),
        pl.BlockSpec((TB, LANES), lambda b, h, c: dstep(b, h, c)),
    ]
    out_shape += [
        jax.ShapeDtypeStruct((nseq, A_WIDTH), BF16),
        jax.ShapeDtypeStruct((nseq, A_HEADS, A_DK, A_DV), F32),
        jax.ShapeDtypeStruct((nseq, A_HEADS, 1, A_DK), F32),
        jax.ShapeDtypeStruct((nseq, A_HEADS * LANES), F32),
    ]
    return pl.pallas_call(
        functools.partial(_mlstm_kernel, n_chunks=nc),
        grid=(batch, A_HEADS // hps, nc),
        in_specs=in_specs,
        out_specs=out_specs,
        out_shape=out_shape,
        scratch_shapes=[pltpu.VMEM((SUBLANES, wd), F32),
                        pltpu.VMEM((SUBLANES, wd), F32)],
        compiler_params=pltpu.CompilerParams(
            dimension_semantics=("parallel", "parallel", "arbitrary"), vmem_limit_bytes=VMEM_LIMIT),
        name="mlstm",
    )(p_all, p_all, p_all, p_all, p_all, p_all, conv_w, conv_w, conv_b, conv_b, gate_bias, a_norm_g,
      p_s, p_s, p_s, p_s, p_s, p_s,
      conv_state, conv_state, conv_state, conv_state, conv_state, conv_state,
      conv_w, conv_w, conv_b, conv_b, gate_bias, a_norm_g, c_state, n_state, m_state)


def _gla_gate_log(bg_tile, wg, bgate):
    lr = bg_tile[:, GATE_LANE_LR:GATE_LANE_LR + GATE_RANK]
    z = _dot(lr.astype(BF16), wg.astype(BF16)) + bgate
    return _log_sigmoid(z) / GATE_TAU


def _gla_kernel(q_ref, k_ref, v_ref, bz_ref, g_ref, wg_ref, bgate_ref, ng_ref, *rest):
    dec_in, w_in, (yb_ref, s_out_ref), dec_out, w_out, (st_ref, ds_ref, sb_ref) = (
        rest[:9], rest[9:12], rest[12:14], rest[14:16], rest[16:19], rest[19:])
    L = GLA_STEP
    B = GLA_BLOCK
    nb = L // B
    c = pl.program_id(2)

    @pl.when(c == 0)
    def _():
        st_ref[...] = jnp.zeros_like(st_ref)

    _gla_decode_step(*dec_in, *dec_out)
    for src_ref, dst_ref in zip(w_in, w_out):
        dst_ref[...] = src_ref[...].astype(BF16)

    row = lax.broadcasted_iota(jnp.int32, (L, L), 0)
    col = lax.broadcasted_iota(jnp.int32, (L, L), 1)
    tri_b = (((row // B) == (col // B)) & (row >= col)).astype(BF16)
    t_in = lax.broadcasted_iota(jnp.int32, (L, 1), 0) % B
    s_lane = lax.broadcasted_iota(jnp.int32, (1, LANES), 1)

    for hh in range(B_HEADS):
        kcols = slice(hh * B_DK, (hh + 1) * B_DK)
        vcols = slice(hh * B_DV, (hh + 1) * B_DV)
        loga = _gla_gate_log(g_ref[...], wg_ref[:, kcols], bgate_ref[:, kcols])
        bc = _mask_dot(tri_b, loga)
        bl = jnp.broadcast_to(bc.reshape(nb, B, B_DK)[:, B - 1:B, :],
                              (nb, B, B_DK)).reshape(L, B_DK)
        k = k_ref[:, kcols]
        qs = q_ref[:, kcols] * (B_DK ** -0.5)
        qt = (qs * jnp.exp(bc)).astype(BF16)
        kt = (k * jnp.exp(bl - bc)).astype(BF16)
        eb = jnp.exp(bl)
        vb = v_ref[:, vcols].astype(BF16)

        bc3, q3, k3 = (x.reshape(nb, B, B_DK) for x in (bc * LOG2_E, qs, k))
        groups = B // SUBLANES
        bcg = [bc3[:, g * SUBLANES:(g + 1) * SUBLANES, :] for g in range(groups)]
        qg = [q3[:, g * SUBLANES:(g + 1) * SUBLANES, :] for g in range(groups)]
        ag = [jnp.zeros((nb, SUBLANES, LANES), F32) for _ in range(groups)]
        for s in range(B):
            bc_s, k_s = bc3[:, s:s + 1, :], k3[:, s:s + 1, :]
            for g in range(s // SUBLANES, groups):
                e = jnp.exp2(bcg[g] - bc_s)
                a_col = jnp.sum(qg[g] * k_s * e, axis=2, keepdims=True)
                ag[g] = jnp.where(s_lane == s, a_col, ag[g])
        a = jnp.concatenate(ag, axis=1).reshape(L, LANES)
        ab = jnp.where(t_in >= s_lane, a, 0.0)[:, 0:B].astype(BF16)

        for j in range(nb):
            ds_ref[hh, j] = _dot_tn(vb[j * B:(j + 1) * B, :], kt[j * B:(j + 1) * B, :])
        st = st_ref[hh]
        for j in range(nb):
            sb_ref[hh, j] = st.astype(BF16)
            st = st * eb[j * B:j * B + 1, :] + ds_ref[hh, j]
        st_ref[hh] = st
        o = jnp.concatenate(
            [_dot(ab[j * B:(j + 1) * B, :], vb[j * B:(j + 1) * B, :])
             + _dot_nt(qt[j * B:(j + 1) * B, :], sb_ref[hh, j]) for j in range(nb)], axis=0)

        on = o * lax.rsqrt(jnp.mean(o * o, axis=1, keepdims=True) + LN_EPS) * ng_ref[:, vcols]
        yb_ref[:, vcols] = (on * _silu(bz_ref[:, vcols])).astype(yb_ref.dtype)

    @pl.when(c == pl.num_programs(2) - 1)
    def _():
        for hh in range(B_HEADS):
            s_out_ref[0, hh] = st_ref[hh].T


def _gla(p_all, batch, seq, w_gate_up, b_gate, b_norm_g, p_s, s_state, cast_weights):
    L = GLA_STEP
    hps = B_HEADS
    nc = seq // L
    t = batch * seq
    kw, vw = hps * B_DK, hps * B_DV

    def rows(b, h, c):
        return b * nc + c

    in_specs = [
        pl.BlockSpec((L, kw), lambda b, h, c: (rows(b, h, c), COL_BQ // kw + h)),
        pl.BlockSpec((L, kw), lambda b, h, c: (rows(b, h, c), COL_BK // kw + h)),
        pl.BlockSpec((L, vw), lambda b, h, c: (rows(b, h, c), COL_BV // vw + h)),
        pl.BlockSpec((L, vw), lambda b, h, c: (rows(b, h, c), COL_BZ // vw + h)),
        pl.BlockSpec((L, LANES), lambda b, h, c: (rows(b, h, c), COL_GATE // LANES)),
        pl.BlockSpec((GATE_RANK, kw), lambda b, h, c: (0, h)),
        pl.BlockSpec((1, kw), lambda b, h, c: (0, h)),
        pl.BlockSpec((1, vw), lambda b, h, c: (0, h)),
    ]
    out_specs = [
        pl.BlockSpec((L, vw), lambda b, h, c: (rows(b, h, c), h)),
        pl.BlockSpec((1, hps, B_DK, B_DV), lambda b, h, c: (b, h, 0, 0)),
    ]
    out_shape = [
        jax.ShapeDtypeStruct((t, B_WIDTH), BF16),
        jax.ShapeDtypeStruct((batch, B_HEADS, B_DK, B_DV), F32),
    ]

    nseq = p_s.shape[0]
    TB = DEC_TOKENS
    dstep = _decode_step(batch, nc, nseq)

    def dcol(base, width):
        return pl.BlockSpec((TB, width), lambda b, h, c: (dstep(b, h, c)[0], base // width + dstep(b, h, c)[1]))

    def dhead(shape):
        return pl.BlockSpec(shape, lambda b, h, c: (0, dstep(b, h, c)[1]))

    dstate = pl.BlockSpec((TB, 1, B_DK, B_DV), lambda b, h, c: dstep(b, h, c) + (0, 0))
    in_specs += [
        dcol(COL_BQ, B_DK), dcol(COL_BK, B_DK), dcol(COL_BV, B_DV), dcol(COL_BZ, B_DV),
        pl.BlockSpec((TB, LANES), lambda b, h, c: (dstep(b, h, c)[0], COL_GATE // LANES)),
        dhead((GATE_RANK, B_DK)), dhead((1, B_DK)), dhead((1, B_DV)), dstate,
    ]
    out_specs += [pl.BlockSpec((TB, B_DV), lambda b, h, c: dstep(b, h, c)), dstate]
    out_shape += [
        jax.ShapeDtypeStruct((nseq, B_WIDTH), BF16),
        jax.ShapeDtypeStruct((nseq, B_HEADS, B_DK, B_DV), F32),
    ]

    n_steps = batch * nc
    for wmat in cast_weights:
        slab = pl.BlockSpec((wmat.shape[0] // n_steps, wmat.shape[1]), lambda b, h, c: (rows(b, h, c), 0))
        in_specs.append(slab)
        out_specs.append(slab)
        out_shape.append(jax.ShapeDtypeStruct(wmat.shape, BF16))
    return pl.pallas_call(
        _gla_kernel,
        grid=(batch, B_HEADS // hps, nc),
        in_specs=in_specs,
        out_specs=out_specs,
        out_shape=out_shape,
        scratch_shapes=[pltpu.VMEM((hps, B_DV, B_DK), F32),
                        pltpu.VMEM((hps, L // GLA_BLOCK, B_DV, B_DK), F32),
                        pltpu.VMEM((hps, L // GLA_BLOCK, B_DV, B_DK), BF16)],
        compiler_params=pltpu.CompilerParams(
            dimension_semantics=("parallel", "parallel", "arbitrary"), vmem_limit_bytes=VMEM_LIMIT),
        name="gla",
    )(p_all, p_all, p_all, p_all, p_all, w_gate_up, b_gate, b_norm_g,
      p_s, p_s, p_s, p_s, p_s, w_gate_up, b_gate, b_norm_g, s_state, *cast_weights)


DEC_TOKENS = 16


def _decode_step(batch, nc, nseq):
    assert batch * nc == (nseq // DEC_TOKENS) * A_HEADS and A_HEADS == B_HEADS

    def step(b, h, c):
        s = b * nc + c
        return s // A_HEADS, s % A_HEADS
    return step


def _mlstm_decode_step(h_idx, qp_ref, kp_ref, av_ref, ao_ref, az_ref, g_ref,
                       sq0_ref, sq1_ref, sq2_ref, sk0_ref, sk1_ref, sk2_ref,
                       cwq_ref, cwk_ref, cbq_ref, cbk_ref, gb_ref, ang_ref, c_ref, n_ref, m_ref,
                       ya_ref, c_out_ref, n_out_ref, m_out_ref):
    TB = DEC_TOKENS

    def conv_silu(s0, s1, s2, x, w_ref, b_ref):
        acc = b_ref[...] + s0[...] * w_ref[0:1, :]
        acc = acc + s1[...] * w_ref[1:2, :]
        acc = acc + s2[...] * w_ref[2:3, :]
        acc = acc + x[...] * w_ref[3:4, :]
        return _silu(acc)

    q = conv_silu(sq0_ref, sq1_ref, sq2_ref, qp_ref, cwq_ref, cbq_ref)
    k = conv_silu(sk0_ref, sk1_ref, sk2_ref, kp_ref, cwk_ref, cbk_ref) * (A_DK ** -0.5)
    v = av_ref[...]
    g = g_ref[...]
    gbias = gb_ref[...]
    gate_lane = lax.broadcasted_iota(jnp.int32, (TB, LANES), 1)
    gsum = g + gbias

    def gate(lane0):
        return jnp.sum(jnp.where(gate_lane == lane0 + h_idx, gsum, 0.0), axis=1, keepdims=True)

    itil = gate(GATE_LANE_I)
    logf = _log_sigmoid(gate(GATE_LANE_F))
    lane_h = lax.broadcasted_iota(jnp.int32, (TB, A_HEADS), 1)
    m_prev = jnp.sum(jnp.where(lane_h == h_idx, m_ref[...], 0.0), axis=1, keepdims=True)
    inter = logf + m_prev
    m_t = jnp.maximum(inter, itil)
    w = jnp.exp(itil - m_t)
    decay = jnp.exp(inter - m_t)
    n_old = n_ref[:, 0, 0, :]
    s = jnp.sum(q * k, axis=1, keepdims=True) * w
    den = decay * jnp.sum(q * n_old, axis=1, keepdims=True) + s
    scale = 1.0 / jnp.maximum(jnp.abs(den), jnp.exp(-m_t))
    kw = k * w
    n_out_ref[:, 0, 0, :] = decay * n_old + kw
    m_out_ref[...] = jnp.broadcast_to(m_t, (TB, LANES))

    rows = lax.broadcasted_iota(jnp.int32, (TB, 1), 0)
    qb, kwb, vb = q.astype(BF16), kw.astype(BF16), v.astype(BF16)
    h_rows = []
    for t in range(TB):
        c_old = c_ref[t, 0]
        qc = _dot(qb, c_old.astype(BF16))[t:t + 1, :]
        h_rows.append((decay[t:t + 1, :] * qc + s[t:t + 1, :] * v[t:t + 1, :]) * scale[t:t + 1, :])
        outer = _dot_tn(jnp.where(rows == t, kwb, jnp.zeros_like(kwb)), vb)
        c_out_ref[t, 0] = decay[t:t + 1, :] * c_old + outer
    h = jnp.concatenate(h_rows, axis=0)
    mu = jnp.mean(h, axis=1, keepdims=True)
    hc = h - mu
    var = jnp.mean(hc * hc, axis=1, keepdims=True)
    hn = hc * lax.rsqrt(var + LN_EPS) * ang_ref[...]
    ya_ref[...] = (hn * _sigmoid(ao_ref[...]) * _silu(az_ref[...])).astype(ya_ref.dtype)


def _gla_decode_step(bq_ref, bk_ref, bv_ref, bz_ref, g_ref, wg_ref, bgate_ref, bng_ref, s_ref,
                     yb_ref, s_out_ref):
    TB = DEC_TOKENS
    gq = bq_ref[...] * (B_DK ** -0.5)
    gk = bk_ref[...]
    gv = bv_ref[...]
    loga = _gla_gate_log(g_ref[...], wg_ref[...], bgate_ref[...])
    eb = jnp.exp(loga)
    a = jnp.sum(gq * gk, axis=1, keepdims=True)
    ebt = jnp.concatenate([eb, jnp.zeros((LANES - TB, B_DK), F32)], axis=0).T
    rows = lax.broadcasted_iota(jnp.int32, (TB, 1), 0)
    qeb, gkb, gvb = (gq * eb).astype(BF16), gk.astype(BF16), gv.astype(BF16)
    o_rows = []
    for t in range(TB):
        s_old = s_ref[t, 0]
        o_rows.append(_dot(qeb, s_old.astype(BF16))[t:t + 1, :] + a[t:t + 1, :] * gv[t:t + 1, :])
        outer = _dot_tn(jnp.where(rows == t, gkb, jnp.zeros_like(gkb)), gvb)
        s_out_ref[t, 0] = ebt[:, t:t + 1] * s_old + outer
    o = jnp.concatenate(o_rows, axis=0)
    on = o * lax.rsqrt(jnp.mean(o * o, axis=1, keepdims=True) + LN_EPS) * bng_ref[...]
    yb_ref[...] = (on * _silu(bz_ref[...])).astype(yb_ref.dtype)


def _out_kernel(ya_ref, yb_ref, ga_ref, gb_ref, x_ref, yas_ref, ybs_ref, gas_ref, gbs_ref, xs_ref,
                wpa_ref, wpb_ref, wo_ref, lng_ref, lnb_ref, o_ref, os_ref):
    def rows(ya, yb, ga, gb, x):
        pa = _dot(ya[...], wpa_ref[...])
        pb = _dot(yb[...], wpb_ref[...])
        merged = _sigmoid(ga[...]) * pa + _sigmoid(gb[...]) * pb
        r = ALPHA * x[...] + _dot(merged.astype(BF16), wo_ref[...])
        mu = jnp.mean(r, axis=1, keepdims=True)
        rc = r - mu
        var = jnp.mean(rc * rc, axis=1, keepdims=True)
        return rc * lax.rsqrt(var + LN_EPS) * lng_ref[...] + lnb_ref[...]

    last = pl.num_programs(0) - 1

    @pl.when(pl.program_id(0) < last)
    def _():
        o_ref[...] = rows(ya_ref, yb_ref, ga_ref, gb_ref, x_ref)

    @pl.when(pl.program_id(0) == last)
    def _():
        os_ref[...] = rows(yas_ref, ybs_ref, gas_ref, gbs_ref, xs_ref)


def _out(ya, yb, p_all, x, ya_s, yb_s, p_s, xs, w_pa, w_pb, w_out, ln_g, ln_b, tm):
    t, ts = x.shape[0], xs.shape[0]
    n_tiles = t // tm
    const = lambda i: (0, 0)
    single = pl.Buffered(1)

    def tile(i):
        return jnp.minimum(i, n_tiles - 1)

    in_specs = [
        pl.BlockSpec((tm, A_WIDTH), lambda i: (tile(i), 0)),
        pl.BlockSpec((tm, B_WIDTH), lambda i: (tile(i), 0)),
        pl.BlockSpec((tm, D_MODEL), lambda i: (tile(i), COL_GA // D_MODEL)),
        pl.BlockSpec((tm, D_MODEL), lambda i: (tile(i), COL_GB // D_MODEL)),
        pl.BlockSpec((tm, D_MODEL), lambda i: (tile(i), 0)),
        pl.BlockSpec((ts, A_WIDTH), const),
        pl.BlockSpec((ts, B_WIDTH), const),
        pl.BlockSpec((ts, D_MODEL), lambda i: (0, COL_GA // D_MODEL)),
        pl.BlockSpec((ts, D_MODEL), lambda i: (0, COL_GB // D_MODEL)),
        pl.BlockSpec((ts, D_MODEL), const),
        pl.BlockSpec((A_WIDTH, D_MODEL), const, pipeline_mode=single),
        pl.BlockSpec((B_WIDTH, D_MODEL), const, pipeline_mode=single),
        pl.BlockSpec((D_MODEL, D_MODEL), const, pipeline_mode=single),
        pl.BlockSpec((1, D_MODEL), const),
        pl.BlockSpec((1, D_MODEL), const),
    ]
    return pl.pallas_call(
        _out_kernel,
        grid=(n_tiles + 1,),
        in_specs=in_specs,
        out_specs=[pl.BlockSpec((tm, D_MODEL), lambda i: (tile(i), 0)),
                   pl.BlockSpec((ts, D_MODEL), const)],
        out_shape=[jax.ShapeDtypeStruct((t, D_MODEL), F32), jax.ShapeDtypeStruct((ts, D_MODEL), F32)],
        compiler_params=pltpu.CompilerParams(
            dimension_semantics=("arbitrary",), vmem_limit_bytes=VMEM_LIMIT),
        name="outproj",
    )(ya, yb, p_all, p_all, x, ya_s, yb_s, p_s, p_s, xs, w_pa, w_pb, w_out, ln_g, ln_b)


RELAYOUT_TN = 1024


def _relayout_kernel(src_ref, gif_ref, gb0_ref, gb1_ref, xs_ref, o_ref, ps_ref):
    j = pl.program_id(0)
    tn = RELAYOUT_TN
    n_main = COL_GATE // tn

    @pl.when(j < n_main)
    def _():
        o_ref[...] = src_ref[...].astype(BF16)

    @pl.when(j == n_main)
    def _():
        pad = jnp.zeros((tn - 2 * A_HEADS - GATE_RANK, src_ref.shape[1]), F32)
        o_ref[...] = jnp.concatenate([gif_ref[...], gb0_ref[...], gb1_ref[...], pad], axis=0).astype(BF16)

    ps_ref[...] = _dot_nt(xs_ref[...].astype(BF16), o_ref[...])


def _relayout_w_in(w_in_t, xs):
    n_src, k = w_in_t.shape
    ts = xs.shape[0]
    tn = RELAYOUT_TN

    def src_row(j):
        shift = 0
        for first, last, src in RELAYOUT_RUNS:
            assert first % tn == 0 and last % tn == 0 and (src - first) % SUBLANES == 0
            shift = jnp.where((j >= first // tn) & (j < last // tn), (src - first) // SUBLANES, shift)
        group = jnp.minimum(j * (tn // SUBLANES) + shift, (n_src - tn) // SUBLANES)
        return group * SUBLANES

    return pl.pallas_call(
        _relayout_kernel,
        grid=(pl.cdiv(N_PROJ, tn),),
        in_specs=[pl.BlockSpec((pl.Element(tn), pl.Element(k)), lambda j: (src_row(j), 0)),
                  pl.BlockSpec((SUBLANES, k), lambda j: (SRC_I // SUBLANES, 0)),
                  pl.BlockSpec((SUBLANES, k), lambda j: (SRC_BG // SUBLANES, 0)),
                  pl.BlockSpec((SUBLANES, k), lambda j: (SRC_BG // SUBLANES + 1, 0)),
                  pl.BlockSpec((ts, k), lambda j: (0, 0))],
        out_specs=[pl.BlockSpec((tn, k), lambda j: (j, 0)),
                   pl.BlockSpec((ts, tn), lambda j: (0, j))],
        out_shape=[jax.ShapeDtypeStruct((N_PROJ, k), BF16), jax.ShapeDtypeStruct((ts, N_PROJ), F32)],
        compiler_params=pltpu.CompilerParams(
            dimension_semantics=("parallel",), vmem_limit_bytes=VMEM_LIMIT),
        name="relayout",
    )(w_in_t, w_in_t, w_in_t, w_in_t, xs)


def kernel(x_prompt, x_sample, state_mlstm_C, state_mlstm_n, state_mlstm_m, state_conv, state_gla_S,
           w_in, conv_w, conv_b, b_i, b_f, a_norm_g, w_gate_up, b_gate, b_norm_g, w_pa, w_pb, w_out,
           ln_g, ln_b):
    batch, seq, _ = x_prompt.shape
    nseq = x_sample.shape[0]
    assert w_in.shape[0] == 1, "single-layer step"
    d = 0

    def layer(a):
        return a.reshape(a.shape[1:])

    xs = x_sample.reshape(nseq, D_MODEL)
    wp, ps_all = _relayout_w_in(layer(w_in).T, xs)
    cw = conv_w[d]
    cb = conv_b[d][None, :]
    gate_bias = jnp.concatenate([b_i[d], b_f[d], jnp.zeros((LANES - 2 * A_HEADS,), F32)])[None, :]
    ang = a_norm_g[d][None, :]
    bng = b_norm_g[d][None, :]
    wg = w_gate_up[d]
    bgate = b_gate[d][None, :]
    lng, lnb = ln_g[d][None, :], ln_b[d][None, :]

    xp = x_prompt.reshape(batch * seq, D_MODEL)
    p_all = _proj(xp, wp, tm=1024, tn=PROJ_TN)

    conv_state = state_conv.reshape(nseq, (CONV_W - 1) * 2 * A_WIDTH)
    ya, p_c, p_n, p_m, ya_s, s_c, s_n, s_m = _mlstm(
        p_all, batch, seq, cw, cb, gate_bias, ang, ps_all, conv_state,
        layer(state_mlstm_C), state_mlstm_n.reshape(nseq, A_HEADS, 1, A_DK), layer(state_mlstm_m))
    yb, p_s, yb_s, s_s, wpa, wpb, wo = _gla(p_all, batch, seq, wg, bgate, bng, ps_all, layer(state_gla_S),
                                            (layer(w_pa), layer(w_pb), layer(w_out)))

    y_prompt, y_sample = _out(ya, yb, p_all, xp, ya_s, yb_s, ps_all, xs, wpa, wpb, wo, lng, lnb, tm=256)
    y_prompt = y_prompt.reshape(batch, seq, D_MODEL)
    y_sample = y_sample.reshape(nseq, 1, D_MODEL)
    p_conv = p_all.reshape(batch, seq, N_PROJ)[:, seq - (CONV_W - 1):, :2 * A_WIDTH]
    s_conv = jnp.concatenate([layer(state_conv)[:, 1:, :], ps_all[:, None, :2 * A_WIDTH]], axis=1)

    def stacked(a):
        return a.reshape((1,) + a.shape)

    return (y_prompt, y_sample,
            stacked(p_c), p_n.reshape(1, batch, A_HEADS, A_DK), stacked(p_m[:, :, 0, 0]),
            stacked(p_conv), stacked(p_s),
            stacked(s_c), s_n.reshape(1, nseq, A_HEADS, A_DK),
            stacked(s_m.reshape(nseq, A_HEADS, LANES)[:, :, 0]), stacked(s_conv), stacked(s_s))
```

```python
import functools

import jax
import jax.numpy as jnp
from jax import lax
from jax.experimental import pallas as pl
from jax.experimental.pallas import tpu as pltpu

F32 = jnp.float32
BF16 = jnp.bfloat16

D_MODEL = 2048
A_HEADS = 4
A_WIDTH = 1024
A_DK = 256
A_DV = 256
CONV_W = 4
B_HEADS = 4
B_WIDTH = 1024
B_KWIDTH = 512
B_DK = 128
B_DV = 256
GATE_RANK = 16
GATE_TAU = 16.0
ALPHA = 2.0 ** 0.25
LN_EPS = 1e-5
LOG2_E = 1.4426950408889634
LN_2 = 0.6931471805599453

LANES = 128
SUBLANES = 8
VMEM_LIMIT = 48 * 1024 * 1024

COL_Q = 0
COL_K = A_WIDTH
COL_BQ = 2 * A_WIDTH
COL_BK = COL_BQ + B_KWIDTH
COL_AV = COL_BK + B_KWIDTH
COL_AO = COL_AV + A_WIDTH
COL_AZ = COL_AO + A_WIDTH
COL_BV = COL_AZ + A_WIDTH
COL_BZ = COL_BV + B_WIDTH
COL_GA = COL_BZ + B_WIDTH
COL_GB = COL_GA + D_MODEL
COL_GATE = COL_GB + D_MODEL
GATE_LANE_I = 0
GATE_LANE_F = A_HEADS
GATE_LANE_LR = 2 * A_HEADS
MXU_COLS = 256
N_PROJ = COL_GATE + MXU_COLS
PROJ_TN = N_PROJ // 7
assert PROJ_TN * 7 == N_PROJ and PROJ_TN % MXU_COLS == 0
PROJ_VMEM_LIMIT = 52 * 1024 * 1024

SRC_QK = 0
SRC_AV = 2 * A_WIDTH
SRC_I = SRC_AV + A_WIDTH
SRC_AO = SRC_I + 2 * A_HEADS
SRC_BQ = SRC_AO + 2 * A_WIDTH
SRC_BV = SRC_BQ + 2 * B_KWIDTH
SRC_BG = SRC_BV + B_WIDTH
SRC_BZ = SRC_BG + GATE_RANK
RELAYOUT_RUNS = ((COL_Q, COL_BQ, SRC_QK), (COL_BQ, COL_AV, SRC_BQ), (COL_AV, COL_AO, SRC_AV),
                 (COL_AO, COL_BV, SRC_AO), (COL_BV, COL_BZ, SRC_BV), (COL_BZ, COL_GATE, SRC_BZ))

MLSTM_CHUNK = 256
GLA_STEP = 256
GLA_CHUNK = 64
GLA_BLOCK = 16


def _dot(a, b):
    return jnp.dot(a, b, preferred_element_type=F32)


def _dot_nt(a, b):
    return lax.dot_general(a, b, (((1,), (1,)), ((), ())), preferred_element_type=F32)


def _dot_tn(a, b):
    return lax.dot_general(a, b, (((0,), (0,)), ((), ())), preferred_element_type=F32)


def _mask_dot(mask_bf16, x):
    hi = x.astype(BF16)
    r1 = x - hi.astype(F32)
    mid = r1.astype(BF16)
    lo = (r1 - mid.astype(F32)).astype(BF16)
    return _dot(mask_bf16, hi) + _dot(mask_bf16, mid) + _dot(mask_bf16, lo)


def _log_sigmoid(z):
    return jnp.minimum(z, 0.0) - LN_2 * jnp.log2(1.0 + jnp.exp2(jnp.abs(z) * (-LOG2_E)))


def _sigmoid(z):
    return 1.0 / (1.0 + jnp.exp2(z * (-LOG2_E)))


def _silu(z):
    return z * _sigmoid(z)


def _proj_kernel(x_ref, w_ref, o_ref, xb_ref):
    @pl.when(pl.program_id(1) == 0)
    def _():
        xb_ref[...] = x_ref[...].astype(BF16)

    o_ref[...] = _dot_nt(xb_ref[...], w_ref[...])


def _proj(x, w_t, tm, tn):
    t, k = x.shape
    n = w_t.shape[0]
    return pl.pallas_call(
        _proj_kernel,
        grid=(t // tm, n // tn),
        in_specs=[pl.BlockSpec((tm, k), lambda i, j: (i, 0)),
                  pl.BlockSpec((tn, k), lambda i, j: (j, 0))],
        out_specs=pl.BlockSpec((tm, tn), lambda i, j: (i, j)),
        out_shape=jax.ShapeDtypeStruct((t, n), F32),
        scratch_shapes=[pltpu.VMEM((tm, k), BF16)],
        compiler_params=pltpu.CompilerParams(
            dimension_semantics=("parallel", "arbitrary"), vmem_limit_bytes=PROJ_VMEM_LIMIT),
        name="proj",
    )(x, w_t)


def _mlstm_kernel(qp_ref, kp_ref, v_ref, ao_ref, az_ref, g_ref, cwq_ref, cwk_ref, cbq_ref, cbk_ref,
                  gb_ref, ng_ref, *rest, n_chunks):
    dec_in, (ya_ref, c_ref, n_ref, m_ref), dec_out, (tailq_ref, tailk_ref) = (
        rest[:21], rest[21:25], rest[25:29], rest[29:])
    L = MLSTM_CHUNK
    c = pl.program_id(2)

    @pl.when(c == 0)
    def _():
        c_ref[...] = jnp.zeros_like(c_ref)
        n_ref[...] = jnp.zeros_like(n_ref)
        m_ref[...] = jnp.zeros_like(m_ref)
        tailq_ref[...] = jnp.zeros_like(tailq_ref)
        tailk_ref[...] = jnp.zeros_like(tailk_ref)

    _mlstm_decode_step((pl.program_id(0) * n_chunks + c) % A_HEADS, *dec_in, *dec_out)

    sub = lax.broadcasted_iota(jnp.int32, (1, SUBLANES, 1), 1)

    def conv_silu(x_ref, tail_ref, w_ref, b_ref):
        x = x_ref[...]
        width = x.shape[1]
        x3 = jnp.concatenate([tail_ref[...], x], axis=0).reshape(L // SUBLANES + 1, SUBLANES, width)
        acc = b_ref[...] + x * w_ref[CONV_W - 1:CONV_W, :]
        for j in range(1, CONV_W):
            rot = pltpu.roll(x3, j, axis=1)
            xs = jnp.where(sub < j, rot[:-1], rot[1:]).reshape(L, width)
            acc = acc + xs * w_ref[CONV_W - 1 - j:CONV_W - j, :]
        tail_ref[...] = x[L - SUBLANES:, :]
        return _silu(acc)

    q_all = conv_silu(qp_ref, tailq_ref, cwq_ref, cbq_ref)
    k_all = conv_silu(kp_ref, tailk_ref, cwk_ref, cbk_ref) * (A_DK ** -0.5)

    lane = lax.broadcasted_iota(jnp.int32, (L, LANES), 1)
    row = lax.broadcasted_iota(jnp.int32, (L, L), 0)
    col = lax.broadcasted_iota(jnp.int32, (L, L), 1)
    causal = row >= col
    causal_b = causal.astype(BF16)

    g = g_ref[...] + gb_ref[...]
    is_f = (lane >= GATE_LANE_F) & (lane < GATE_LANE_F + A_HEADS)
    g2 = jnp.where(is_f, _log_sigmoid(g), g)
    x2 = jnp.where(is_f, _mask_dot(causal_b, g2), g2)
    x2t = x2.T

    for hh in range(A_HEADS):
        cols = slice(hh * A_DK, (hh + 1) * A_DK)
        q, k, v = q_all[:, cols], k_all[:, cols], v_ref[:, cols]
        li, lf = GATE_LANE_I + hh, GATE_LANE_F + hh
        itil_col, b_col = x2[:, li:li + 1], x2[:, lf:lf + 1]
        itil_row, b_row = x2t[li:li + 1, :], x2t[lf:lf + 1, :]

        m_prev = m_ref[0, hh, 0:1, 0:1]
        dmat = jnp.where(causal, b_col - b_row + itil_row, -jnp.inf)
        inter = b_col + m_prev
        m_t = jnp.maximum(inter, jnp.max(dmat, axis=1, keepdims=True))
        w = jnp.exp(dmat - m_t)
        decay = jnp.exp(inter - m_t)

        qb, kb, vb = q.astype(BF16), k.astype(BF16), v.astype(BF16)
        c_old = c_ref[0, hh]
        n_old = n_ref[0, hh]
        s = _dot_nt(qb, kb) * w
        num = decay * _dot(qb, c_old.astype(BF16)) + _dot(s.astype(BF16), vb)
        den = decay * jnp.sum(q * n_old, axis=1, keepdims=True) + jnp.sum(s, axis=1, keepdims=True)
        inv = 1.0 / jnp.maximum(jnp.abs(den), jnp.exp(-m_t))

        m_new = m_t[L - 1:L, :]
        b_last = b_col[L - 1:L, :]
        wk = jnp.exp(b_last - b_col + itil_col - m_new)
        dec = jnp.exp(b_last + m_prev - m_new)
        kw = k * wk
        c_ref[0, hh] = dec * c_old + _dot_tn(kw.astype(BF16), vb)
        n_ref[0, hh] = dec * n_old + jnp.sum(kw, axis=0, keepdims=True)
        m_ref[0, hh] = jnp.broadcast_to(m_new, (1, LANES))

        mu = jnp.mean(num, axis=1, keepdims=True)
        hc = num - mu
        var = jnp.mean(hc * hc, axis=1, keepdims=True)
        hn = hc * (inv * lax.rsqrt(var * (inv * inv) + LN_EPS)) * ng_ref[:, cols]
        ya_ref[:, cols] = (hn * _sigmoid(ao_ref[:, cols]) * _silu(az_ref[:, cols])).astype(ya_ref.dtype)


def _mlstm(p_all, batch, seq, conv_w, conv_b, gate_bias, a_norm_g, p_s, conv_state, c_state, n_state, m_state):
    L = MLSTM_CHUNK
    hps = A_HEADS
    nc = seq // L
    t = batch * seq
    wd = hps * A_DK
    k_off = A_WIDTH // wd

    def rows(b, h, c):
        return b * nc + c

    def pcol(base):
        return pl.BlockSpec((L, wd), lambda b, h, c: (rows(b, h, c), base // wd + h))

    in_specs = [
        pcol(COL_Q), pcol(COL_K), pcol(COL_AV), pcol(COL_AO), pcol(COL_AZ),
        pl.BlockSpec((L, LANES), lambda b, h, c: (rows(b, h, c), COL_GATE // LANES)),
        pl.BlockSpec((CONV_W, wd), lambda b, h, c: (0, h)),
        pl.BlockSpec((CONV_W, wd), lambda b, h, c: (0, k_off + h)),
        pl.BlockSpec((1, wd), lambda b, h, c: (0, h)),
        pl.BlockSpec((1, wd), lambda b, h, c: (0, k_off + h)),
        pl.BlockSpec((1, LANES), lambda b, h, c: (0, 0)),
        pl.BlockSpec((1, wd), lambda b, h, c: (0, h)),
    ]
    out_specs = [
        pl.BlockSpec((L, wd), lambda b, h, c: (rows(b, h, c), h)),
        pl.BlockSpec((1, hps, A_DK, A_DV), lambda b, h, c: (b, h, 0, 0)),
        pl.BlockSpec((1, hps, 1, A_DK), lambda b, h, c: (b, h, 0, 0)),
        pl.BlockSpec((1, hps, 1, LANES), lambda b, h, c: (b, h, 0, 0)),
    ]
    out_shape = [
        jax.ShapeDtypeStruct((t, A_WIDTH), BF16),
        jax.ShapeDtypeStruct((batch, A_HEADS, A_DK, A_DV), F32),
        jax.ShapeDtypeStruct((batch, A_HEADS, 1, A_DK), F32),
        jax.ShapeDtypeStruct((batch, A_HEADS, 1, LANES), F32),
    ]

    nseq = p_s.shape[0]
    TB = DEC_TOKENS
    dstep = _decode_step(batch, nc, nseq)

    def dcol(base, width):
        return pl.BlockSpec((TB, width), lambda b, h, c: (dstep(b, h, c)[0], base // width + dstep(b, h, c)[1]))

    def dconv(j, base):
        return pl.BlockSpec((TB, A_DK), lambda b, h, c: (dstep(b, h, c)[0],
                                                        (j * 2 * A_WIDTH + base) // A_DK + dstep(b, h, c)[1]))

    def dhead(shape, off=0):
        return pl.BlockSpec(shape, lambda b, h, c: (0, off + dstep(b, h, c)[1]))

    def dstate(*dims):
        return pl.BlockSpec((TB, 1) + dims, lambda b, h, c: dstep(b, h, c) + (0,) * len(dims))

    in_specs += [
        dcol(COL_Q, A_DK), dcol(COL_K, A_DK), dcol(COL_AV, A_DV), dcol(COL_AO, A_DV), dcol(COL_AZ, A_DV),
        pl.BlockSpec((TB, LANES), lambda b, h, c: (dstep(b, h, c)[0], COL_GATE // LANES)),
        dconv(0, 0), dconv(1, 0), dconv(2, 0), dconv(0, A_WIDTH), dconv(1, A_WIDTH), dconv(2, A_WIDTH),
        dhead((CONV_W, A_DK)), dhead((CONV_W, A_DK), A_HEADS), dhead((1, A_DK)), dhead((1, A_DK), A_HEADS),
        pl.BlockSpec((1, LANES), lambda b, h, c: (0, 0)), dhead((1, A_DV)),
        dstate(A_DK, A_DV), dstate(1, A_DK),
        pl.BlockSpec((TB, A_HEADS), lambda b, h, c: (dstep(b, h, c)[0], 0)),
    ]
    out_specs += [
        pl.BlockSpec((TB, A_DV), lambda b, h, c: dstep(b, h, c)),
        dstate(A_DK, A_DV), dstate(1, A_DK),
        pl.BlockSpec((TB, LANES), lambda b, h, c: dstep(b, h, c)),
    ]
    out_shape += [
        jax.ShapeDtypeStruct((nseq, A_WIDTH), BF16),
        jax.ShapeDtypeStruct((nseq, A_HEADS, A_DK, A_DV), F32),
        jax.ShapeDtypeStruct((nseq, A_HEADS, 1, A_DK), F32),
        jax.ShapeDtypeStruct((nseq, A_HEADS * LANES), F32),
    ]
    return pl.pallas_call(
        functools.partial(_mlstm_kernel, n_chunks=nc),
        grid=(batch, A_HEADS // hps, nc),
        in_specs=in_specs,
        out_specs=out_specs,
        out_shape=out_shape,
        scratch_shapes=[pltpu.VMEM((SUBLANES, wd), F32),
                        pltpu.VMEM((SUBLANES, wd), F32)],
        compiler_params=pltpu.CompilerParams(
            dimension_semantics=("parallel", "parallel", "arbitrary"), vmem_limit_bytes=VMEM_LIMIT),
        name="mlstm",
    )(p_all, p_all, p_all, p_all, p_all, p_all, conv_w, conv_w, conv_b, conv_b, gate_bias, a_norm_g,
      p_s, p_s, p_s, p_s, p_s, p_s,
      conv_state, conv_state, conv_state, conv_state, conv_state, conv_state,
      conv_w, conv_w, conv_b, conv_b, gate_bias, a_norm_g, c_state, n_state, m_state)


def _gla_gate_log(bg_tile, wg, bgate):
    lr = bg_tile[:, GATE_LANE_LR:GATE_LANE_LR + GATE_RANK]
    z = _dot(lr.astype(BF16), wg.astype(BF16)) + bgate
    return _log_sigmoid(z) / GATE_TAU


def _gla_kernel(q_ref, k_ref, v_ref, bz_ref, g_ref, wg_ref, bgate_ref, ng_ref, *rest):
    dec_in, w_in, (yb_ref, s_out_ref), dec_out, w_out, (st_ref, ds_ref, sb_ref) = (
        rest[:9], rest[9:12], rest[12:14], rest[14:16], rest[16:19], rest[19:])
    L = GLA_STEP
    B = GLA_BLOCK
    nb = L // B
    c = pl.program_id(2)

    @pl.when(c == 0)
    def _():
        st_ref[...] = jnp.zeros_like(st_ref)

    _gla_decode_step(*dec_in, *dec_out)
    for src_ref, dst_ref in zip(w_in, w_out):
        dst_ref[...] = src_ref[...].astype(BF16)

    CH = GLA_CHUNK
    bpc, nch = CH // B, L // CH
    row = lax.broadcasted_iota(jnp.int32, (L, L), 0)
    col = lax.broadcasted_iota(jnp.int32, (L, L), 1)
    tri_c = (((row // CH) == (col // CH)) & (row >= col)).astype(BF16)
    blk3 = lax.broadcasted_iota(jnp.int32, (nb, 1, LANES), 0)
    lane_rel = lax.broadcasted_iota(jnp.int32, (nb, 1, LANES), 2) - (blk3 % bpc) * B
    t3 = lax.broadcasted_iota(jnp.int32, (1, B, 1), 1)

    for hh in range(B_HEADS):
        kcols = slice(hh * B_DK, (hh + 1) * B_DK)
        vcols = slice(hh * B_DV, (hh + 1) * B_DV)
        loga = _gla_gate_log(g_ref[...], wg_ref[:, kcols], bgate_ref[:, kcols])
        bc = _mask_dot(tri_c, loga) * LOG2_E
        bl = jnp.broadcast_to(bc.reshape(nch, CH, B_DK)[:, CH - 1:CH, :],
                              (nch, CH, B_DK)).reshape(L, B_DK)
        k = k_ref[:, kcols]
        qs = q_ref[:, kcols] * (B_DK ** -0.5)
        qt = (qs * jnp.exp2(bc)).astype(BF16)
        kt = (k * jnp.exp2(bl - bc)).astype(BF16)
        eb = jnp.exp2(bl)
        vb = v_ref[:, vcols].astype(BF16)

        bc3, q3, k3 = (x.reshape(nb, B, B_DK) for x in (bc, qs, k))
        groups = B // SUBLANES
        bcg = [bc3[:, g * SUBLANES:(g + 1) * SUBLANES, :] for g in range(groups)]
        qg = [q3[:, g * SUBLANES:(g + 1) * SUBLANES, :] for g in range(groups)]
        ag = [jnp.zeros((nb, SUBLANES, LANES), F32) for _ in range(groups)]
        for s in range(B):
            bc_s, k_s = bc3[:, s:s + 1, :], k3[:, s:s + 1, :]
            for g in range(s // SUBLANES, groups):
                e = jnp.exp2(bcg[g] - bc_s)
                a_col = jnp.sum(qg[g] * k_s * e, axis=2, keepdims=True)
                ag[g] = jnp.where(lane_rel == s, a_col, ag[g])
        a3 = jnp.concatenate(ag, axis=1)
        a3 = jnp.where((lane_rel >= 0) & (lane_rel <= t3), a3, 0.0)
        a_chunks = []
        for ci in range(nch):
            rows_c = [a3[ci * bpc]]
            for i in range(1, bpc):
                blk = ci * bpc + i
                start = bc3[blk - 1, B - 1:B, :]
                qd = (q3[blk] * jnp.exp2(bc3[blk] - start)).astype(BF16)
                ks = k3[ci * bpc:blk].reshape(i * B, B_DK)
                kd = ks * jnp.exp2(start - bc3[ci * bpc:blk].reshape(i * B, B_DK))
                kd = jnp.concatenate([kd, jnp.zeros((LANES - i * B, B_DK), F32)], axis=0).astype(BF16)
                rows_c.append(a3[blk] + _dot_nt(qd, kd))
            a_chunks.append(jnp.concatenate(rows_c, axis=0)[:, 0:CH].astype(BF16))

        for ci in range(nch):
            ds_ref[hh, ci] = _dot_tn(vb[ci * CH:(ci + 1) * CH, :], kt[ci * CH:(ci + 1) * CH, :])
        st = st_ref[hh]
        for ci in range(nch):
            sb_ref[hh, ci] = st.astype(BF16)
            st = st * eb[ci * CH:ci * CH + 1, :] + ds_ref[hh, ci]
        st_ref[hh] = st
        o = jnp.concatenate(
            [_dot(a_chunks[ci], vb[ci * CH:(ci + 1) * CH, :])
             + _dot_nt(qt[ci * CH:(ci + 1) * CH, :], sb_ref[hh, ci]) for ci in range(nch)], axis=0)

        on = o * lax.rsqrt(jnp.mean(o * o, axis=1, keepdims=True) + LN_EPS) * ng_ref[:, vcols]
        yb_ref[:, vcols] = (on * _silu(bz_ref[:, vcols])).astype(yb_ref.dtype)

    @pl.when(c == pl.num_programs(2) - 1)
    def _():
        for hh in range(B_HEADS):
            s_out_ref[0, hh] = st_ref[hh].T


def _gla(p_all, batch, seq, w_gate_up, b_gate, b_norm_g, p_s, s_state, cast_weights):
    L = GLA_STEP
    hps = B_HEADS
    nc = seq // L
    t = batch * seq
    kw, vw = hps * B_DK, hps * B_DV

    def rows(b, h, c):
        return b * nc + c

    in_specs = [
        pl.BlockSpec((L, kw), lambda b, h, c: (rows(b, h, c), COL_BQ // kw + h)),
        pl.BlockSpec((L, kw), lambda b, h, c: (rows(b, h, c), COL_BK // kw + h)),
        pl.BlockSpec((L, vw), lambda b, h, c: (rows(b, h, c), COL_BV // vw + h)),
        pl.BlockSpec((L, vw), lambda b, h, c: (rows(b, h, c), COL_BZ // vw + h)),
        pl.BlockSpec((L, LANES), lambda b, h, c: (rows(b, h, c), COL_GATE // LANES)),
        pl.BlockSpec((GATE_RANK, kw), lambda b, h, c: (0, h)),
        pl.BlockSpec((1, kw), lambda b, h, c: (0, h)),
        pl.BlockSpec((1, vw), lambda b, h, c: (0, h)),
    ]
    out_specs = [
        pl.BlockSpec((L, vw), lambda b, h, c: (rows(b, h, c), h)),
        pl.BlockSpec((1, hps, B_DK, B_DV), lambda b, h, c: (b, h, 0, 0)),
    ]
    out_shape = [
        jax.ShapeDtypeStruct((t, B_WIDTH), BF16),
        jax.ShapeDtypeStruct((batch, B_HEADS, B_DK, B_DV), F32),
    ]

    nseq = p_s.shape[0]
    TB = DEC_TOKENS
    dstep = _decode_step(batch, nc, nseq)

    def dcol(base, width):
        return pl.BlockSpec((TB, width), lambda b, h, c: (dstep(b, h, c)[0], base // width + dstep(b, h, c)[1]))

    def dhead(shape):
        return pl.BlockSpec(shape, lambda b, h, c: (0, dstep(b, h, c)[1]))

    dstate = pl.BlockSpec((TB, 1, B_DK, B_DV), lambda b, h, c: dstep(b, h, c) + (0, 0))
    in_specs += [
        dcol(COL_BQ, B_DK), dcol(COL_BK, B_DK), dcol(COL_BV, B_DV), dcol(COL_BZ, B_DV),
        pl.BlockSpec((TB, LANES), lambda b, h, c: (dstep(b, h, c)[0], COL_GATE // LANES)),
        dhead((GATE_RANK, B_DK)), dhead((1, B_DK)), dhead((1, B_DV)), dstate,
    ]
    out_specs += [pl.BlockSpec((TB, B_DV), lambda b, h, c: dstep(b, h, c)), dstate]
    out_shape += [
        jax.ShapeDtypeStruct((nseq, B_WIDTH), BF16),
        jax.ShapeDtypeStruct((nseq, B_HEADS, B_DK, B_DV), F32),
    ]

    n_steps = batch * nc
    for wmat in cast_weights:
        slab = pl.BlockSpec((wmat.shape[0] // n_steps, wmat.shape[1]), lambda b, h, c: (rows(b, h, c), 0))
        in_specs.append(slab)
        out_specs.append(slab)
        out_shape.append(jax.ShapeDtypeStruct(wmat.shape, BF16))
    return pl.pallas_call(
        _gla_kernel,
        grid=(batch, B_HEADS // hps, nc),
        in_specs=in_specs,
        out_specs=out_specs,
        out_shape=out_shape,
        scratch_shapes=[pltpu.VMEM((hps, B_DV, B_DK), F32),
                        pltpu.VMEM((hps, L // GLA_CHUNK, B_DV, B_DK), F32),
                        pltpu.VMEM((hps, L // GLA_CHUNK, B_DV, B_DK), BF16)],
        compiler_params=pltpu.CompilerParams(
            dimension_semantics=("parallel", "parallel", "arbitrary"), vmem_limit_bytes=VMEM_LIMIT),
        name="gla",
    )(p_all, p_all, p_all, p_all, p_all, w_gate_up, b_gate, b_norm_g,
      p_s, p_s, p_s, p_s, p_s, w_gate_up, b_gate, b_norm_g, s_state, *cast_weights)


DEC_TOKENS = 16


def _decode_step(batch, nc, nseq):
    assert batch * nc == (nseq // DEC_TOKENS) * A_HEADS and A_HEADS == B_HEADS

    def step(b, h, c):
        s = b * nc + c
        return s // A_HEADS, s % A_HEADS
    return step


def _mlstm_decode_step(h_idx, qp_ref, kp_ref, av_ref, ao_ref, az_ref, g_ref,
                       sq0_ref, sq1_ref, sq2_ref, sk0_ref, sk1_ref, sk2_ref,
                       cwq_ref, cwk_ref, cbq_ref, cbk_ref, gb_ref, ang_ref, c_ref, n_ref, m_ref,
                       ya_ref, c_out_ref, n_out_ref, m_out_ref):
    TB = DEC_TOKENS

    def conv_silu(s0, s1, s2, x, w_ref, b_ref):
        acc = b_ref[...] + s0[...] * w_ref[0:1, :]
        acc = acc + s1[...] * w_ref[1:2, :]
        acc = acc + s2[...] * w_ref[2:3, :]
        acc = acc + x[...] * w_ref[3:4, :]
        return _silu(acc)

    q = conv_silu(sq0_ref, sq1_ref, sq2_ref, qp_ref, cwq_ref, cbq_ref)
    k = conv_silu(sk0_ref, sk1_ref, sk2_ref, kp_ref, cwk_ref, cbk_ref) * (A_DK ** -0.5)
    v = av_ref[...]
    g = g_ref[...]
    gbias = gb_ref[...]
    gate_lane = lax.broadcasted_iota(jnp.int32, (TB, LANES), 1)
    gsum = g + gbias

    def gate(lane0):
        return jnp.sum(jnp.where(gate_lane == lane0 + h_idx, gsum, 0.0), axis=1, keepdims=True)

    itil = gate(GATE_LANE_I)
    logf = _log_sigmoid(gate(GATE_LANE_F))
    lane_h = lax.broadcasted_iota(jnp.int32, (TB, A_HEADS), 1)
    m_prev = jnp.sum(jnp.where(lane_h == h_idx, m_ref[...], 0.0), axis=1, keepdims=True)
    inter = logf + m_prev
    m_t = jnp.maximum(inter, itil)
    w = jnp.exp(itil - m_t)
    decay = jnp.exp(inter - m_t)
    n_old = n_ref[:, 0, 0, :]
    s = jnp.sum(q * k, axis=1, keepdims=True) * w
    den = decay * jnp.sum(q * n_old, axis=1, keepdims=True) + s
    scale = 1.0 / jnp.maximum(jnp.abs(den), jnp.exp(-m_t))
    kw = k * w
    n_out_ref[:, 0, 0, :] = decay * n_old + kw
    m_out_ref[...] = jnp.broadcast_to(m_t, (TB, LANES))

    rows = lax.broadcasted_iota(jnp.int32, (TB, 1), 0)
    qb, kwb, vb = q.astype(BF16), kw.astype(BF16), v.astype(BF16)
    h_rows = []
    for t in range(TB):
        c_old = c_ref[t, 0]
        qc = _dot(qb, c_old.astype(BF16))[t:t + 1, :]
        h_rows.append((decay[t:t + 1, :] * qc + s[t:t + 1, :] * v[t:t + 1, :]) * scale[t:t + 1, :])
        outer = _dot_tn(jnp.where(rows == t, kwb, jnp.zeros_like(kwb)), vb)
        c_out_ref[t, 0] = decay[t:t + 1, :] * c_old + outer
    h = jnp.concatenate(h_rows, axis=0)
    mu = jnp.mean(h, axis=1, keepdims=True)
    hc = h - mu
    var = jnp.mean(hc * hc, axis=1, keepdims=True)
    hn = hc * lax.rsqrt(var + LN_EPS) * ang_ref[...]
    ya_ref[...] = (hn * _sigmoid(ao_ref[...]) * _silu(az_ref[...])).astype(ya_ref.dtype)


def _gla_decode_step(bq_ref, bk_ref, bv_ref, bz_ref, g_ref, wg_ref, bgate_ref, bng_ref, s_ref,
                     yb_ref, s_out_ref):
    TB = DEC_TOKENS
    gq = bq_ref[...] * (B_DK ** -0.5)
    gk = bk_ref[...]
    gv = bv_ref[...]
    loga = _gla_gate_log(g_ref[...], wg_ref[...], bgate_ref[...])
    eb = jnp.exp(loga)
    a = jnp.sum(gq * gk, axis=1, keepdims=True)
    ebt = jnp.concatenate([eb, jnp.zeros((LANES - TB, B_DK), F32)], axis=0).T
    rows = lax.broadcasted_iota(jnp.int32, (TB, 1), 0)
    qeb, gkb, gvb = (gq * eb).astype(BF16), gk.astype(BF16), gv.astype(BF16)
    o_rows = []
    for t in range(TB):
        s_old = s_ref[t, 0]
        o_rows.append(_dot(qeb, s_old.astype(BF16))[t:t + 1, :] + a[t:t + 1, :] * gv[t:t + 1, :])
        outer = _dot_tn(jnp.where(rows == t, gkb, jnp.zeros_like(gkb)), gvb)
        s_out_ref[t, 0] = ebt[:, t:t + 1] * s_old + outer
    o = jnp.concatenate(o_rows, axis=0)
    on = o * lax.rsqrt(jnp.mean(o * o, axis=1, keepdims=True) + LN_EPS) * bng_ref[...]
    yb_ref[...] = (on * _silu(bz_ref[...])).astype(yb_ref.dtype)


def _out_kernel(ya_ref, yb_ref, ga_ref, gb_ref, x_ref, yas_ref, ybs_ref, gas_ref, gbs_ref, xs_ref,
                wpa_ref, wpb_ref, wo_ref, lng_ref, lnb_ref, o_ref, os_ref):
    def rows(ya, yb, ga, gb, x):
        pa = _dot(ya[...], wpa_ref[...])
        pb = _dot(yb[...], wpb_ref[...])
        merged = _sigmoid(ga[...]) * pa + _sigmoid(gb[...]) * pb
        r = ALPHA * x[...] + _dot(merged.astype(BF16), wo_ref[...])
        mu = jnp.mean(r, axis=1, keepdims=True)
        rc = r - mu
        var = jnp.mean(rc * rc, axis=1, keepdims=True)
        return rc * lax.rsqrt(var + LN_EPS) * lng_ref[...] + lnb_ref[...]

    last = pl.num_programs(0) - 1

    @pl.when(pl.program_id(0) < last)
    def _():
        o_ref[...] = rows(ya_ref, yb_ref, ga_ref, gb_ref, x_ref)

    @pl.when(pl.program_id(0) == last)
    def _():
        os_ref[...] = rows(yas_ref, ybs_ref, gas_ref, gbs_ref, xs_ref)


def _out(ya, yb, p_all, x, ya_s, yb_s, p_s, xs, w_pa, w_pb, w_out, ln_g, ln_b, tm):
    t, ts = x.shape[0], xs.shape[0]
    n_tiles = t // tm
    const = lambda i: (0, 0)
    single = pl.Buffered(1)

    def tile(i):
        return jnp.minimum(i, n_tiles - 1)

    in_specs = [
        pl.BlockSpec((tm, A_WIDTH), lambda i: (tile(i), 0)),
        pl.BlockSpec((tm, B_WIDTH), lambda i: (tile(i), 0)),
        pl.BlockSpec((tm, D_MODEL), lambda i: (tile(i), COL_GA // D_MODEL)),
        pl.BlockSpec((tm, D_MODEL), lambda i: (tile(i), COL_GB // D_MODEL)),
        pl.BlockSpec((tm, D_MODEL), lambda i: (tile(i), 0)),
        pl.BlockSpec((ts, A_WIDTH), const),
        pl.BlockSpec((ts, B_WIDTH), const),
        pl.BlockSpec((ts, D_MODEL), lambda i: (0, COL_GA // D_MODEL)),
        pl.BlockSpec((ts, D_MODEL), lambda i: (0, COL_GB // D_MODEL)),
        pl.BlockSpec((ts, D_MODEL), const),
        pl.BlockSpec((A_WIDTH, D_MODEL), const, pipeline_mode=single),
        pl.BlockSpec((B_WIDTH, D_MODEL), const, pipeline_mode=single),
        pl.BlockSpec((D_MODEL, D_MODEL), const, pipeline_mode=single),
        pl.BlockSpec((1, D_MODEL), const),
        pl.BlockSpec((1, D_MODEL), const),
    ]
    return pl.pallas_call(
        _out_kernel,
        grid=(n_tiles + 1,),
        in_specs=in_specs,
        out_specs=[pl.BlockSpec((tm, D_MODEL), lambda i: (tile(i), 0)),
                   pl.BlockSpec((ts, D_MODEL), const)],
        out_shape=[jax.ShapeDtypeStruct((t, D_MODEL), F32), jax.ShapeDtypeStruct((ts, D_MODEL), F32)],
        compiler_params=pltpu.CompilerParams(
            dimension_semantics=("arbitrary",), vmem_limit_bytes=VMEM_LIMIT),
        name="outproj",
    )(ya, yb, p_all, p_all, x, ya_s, yb_s, p_s, p_s, xs, w_pa, w_pb, w_out, ln_g, ln_b)


RELAYOUT_TN = 1024


def _relayout_kernel(src_ref, gif_ref, gb0_ref, gb1_ref, xs_ref, o_ref, ps_ref):
    j = pl.program_id(0)
    tn = RELAYOUT_TN
    n_main = COL_GATE // tn

    @pl.when(j < n_main)
    def _():
        o_ref[...] = src_ref[...].astype(BF16)

    @pl.when(j == n_main)
    def _():
        pad = jnp.zeros((tn - 2 * A_HEADS - GATE_RANK, src_ref.shape[1]), F32)
        o_ref[...] = jnp.concatenate([gif_ref[...], gb0_ref[...], gb1_ref[...], pad], axis=0).astype(BF16)

    ps_ref[...] = _dot_nt(xs_ref[...].astype(BF16), o_ref[...])


def _relayout_w_in(w_in_t, xs):
    n_src, k = w_in_t.shape
    ts = xs.shape[0]
    tn = RELAYOUT_TN

    def src_row(j):
        shift = 0
        for first, last, src in RELAYOUT_RUNS:
            assert first % tn == 0 and last % tn == 0 and (src - first) % SUBLANES == 0
            shift = jnp.where((j >= first // tn) & (j < last // tn), (src - first) // SUBLANES, shift)
        group = jnp.minimum(j * (tn // SUBLANES) + shift, (n_src - tn) // SUBLANES)
        return group * SUBLANES

    return pl.pallas_call(
        _relayout_kernel,
        grid=(pl.cdiv(N_PROJ, tn),),
        in_specs=[pl.BlockSpec((pl.Element(tn), pl.Element(k)), lambda j: (src_row(j), 0)),
                  pl.BlockSpec((SUBLANES, k), lambda j: (SRC_I // SUBLANES, 0)),
                  pl.BlockSpec((SUBLANES, k), lambda j: (SRC_BG // SUBLANES, 0)),
                  pl.BlockSpec((SUBLANES, k), lambda j: (SRC_BG // SUBLANES + 1, 0)),
                  pl.BlockSpec((ts, k), lambda j: (0, 0))],
        out_specs=[pl.BlockSpec((tn, k), lambda j: (j, 0)),
                   pl.BlockSpec((ts, tn), lambda j: (0, j))],
        out_shape=[jax.ShapeDtypeStruct((N_PROJ, k), BF16), jax.ShapeDtypeStruct((ts, N_PROJ), F32)],
        compiler_params=pltpu.CompilerParams(
            dimension_semantics=("parallel",), vmem_limit_bytes=VMEM_LIMIT),
        name="relayout",
    )(w_in_t, w_in_t, w_in_t, w_in_t, xs)


def kernel(x_prompt, x_sample, state_mlstm_C, state_mlstm_n, state_mlstm_m, state_conv, state_gla_S,
           w_in, conv_w, conv_b, b_i, b_f, a_norm_g, w_gate_up, b_gate, b_norm_g, w_pa, w_pb, w_out,
           ln_g, ln_b):
    batch, seq, _ = x_prompt.shape
    nseq = x_sample.shape[0]
    assert w_in.shape[0] == 1, "single-layer step"
    d = 0

    def layer(a):
        return a.reshape(a.shape[1:])

    xs = x_sample.reshape(nseq, D_MODEL)
    wp, ps_all = _relayout_w_in(layer(w_in).T, xs)
    cw = conv_w[d]
    cb = conv_b[d][None, :]
    gate_bias = jnp.concatenate([b_i[d], b_f[d], jnp.zeros((LANES - 2 * A_HEADS,), F32)])[None, :]
    ang = a_norm_g[d][None, :]
    bng = b_norm_g[d][None, :]
    wg = w_gate_up[d]
    bgate = b_gate[d][None, :]
    lng, lnb = ln_g[d][None, :], ln_b[d][None, :]

    xp = x_prompt.reshape(batch * seq, D_MODEL)
    p_all = _proj(xp, wp, tm=1024, tn=PROJ_TN)

    conv_state = state_conv.reshape(nseq, (CONV_W - 1) * 2 * A_WIDTH)
    ya, p_c, p_n, p_m, ya_s, s_c, s_n, s_m = _mlstm(
        p_all, batch, seq, cw, cb, gate_bias, ang, ps_all, conv_state,
        layer(state_mlstm_C), state_mlstm_n.reshape(nseq, A_HEADS, 1, A_DK), layer(state_mlstm_m))
    yb, p_s, yb_s, s_s, wpa, wpb, wo = _gla(p_all, batch, seq, wg, bgate, bng, ps_all, layer(state_gla_S),
                                            (layer(w_pa), layer(w_pb), layer(w_out)))

    y_prompt, y_sample = _out(ya, yb, p_all, xp, ya_s, yb_s, ps_all, xs, wpa, wpb, wo, lng, lnb, tm=256)
    y_prompt = y_prompt.reshape(batch, seq, D_MODEL)
    y_sample = y_sample.reshape(nseq, 1, D_MODEL)
    p_conv = p_all.reshape(batch, seq, N_PROJ)[:, seq - (CONV_W - 1):, :2 * A_WIDTH]
    s_conv = jnp.concatenate([layer(state_conv)[:, 1:, :], ps_all[:, None, :2 * A_WIDTH]], axis=1)

    def stacked(a):
        return a.reshape((1,) + a.shape)

    return (y_prompt, y_sample,
            stacked(p_c), p_n.reshape(1, batch, A_HEADS, A_DK), stacked(p_m[:, :, 0, 0]),
            stacked(p_conv), stacked(p_s),
            stacked(s_c), s_n.reshape(1, nseq, A_HEADS, A_DK),
            stacked(s_m.reshape(nseq, A_HEADS, LANES)[:, :, 0]), stacked(s_conv), stacked(s_s))
```

```python
import functools

import jax
import jax.numpy as jnp
from jax import lax
from jax.experimental import pallas as pl
from jax.experimental.pallas import tpu as pltpu

F32 = jnp.float32
BF16 = jnp.bfloat16

D_MODEL = 2048
A_HEADS = 4
A_WIDTH = 1024
A_DK = 256
A_DV = 256
CONV_W = 4
B_HEADS = 4
B_WIDTH = 1024
B_KWIDTH = 512
B_DK = 128
B_DV = 256
GATE_RANK = 16
GATE_TAU = 16.0
ALPHA = 2.0 ** 0.25
LN_EPS = 1e-5
LOG2_E = 1.4426950408889634
LN_2 = 0.6931471805599453

LANES = 128
SUBLANES = 8
VMEM_LIMIT = 48 * 1024 * 1024

COL_Q = 0
COL_K = A_WIDTH
COL_BQ = 2 * A_WIDTH
COL_BK = COL_BQ + B_KWIDTH
COL_AV = COL_BK + B_KWIDTH
COL_AO = COL_AV + A_WIDTH
COL_AZ = COL_AO + A_WIDTH
COL_BV = COL_AZ + A_WIDTH
COL_BZ = COL_BV + B_WIDTH
COL_GA = COL_BZ + B_WIDTH
COL_GB = COL_GA + D_MODEL
COL_GATE = COL_GB + D_MODEL
GATE_LANE_I = 0
GATE_LANE_F = A_HEADS
GATE_LANE_LR = 2 * A_HEADS
MXU_COLS = 256
N_PROJ = COL_GATE + MXU_COLS
PROJ_TN = N_PROJ // 7
assert PROJ_TN * 7 == N_PROJ and PROJ_TN % MXU_COLS == 0
PROJ_VMEM_LIMIT = 52 * 1024 * 1024

SRC_QK = 0
SRC_AV = 2 * A_WIDTH
SRC_I = SRC_AV + A_WIDTH
SRC_AO = SRC_I + 2 * A_HEADS
SRC_BQ = SRC_AO + 2 * A_WIDTH
SRC_BV = SRC_BQ + 2 * B_KWIDTH
SRC_BG = SRC_BV + B_WIDTH
SRC_BZ = SRC_BG + GATE_RANK
RELAYOUT_RUNS = ((COL_Q, COL_BQ, SRC_QK), (COL_BQ, COL_AV, SRC_BQ), (COL_AV, COL_AO, SRC_AV),
                 (COL_AO, COL_BV, SRC_AO), (COL_BV, COL_BZ, SRC_BV), (COL_BZ, COL_GATE, SRC_BZ))

MLSTM_CHUNK = 256
GLA_STEP = 256
GLA_CHUNK = 64
GLA_BLOCK = 16


def _dot(a, b):
    return jnp.dot(a, b, preferred_element_type=F32)


def _dot_nt(a, b):
    return lax.dot_general(a, b, (((1,), (1,)), ((), ())), preferred_element_type=F32)


def _dot_tn(a, b):
    return lax.dot_general(a, b, (((0,), (0,)), ((), ())), preferred_element_type=F32)


def _mask_dot(mask_bf16, x):
    hi = x.astype(BF16)
    r1 = x - hi.astype(F32)
    mid = r1.astype(BF16)
    lo = (r1 - mid.astype(F32)).astype(BF16)
    return _dot(mask_bf16, hi) + _dot(mask_bf16, mid) + _dot(mask_bf16, lo)


def _log_sigmoid(z):
    return jnp.minimum(z, 0.0) - LN_2 * jnp.log2(1.0 + jnp.exp2(jnp.abs(z) * (-LOG2_E)))


def _sigmoid(z):
    return 1.0 / (1.0 + jnp.exp2(z * (-LOG2_E)))


def _silu(z):
    return z * _sigmoid(z)


def _proj_kernel(x_ref, w_ref, o_ref):
    o_ref[...] = _dot_nt(x_ref[...].astype(BF16), w_ref[...])


def _proj(x, w_t, tm, tn):
    t, k = x.shape
    n = w_t.shape[0]
    return pl.pallas_call(
        _proj_kernel,
        grid=(t // tm, n // tn),
        in_specs=[pl.BlockSpec((tm, k), lambda i, j: (i, 0)),
                  pl.BlockSpec((tn, k), lambda i, j: (j, 0))],
        out_specs=pl.BlockSpec((tm, tn), lambda i, j: (i, j)),
        out_shape=jax.ShapeDtypeStruct((t, n), F32),
        compiler_params=pltpu.CompilerParams(
            dimension_semantics=("parallel", "arbitrary"), vmem_limit_bytes=PROJ_VMEM_LIMIT),
        name="proj",
    )(x, w_t)


def _mlstm_kernel(qp_ref, kp_ref, v_ref, ao_ref, az_ref, g_ref, cwq_ref, cwk_ref, cbq_ref, cbk_ref,
                  gb_ref, ng_ref, *rest, n_chunks):
    dec_in, (ya_ref, c_ref, n_ref, m_ref), dec_out, (tailq_ref, tailk_ref) = (
        rest[:21], rest[21:25], rest[25:29], rest[29:])
    L = MLSTM_CHUNK
    c = pl.program_id(2)

    @pl.when(c == 0)
    def _():
        c_ref[...] = jnp.zeros_like(c_ref)
        n_ref[...] = jnp.zeros_like(n_ref)
        m_ref[...] = jnp.zeros_like(m_ref)
        tailq_ref[...] = jnp.zeros_like(tailq_ref)
        tailk_ref[...] = jnp.zeros_like(tailk_ref)

    _mlstm_decode_step((pl.program_id(0) * n_chunks + c) % A_HEADS, *dec_in, *dec_out)

    sub = lax.broadcasted_iota(jnp.int32, (1, SUBLANES, 1), 1)

    def conv_silu(x_ref, tail_ref, w_ref, b_ref):
        x = x_ref[...]
        width = x.shape[1]
        x3 = jnp.concatenate([tail_ref[...], x], axis=0).reshape(L // SUBLANES + 1, SUBLANES, width)
        acc = b_ref[...] + x * w_ref[CONV_W - 1:CONV_W, :]
        for j in range(1, CONV_W):
            rot = pltpu.roll(x3, j, axis=1)
            xs = jnp.where(sub < j, rot[:-1], rot[1:]).reshape(L, width)
            acc = acc + xs * w_ref[CONV_W - 1 - j:CONV_W - j, :]
        tail_ref[...] = x[L - SUBLANES:, :]
        return _silu(acc)

    q_all = conv_silu(qp_ref, tailq_ref, cwq_ref, cbq_ref)
    k_all = conv_silu(kp_ref, tailk_ref, cwk_ref, cbk_ref) * (A_DK ** -0.5)

    lane = lax.broadcasted_iota(jnp.int32, (L, LANES), 1)
    row = lax.broadcasted_iota(jnp.int32, (L, L), 0)
    col = lax.broadcasted_iota(jnp.int32, (L, L), 1)
    causal = row >= col
    causal_b = causal.astype(BF16)

    g = g_ref[...] + gb_ref[...]
    is_f = (lane >= GATE_LANE_F) & (lane < GATE_LANE_F + A_HEADS)
    g2 = jnp.where(is_f, _log_sigmoid(g), g)
    x2 = jnp.where(is_f, _mask_dot(causal_b, g2), g2)
    x2t = x2.T

    for hh in range(A_HEADS):
        cols = slice(hh * A_DK, (hh + 1) * A_DK)
        q, k, v = q_all[:, cols], k_all[:, cols], v_ref[:, cols]
        li, lf = GATE_LANE_I + hh, GATE_LANE_F + hh
        itil_col, b_col = x2[:, li:li + 1], x2[:, lf:lf + 1]
        itil_row, b_row = x2t[li:li + 1, :], x2t[lf:lf + 1, :]

        m_prev = m_ref[0, hh, 0:1, 0:1]
        dmat = jnp.where(causal, b_col - b_row + itil_row, -jnp.inf)
        inter = b_col + m_prev
        m_t = jnp.maximum(inter, jnp.max(dmat, axis=1, keepdims=True))
        w = jnp.exp(dmat - m_t)
        decay = jnp.exp(inter - m_t)

        qb, kb, vb = q.astype(BF16), k.astype(BF16), v.astype(BF16)
        c_old = c_ref[0, hh]
        n_old = n_ref[0, hh]
        s = _dot_nt(qb, kb) * w
        num = decay * _dot(qb, c_old.astype(BF16)) + _dot(s.astype(BF16), vb)
        den = decay * jnp.sum(q * n_old, axis=1, keepdims=True) + jnp.sum(s, axis=1, keepdims=True)
        inv = 1.0 / jnp.maximum(jnp.abs(den), jnp.exp(-m_t))

        m_new = m_t[L - 1:L, :]
        b_last = b_col[L - 1:L, :]
        wk = jnp.exp(b_last - b_col + itil_col - m_new)
        dec = jnp.exp(b_last + m_prev - m_new)
        kw = k * wk
        c_ref[0, hh] = dec * c_old + _dot_tn(kw.astype(BF16), vb)
        n_ref[0, hh] = dec * n_old + jnp.sum(kw, axis=0, keepdims=True)
        m_ref[0, hh] = jnp.broadcast_to(m_new, (1, LANES))

        mu = jnp.mean(num, axis=1, keepdims=True)
        hc = num - mu
        var = jnp.mean(hc * hc, axis=1, keepdims=True)
        hn = hc * (inv * lax.rsqrt(var * (inv * inv) + LN_EPS)) * ng_ref[:, cols]
        ya_ref[:, cols] = (hn * _sigmoid(ao_ref[:, cols]) * _silu(az_ref[:, cols])).astype(ya_ref.dtype)


def _mlstm(p_all, batch, seq, conv_w, conv_b, gate_bias, a_norm_g, p_s, conv_state, c_state, n_state, m_state):
    L = MLSTM_CHUNK
    hps = A_HEADS
    nc = seq // L
    t = batch * seq
    wd = hps * A_DK
    k_off = A_WIDTH // wd

    def rows(b, h, c):
        return b * nc + c

    def pcol(base):
        return pl.BlockSpec((L, wd), lambda b, h, c: (rows(b, h, c), base // wd + h))

    in_specs = [
        pcol(COL_Q), pcol(COL_K), pcol(COL_AV), pcol(COL_AO), pcol(COL_AZ),
        pl.BlockSpec((L, LANES), lambda b, h, c: (rows(b, h, c), COL_GATE // LANES)),
        pl.BlockSpec((CONV_W, wd), lambda b, h, c: (0, h)),
        pl.BlockSpec((CONV_W, wd), lambda b, h, c: (0, k_off + h)),
        pl.BlockSpec((1, wd), lambda b, h, c: (0, h)),
        pl.BlockSpec((1, wd), lambda b, h, c: (0, k_off + h)),
        pl.BlockSpec((1, LANES), lambda b, h, c: (0, 0)),
        pl.BlockSpec((1, wd), lambda b, h, c: (0, h)),
    ]
    out_specs = [
        pl.BlockSpec((L, wd), lambda b, h, c: (rows(b, h, c), h)),
        pl.BlockSpec((1, hps, A_DK, A_DV), lambda b, h, c: (b, h, 0, 0)),
        pl.BlockSpec((1, hps, 1, A_DK), lambda b, h, c: (b, h, 0, 0)),
        pl.BlockSpec((1, hps, 1, LANES), lambda b, h, c: (b, h, 0, 0)),
    ]
    out_shape = [
        jax.ShapeDtypeStruct((t, A_WIDTH), BF16),
        jax.ShapeDtypeStruct((batch, A_HEADS, A_DK, A_DV), F32),
        jax.ShapeDtypeStruct((batch, A_HEADS, 1, A_DK), F32),
        jax.ShapeDtypeStruct((batch, A_HEADS, 1, LANES), F32),
    ]

    nseq = p_s.shape[0]
    TB = DEC_TOKENS
    dstep = _decode_step(batch, nc, nseq)

    def dcol(base, width):
        return pl.BlockSpec((TB, width), lambda b, h, c: (dstep(b, h, c)[0], base // width + dstep(b, h, c)[1]))

    def dconv(j, base):
        return pl.BlockSpec((TB, A_DK), lambda b, h, c: (dstep(b, h, c)[0],
                                                        (j * 2 * A_WIDTH + base) // A_DK + dstep(b, h, c)[1]))

    def dhead(shape, off=0):
        return pl.BlockSpec(shape, lambda b, h, c: (0, off + dstep(b, h, c)[1]))

    def dstate(*dims):
        return pl.BlockSpec((TB, 1) + dims, lambda b, h, c: dstep(b, h, c) + (0,) * len(dims))

    in_specs += [
        dcol(COL_Q, A_DK), dcol(COL_K, A_DK), dcol(COL_AV, A_DV), dcol(COL_AO, A_DV), dcol(COL_AZ, A_DV),
        pl.BlockSpec((TB, LANES), lambda b, h, c: (dstep(b, h, c)[0], COL_GATE // LANES)),
        dconv(0, 0), dconv(1, 0), dconv(2, 0), dconv(0, A_WIDTH), dconv(1, A_WIDTH), dconv(2, A_WIDTH),
        dhead((CONV_W, A_DK)), dhead((CONV_W, A_DK), A_HEADS), dhead((1, A_DK)), dhead((1, A_DK), A_HEADS),
        pl.BlockSpec((1, LANES), lambda b, h, c: (0, 0)), dhead((1, A_DV)),
        dstate(A_DK, A_DV), dstate(1, A_DK),
        pl.BlockSpec((TB, A_HEADS), lambda b, h, c: (dstep(b, h, c)[0], 0)),
    ]
    out_specs += [
        pl.BlockSpec((TB, A_DV), lambda b, h, c: dstep(b, h, c)),
        dstate(A_DK, A_DV), dstate(1, A_DK),
        pl.BlockSpec((TB, LANES), lambda b, h, c: dstep(b, h, c)),
    ]
    out_shape += [
        jax.ShapeDtypeStruct((nseq, A_WIDTH), BF16),
        jax.ShapeDtypeStruct((nseq, A_HEADS, A_DK, A_DV), F32),
        jax.ShapeDtypeStruct((nseq, A_HEADS, 1, A_DK), F32),
        jax.ShapeDtypeStruct((nseq, A_HEADS * LANES), F32),
    ]
    return pl.pallas_call(
        functools.partial(_mlstm_kernel, n_chunks=nc),
        grid=(batch, A_HEADS // hps, nc),
        in_specs=in_specs,
        out_specs=out_specs,
        out_shape=out_shape,
        scratch_shapes=[pltpu.VMEM((SUBLANES, wd), F32),
                        pltpu.VMEM((SUBLANES, wd), F32)],
        compiler_params=pltpu.CompilerParams(
            dimension_semantics=("parallel", "parallel", "arbitrary"), vmem_limit_bytes=VMEM_LIMIT),
        name="mlstm",
    )(p_all, p_all, p_all, p_all, p_all, p_all, conv_w, conv_w, conv_b, conv_b, gate_bias, a_norm_g,
      p_s, p_s, p_s, p_s, p_s, p_s,
      conv_state, conv_state, conv_state, conv_state, conv_state, conv_state,
      conv_w, conv_w, conv_b, conv_b, gate_bias, a_norm_g, c_state, n_state, m_state)


def _gla_gate_log(bg_tile, wg, bgate):
    lr = bg_tile[:, GATE_LANE_LR:GATE_LANE_LR + GATE_RANK]
    z = _dot(lr.astype(BF16), wg.astype(BF16)) + bgate
    return _log_sigmoid(z) / GATE_TAU


def _gla_kernel(q_ref, k_ref, v_ref, bz_ref, g_ref, wg_ref, bgate_ref, ng_ref, *rest):
    dec_in, w_in, (yb_ref, s_out_ref), dec_out, w_out, (st_ref, ds_ref, sb_ref) = (
        rest[:9], rest[9:12], rest[12:14], rest[14:16], rest[16:19], rest[19:])
    L = GLA_STEP
    B = GLA_BLOCK
    nb = L // B
    c = pl.program_id(2)

    @pl.when(c == 0)
    def _():
        st_ref[...] = jnp.zeros_like(st_ref)

    _gla_decode_step(*dec_in, *dec_out)
    for src_ref, dst_ref in zip(w_in, w_out):
        dst_ref[...] = src_ref[...].astype(BF16)

    CH = GLA_CHUNK
    bpc, nch = CH // B, L // CH
    row = lax.broadcasted_iota(jnp.int32, (L, L), 0)
    col = lax.broadcasted_iota(jnp.int32, (L, L), 1)
    tri_c = (((row // CH) == (col // CH)) & (row >= col)).astype(BF16)
    blk3 = lax.broadcasted_iota(jnp.int32, (nb, 1, LANES), 0)
    lane_rel = lax.broadcasted_iota(jnp.int32, (nb, 1, LANES), 2) - (blk3 % bpc) * B
    t3 = lax.broadcasted_iota(jnp.int32, (1, B, 1), 1)

    for hh in range(B_HEADS):
        kcols = slice(hh * B_DK, (hh + 1) * B_DK)
        vcols = slice(hh * B_DV, (hh + 1) * B_DV)
        loga = _gla_gate_log(g_ref[...], wg_ref[:, kcols], bgate_ref[:, kcols])
        bc = _mask_dot(tri_c, loga) * LOG2_E
        bl = jnp.broadcast_to(bc.reshape(nch, CH, B_DK)[:, CH - 1:CH, :],
                              (nch, CH, B_DK)).reshape(L, B_DK)
        k = k_ref[:, kcols]
        qs = q_ref[:, kcols] * (B_DK ** -0.5)
        qt = (qs * jnp.exp2(bc)).astype(BF16)
        kt = (k * jnp.exp2(bl - bc)).astype(BF16)
        eb = jnp.exp2(bl)
        vb = v_ref[:, vcols].astype(BF16)

        bc3, q3, k3 = (x.reshape(nb, B, B_DK) for x in (bc, qs, k))
        groups = B // SUBLANES
        bcg = [bc3[:, g * SUBLANES:(g + 1) * SUBLANES, :] for g in range(groups)]
        qg = [q3[:, g * SUBLANES:(g + 1) * SUBLANES, :] for g in range(groups)]
        ag = [jnp.zeros((nb, SUBLANES, LANES), F32) for _ in range(groups)]
        for s in range(B):
            bc_s, k_s = bc3[:, s:s + 1, :], k3[:, s:s + 1, :]
            for g in range(s // SUBLANES, groups):
                e = jnp.exp2(bcg[g] - bc_s)
                a_col = jnp.sum(qg[g] * k_s * e, axis=2, keepdims=True)
                ag[g] = jnp.where(lane_rel == s, a_col, ag[g])
        a3 = jnp.concatenate(ag, axis=1)
        a3 = jnp.where((lane_rel >= 0) & (lane_rel <= t3), a3, 0.0)
        a_chunks = []
        for ci in range(nch):
            rows_c = [a3[ci * bpc]]
            for i in range(1, bpc):
                blk = ci * bpc + i
                start = bc3[blk - 1, B - 1:B, :]
                qd = (q3[blk] * jnp.exp2(bc3[blk] - start)).astype(BF16)
                ks = k3[ci * bpc:blk].reshape(i * B, B_DK)
                kd = ks * jnp.exp2(start - bc3[ci * bpc:blk].reshape(i * B, B_DK))
                kd = jnp.concatenate([kd, jnp.zeros((LANES - i * B, B_DK), F32)], axis=0).astype(BF16)
                rows_c.append(a3[blk] + _dot_nt(qd, kd))
            a_chunks.append(jnp.concatenate(rows_c, axis=0)[:, 0:CH].astype(BF16))

        for ci in range(nch):
            ds_ref[hh, ci] = _dot_tn(vb[ci * CH:(ci + 1) * CH, :], kt[ci * CH:(ci + 1) * CH, :])
        st = st_ref[hh]
        for ci in range(nch):
            sb_ref[hh, ci] = st.astype(BF16)
            st = st * eb[ci * CH:ci * CH + 1, :] + ds_ref[hh, ci]
        st_ref[hh] = st
        o = jnp.concatenate(
            [_dot(a_chunks[ci], vb[ci * CH:(ci + 1) * CH, :])
             + _dot_nt(qt[ci * CH:(ci + 1) * CH, :], sb_ref[hh, ci]) for ci in range(nch)], axis=0)

        on = o * lax.rsqrt(jnp.mean(o * o, axis=1, keepdims=True) + LN_EPS) * ng_ref[:, vcols]
        yb_ref[:, vcols] = (on * _silu(bz_ref[:, vcols])).astype(yb_ref.dtype)

    @pl.when(c == pl.num_programs(2) - 1)
    def _():
        for hh in range(B_HEADS):
            s_out_ref[0, hh] = st_ref[hh].T


def _gla(p_all, batch, seq, w_gate_up, b_gate, b_norm_g, p_s, s_state, cast_weights):
    L = GLA_STEP
    hps = B_HEADS
    nc = seq // L
    t = batch * seq
    kw, vw = hps * B_DK, hps * B_DV

    def rows(b, h, c):
        return b * nc + c

    in_specs = [
        pl.BlockSpec((L, kw), lambda b, h, c: (rows(b, h, c), COL_BQ // kw + h)),
        pl.BlockSpec((L, kw), lambda b, h, c: (rows(b, h, c), COL_BK // kw + h)),
        pl.BlockSpec((L, vw), lambda b, h, c: (rows(b, h, c), COL_BV // vw + h)),
        pl.BlockSpec((L, vw), lambda b, h, c: (rows(b, h, c), COL_BZ // vw + h)),
        pl.BlockSpec((L, LANES), lambda b, h, c: (rows(b, h, c), COL_GATE // LANES)),
        pl.BlockSpec((GATE_RANK, kw), lambda b, h, c: (0, h)),
        pl.BlockSpec((1, kw), lambda b, h, c: (0, h)),
        pl.BlockSpec((1, vw), lambda b, h, c: (0, h)),
    ]
    out_specs = [
        pl.BlockSpec((L, vw), lambda b, h, c: (rows(b, h, c), h)),
        pl.BlockSpec((1, hps, B_DK, B_DV), lambda b, h, c: (b, h, 0, 0)),
    ]
    out_shape = [
        jax.ShapeDtypeStruct((t, B_WIDTH), BF16),
        jax.ShapeDtypeStruct((batch, B_HEADS, B_DK, B_DV), F32),
    ]

    nseq = p_s.shape[0]
    TB = DEC_TOKENS
    dstep = _decode_step(batch, nc, nseq)

    def dcol(base, width):
        return pl.BlockSpec((TB, width), lambda b, h, c: (dstep(b, h, c)[0], base // width + dstep(b, h, c)[1]))

    def dhead(shape):
        return pl.BlockSpec(shape, lambda b, h, c: (0, dstep(b, h, c)[1]))

    dstate = pl.BlockSpec((TB, 1, B_DK, B_DV), lambda b, h, c: dstep(b, h, c) + (0, 0))
    in_specs += [
        dcol(COL_BQ, B_DK), dcol(COL_BK, B_DK), dcol(COL_BV, B_DV), dcol(COL_BZ, B_DV),
        pl.BlockSpec((TB, LANES), lambda b, h, c: (dstep(b, h, c)[0], COL_GATE // LANES)),
        dhead((GATE_RANK, B_DK)), dhead((1, B_DK)), dhead((1, B_DV)), dstate,
    ]
    out_specs += [pl.BlockSpec((TB, B_DV), lambda b, h, c: dstep(b, h, c)), dstate]
    out_shape += [
        jax.ShapeDtypeStruct((nseq, B_WIDTH), BF16),
        jax.ShapeDtypeStruct((nseq, B_HEADS, B_DK, B_DV), F32),
    ]

    n_steps = batch * nc
    for wmat in cast_weights:
        slab = pl.BlockSpec((wmat.shape[0] // n_steps, wmat.shape[1]), lambda b, h, c: (rows(b, h, c), 0))
        in_specs.append(slab)
        out_specs.append(slab)
        out_shape.append(jax.ShapeDtypeStruct(wmat.shape, BF16))
    return pl.pallas_call(
        _gla_kernel,
        grid=(batch, B_HEADS // hps, nc),
        in_specs=in_specs,
        out_specs=out_specs,
        out_shape=out_shape,
        scratch_shapes=[pltpu.VMEM((hps, B_DV, B_DK), F32),
                        pltpu.VMEM((hps, L // GLA_CHUNK, B_DV, B_DK), F32),
                        pltpu.VMEM((hps, L // GLA_CHUNK, B_DV, B_DK), BF16)],
        compiler_params=pltpu.CompilerParams(
            dimension_semantics=("parallel", "parallel", "arbitrary"), vmem_limit_bytes=VMEM_LIMIT),
        name="gla",
    )(p_all, p_all, p_all, p_all, p_all, w_gate_up, b_gate, b_norm_g,
      p_s, p_s, p_s, p_s, p_s, w_gate_up, b_gate, b_norm_g, s_state, *cast_weights)


DEC_TOKENS = 16


def _decode_step(batch, nc, nseq):
    assert batch * nc == (nseq // DEC_TOKENS) * A_HEADS and A_HEADS == B_HEADS

    def step(b, h, c):
        s = b * nc + c
        return s // A_HEADS, s % A_HEADS
    return step


def _mlstm_decode_step(h_idx, qp_ref, kp_ref, av_ref, ao_ref, az_ref, g_ref,
                       sq0_ref, sq1_ref, sq2_ref, sk0_ref, sk1_ref, sk2_ref,
                       cwq_ref, cwk_ref, cbq_ref, cbk_ref, gb_ref, ang_ref, c_ref, n_ref, m_ref,
                       ya_ref, c_out_ref, n_out_ref, m_out_ref):
    TB = DEC_TOKENS

    def conv_silu(s0, s1, s2, x, w_ref, b_ref):
        acc = b_ref[...] + s0[...] * w_ref[0:1, :]
        acc = acc + s1[...] * w_ref[1:2, :]
        acc = acc + s2[...] * w_ref[2:3, :]
        acc = acc + x[...] * w_ref[3:4, :]
        return _silu(acc)

    q = conv_silu(sq0_ref, sq1_ref, sq2_ref, qp_ref, cwq_ref, cbq_ref)
    k = conv_silu(sk0_ref, sk1_ref, sk2_ref, kp_ref, cwk_ref, cbk_ref) * (A_DK ** -0.5)
    v = av_ref[...]
    g = g_ref[...]
    gbias = gb_ref[...]
    gate_lane = lax.broadcasted_iota(jnp.int32, (TB, LANES), 1)
    gsum = g + gbias

    def gate(lane0):
        return jnp.sum(jnp.where(gate_lane == lane0 + h_idx, gsum, 0.0), axis=1, keepdims=True)

    itil = gate(GATE_LANE_I)
    logf = _log_sigmoid(gate(GATE_LANE_F))
    lane_h = lax.broadcasted_iota(jnp.int32, (TB, A_HEADS), 1)
    m_prev = jnp.sum(jnp.where(lane_h == h_idx, m_ref[...], 0.0), axis=1, keepdims=True)
    inter = logf + m_prev
    m_t = jnp.maximum(inter, itil)
    w = jnp.exp(itil - m_t)
    decay = jnp.exp(inter - m_t)
    n_old = n_ref[:, 0, 0, :]
    s = jnp.sum(q * k, axis=1, keepdims=True) * w
    den = decay * jnp.sum(q * n_old, axis=1, keepdims=True) + s
    scale = 1.0 / jnp.maximum(jnp.abs(den), jnp.exp(-m_t))
    kw = k * w
    n_out_ref[:, 0, 0, :] = decay * n_old + kw
    m_out_ref[...] = jnp.broadcast_to(m_t, (TB, LANES))

    rows = lax.broadcasted_iota(jnp.int32, (TB, 1), 0)
    qb, kwb, vb = q.astype(BF16), kw.astype(BF16), v.astype(BF16)
    h_rows = []
    for t in range(TB):
        c_old = c_ref[t, 0]
        qc = _dot(qb, c_old.astype(BF16))[t:t + 1, :]
        h_rows.append((decay[t:t + 1, :] * qc + s[t:t + 1, :] * v[t:t + 1, :]) * scale[t:t + 1, :])
        outer = _dot_tn(jnp.where(rows == t, kwb, jnp.zeros_like(kwb)), vb)
        c_out_ref[t, 0] = decay[t:t + 1, :] * c_old + outer
    h = jnp.concatenate(h_rows, axis=0)
    mu = jnp.mean(h, axis=1, keepdims=True)
    hc = h - mu
    var = jnp.mean(hc * hc, axis=1, keepdims=True)
    hn = hc * lax.rsqrt(var + LN_EPS) * ang_ref[...]
    ya_ref[...] = (hn * _sigmoid(ao_ref[...]) * _silu(az_ref[...])).astype(ya_ref.dtype)


def _gla_decode_step(bq_ref, bk_ref, bv_ref, bz_ref, g_ref, wg_ref, bgate_ref, bng_ref, s_ref,
                     yb_ref, s_out_ref):
    TB = DEC_TOKENS
    gq = bq_ref[...] * (B_DK ** -0.5)
    gk = bk_ref[...]
    gv = bv_ref[...]
    loga = _gla_gate_log(g_ref[...], wg_ref[...], bgate_ref[...])
    eb = jnp.exp(loga)
    a = jnp.sum(gq * gk, axis=1, keepdims=True)
    ebt = jnp.concatenate([eb, jnp.zeros((LANES - TB, B_DK), F32)], axis=0).T
    rows = lax.broadcasted_iota(jnp.int32, (TB, 1), 0)
    qeb, gkb, gvb = (gq * eb).astype(BF16), gk.astype(BF16), gv.astype(BF16)
    o_rows = []
    for t in range(TB):
        s_old = s_ref[t, 0]
        o_rows.append(_dot(qeb, s_old.astype(BF16))[t:t + 1, :] + a[t:t + 1, :] * gv[t:t + 1, :])
        outer = _dot_tn(jnp.where(rows == t, gkb, jnp.zeros_like(gkb)), gvb)
        s_out_ref[t, 0] = ebt[:, t:t + 1] * s_old + outer
    o = jnp.concatenate(o_rows, axis=0)
    on = o * lax.rsqrt(jnp.mean(o * o, axis=1, keepdims=True) + LN_EPS) * bng_ref[...]
    yb_ref[...] = (on * _silu(bz_ref[...])).astype(yb_ref.dtype)


def _out_kernel(ya_ref, yb_ref, ga_ref, gb_ref, x_ref, yas_ref, ybs_ref, gas_ref, gbs_ref, xs_ref,
                wpa_ref, wpb_ref, wo_ref, lng_ref, lnb_ref, o_ref, os_ref):
    def rows(ya, yb, ga, gb, x):
        pa = _dot(ya[...], wpa_ref[...])
        pb = _dot(yb[...], wpb_ref[...])
        merged = _sigmoid(ga[...]) * pa + _sigmoid(gb[...]) * pb
        r = ALPHA * x[...] + _dot(merged.astype(BF16), wo_ref[...])
        mu = jnp.mean(r, axis=1, keepdims=True)
        rc = r - mu
        var = jnp.mean(rc * rc, axis=1, keepdims=True)
        return rc * lax.rsqrt(var + LN_EPS) * lng_ref[...] + lnb_ref[...]

    last = pl.num_programs(0) - 1

    @pl.when(pl.program_id(0) < last)
    def _():
        o_ref[...] = rows(ya_ref, yb_ref, ga_ref, gb_ref, x_ref)

    @pl.when(pl.program_id(0) == last)
    def _():
        os_ref[...] = rows(yas_ref, ybs_ref, gas_ref, gbs_ref, xs_ref)


def _out(ya, yb, p_all, x, ya_s, yb_s, p_s, xs, w_pa, w_pb, w_out, ln_g, ln_b, tm):
    t, ts = x.shape[0], xs.shape[0]
    n_tiles = t // tm
    const = lambda i: (0, 0)
    single = pl.Buffered(1)

    def tile(i):
        return jnp.minimum(i, n_tiles - 1)

    in_specs = [
        pl.BlockSpec((tm, A_WIDTH), lambda i: (tile(i), 0)),
        pl.BlockSpec((tm, B_WIDTH), lambda i: (tile(i), 0)),
        pl.BlockSpec((tm, D_MODEL), lambda i: (tile(i), COL_GA // D_MODEL)),
        pl.BlockSpec((tm, D_MODEL), lambda i: (tile(i), COL_GB // D_MODEL)),
        pl.BlockSpec((tm, D_MODEL), lambda i: (tile(i), 0)),
        pl.BlockSpec((ts, A_WIDTH), const),
        pl.BlockSpec((ts, B_WIDTH), const),
        pl.BlockSpec((ts, D_MODEL), lambda i: (0, COL_GA // D_MODEL)),
        pl.BlockSpec((ts, D_MODEL), lambda i: (0, COL_GB // D_MODEL)),
        pl.BlockSpec((ts, D_MODEL), const),
        pl.BlockSpec((A_WIDTH, D_MODEL), const, pipeline_mode=single),
        pl.BlockSpec((B_WIDTH, D_MODEL), const, pipeline_mode=single),
        pl.BlockSpec((D_MODEL, D_MODEL), const, pipeline_mode=single),
        pl.BlockSpec((1, D_MODEL), const),
        pl.BlockSpec((1, D_MODEL), const),
    ]
    return pl.pallas_call(
        _out_kernel,
        grid=(n_tiles + 1,),
        in_specs=in_specs,
        out_specs=[pl.BlockSpec((tm, D_MODEL), lambda i: (tile(i), 0)),
                   pl.BlockSpec((ts, D_MODEL), const)],
        out_shape=[jax.ShapeDtypeStruct((t, D_MODEL), F32), jax.ShapeDtypeStruct((ts, D_MODEL), F32)],
        compiler_params=pltpu.CompilerParams(
            dimension_semantics=("arbitrary",), vmem_limit_bytes=VMEM_LIMIT),
        name="outproj",
    )(ya, yb, p_all, p_all, x, ya_s, yb_s, p_s, p_s, xs, w_pa, w_pb, w_out, ln_g, ln_b)


RELAYOUT_TN = 1024


def _relayout_kernel(src_ref, gif_ref, gb0_ref, gb1_ref, xs_ref, o_ref, ps_ref):
    j = pl.program_id(0)
    tn = RELAYOUT_TN
    n_main = COL_GATE // tn

    @pl.when(j < n_main)
    def _():
        o_ref[...] = src_ref[...].astype(BF16)

    @pl.when(j == n_main)
    def _():
        pad = jnp.zeros((tn - 2 * A_HEADS - GATE_RANK, src_ref.shape[1]), F32)
        o_ref[...] = jnp.concatenate([gif_ref[...], gb0_ref[...], gb1_ref[...], pad], axis=0).astype(BF16)

    ps_ref[...] = _dot_nt(xs_ref[...].astype(BF16), o_ref[...])


def _relayout_w_in(w_in_t, xs):
    n_src, k = w_in_t.shape
    ts = xs.shape[0]
    tn = RELAYOUT_TN

    def src_row(j):
        shift = 0
        for first, last, src in RELAYOUT_RUNS:
            assert first % tn == 0 and last % tn == 0 and (src - first) % SUBLANES == 0
            shift = jnp.where((j >= first // tn) & (j < last // tn), (src - first) // SUBLANES, shift)
        group = jnp.minimum(j * (tn // SUBLANES) + shift, (n_src - tn) // SUBLANES)
        return group * SUBLANES

    return pl.pallas_call(
        _relayout_kernel,
        grid=(pl.cdiv(N_PROJ, tn),),
        in_specs=[pl.BlockSpec((pl.Element(tn), pl.Element(k)), lambda j: (src_row(j), 0)),
                  pl.BlockSpec((SUBLANES, k), lambda j: (SRC_I // SUBLANES, 0)),
                  pl.BlockSpec((SUBLANES, k), lambda j: (SRC_BG // SUBLANES, 0)),
                  pl.BlockSpec((SUBLANES, k), lambda j: (SRC_BG // SUBLANES + 1, 0)),
                  pl.BlockSpec((ts, k), lambda j: (0, 0))],
        out_specs=[pl.BlockSpec((tn, k), lambda j: (j, 0)),
                   pl.BlockSpec((ts, tn), lambda j: (0, j))],
        out_shape=[jax.ShapeDtypeStruct((N_PROJ, k), BF16), jax.ShapeDtypeStruct((ts, N_PROJ), F32)],
        compiler_params=pltpu.CompilerParams(
            dimension_semantics=("parallel",), vmem_limit_bytes=VMEM_LIMIT),
        name="relayout",
    )(w_in_t, w_in_t, w_in_t, w_in_t, xs)


def kernel(x_prompt, x_sample, state_mlstm_C, state_mlstm_n, state_mlstm_m, state_conv, state_gla_S,
           w_in, conv_w, conv_b, b_i, b_f, a_norm_g, w_gate_up, b_gate, b_norm_g, w_pa, w_pb, w_out,
           ln_g, ln_b):
    batch, seq, _ = x_prompt.shape
    nseq = x_sample.shape[0]
    assert w_in.shape[0] == 1, "single-layer step"
    d = 0

    def layer(a):
        return a.reshape(a.shape[1:])

    xs = x_sample.reshape(nseq, D_MODEL)
    wp, ps_all = _relayout_w_in(layer(w_in).T, xs)
    cw = conv_w[d]
    cb = conv_b[d][None, :]
    gate_bias = jnp.concatenate([b_i[d], b_f[d], jnp.zeros((LANES - 2 * A_HEADS,), F32)])[None, :]
    ang = a_norm_g[d][None, :]
    bng = b_norm_g[d][None, :]
    wg = w_gate_up[d]
    bgate = b_gate[d][None, :]
    lng, lnb = ln_g[d][None, :], ln_b[d][None, :]

    xp = x_prompt.reshape(batch * seq, D_MODEL)
    p_all = _proj(xp, wp, tm=1024, tn=PROJ_TN)

    conv_state = state_conv.reshape(nseq, (CONV_W - 1) * 2 * A_WIDTH)
    ya, p_c, p_n, p_m, ya_s, s_c, s_n, s_m = _mlstm(
        p_all, batch, seq, cw, cb, gate_bias, ang, ps_all, conv_state,
        layer(state_mlstm_C), state_mlstm_n.reshape(nseq, A_HEADS, 1, A_DK), layer(state_mlstm_m))
    yb, p_s, yb_s, s_s, wpa, wpb, wo = _gla(p_all, batch, seq, wg, bgate, bng, ps_all, layer(state_gla_S),
                                            (layer(w_pa), layer(w_pb), layer(w_out)))

    y_prompt, y_sample = _out(ya, yb, p_all, xp, ya_s, yb_s, ps_all, xs, wpa, wpb, wo, lng, lnb, tm=256)
    y_prompt = y_prompt.reshape(batch, seq, D_MODEL)
    y_sample = y_sample.reshape(nseq, 1, D_MODEL)
    p_conv = p_all.reshape(batch, seq, N_PROJ)[:, seq - (CONV_W - 1):, :2 * A_WIDTH]
    s_conv = jnp.concatenate([layer(state_conv)[:, 1:, :], ps_all[:, None, :2 * A_WIDTH]], axis=1)

    def stacked(a):
        return a.reshape((1,) + a.shape)

    return (y_prompt, y_sample,
            stacked(p_c), p_n.reshape(1, batch, A_HEADS, A_DK), stacked(p_m[:, :, 0, 0]),
            stacked(p_conv), stacked(p_s),
            stacked(s_c), s_n.reshape(1, nseq, A_HEADS, A_DK),
            stacked(s_m.reshape(nseq, A_HEADS, LANES)[:, :, 0]), stacked(s_conv), stacked(s_s))
```

```python
import functools

import jax
import jax.numpy as jnp
from jax import lax
from jax.experimental import pallas as pl
from jax.experimental.pallas import tpu as pltpu

F32 = jnp.float32
BF16 = jnp.bfloat16

D_MODEL = 2048
A_HEADS = 4
A_WIDTH = 1024
A_DK = 256
A_DV = 256
CONV_W = 4
B_HEADS = 4
B_WIDTH = 1024
B_KWIDTH = 512
B_DK = 128
B_DV = 256
GATE_RANK = 16
GATE_TAU = 16.0
ALPHA = 2.0 ** 0.25
LN_EPS = 1e-5
LOG2_E = 1.4426950408889634
LN_2 = 0.6931471805599453

LANES = 128
SUBLANES = 8
VMEM_LIMIT = 48 * 1024 * 1024

COL_Q = 0
COL_K = A_WIDTH
COL_BQ = 2 * A_WIDTH
COL_BK = COL_BQ + B_KWIDTH
COL_AV = COL_BK + B_KWIDTH
COL_AO = COL_AV + A_WIDTH
COL_AZ = COL_AO + A_WIDTH
COL_BV = COL_AZ + A_WIDTH
COL_BZ = COL_BV + B_WIDTH
COL_GA = COL_BZ + B_WIDTH
COL_GB = COL_GA + D_MODEL
COL_GATE = COL_GB + D_MODEL
GATE_LANE_I = 0
GATE_LANE_F = A_HEADS
GATE_LANE_LR = 2 * A_HEADS
MXU_COLS = 256
N_PROJ = COL_GATE + MXU_COLS
PROJ_TN = N_PROJ // 7
assert PROJ_TN * 7 == N_PROJ and PROJ_TN % MXU_COLS == 0
PROJ_VMEM_LIMIT = 58 * 1024 * 1024

SRC_QK = 0
SRC_AV = 2 * A_WIDTH
SRC_I = SRC_AV + A_WIDTH
SRC_AO = SRC_I + 2 * A_HEADS
SRC_BQ = SRC_AO + 2 * A_WIDTH
SRC_BV = SRC_BQ + 2 * B_KWIDTH
SRC_BG = SRC_BV + B_WIDTH
SRC_BZ = SRC_BG + GATE_RANK
RELAYOUT_RUNS = ((COL_Q, COL_BQ, SRC_QK), (COL_BQ, COL_AV, SRC_BQ), (COL_AV, COL_AO, SRC_AV),
                 (COL_AO, COL_BV, SRC_AO), (COL_BV, COL_BZ, SRC_BV), (COL_BZ, COL_GATE, SRC_BZ))

MLSTM_CHUNK = 256
GLA_STEP = 256
GLA_CHUNK = 64
GLA_BLOCK = 16


def _dot(a, b):
    return jnp.dot(a, b, preferred_element_type=F32)


def _dot_nt(a, b):
    return lax.dot_general(a, b, (((1,), (1,)), ((), ())), preferred_element_type=F32)


def _dot_tn(a, b):
    return lax.dot_general(a, b, (((0,), (0,)), ((), ())), preferred_element_type=F32)


def _mask_dot(mask_bf16, x):
    hi = x.astype(BF16)
    r1 = x - hi.astype(F32)
    mid = r1.astype(BF16)
    lo = (r1 - mid.astype(F32)).astype(BF16)
    return _dot(mask_bf16, hi) + _dot(mask_bf16, mid) + _dot(mask_bf16, lo)


def _log_sigmoid(z):
    return jnp.minimum(z, 0.0) - LN_2 * jnp.log2(1.0 + jnp.exp2(jnp.abs(z) * (-LOG2_E)))


def _sigmoid(z):
    return 1.0 / (1.0 + jnp.exp2(z * (-LOG2_E)))


def _silu(z):
    return z * _sigmoid(z)


PROJ_RING = 3


def _proj_kernel(x_ref, w_hbm, o_ref, wbuf_ref, sem_ref, *, tn, n_col):
    step = pl.program_id(0) * n_col + pl.program_id(1)
    total = pl.num_programs(0) * n_col

    def tile_copy(s):
        slot = s % PROJ_RING
        row0 = pl.multiple_of((s % n_col) * tn, tn)
        return pltpu.make_async_copy(w_hbm.at[pl.ds(row0, tn), :], wbuf_ref.at[slot], sem_ref.at[slot])

    @pl.when(step == 0)
    def _():
        for ahead in range(PROJ_RING - 1):
            tile_copy(step + ahead).start()

    @pl.when(step + (PROJ_RING - 1) < total)
    def _():
        tile_copy(step + (PROJ_RING - 1)).start()

    tile_copy(step).wait()
    o_ref[...] = _dot_nt(x_ref[...].astype(BF16), wbuf_ref[step % PROJ_RING])


def _proj(x, w_t, tm, tn):
    t, k = x.shape
    n = w_t.shape[0]
    n_col = n // tn
    return pl.pallas_call(
        functools.partial(_proj_kernel, tn=tn, n_col=n_col),
        grid=(t // tm, n_col),
        in_specs=[pl.BlockSpec((tm, k), lambda i, j: (i, 0)),
                  pl.BlockSpec(memory_space=pl.ANY)],
        out_specs=pl.BlockSpec((tm, tn), lambda i, j: (i, j)),
        out_shape=jax.ShapeDtypeStruct((t, n), F32),
        scratch_shapes=[pltpu.VMEM((PROJ_RING, tn, k), BF16), pltpu.SemaphoreType.DMA((PROJ_RING,))],
        compiler_params=pltpu.CompilerParams(
            dimension_semantics=("arbitrary", "arbitrary"), vmem_limit_bytes=PROJ_VMEM_LIMIT),
        name="proj",
    )(x, w_t)


def _mlstm_kernel(qp_ref, kp_ref, v_ref, ao_ref, az_ref, g_ref, cwq_ref, cwk_ref, cbq_ref, cbk_ref,
                  gb_ref, ng_ref, *rest, n_chunks):
    dec_in, (ya_ref, c_ref, n_ref, m_ref), dec_out, (tailq_ref, tailk_ref) = (
        rest[:21], rest[21:25], rest[25:29], rest[29:])
    L = MLSTM_CHUNK
    c = pl.program_id(2)

    @pl.when(c == 0)
    def _():
        c_ref[...] = jnp.zeros_like(c_ref)
        n_ref[...] = jnp.zeros_like(n_ref)
        m_ref[...] = jnp.zeros_like(m_ref)
        tailq_ref[...] = jnp.zeros_like(tailq_ref)
        tailk_ref[...] = jnp.zeros_like(tailk_ref)

    _mlstm_decode_step((pl.program_id(0) * n_chunks + c) % A_HEADS, *dec_in, *dec_out)

    sub = lax.broadcasted_iota(jnp.int32, (1, SUBLANES, 1), 1)

    def conv_silu(x_ref, tail_ref, w_ref, b_ref):
        x = x_ref[...]
        width = x.shape[1]
        x3 = jnp.concatenate([tail_ref[...], x], axis=0).reshape(L // SUBLANES + 1, SUBLANES, width)
        acc = b_ref[...] + x * w_ref[CONV_W - 1:CONV_W, :]
        for j in range(1, CONV_W):
            rot = pltpu.roll(x3, j, axis=1)
            xs = jnp.where(sub < j, rot[:-1], rot[1:]).reshape(L, width)
            acc = acc + xs * w_ref[CONV_W - 1 - j:CONV_W - j, :]
        tail_ref[...] = x[L - SUBLANES:, :]
        return _silu(acc)

    q_all = conv_silu(qp_ref, tailq_ref, cwq_ref, cbq_ref)
    k_all = conv_silu(kp_ref, tailk_ref, cwk_ref, cbk_ref) * (A_DK ** -0.5)

    lane = lax.broadcasted_iota(jnp.int32, (L, LANES), 1)
    row = lax.broadcasted_iota(jnp.int32, (L, L), 0)
    col = lax.broadcasted_iota(jnp.int32, (L, L), 1)
    causal = row >= col
    causal_b = causal.astype(BF16)

    g = g_ref[...] + gb_ref[...]
    is_f = (lane >= GATE_LANE_F) & (lane < GATE_LANE_F + A_HEADS)
    g2 = jnp.where(is_f, _log_sigmoid(g), g)
    x2 = jnp.where(is_f, _mask_dot(causal_b, g2), g2)
    x2t = x2.T

    for hh in range(A_HEADS):
        cols = slice(hh * A_DK, (hh + 1) * A_DK)
        q, k, v = q_all[:, cols], k_all[:, cols], v_ref[:, cols]
        li, lf = GATE_LANE_I + hh, GATE_LANE_F + hh
        itil_col, b_col = x2[:, li:li + 1], x2[:, lf:lf + 1]
        itil_row, b_row = x2t[li:li + 1, :], x2t[lf:lf + 1, :]

        m_prev = m_ref[0, hh, 0:1, 0:1]
        dmat = jnp.where(causal, b_col - b_row + itil_row, -jnp.inf)
        inter = b_col + m_prev
        m_t = jnp.maximum(inter, jnp.max(dmat, axis=1, keepdims=True))
        w = jnp.exp(dmat - m_t)
        decay = jnp.exp(inter - m_t)

        qb, kb, vb = q.astype(BF16), k.astype(BF16), v.astype(BF16)
        c_old = c_ref[0, hh]
        n_old = n_ref[0, hh]
        s = _dot_nt(qb, kb) * w
        num = decay * _dot(qb, c_old.astype(BF16)) + _dot(s.astype(BF16), vb)
        den = decay * jnp.sum(q * n_old, axis=1, keepdims=True) + jnp.sum(s, axis=1, keepdims=True)
        inv = 1.0 / jnp.maximum(jnp.abs(den), jnp.exp(-m_t))

        m_new = m_t[L - 1:L, :]
        b_last = b_col[L - 1:L, :]
        wk = jnp.exp(b_last - b_col + itil_col - m_new)
        dec = jnp.exp(b_last + m_prev - m_new)
        kw = k * wk
        c_ref[0, hh] = dec * c_old + _dot_tn(kw.astype(BF16), vb)
        n_ref[0, hh] = dec * n_old + jnp.sum(kw, axis=0, keepdims=True)
        m_ref[0, hh] = jnp.broadcast_to(m_new, (1, LANES))

        mu = jnp.mean(num, axis=1, keepdims=True)
        hc = num - mu
        var = jnp.mean(hc * hc, axis=1, keepdims=True)
        hn = hc * (inv * lax.rsqrt(var * (inv * inv) + LN_EPS)) * ng_ref[:, cols]
        ya_ref[:, cols] = (hn * _sigmoid(ao_ref[:, cols]) * _silu(az_ref[:, cols])).astype(ya_ref.dtype)


def _mlstm(p_all, batch, seq, conv_w, conv_b, gate_bias, a_norm_g, p_s, conv_state, c_state, n_state, m_state):
    L = MLSTM_CHUNK
    hps = A_HEADS
    nc = seq // L
    t = batch * seq
    wd = hps * A_DK
    k_off = A_WIDTH // wd

    def rows(b, h, c):
        return b * nc + c

    def pcol(base):
        return pl.BlockSpec((L, wd), lambda b, h, c: (rows(b, h, c), base // wd + h))

    in_specs = [
        pcol(COL_Q), pcol(COL_K), pcol(COL_AV), pcol(COL_AO), pcol(COL_AZ),
        pl.BlockSpec((L, LANES), lambda b, h, c: (rows(b, h, c), COL_GATE // LANES)),
        pl.BlockSpec((CONV_W, wd), lambda b, h, c: (0, h)),
        pl.BlockSpec((CONV_W, wd), lambda b, h, c: (0, k_off + h)),
        pl.BlockSpec((1, wd), lambda b, h, c: (0, h)),
        pl.BlockSpec((1, wd), lambda b, h, c: (0, k_off + h)),
        pl.BlockSpec((1, LANES), lambda b, h, c: (0, 0)),
        pl.BlockSpec((1, wd), lambda b, h, c: (0, h)),
    ]
    out_specs = [
        pl.BlockSpec((L, wd), lambda b, h, c: (rows(b, h, c), h)),
        pl.BlockSpec((1, hps, A_DK, A_DV), lambda b, h, c: (b, h, 0, 0)),
        pl.BlockSpec((1, hps, 1, A_DK), lambda b, h, c: (b, h, 0, 0)),
        pl.BlockSpec((1, hps, 1, LANES), lambda b, h, c: (b, h, 0, 0)),
    ]
    out_shape = [
        jax.ShapeDtypeStruct((t, A_WIDTH), BF16),
        jax.ShapeDtypeStruct((batch, A_HEADS, A_DK, A_DV), F32),
        jax.ShapeDtypeStruct((batch, A_HEADS, 1, A_DK), F32),
        jax.ShapeDtypeStruct((batch, A_HEADS, 1, LANES), F32),
    ]

    nseq = p_s.shape[0]
    TB = DEC_TOKENS
    dstep = _decode_step(batch, nc, nseq)

    def dcol(base, width):
        return pl.BlockSpec((TB, width), lambda b, h, c: (dstep(b, h, c)[0], base // width + dstep(b, h, c)[1]))

    def dconv(j, base):
        return pl.BlockSpec((TB, A_DK), lambda b, h, c: (dstep(b, h, c)[0],
                                                        (j * 2 * A_WIDTH + base) // A_DK + dstep(b, h, c)[1]))

    def dhead(shape, off=0):
        return pl.BlockSpec(shape, lambda b, h, c: (0, off + dstep(b, h, c)[1]))

    def dstate(*dims):
        return pl.BlockSpec((TB, 1) + dims, lambda b, h, c: dstep(b, h, c) + (0,) * len(dims))

    in_specs += [
        dcol(COL_Q, A_DK), dcol(COL_K, A_DK), dcol(COL_AV, A_DV), dcol(COL_AO, A_DV), dcol(COL_AZ, A_DV),
        pl.BlockSpec((TB, LANES), lambda b, h, c: (dstep(b, h, c)[0], COL_GATE // LANES)),
        dconv(0, 0), dconv(1, 0), dconv(2, 0), dconv(0, A_WIDTH), dconv(1, A_WIDTH), dconv(2, A_WIDTH),
        dhead((CONV_W, A_DK)), dhead((CONV_W, A_DK), A_HEADS), dhead((1, A_DK)), dhead((1, A_DK), A_HEADS),
        pl.BlockSpec((1, LANES), lambda b, h, c: (0, 0)), dhead((1, A_DV)),
        dstate(A_DK, A_DV), dstate(1, A_DK),
        pl.BlockSpec((TB, A_HEADS), lambda b, h, c: (dstep(b, h, c)[0], 0)),
    ]
    out_specs += [
        pl.BlockSpec((TB, A_DV), lambda b, h, c: dstep(b, h, c)),
        dstate(A_DK, A_DV), dstate(1, A_DK),
        pl.BlockSpec((TB, LANES), lambda b, h, c: dstep(b, h, c)),
    ]
    out_shape += [
        jax.ShapeDtypeStruct((nseq, A_WIDTH), BF16),
        jax.ShapeDtypeStruct((nseq, A_HEADS, A_DK, A_DV), F32),
        jax.ShapeDtypeStruct((nseq, A_HEADS, 1, A_DK), F32),
        jax.ShapeDtypeStruct((nseq, A_HEADS * LANES), F32),
    ]
    return pl.pallas_call(
        functools.partial(_mlstm_kernel, n_chunks=nc),
        grid=(batch, A_HEADS // hps, nc),
        in_specs=in_specs,
        out_specs=out_specs,
        out_shape=out_shape,
        scratch_shapes=[pltpu.VMEM((SUBLANES, wd), F32),
                        pltpu.VMEM((SUBLANES, wd), F32)],
        compiler_params=pltpu.CompilerParams(
            dimension_semantics=("parallel", "parallel", "arbitrary"), vmem_limit_bytes=VMEM_LIMIT),
        name="mlstm",
    )(p_all, p_all, p_all, p_all, p_all, p_all, conv_w, conv_w, conv_b, conv_b, gate_bias, a_norm_g,
      p_s, p_s, p_s, p_s, p_s, p_s,
      conv_state, conv_state, conv_state, conv_state, conv_state, conv_state,
      conv_w, conv_w, conv_b, conv_b, gate_bias, a_norm_g, c_state, n_state, m_state)


def _gla_gate_log(bg_tile, wg, bgate):
    lr = bg_tile[:, GATE_LANE_LR:GATE_LANE_LR + GATE_RANK]
    z = _dot(lr.astype(BF16), wg.astype(BF16)) + bgate
    return _log_sigmoid(z) / GATE_TAU


def _gla_kernel(q_ref, k_ref, v_ref, bz_ref, g_ref, wg_ref, bgate_ref, ng_ref, *rest):
    dec_in, w_in, (yb_ref, s_out_ref), dec_out, w_out, (st_ref, ds_ref, sb_ref) = (
        rest[:9], rest[9:12], rest[12:14], rest[14:16], rest[16:19], rest[19:])
    L = GLA_STEP
    B = GLA_BLOCK
    nb = L // B
    c = pl.program_id(2)

    @pl.when(c == 0)
    def _():
        st_ref[...] = jnp.zeros_like(st_ref)

    _gla_decode_step(*dec_in, *dec_out)
    for src_ref, dst_ref in zip(w_in, w_out):
        dst_ref[...] = src_ref[...].astype(BF16)

    CH = GLA_CHUNK
    bpc, nch = CH // B, L // CH
    row = lax.broadcasted_iota(jnp.int32, (L, L), 0)
    col = lax.broadcasted_iota(jnp.int32, (L, L), 1)
    tri_c = (((row // CH) == (col // CH)) & (row >= col)).astype(BF16)
    blk3 = lax.broadcasted_iota(jnp.int32, (nb, 1, LANES), 0)
    lane_rel = lax.broadcasted_iota(jnp.int32, (nb, 1, LANES), 2) - (blk3 % bpc) * B
    t3 = lax.broadcasted_iota(jnp.int32, (1, B, 1), 1)

    for hh in range(B_HEADS):
        kcols = slice(hh * B_DK, (hh + 1) * B_DK)
        vcols = slice(hh * B_DV, (hh + 1) * B_DV)
        loga = _gla_gate_log(g_ref[...], wg_ref[:, kcols], bgate_ref[:, kcols])
        bc = _mask_dot(tri_c, loga) * LOG2_E
        bl = jnp.broadcast_to(bc.reshape(nch, CH, B_DK)[:, CH - 1:CH, :],
                              (nch, CH, B_DK)).reshape(L, B_DK)
        k = k_ref[:, kcols]
        qs = q_ref[:, kcols] * (B_DK ** -0.5)
        qt = (qs * jnp.exp2(bc)).astype(BF16)
        kt = (k * jnp.exp2(bl - bc)).astype(BF16)
        eb = jnp.exp2(bl)
        vb = v_ref[:, vcols].astype(BF16)

        bc3, q3, k3 = (x.reshape(nb, B, B_DK) for x in (bc, qs, k))
        groups = B // SUBLANES
        bcg = [bc3[:, g * SUBLANES:(g + 1) * SUBLANES, :] for g in range(groups)]
        qg = [q3[:, g * SUBLANES:(g + 1) * SUBLANES, :] for g in range(groups)]
        ag = [jnp.zeros((nb, SUBLANES, LANES), F32) for _ in range(groups)]
        for s in range(B):
            bc_s, k_s = bc3[:, s:s + 1, :], k3[:, s:s + 1, :]
            for g in range(s // SUBLANES, groups):
                e = jnp.exp2(bcg[g] - bc_s)
                a_col = jnp.sum(qg[g] * k_s * e, axis=2, keepdims=True)
                ag[g] = jnp.where(lane_rel == s, a_col, ag[g])
        a3 = jnp.concatenate(ag, axis=1)
        a3 = jnp.where((lane_rel >= 0) & (lane_rel <= t3), a3, 0.0)
        a_chunks = []
        for ci in range(nch):
            rows_c = [a3[ci * bpc]]
            for i in range(1, bpc):
                blk = ci * bpc + i
                start = bc3[blk - 1, B - 1:B, :]
                qd = (q3[blk] * jnp.exp2(bc3[blk] - start)).astype(BF16)
                ks = k3[ci * bpc:blk].reshape(i * B, B_DK)
                kd = ks * jnp.exp2(start - bc3[ci * bpc:blk].reshape(i * B, B_DK))
                kd = jnp.concatenate([kd, jnp.zeros((LANES - i * B, B_DK), F32)], axis=0).astype(BF16)
                rows_c.append(a3[blk] + _dot_nt(qd, kd))
            a_chunks.append(jnp.concatenate(rows_c, axis=0)[:, 0:CH].astype(BF16))

        for ci in range(nch):
            ds_ref[hh, ci] = _dot_tn(vb[ci * CH:(ci + 1) * CH, :], kt[ci * CH:(ci + 1) * CH, :])
        st = st_ref[hh]
        for ci in range(nch):
            sb_ref[hh, ci] = st.astype(BF16)
            st = st * eb[ci * CH:ci * CH + 1, :] + ds_ref[hh, ci]
        st_ref[hh] = st
        o = jnp.concatenate(
            [_dot(a_chunks[ci], vb[ci * CH:(ci + 1) * CH, :])
             + _dot_nt(qt[ci * CH:(ci + 1) * CH, :], sb_ref[hh, ci]) for ci in range(nch)], axis=0)

        on = o * lax.rsqrt(jnp.mean(o * o, axis=1, keepdims=True) + LN_EPS) * ng_ref[:, vcols]
        yb_ref[:, vcols] = (on * _silu(bz_ref[:, vcols])).astype(yb_ref.dtype)

    @pl.when(c == pl.num_programs(2) - 1)
    def _():
        for hh in range(B_HEADS):
            s_out_ref[0, hh] = st_ref[hh].T


def _gla(p_all, batch, seq, w_gate_up, b_gate, b_norm_g, p_s, s_state, cast_weights):
    L = GLA_STEP
    hps = B_HEADS
    nc = seq // L
    t = batch * seq
    kw, vw = hps * B_DK, hps * B_DV

    def rows(b, h, c):
        return b * nc + c

    in_specs = [
        pl.BlockSpec((L, kw), lambda b, h, c: (rows(b, h, c), COL_BQ // kw + h)),
        pl.BlockSpec((L, kw), lambda b, h, c: (rows(b, h, c), COL_BK // kw + h)),
        pl.BlockSpec((L, vw), lambda b, h, c: (rows(b, h, c), COL_BV // vw + h)),
        pl.BlockSpec((L, vw), lambda b, h, c: (rows(b, h, c), COL_BZ // vw + h)),
        pl.BlockSpec((L, LANES), lambda b, h, c: (rows(b, h, c), COL_GATE // LANES)),
        pl.BlockSpec((GATE_RANK, kw), lambda b, h, c: (0, h)),
        pl.BlockSpec((1, kw), lambda b, h, c: (0, h)),
        pl.BlockSpec((1, vw), lambda b, h, c: (0, h)),
    ]
    out_specs = [
        pl.BlockSpec((L, vw), lambda b, h, c: (rows(b, h, c), h)),
        pl.BlockSpec((1, hps, B_DK, B_DV), lambda b, h, c: (b, h, 0, 0)),
    ]
    out_shape = [
        jax.ShapeDtypeStruct((t, B_WIDTH), BF16),
        jax.ShapeDtypeStruct((batch, B_HEADS, B_DK, B_DV), F32),
    ]

    nseq = p_s.shape[0]
    TB = DEC_TOKENS
    dstep = _decode_step(batch, nc, nseq)

    def dcol(base, width):
        return pl.BlockSpec((TB, width), lambda b, h, c: (dstep(b, h, c)[0], base // width + dstep(b, h, c)[1]))

    def dhead(shape):
        return pl.BlockSpec(shape, lambda b, h, c: (0, dstep(b, h, c)[1]))

    dstate = pl.BlockSpec((TB, 1, B_DK, B_DV), lambda b, h, c: dstep(b, h, c) + (0, 0))
    in_specs += [
        dcol(COL_BQ, B_DK), dcol(COL_BK, B_DK), dcol(COL_BV, B_DV), dcol(COL_BZ, B_DV),
        pl.BlockSpec((TB, LANES), lambda b, h, c: (dstep(b, h, c)[0], COL_GATE // LANES)),
        dhead((GATE_RANK, B_DK)), dhead((1, B_DK)), dhead((1, B_DV)), dstate,
    ]
    out_specs += [pl.BlockSpec((TB, B_DV), lambda b, h, c: dstep(b, h, c)), dstate]
    out_shape += [
        jax.ShapeDtypeStruct((nseq, B_WIDTH), BF16),
        jax.ShapeDtypeStruct((nseq, B_HEADS, B_DK, B_DV), F32),
    ]

    n_steps = batch * nc
    for wmat in cast_weights:
        slab = pl.BlockSpec((wmat.shape[0] // n_steps, wmat.shape[1]), lambda b, h, c: (rows(b, h, c), 0))
        in_specs.append(slab)
        out_specs.append(slab)
        out_shape.append(jax.ShapeDtypeStruct(wmat.shape, BF16))
    return pl.pallas_call(
        _gla_kernel,
        grid=(batch, B_HEADS // hps, nc),
        in_specs=in_specs,
        out_specs=out_specs,
        out_shape=out_shape,
        scratch_shapes=[pltpu.VMEM((hps, B_DV, B_DK), F32),
                        pltpu.VMEM((hps, L // GLA_CHUNK, B_DV, B_DK), F32),
                        pltpu.VMEM((hps, L // GLA_CHUNK, B_DV, B_DK), BF16)],
        compiler_params=pltpu.CompilerParams(
            dimension_semantics=("parallel", "parallel", "arbitrary"), vmem_limit_bytes=VMEM_LIMIT),
        name="gla",
    )(p_all, p_all, p_all, p_all, p_all, w_gate_up, b_gate, b_norm_g,
      p_s, p_s, p_s, p_s, p_s, w_gate_up, b_gate, b_norm_g, s_state, *cast_weights)


DEC_TOKENS = 16


def _decode_step(batch, nc, nseq):
    assert batch * nc == (nseq // DEC_TOKENS) * A_HEADS and A_HEADS == B_HEADS

    def step(b, h, c):
        s = b * nc + c
        return s // A_HEADS, s % A_HEADS
    return step


def _mlstm_decode_step(h_idx, qp_ref, kp_ref, av_ref, ao_ref, az_ref, g_ref,
                       sq0_ref, sq1_ref, sq2_ref, sk0_ref, sk1_ref, sk2_ref,
                       cwq_ref, cwk_ref, cbq_ref, cbk_ref, gb_ref, ang_ref, c_ref, n_ref, m_ref,
                       ya_ref, c_out_ref, n_out_ref, m_out_ref):
    TB = DEC_TOKENS

    def conv_silu(s0, s1, s2, x, w_ref, b_ref):
        acc = b_ref[...] + s0[...] * w_ref[0:1, :]
        acc = acc + s1[...] * w_ref[1:2, :]
        acc = acc + s2[...] * w_ref[2:3, :]
        acc = acc + x[...] * w_ref[3:4, :]
        return _silu(acc)

    q = conv_silu(sq0_ref, sq1_ref, sq2_ref, qp_ref, cwq_ref, cbq_ref)
    k = conv_silu(sk0_ref, sk1_ref, sk2_ref, kp_ref, cwk_ref, cbk_ref) * (A_DK ** -0.5)
    v = av_ref[...]
    g = g_ref[...]
    gbias = gb_ref[...]
    gate_lane = lax.broadcasted_iota(jnp.int32, (TB, LANES), 1)
    gsum = g + gbias

    def gate(lane0):
        return jnp.sum(jnp.where(gate_lane == lane0 + h_idx, gsum, 0.0), axis=1, keepdims=True)

    itil = gate(GATE_LANE_I)
    logf = _log_sigmoid(gate(GATE_LANE_F))
    lane_h = lax.broadcasted_iota(jnp.int32, (TB, A_HEADS), 1)
    m_prev = jnp.sum(jnp.where(lane_h == h_idx, m_ref[...], 0.0), axis=1, keepdims=True)
    inter = logf + m_prev
    m_t = jnp.maximum(inter, itil)
    w = jnp.exp(itil - m_t)
    decay = jnp.exp(inter - m_t)
    n_old = n_ref[:, 0, 0, :]
    s = jnp.sum(q * k, axis=1, keepdims=True) * w
    den = decay * jnp.sum(q * n_old, axis=1, keepdims=True) + s
    scale = 1.0 / jnp.maximum(jnp.abs(den), jnp.exp(-m_t))
    kw = k * w
    n_out_ref[:, 0, 0, :] = decay * n_old + kw
    m_out_ref[...] = jnp.broadcast_to(m_t, (TB, LANES))

    rows = lax.broadcasted_iota(jnp.int32, (TB, 1), 0)
    qb, kwb, vb = q.astype(BF16), kw.astype(BF16), v.astype(BF16)
    h_rows = []
    for t in range(TB):
        c_old = c_ref[t, 0]
        qc = _dot(qb, c_old.astype(BF16))[t:t + 1, :]
        h_rows.append((decay[t:t + 1, :] * qc + s[t:t + 1, :] * v[t:t + 1, :]) * scale[t:t + 1, :])
        outer = _dot_tn(jnp.where(rows == t, kwb, jnp.zeros_like(kwb)), vb)
        c_out_ref[t, 0] = decay[t:t + 1, :] * c_old + outer
    h = jnp.concatenate(h_rows, axis=0)
    mu = jnp.mean(h, axis=1, keepdims=True)
    hc = h - mu
    var = jnp.mean(hc * hc, axis=1, keepdims=True)
    hn = hc * lax.rsqrt(var + LN_EPS) * ang_ref[...]
    ya_ref[...] = (hn * _sigmoid(ao_ref[...]) * _silu(az_ref[...])).astype(ya_ref.dtype)


def _gla_decode_step(bq_ref, bk_ref, bv_ref, bz_ref, g_ref, wg_ref, bgate_ref, bng_ref, s_ref,
                     yb_ref, s_out_ref):
    TB = DEC_TOKENS
    gq = bq_ref[...] * (B_DK ** -0.5)
    gk = bk_ref[...]
    gv = bv_ref[...]
    loga = _gla_gate_log(g_ref[...], wg_ref[...], bgate_ref[...])
    eb = jnp.exp(loga)
    a = jnp.sum(gq * gk, axis=1, keepdims=True)
    ebt = jnp.concatenate([eb, jnp.zeros((LANES - TB, B_DK), F32)], axis=0).T
    rows = lax.broadcasted_iota(jnp.int32, (TB, 1), 0)
    qeb, gkb, gvb = (gq * eb).astype(BF16), gk.astype(BF16), gv.astype(BF16)
    o_rows = []
    for t in range(TB):
        s_old = s_ref[t, 0]
        o_rows.append(_dot(qeb, s_old.astype(BF16))[t:t + 1, :] + a[t:t + 1, :] * gv[t:t + 1, :])
        outer = _dot_tn(jnp.where(rows == t, gkb, jnp.zeros_like(gkb)), gvb)
        s_out_ref[t, 0] = ebt[:, t:t + 1] * s_old + outer
    o = jnp.concatenate(o_rows, axis=0)
    on = o * lax.rsqrt(jnp.mean(o * o, axis=1, keepdims=True) + LN_EPS) * bng_ref[...]
    yb_ref[...] = (on * _silu(bz_ref[...])).astype(yb_ref.dtype)


def _out_kernel(ya_ref, yb_ref, ga_ref, gb_ref, x_ref, yas_ref, ybs_ref, gas_ref, gbs_ref, xs_ref,
                wpa_ref, wpb_ref, wo_ref, lng_ref, lnb_ref, o_ref, os_ref):
    def rows(ya, yb, ga, gb, x):
        pa = _dot(ya[...], wpa_ref[...])
        pb = _dot(yb[...], wpb_ref[...])
        merged = _sigmoid(ga[...]) * pa + _sigmoid(gb[...]) * pb
        r = ALPHA * x[...] + _dot(merged.astype(BF16), wo_ref[...])
        mu = jnp.mean(r, axis=1, keepdims=True)
        rc = r - mu
        var = jnp.mean(rc * rc, axis=1, keepdims=True)
        return rc * lax.rsqrt(var + LN_EPS) * lng_ref[...] + lnb_ref[...]

    last = pl.num_programs(0) - 1

    @pl.when(pl.program_id(0) < last)
    def _():
        o_ref[...] = rows(ya_ref, yb_ref, ga_ref, gb_ref, x_ref)

    @pl.when(pl.program_id(0) == last)
    def _():
        os_ref[...] = rows(yas_ref, ybs_ref, gas_ref, gbs_ref, xs_ref)


def _out(ya, yb, p_all, x, ya_s, yb_s, p_s, xs, w_pa, w_pb, w_out, ln_g, ln_b, tm):
    t, ts = x.shape[0], xs.shape[0]
    n_tiles = t // tm
    const = lambda i: (0, 0)
    single = pl.Buffered(1)

    def tile(i):
        return jnp.minimum(i, n_tiles - 1)

    in_specs = [
        pl.BlockSpec((tm, A_WIDTH), lambda i: (tile(i), 0)),
        pl.BlockSpec((tm, B_WIDTH), lambda i: (tile(i), 0)),
        pl.BlockSpec((tm, D_MODEL), lambda i: (tile(i), COL_GA // D_MODEL)),
        pl.BlockSpec((tm, D_MODEL), lambda i: (tile(i), COL_GB // D_MODEL)),
        pl.BlockSpec((tm, D_MODEL), lambda i: (tile(i), 0)),
        pl.BlockSpec((ts, A_WIDTH), const),
        pl.BlockSpec((ts, B_WIDTH), const),
        pl.BlockSpec((ts, D_MODEL), lambda i: (0, COL_GA // D_MODEL)),
        pl.BlockSpec((ts, D_MODEL), lambda i: (0, COL_GB // D_MODEL)),
        pl.BlockSpec((ts, D_MODEL), const),
        pl.BlockSpec((A_WIDTH, D_MODEL), const, pipeline_mode=single),
        pl.BlockSpec((B_WIDTH, D_MODEL), const, pipeline_mode=single),
        pl.BlockSpec((D_MODEL, D_MODEL), const, pipeline_mode=single),
        pl.BlockSpec((1, D_MODEL), const),
        pl.BlockSpec((1, D_MODEL), const),
    ]
    return pl.pallas_call(
        _out_kernel,
        grid=(n_tiles + 1,),
        in_specs=in_specs,
        out_specs=[pl.BlockSpec((tm, D_MODEL), lambda i: (tile(i), 0)),
                   pl.BlockSpec((ts, D_MODEL), const)],
        out_shape=[jax.ShapeDtypeStruct((t, D_MODEL), F32), jax.ShapeDtypeStruct((ts, D_MODEL), F32)],
        compiler_params=pltpu.CompilerParams(
            dimension_semantics=("arbitrary",), vmem_limit_bytes=VMEM_LIMIT),
        name="outproj",
    )(ya, yb, p_all, p_all, x, ya_s, yb_s, p_s, p_s, xs, w_pa, w_pb, w_out, ln_g, ln_b)


RELAYOUT_TN = 1024


def _relayout_kernel(src_ref, gif_ref, gb0_ref, gb1_ref, xs_ref, o_ref, ps_ref):
    j = pl.program_id(0)
    tn = RELAYOUT_TN
    n_main = COL_GATE // tn

    @pl.when(j < n_main)
    def _():
        o_ref[...] = src_ref[...].astype(BF16)

    @pl.when(j == n_main)
    def _():
        pad = jnp.zeros((tn - 2 * A_HEADS - GATE_RANK, src_ref.shape[1]), F32)
        o_ref[...] = jnp.concatenate([gif_ref[...], gb0_ref[...], gb1_ref[...], pad], axis=0).astype(BF16)

    ps_ref[...] = _dot_nt(xs_ref[...].astype(BF16), o_ref[...])


def _relayout_w_in(w_in_t, xs):
    n_src, k = w_in_t.shape
    ts = xs.shape[0]
    tn = RELAYOUT_TN

    def src_row(j):
        shift = 0
        for first, last, src in RELAYOUT_RUNS:
            assert first % tn == 0 and last % tn == 0 and (src - first) % SUBLANES == 0
            shift = jnp.where((j >= first // tn) & (j < last // tn), (src - first) // SUBLANES, shift)
        group = jnp.minimum(j * (tn // SUBLANES) + shift, (n_src - tn) // SUBLANES)
        return group * SUBLANES

    return pl.pallas_call(
        _relayout_kernel,
        grid=(pl.cdiv(N_PROJ, tn),),
        in_specs=[pl.BlockSpec((pl.Element(tn), pl.Element(k)), lambda j: (src_row(j), 0)),
                  pl.BlockSpec((SUBLANES, k), lambda j: (SRC_I // SUBLANES, 0)),
                  pl.BlockSpec((SUBLANES, k), lambda j: (SRC_BG // SUBLANES, 0)),
                  pl.BlockSpec((SUBLANES, k), lambda j: (SRC_BG // SUBLANES + 1, 0)),
                  pl.BlockSpec((ts, k), lambda j: (0, 0))],
        out_specs=[pl.BlockSpec((tn, k), lambda j: (j, 0)),
                   pl.BlockSpec((ts, tn), lambda j: (0, j))],
        out_shape=[jax.ShapeDtypeStruct((N_PROJ, k), BF16), jax.ShapeDtypeStruct((ts, N_PROJ), F32)],
        compiler_params=pltpu.CompilerParams(
            dimension_semantics=("parallel",), vmem_limit_bytes=VMEM_LIMIT),
        name="relayout",
    )(w_in_t, w_in_t, w_in_t, w_in_t, xs)


def kernel(x_prompt, x_sample, state_mlstm_C, state_mlstm_n, state_mlstm_m, state_conv, state_gla_S,
           w_in, conv_w, conv_b, b_i, b_f, a_norm_g, w_gate_up, b_gate, b_norm_g, w_pa, w_pb, w_out,
           ln_g, ln_b):
    batch, seq, _ = x_prompt.shape
    nseq = x_sample.shape[0]
    assert w_in.shape[0] == 1, "single-layer step"
    d = 0

    def layer(a):
        return a.reshape(a.shape[1:])

    xs = x_sample.reshape(nseq, D_MODEL)
    wp, ps_all = _relayout_w_in(layer(w_in).T, xs)
    cw = conv_w[d]
    cb = conv_b[d][None, :]
    gate_bias = jnp.concatenate([b_i[d], b_f[d], jnp.zeros((LANES - 2 * A_HEADS,), F32)])[None, :]
    ang = a_norm_g[d][None, :]
    bng = b_norm_g[d][None, :]
    wg = w_gate_up[d]
    bgate = b_gate[d][None, :]
    lng, lnb = ln_g[d][None, :], ln_b[d][None, :]

    xp = x_prompt.reshape(batch * seq, D_MODEL)
    p_all = _proj(xp, wp, tm=1024, tn=PROJ_TN)

    conv_state = state_conv.reshape(nseq, (CONV_W - 1) * 2 * A_WIDTH)
    ya, p_c, p_n, p_m, ya_s, s_c, s_n, s_m = _mlstm(
        p_all, batch, seq, cw, cb, gate_bias, ang, ps_all, conv_state,
        layer(state_mlstm_C), state_mlstm_n.reshape(nseq, A_HEADS, 1, A_DK), layer(state_mlstm_m))
    yb, p_s, yb_s, s_s, wpa, wpb, wo = _gla(p_all, batch, seq, wg, bgate, bng, ps_all, layer(state_gla_S),
                                            (layer(w_pa), layer(w_pb), layer(w_out)))

    y_prompt, y_sample = _out(ya, yb, p_all, xp, ya_s, yb_s, ps_all, xs, wpa, wpb, wo, lng, lnb, tm=256)
    y_prompt = y_prompt.reshape(batch, seq, D_MODEL)
    y_sample = y_sample.reshape(nseq, 1, D_MODEL)
    p_conv = p_all.reshape(batch, seq, N_PROJ)[:, seq - (CONV_W - 1):, :2 * A_WIDTH]
    s_conv = jnp.concatenate([layer(state_conv)[:, 1:, :], ps_all[:, None, :2 * A_WIDTH]], axis=1)

    def stacked(a):
        return a.reshape((1,) + a.shape)

    return (y_prompt, y_sample,
            stacked(p_c), p_n.reshape(1, batch, A_HEADS, A_DK), stacked(p_m[:, :, 0, 0]),
            stacked(p_conv), stacked(p_s),
            stacked(s_c), s_n.reshape(1, nseq, A_HEADS, A_DK),
            stacked(s_m.reshape(nseq, A_HEADS, LANES)[:, :, 0]), stacked(s_conv), stacked(s_s))
```

```python
import functools

import jax
import jax.numpy as jnp
from jax import lax
from jax.experimental import pallas as pl
from jax.experimental.pallas import tpu as pltpu

F32 = jnp.float32
BF16 = jnp.bfloat16

D_MODEL = 2048
A_HEADS = 4
A_WIDTH = 1024
A_DK = 256
A_DV = 256
CONV_W = 4
B_HEADS = 4
B_WIDTH = 1024
B_KWIDTH = 512
B_DK = 128
B_DV = 256
GATE_RANK = 16
GATE_TAU = 16.0
ALPHA = 2.0 ** 0.25
LN_EPS = 1e-5
LOG2_E = 1.4426950408889634
LN_2 = 0.6931471805599453

LANES = 128
SUBLANES = 8
VMEM_LIMIT = 48 * 1024 * 1024

COL_Q = 0
COL_K = A_WIDTH
COL_BQ = 2 * A_WIDTH
COL_BK = COL_BQ + B_KWIDTH
COL_AV = COL_BK + B_KWIDTH
COL_AO = COL_AV + A_WIDTH
COL_AZ = COL_AO + A_WIDTH
COL_BV = COL_AZ + A_WIDTH
COL_BZ = COL_BV + B_WIDTH
COL_GA = COL_BZ + B_WIDTH
COL_GB = COL_GA + D_MODEL
COL_GATE = COL_GB + D_MODEL
GATE_LANE_I = 0
GATE_LANE_F = A_HEADS
GATE_LANE_LR = 2 * A_HEADS
MXU_COLS = 256
N_PROJ = COL_GATE + MXU_COLS
PROJ_TN = N_PROJ // 7
assert PROJ_TN * 7 == N_PROJ and PROJ_TN % MXU_COLS == 0
PROJ_VMEM_LIMIT = 52 * 1024 * 1024

SRC_QK = 0
SRC_AV = 2 * A_WIDTH
SRC_I = SRC_AV + A_WIDTH
SRC_AO = SRC_I + 2 * A_HEADS
SRC_BQ = SRC_AO + 2 * A_WIDTH
SRC_BV = SRC_BQ + 2 * B_KWIDTH
SRC_BG = SRC_BV + B_WIDTH
SRC_BZ = SRC_BG + GATE_RANK
RELAYOUT_RUNS = ((COL_Q, COL_BQ, SRC_QK), (COL_BQ, COL_AV, SRC_BQ), (COL_AV, COL_AO, SRC_AV),
                 (COL_AO, COL_BV, SRC_AO), (COL_BV, COL_BZ, SRC_BV), (COL_BZ, COL_GATE, SRC_BZ))

MLSTM_CHUNK = 256
GLA_STEP = 256
GLA_CHUNK = 64
GLA_BLOCK = 16


def _dot(a, b):
    return jnp.dot(a, b, preferred_element_type=F32)


def _dot_nt(a, b):
    return lax.dot_general(a, b, (((1,), (1,)), ((), ())), preferred_element_type=F32)


def _dot_tn(a, b):
    return lax.dot_general(a, b, (((0,), (0,)), ((), ())), preferred_element_type=F32)


def _mask_dot(mask_bf16, x):
    hi = x.astype(BF16)
    r1 = x - hi.astype(F32)
    mid = r1.astype(BF16)
    lo = (r1 - mid.astype(F32)).astype(BF16)
    return _dot(mask_bf16, hi) + _dot(mask_bf16, mid) + _dot(mask_bf16, lo)


def _log_sigmoid(z):
    return jnp.minimum(z, 0.0) - LN_2 * jnp.log2(1.0 + jnp.exp2(jnp.abs(z) * (-LOG2_E)))


def _sigmoid(z):
    return 1.0 / (1.0 + jnp.exp2(z * (-LOG2_E)))


def _silu(z):
    return z * _sigmoid(z)


def _proj_kernel(x_ref, w_ref, o_ref):
    o_ref[...] = _dot_nt(x_ref[...].astype(BF16), w_ref[...])


def _proj(x, w_t, tm, tn):
    t, k = x.shape
    n = w_t.shape[0]
    return pl.pallas_call(
        _proj_kernel,
        grid=(t // tm, n // tn),
        in_specs=[pl.BlockSpec((tm, k), lambda i, j: (i, 0)),
                  pl.BlockSpec((tn, k), lambda i, j: (j, 0))],
        out_specs=pl.BlockSpec((tm, tn), lambda i, j: (i, j)),
        out_shape=jax.ShapeDtypeStruct((t, n), F32),
        compiler_params=pltpu.CompilerParams(
            dimension_semantics=("parallel", "arbitrary"), vmem_limit_bytes=PROJ_VMEM_LIMIT),
        name="proj",
    )(x, w_t)


def _mlstm_kernel(qk_ref, voz_ref, g_ref, cw_ref, cb_ref, gb_ref, ng_ref, *rest, n_chunks):
    dec_in, (ya_ref, c_ref, n_ref, m_ref), dec_out, (tailq_ref, tailk_ref) = (
        rest[:21], rest[21:25], rest[25:29], rest[29:])
    L = MLSTM_CHUNK
    c = pl.program_id(2)

    @pl.when(c == 0)
    def _():
        c_ref[...] = jnp.zeros_like(c_ref)
        n_ref[...] = jnp.zeros_like(n_ref)
        m_ref[...] = jnp.zeros_like(m_ref)
        tailq_ref[...] = jnp.zeros_like(tailq_ref)
        tailk_ref[...] = jnp.zeros_like(tailk_ref)

    _mlstm_decode_step((pl.program_id(0) * n_chunks + c) % A_HEADS, *dec_in, *dec_out)

    sub = lax.broadcasted_iota(jnp.int32, (1, SUBLANES, 1), 1)

    def conv_silu(col0, tail_ref):
        width = A_WIDTH
        x = qk_ref[:, col0:col0 + width]
        w_ref, b_ref = cw_ref.at[:, col0:col0 + width], cb_ref.at[:, col0:col0 + width]
        x3 = jnp.concatenate([tail_ref[...], x], axis=0).reshape(L // SUBLANES + 1, SUBLANES, width)
        acc = b_ref[...] + x * w_ref[CONV_W - 1:CONV_W, :]
        for j in range(1, CONV_W):
            rot = pltpu.roll(x3, j, axis=1)
            xs = jnp.where(sub < j, rot[:-1], rot[1:]).reshape(L, width)
            acc = acc + xs * w_ref[CONV_W - 1 - j:CONV_W - j, :]
        tail_ref[...] = x[L - SUBLANES:, :]
        return _silu(acc)

    q_all = conv_silu(0, tailq_ref)
    k_all = conv_silu(A_WIDTH, tailk_ref) * (A_DK ** -0.5)
    v_ref, ao_ref, az_ref = (voz_ref.at[:, i * A_WIDTH:(i + 1) * A_WIDTH] for i in range(3))

    lane = lax.broadcasted_iota(jnp.int32, (L, LANES), 1)
    row = lax.broadcasted_iota(jnp.int32, (L, L), 0)
    col = lax.broadcasted_iota(jnp.int32, (L, L), 1)
    causal = row >= col
    causal_b = causal.astype(BF16)

    g = g_ref[...] + gb_ref[...]
    is_f = (lane >= GATE_LANE_F) & (lane < GATE_LANE_F + A_HEADS)
    g2 = jnp.where(is_f, _log_sigmoid(g), g)
    x2 = jnp.where(is_f, _mask_dot(causal_b, g2), g2)
    x2t = x2.T

    for hh in range(A_HEADS):
        cols = slice(hh * A_DK, (hh + 1) * A_DK)
        q, k, v = q_all[:, cols], k_all[:, cols], v_ref[:, cols]
        li, lf = GATE_LANE_I + hh, GATE_LANE_F + hh
        itil_col, b_col = x2[:, li:li + 1], x2[:, lf:lf + 1]
        itil_row, b_row = x2t[li:li + 1, :], x2t[lf:lf + 1, :]

        m_prev = m_ref[0, hh, 0:1, 0:1]
        dmat = jnp.where(causal, b_col - b_row + itil_row, -jnp.inf)
        inter = b_col + m_prev
        m_t = jnp.maximum(inter, jnp.max(dmat, axis=1, keepdims=True))
        w = jnp.exp(dmat - m_t)
        decay = jnp.exp(inter - m_t)

        qb, kb, vb = q.astype(BF16), k.astype(BF16), v.astype(BF16)
        c_old = c_ref[0, hh]
        n_old = n_ref[0, hh]
        s = _dot_nt(qb, kb) * w
        num = decay * _dot(qb, c_old.astype(BF16)) + _dot(s.astype(BF16), vb)
        den = decay * jnp.sum(q * n_old, axis=1, keepdims=True) + jnp.sum(s, axis=1, keepdims=True)
        inv = 1.0 / jnp.maximum(jnp.abs(den), jnp.exp(-m_t))

        m_new = m_t[L - 1:L, :]
        b_last = b_col[L - 1:L, :]
        wk = jnp.exp(b_last - b_col + itil_col - m_new)
        dec = jnp.exp(b_last + m_prev - m_new)
        kw = k * wk
        c_ref[0, hh] = dec * c_old + _dot_tn(kw.astype(BF16), vb)
        n_ref[0, hh] = dec * n_old + jnp.sum(kw, axis=0, keepdims=True)
        m_ref[0, hh] = jnp.broadcast_to(m_new, (1, LANES))

        mu = jnp.mean(num, axis=1, keepdims=True)
        hc = num - mu
        var = jnp.mean(hc * hc, axis=1, keepdims=True)
        hn = hc * (inv * lax.rsqrt(var * (inv * inv) + LN_EPS)) * ng_ref[:, cols]
        ya_ref[:, cols] = (hn * _sigmoid(ao_ref[:, cols]) * _silu(az_ref[:, cols])).astype(ya_ref.dtype)


def _mlstm(p_all, batch, seq, conv_w, conv_b, gate_bias, a_norm_g, p_s, conv_state, c_state, n_state, m_state):
    L = MLSTM_CHUNK
    hps = A_HEADS
    nc = seq // L
    t = batch * seq
    wd = hps * A_DK
    k_off = A_WIDTH // wd

    def rows(b, h, c):
        return b * nc + c

    def pcol(base):
        return pl.BlockSpec((L, wd), lambda b, h, c: (rows(b, h, c), base // wd + h))

    in_specs = [
        pl.BlockSpec((L, 2 * wd), lambda b, h, c: (rows(b, h, c), COL_Q // (2 * wd))),
        pl.BlockSpec((L, 3 * wd), lambda b, h, c: (rows(b, h, c), COL_AV // (3 * wd))),
        pl.BlockSpec((L, LANES), lambda b, h, c: (rows(b, h, c), COL_GATE // LANES)),
        pl.BlockSpec((CONV_W, 2 * wd), lambda b, h, c: (0, 0)),
        pl.BlockSpec((1, 2 * wd), lambda b, h, c: (0, 0)),
        pl.BlockSpec((1, LANES), lambda b, h, c: (0, 0)),
        pl.BlockSpec((1, wd), lambda b, h, c: (0, h)),
    ]
    out_specs = [
        pl.BlockSpec((L, wd), lambda b, h, c: (rows(b, h, c), h)),
        pl.BlockSpec((1, hps, A_DK, A_DV), lambda b, h, c: (b, h, 0, 0)),
        pl.BlockSpec((1, hps, 1, A_DK), lambda b, h, c: (b, h, 0, 0)),
        pl.BlockSpec((1, hps, 1, LANES), lambda b, h, c: (b, h, 0, 0)),
    ]
    out_shape = [
        jax.ShapeDtypeStruct((t, A_WIDTH), BF16),
        jax.ShapeDtypeStruct((batch, A_HEADS, A_DK, A_DV), F32),
        jax.ShapeDtypeStruct((batch, A_HEADS, 1, A_DK), F32),
        jax.ShapeDtypeStruct((batch, A_HEADS, 1, LANES), F32),
    ]

    nseq = p_s.shape[0]
    TB = DEC_TOKENS
    dstep = _decode_step(batch, nc, nseq)

    def dcol(base, width):
        return pl.BlockSpec((TB, width), lambda b, h, c: (dstep(b, h, c)[0], base // width + dstep(b, h, c)[1]))

    def dconv(j, base):
        return pl.BlockSpec((TB, A_DK), lambda b, h, c: (dstep(b, h, c)[0],
                                                        (j * 2 * A_WIDTH + base) // A_DK + dstep(b, h, c)[1]))

    def dhead(shape, off=0):
        return pl.BlockSpec(shape, lambda b, h, c: (0, off + dstep(b, h, c)[1]))

    def dstate(*dims):
        return pl.BlockSpec((TB, 1) + dims, lambda b, h, c: dstep(b, h, c) + (0,) * len(dims))

    in_specs += [
        dcol(COL_Q, A_DK), dcol(COL_K, A_DK), dcol(COL_AV, A_DV), dcol(COL_AO, A_DV), dcol(COL_AZ, A_DV),
        pl.BlockSpec((TB, LANES), lambda b, h, c: (dstep(b, h, c)[0], COL_GATE // LANES)),
        dconv(0, 0), dconv(1, 0), dconv(2, 0), dconv(0, A_WIDTH), dconv(1, A_WIDTH), dconv(2, A_WIDTH),
        dhead((CONV_W, A_DK)), dhead((CONV_W, A_DK), A_HEADS), dhead((1, A_DK)), dhead((1, A_DK), A_HEADS),
        pl.BlockSpec((1, LANES), lambda b, h, c: (0, 0)), dhead((1, A_DV)),
        dstate(A_DK, A_DV), dstate(1, A_DK),
        pl.BlockSpec((TB, A_HEADS), lambda b, h, c: (dstep(b, h, c)[0], 0)),
    ]
    out_specs += [
        pl.BlockSpec((TB, A_DV), lambda b, h, c: dstep(b, h, c)),
        dstate(A_DK, A_DV), dstate(1, A_DK),
        pl.BlockSpec((TB, LANES), lambda b, h, c: dstep(b, h, c)),
    ]
    out_shape += [
        jax.ShapeDtypeStruct((nseq, A_WIDTH), BF16),
        jax.ShapeDtypeStruct((nseq, A_HEADS, A_DK, A_DV), F32),
        jax.ShapeDtypeStruct((nseq, A_HEADS, 1, A_DK), F32),
        jax.ShapeDtypeStruct((nseq, A_HEADS * LANES), F32),
    ]
    return pl.pallas_call(
        functools.partial(_mlstm_kernel, n_chunks=nc),
        grid=(batch, A_HEADS // hps, nc),
        in_specs=in_specs,
        out_specs=out_specs,
        out_shape=out_shape,
        scratch_shapes=[pltpu.VMEM((SUBLANES, wd), F32),
                        pltpu.VMEM((SUBLANES, wd), F32)],
        compiler_params=pltpu.CompilerParams(
            dimension_semantics=("parallel", "parallel", "arbitrary"), vmem_limit_bytes=VMEM_LIMIT),
        name="mlstm",
    )(p_all, p_all, p_all, conv_w, conv_b, gate_bias, a_norm_g,
      p_s, p_s, p_s, p_s, p_s, p_s,
      conv_state, conv_state, conv_state, conv_state, conv_state, conv_state,
      conv_w, conv_w, conv_b, conv_b, gate_bias, a_norm_g, c_state, n_state, m_state)


def _gla_gate_log(bg_tile, wg, bgate):
    lr = bg_tile[:, GATE_LANE_LR:GATE_LANE_LR + GATE_RANK]
    z = _dot(lr.astype(BF16), wg.astype(BF16)) + bgate
    return _log_sigmoid(z) / GATE_TAU


def _gla_kernel(q_ref, k_ref, v_ref, bz_ref, g_ref, wg_ref, bgate_ref, ng_ref, *rest):
    dec_in, w_in, (yb_ref, s_out_ref), dec_out, w_out, (st_ref, ds_ref, sb_ref) = (
        rest[:9], rest[9:12], rest[12:14], rest[14:16], rest[16:19], rest[19:])
    L = GLA_STEP
    B = GLA_BLOCK
    nb = L // B
    c = pl.program_id(2)

    @pl.when(c == 0)
    def _():
        st_ref[...] = jnp.zeros_like(st_ref)

    _gla_decode_step(*dec_in, *dec_out)
    for src_ref, dst_ref in zip(w_in, w_out):
        dst_ref[...] = src_ref[...].astype(BF16)

    CH = GLA_CHUNK
    bpc, nch = CH // B, L // CH
    row = lax.broadcasted_iota(jnp.int32, (L, L), 0)
    col = lax.broadcasted_iota(jnp.int32, (L, L), 1)
    tri_c = (((row // CH) == (col // CH)) & (row >= col)).astype(BF16)
    blk3 = lax.broadcasted_iota(jnp.int32, (nb, 1, LANES), 0)
    lane_rel = lax.broadcasted_iota(jnp.int32, (nb, 1, LANES), 2) - (blk3 % bpc) * B
    t3 = lax.broadcasted_iota(jnp.int32, (1, B, 1), 1)

    for hh in range(B_HEADS):
        kcols = slice(hh * B_DK, (hh + 1) * B_DK)
        vcols = slice(hh * B_DV, (hh + 1) * B_DV)
        loga = _gla_gate_log(g_ref[...], wg_ref[:, kcols], bgate_ref[:, kcols])
        bc = _mask_dot(tri_c, loga) * LOG2_E
        bl = jnp.broadcast_to(bc.reshape(nch, CH, B_DK)[:, CH - 1:CH, :],
                              (nch, CH, B_DK)).reshape(L, B_DK)
        k = k_ref[:, kcols]
        qs = q_ref[:, kcols] * (B_DK ** -0.5)
        qt = (qs * jnp.exp2(bc)).astype(BF16)
        kt = (k * jnp.exp2(bl - bc)).astype(BF16)
        eb = jnp.exp2(bl)
        vb = v_ref[:, vcols].astype(BF16)

        bc3, q3, k3 = (x.reshape(nb, B, B_DK) for x in (bc, qs, k))
        groups = B // SUBLANES
        bcg = [bc3[:, g * SUBLANES:(g + 1) * SUBLANES, :] for g in range(groups)]
        qg = [q3[:, g * SUBLANES:(g + 1) * SUBLANES, :] for g in range(groups)]
        ag = [jnp.zeros((nb, SUBLANES, LANES), F32) for _ in range(groups)]
        for s in range(B):
            bc_s, k_s = bc3[:, s:s + 1, :], k3[:, s:s + 1, :]
            for g in range(s // SUBLANES, groups):
                e = jnp.exp2(bcg[g] - bc_s)
                a_col = jnp.sum(qg[g] * k_s * e, axis=2, keepdims=True)
                ag[g] = jnp.where(lane_rel == s, a_col, ag[g])
        a3 = jnp.concatenate(ag, axis=1)
        a3 = jnp.where((lane_rel >= 0) & (lane_rel <= t3), a3, 0.0)
        a_chunks = []
        for ci in range(nch):
            rows_c = [a3[ci * bpc]]
            for i in range(1, bpc):
                blk = ci * bpc + i
                start = bc3[blk - 1, B - 1:B, :]
                qd = (q3[blk] * jnp.exp2(bc3[blk] - start)).astype(BF16)
                ks = k3[ci * bpc:blk].reshape(i * B, B_DK)
                kd = ks * jnp.exp2(start - bc3[ci * bpc:blk].reshape(i * B, B_DK))
                kd = jnp.concatenate([kd, jnp.zeros((LANES - i * B, B_DK), F32)], axis=0).astype(BF16)
                rows_c.append(a3[blk] + _dot_nt(qd, kd))
            a_chunks.append(jnp.concatenate(rows_c, axis=0)[:, 0:CH].astype(BF16))

        for ci in range(nch):
            ds_ref[hh, ci] = _dot_tn(vb[ci * CH:(ci + 1) * CH, :], kt[ci * CH:(ci + 1) * CH, :])
        st = st_ref[hh]
        for ci in range(nch):
            sb_ref[hh, ci] = st.astype(BF16)
            st = st * eb[ci * CH:ci * CH + 1, :] + ds_ref[hh, ci]
        st_ref[hh] = st
        o = jnp.concatenate(
            [_dot(a_chunks[ci], vb[ci * CH:(ci + 1) * CH, :])
             + _dot_nt(qt[ci * CH:(ci + 1) * CH, :], sb_ref[hh, ci]) for ci in range(nch)], axis=0)

        on = o * lax.rsqrt(jnp.mean(o * o, axis=1, keepdims=True) + LN_EPS) * ng_ref[:, vcols]
        yb_ref[:, vcols] = (on * _silu(bz_ref[:, vcols])).astype(yb_ref.dtype)

    @pl.when(c == pl.num_programs(2) - 1)
    def _():
        for hh in range(B_HEADS):
            s_out_ref[0, hh] = st_ref[hh].T


def _gla(p_all, batch, seq, w_gate_up, b_gate, b_norm_g, p_s, s_state, cast_weights):
    L = GLA_STEP
    hps = B_HEADS
    nc = seq // L
    t = batch * seq
    kw, vw = hps * B_DK, hps * B_DV

    def rows(b, h, c):
        return b * nc + c

    in_specs = [
        pl.BlockSpec((L, kw), lambda b, h, c: (rows(b, h, c), COL_BQ // kw + h)),
        pl.BlockSpec((L, kw), lambda b, h, c: (rows(b, h, c), COL_BK // kw + h)),
        pl.BlockSpec((L, vw), lambda b, h, c: (rows(b, h, c), COL_BV // vw + h)),
        pl.BlockSpec((L, vw), lambda b, h, c: (rows(b, h, c), COL_BZ // vw + h)),
        pl.BlockSpec((L, LANES), lambda b, h, c: (rows(b, h, c), COL_GATE // LANES)),
        pl.BlockSpec((GATE_RANK, kw), lambda b, h, c: (0, h)),
        pl.BlockSpec((1, kw), lambda b, h, c: (0, h)),
        pl.BlockSpec((1, vw), lambda b, h, c: (0, h)),
    ]
    out_specs = [
        pl.BlockSpec((L, vw), lambda b, h, c: (rows(b, h, c), h)),
        pl.BlockSpec((1, hps, B_DK, B_DV), lambda b, h, c: (b, h, 0, 0)),
    ]
    out_shape = [
        jax.ShapeDtypeStruct((t, B_WIDTH), BF16),
        jax.ShapeDtypeStruct((batch, B_HEADS, B_DK, B_DV), F32),
    ]

    nseq = p_s.shape[0]
    TB = DEC_TOKENS
    dstep = _decode_step(batch, nc, nseq)

    def dcol(base, width):
        return pl.BlockSpec((TB, width), lambda b, h, c: (dstep(b, h, c)[0], base // width + dstep(b, h, c)[1]))

    def dhead(shape):
        return pl.BlockSpec(shape, lambda b, h, c: (0, dstep(b, h, c)[1]))

    dstate = pl.BlockSpec((TB, 1, B_DK, B_DV), lambda b, h, c: dstep(b, h, c) + (0, 0))
    in_specs += [
        dcol(COL_BQ, B_DK), dcol(COL_BK, B_DK), dcol(COL_BV, B_DV), dcol(COL_BZ, B_DV),
        pl.BlockSpec((TB, LANES), lambda b, h, c: (dstep(b, h, c)[0], COL_GATE // LANES)),
        dhead((GATE_RANK, B_DK)), dhead((1, B_DK)), dhead((1, B_DV)), dstate,
    ]
    out_specs += [pl.BlockSpec((TB, B_DV), lambda b, h, c: dstep(b, h, c)), dstate]
    out_shape += [
        jax.ShapeDtypeStruct((nseq, B_WIDTH), BF16),
        jax.ShapeDtypeStruct((nseq, B_HEADS, B_DK, B_DV), F32),
    ]

    n_steps = batch * nc
    for wmat in cast_weights:
        slab = pl.BlockSpec((wmat.shape[0] // n_steps, wmat.shape[1]), lambda b, h, c: (rows(b, h, c), 0))
        in_specs.append(slab)
        out_specs.append(slab)
        out_shape.append(jax.ShapeDtypeStruct(wmat.shape, BF16))
    return pl.pallas_call(
        _gla_kernel,
        grid=(batch, B_HEADS // hps, nc),
        in_specs=in_specs,
        out_specs=out_specs,
        out_shape=out_shape,
        scratch_shapes=[pltpu.VMEM((hps, B_DV, B_DK), F32),
                        pltpu.VMEM((hps, L // GLA_CHUNK, B_DV, B_DK), F32),
                        pltpu.VMEM((hps, L // GLA_CHUNK, B_DV, B_DK), BF16)],
        compiler_params=pltpu.CompilerParams(
            dimension_semantics=("parallel", "parallel", "arbitrary"), vmem_limit_bytes=VMEM_LIMIT),
        name="gla",
    )(p_all, p_all, p_all, p_all, p_all, w_gate_up, b_gate, b_norm_g,
      p_s, p_s, p_s, p_s, p_s, w_gate_up, b_gate, b_norm_g, s_state, *cast_weights)


DEC_TOKENS = 16


def _decode_step(batch, nc, nseq):
    assert batch * nc == (nseq // DEC_TOKENS) * A_HEADS and A_HEADS == B_HEADS

    def step(b, h, c):
        s = b * nc + c
        return s // A_HEADS, s % A_HEADS
    return step


def _mlstm_decode_step(h_idx, qp_ref, kp_ref, av_ref, ao_ref, az_ref, g_ref,
                       sq0_ref, sq1_ref, sq2_ref, sk0_ref, sk1_ref, sk2_ref,
                       cwq_ref, cwk_ref, cbq_ref, cbk_ref, gb_ref, ang_ref, c_ref, n_ref, m_ref,
                       ya_ref, c_out_ref, n_out_ref, m_out_ref):
    TB = DEC_TOKENS

    def conv_silu(s0, s1, s2, x, w_ref, b_ref):
        acc = b_ref[...] + s0[...] * w_ref[0:1, :]
        acc = acc + s1[...] * w_ref[1:2, :]
        acc = acc + s2[...] * w_ref[2:3, :]
        acc = acc + x[...] * w_ref[3:4, :]
        return _silu(acc)

    q = conv_silu(sq0_ref, sq1_ref, sq2_ref, qp_ref, cwq_ref, cbq_ref)
    k = conv_silu(sk0_ref, sk1_ref, sk2_ref, kp_ref, cwk_ref, cbk_ref) * (A_DK ** -0.5)
    v = av_ref[...]
    g = g_ref[...]
    gbias = gb_ref[...]
    gate_lane = lax.broadcasted_iota(jnp.int32, (TB, LANES), 1)
    gsum = g + gbias

    def gate(lane0):
        return jnp.sum(jnp.where(gate_lane == lane0 + h_idx, gsum, 0.0), axis=1, keepdims=True)

    itil = gate(GATE_LANE_I)
    logf = _log_sigmoid(gate(GATE_LANE_F))
    lane_h = lax.broadcasted_iota(jnp.int32, (TB, A_HEADS), 1)
    m_prev = jnp.sum(jnp.where(lane_h == h_idx, m_ref[...], 0.0), axis=1, keepdims=True)
    inter = logf + m_prev
    m_t = jnp.maximum(inter, itil)
    w = jnp.exp(itil - m_t)
    decay = jnp.exp(inter - m_t)
    n_old = n_ref[:, 0, 0, :]
    s = jnp.sum(q * k, axis=1, keepdims=True) * w
    den = decay * jnp.sum(q * n_old, axis=1, keepdims=True) + s
    scale = 1.0 / jnp.maximum(jnp.abs(den), jnp.exp(-m_t))
    kw = k * w
    n_out_ref[:, 0, 0, :] = decay * n_old + kw
    m_out_ref[...] = jnp.broadcast_to(m_t, (TB, LANES))

    rows = lax.broadcasted_iota(jnp.int32, (TB, 1), 0)
    qb, kwb, vb = q.astype(BF16), kw.astype(BF16), v.astype(BF16)
    h_rows = []
    for t in range(TB):
        c_old = c_ref[t, 0]
        qc = _dot(qb, c_old.astype(BF16))[t:t + 1, :]
        h_rows.append((decay[t:t + 1, :] * qc + s[t:t + 1, :] * v[t:t + 1, :]) * scale[t:t + 1, :])
        outer = _dot_tn(jnp.where(rows == t, kwb, jnp.zeros_like(kwb)), vb)
        c_out_ref[t, 0] = decay[t:t + 1, :] * c_old + outer
    h = jnp.concatenate(h_rows, axis=0)
    mu = jnp.mean(h, axis=1, keepdims=True)
    hc = h - mu
    var = jnp.mean(hc * hc, axis=1, keepdims=True)
    hn = hc * lax.rsqrt(var + LN_EPS) * ang_ref[...]
    ya_ref[...] = (hn * _sigmoid(ao_ref[...]) * _silu(az_ref[...])).astype(ya_ref.dtype)


def _gla_decode_step(bq_ref, bk_ref, bv_ref, bz_ref, g_ref, wg_ref, bgate_ref, bng_ref, s_ref,
                     yb_ref, s_out_ref):
    TB = DEC_TOKENS
    gq = bq_ref[...] * (B_DK ** -0.5)
    gk = bk_ref[...]
    gv = bv_ref[...]
    loga = _gla_gate_log(g_ref[...], wg_ref[...], bgate_ref[...])
    eb = jnp.exp(loga)
    a = jnp.sum(gq * gk, axis=1, keepdims=True)
    ebt = jnp.concatenate([eb, jnp.zeros((LANES - TB, B_DK), F32)], axis=0).T
    rows = lax.broadcasted_iota(jnp.int32, (TB, 1), 0)
    qeb, gkb, gvb = (gq * eb).astype(BF16), gk.astype(BF16), gv.astype(BF16)
    o_rows = []
    for t in range(TB):
        s_old = s_ref[t, 0]
        o_rows.append(_dot(qeb, s_old.astype(BF16))[t:t + 1, :] + a[t:t + 1, :] * gv[t:t + 1, :])
        outer = _dot_tn(jnp.where(rows == t, gkb, jnp.zeros_like(gkb)), gvb)
        s_out_ref[t, 0] = ebt[:, t:t + 1] * s_old + outer
    o = jnp.concatenate(o_rows, axis=0)
    on = o * lax.rsqrt(jnp.mean(o * o, axis=1, keepdims=True) + LN_EPS) * bng_ref[...]
    yb_ref[...] = (on * _silu(bz_ref[...])).astype(yb_ref.dtype)


def _out_kernel(ya_ref, yb_ref, ga_ref, gb_ref, x_ref, yas_ref, ybs_ref, gas_ref, gbs_ref, xs_ref,
                wpa_ref, wpb_ref, wo_ref, lng_ref, lnb_ref, o_ref, os_ref):
    def rows(ya, yb, ga, gb, x):
        pa = _dot(ya[...], wpa_ref[...])
        pb = _dot(yb[...], wpb_ref[...])
        merged = _sigmoid(ga[...]) * pa + _sigmoid(gb[...]) * pb
        r = ALPHA * x[...] + _dot(merged.astype(BF16), wo_ref[...])
        mu = jnp.mean(r, axis=1, keepdims=True)
        rc = r - mu
        var = jnp.mean(rc * rc, axis=1, keepdims=True)
        return rc * lax.rsqrt(var + LN_EPS) * lng_ref[...] + lnb_ref[...]

    last = pl.num_programs(0) - 1

    @pl.when(pl.program_id(0) < last)
    def _():
        o_ref[...] = rows(ya_ref, yb_ref, ga_ref, gb_ref, x_ref)

    @pl.when(pl.program_id(0) == last)
    def _():
        os_ref[...] = rows(yas_ref, ybs_ref, gas_ref, gbs_ref, xs_ref)


def _out(ya, yb, p_all, x, ya_s, yb_s, p_s, xs, w_pa, w_pb, w_out, ln_g, ln_b, tm):
    t, ts = x.shape[0], xs.shape[0]
    n_tiles = t // tm
    const = lambda i: (0, 0)
    single = pl.Buffered(1)

    def tile(i):
        return jnp.minimum(i, n_tiles - 1)

    in_specs = [
        pl.BlockSpec((tm, A_WIDTH), lambda i: (tile(i), 0)),
        pl.BlockSpec((tm, B_WIDTH), lambda i: (tile(i), 0)),
        pl.BlockSpec((tm, D_MODEL), lambda i: (tile(i), COL_GA // D_MODEL)),
        pl.BlockSpec((tm, D_MODEL), lambda i: (tile(i), COL_GB // D_MODEL)),
        pl.BlockSpec((tm, D_MODEL), lambda i: (tile(i), 0)),
        pl.BlockSpec((ts, A_WIDTH), const),
        pl.BlockSpec((ts, B_WIDTH), const),
        pl.BlockSpec((ts, D_MODEL), lambda i: (0, COL_GA // D_MODEL)),
        pl.BlockSpec((ts, D_MODEL), lambda i: (0, COL_GB // D_MODEL)),
        pl.BlockSpec((ts, D_MODEL), const),
        pl.BlockSpec((A_WIDTH, D_MODEL), const, pipeline_mode=single),
        pl.BlockSpec((B_WIDTH, D_MODEL), const, pipeline_mode=single),
        pl.BlockSpec((D_MODEL, D_MODEL), const, pipeline_mode=single),
        pl.BlockSpec((1, D_MODEL), const),
        pl.BlockSpec((1, D_MODEL), const),
    ]
    return pl.pallas_call(
        _out_kernel,
        grid=(n_tiles + 1,),
        in_specs=in_specs,
        out_specs=[pl.BlockSpec((tm, D_MODEL), lambda i: (tile(i), 0)),
                   pl.BlockSpec((ts, D_MODEL), const)],
        out_shape=[jax.ShapeDtypeStruct((t, D_MODEL), F32), jax.ShapeDtypeStruct((ts, D_MODEL), F32)],
        compiler_params=pltpu.CompilerParams(
            dimension_semantics=("arbitrary",), vmem_limit_bytes=VMEM_LIMIT),
        name="outproj",
    )(ya, yb, p_all, p_all, x, ya_s, yb_s, p_s, p_s, xs, w_pa, w_pb, w_out, ln_g, ln_b)


RELAYOUT_TN = 1024


def _relayout_kernel(src_ref, gif_ref, gb0_ref, gb1_ref, xs_ref, o_ref, ps_ref):
    j = pl.program_id(0)
    tn = RELAYOUT_TN
    n_main = COL_GATE // tn

    @pl.when(j < n_main)
    def _():
        o_ref[...] = src_ref[...].astype(BF16)

    @pl.when(j == n_main)
    def _():
        pad = jnp.zeros((tn - 2 * A_HEADS - GATE_RANK, src_ref.shape[1]), F32)
        o_ref[...] = jnp.concatenate([gif_ref[...], gb0_ref[...], gb1_ref[...], pad], axis=0).astype(BF16)

    ps_ref[...] = _dot_nt(xs_ref[...].astype(BF16), o_ref[...])


def _relayout_w_in(w_in_t, xs):
    n_src, k = w_in_t.shape
    ts = xs.shape[0]
    tn = RELAYOUT_TN

    def src_row(j):
        shift = 0
        for first, last, src in RELAYOUT_RUNS:
            assert first % tn == 0 and last % tn == 0 and (src - first) % SUBLANES == 0
            shift = jnp.where((j >= first // tn) & (j < last // tn), (src - first) // SUBLANES, shift)
        group = jnp.minimum(j * (tn // SUBLANES) + shift, (n_src - tn) // SUBLANES)
        return group * SUBLANES

    return pl.pallas_call(
        _relayout_kernel,
        grid=(pl.cdiv(N_PROJ, tn),),
        in_specs=[pl.BlockSpec((pl.Element(tn), pl.Element(k)), lambda j: (src_row(j), 0)),
                  pl.BlockSpec((SUBLANES, k), lambda j: (SRC_I // SUBLANES, 0)),
                  pl.BlockSpec((SUBLANES, k), lambda j: (SRC_BG // SUBLANES, 0)),
                  pl.BlockSpec((SUBLANES, k), lambda j: (SRC_BG // SUBLANES + 1, 0)),
                  pl.BlockSpec((ts, k), lambda j: (0, 0))],
        out_specs=[pl.BlockSpec((tn, k), lambda j: (j, 0)),
                   pl.BlockSpec((ts, tn), lambda j: (0, j))],
        out_shape=[jax.ShapeDtypeStruct((N_PROJ, k), BF16), jax.ShapeDtypeStruct((ts, N_PROJ), F32)],
        compiler_params=pltpu.CompilerParams(
            dimension_semantics=("parallel",), vmem_limit_bytes=VMEM_LIMIT),
        name="relayout",
    )(w_in_t, w_in_t, w_in_t, w_in_t, xs)


def kernel(x_prompt, x_sample, state_mlstm_C, state_mlstm_n, state_mlstm_m, state_conv, state_gla_S,
           w_in, conv_w, conv_b, b_i, b_f, a_norm_g, w_gate_up, b_gate, b_norm_g, w_pa, w_pb, w_out,
           ln_g, ln_b):
    batch, seq, _ = x_prompt.shape
    nseq = x_sample.shape[0]
    assert w_in.shape[0] == 1, "single-layer step"
    d = 0

    def layer(a):
        return a.reshape(a.shape[1:])

    xs = x_sample.reshape(nseq, D_MODEL)
    wp, ps_all = _relayout_w_in(layer(w_in).T, xs)
    cw = conv_w[d]
    cb = conv_b[d][None, :]
    gate_bias = jnp.concatenate([b_i[d], b_f[d], jnp.zeros((LANES - 2 * A_HEADS,), F32)])[None, :]
    ang = a_norm_g[d][None, :]
    bng = b_norm_g[d][None, :]
    wg = w_gate_up[d]
    bgate = b_gate[d][None, :]
    lng, lnb = ln_g[d][None, :], ln_b[d][None, :]

    xp = x_prompt.reshape(batch * seq, D_MODEL)
    p_all = _proj(xp, wp, tm=1024, tn=PROJ_TN)

    conv_state = state_conv.reshape(nseq, (CONV_W - 1) * 2 * A_WIDTH)
    ya, p_c, p_n, p_m, ya_s, s_c, s_n, s_m = _mlstm(
        p_all, batch, seq, cw, cb, gate_bias, ang, ps_all, conv_state,
        layer(state_mlstm_C), state_mlstm_n.reshape(nseq, A_HEADS, 1, A_DK), layer(state_mlstm_m))
    yb, p_s, yb_s, s_s, wpa, wpb, wo = _gla(p_all, batch, seq, wg, bgate, bng, ps_all, layer(state_gla_S),
                                            (layer(w_pa), layer(w_pb), layer(w_out)))

    y_prompt, y_sample = _out(ya, yb, p_all, xp, ya_s, yb_s, ps_all, xs, wpa, wpb, wo, lng, lnb, tm=256)
    y_prompt = y_prompt.reshape(batch, seq, D_MODEL)
    y_sample = y_sample.reshape(nseq, 1, D_MODEL)
    p_conv = p_all.reshape(batch, seq, N_PROJ)[:, seq - (CONV_W - 1):, :2 * A_WIDTH]
    s_conv = jnp.concatenate([layer(state_conv)[:, 1:, :], ps_all[:, None, :2 * A_WIDTH]], axis=1)

    def stacked(a):
        return a.reshape((1,) + a.shape)

    return (y_prompt, y_sample,
            stacked(p_c), p_n.reshape(1, batch, A_HEADS, A_DK), stacked(p_m[:, :, 0, 0]),
            stacked(p_conv), stacked(p_s),
            stacked(s_c), s_n.reshape(1, nseq, A_HEADS, A_DK),
            stacked(s_m.reshape(nseq, A_HEADS, LANES)[:, :, 0]), stacked(s_conv), stacked(s_s))
```
